```python
import math
import jax, jax.numpy as jnp
from jax import lax
import numpy as np

D_MODEL = 2048
BATCH = 1
SEQ = 8192
DEPTH = 2
DEC_BATCH = 128
DEC_SEQ = 4
PAST_LEN = 8192
PAGE_SIZE = 128

N_EVEN = (DEPTH + 1) // 2
N_ODD = DEPTH // 2
HEAD_DIM = 64
ROT_DIM = HEAD_DIM // 4
ROPE_THETA = 500000.0
NORM_EPS = 1e-5
BAND = 128

SSD_HEADDIM = 64
SSD_INNER = D_MODEL
SSD_HEADS = SSD_INNER // SSD_HEADDIM
SSD_GROUPS = 4
SSD_STATE = 128
SSD_CONV = 4
SSD_CHUNK = 128
CONV_DIM = SSD_INNER + 2 * SSD_GROUPS * SSD_STATE

SWA_WINDOW = 128
SWA_HEADS = D_MODEL // HEAD_DIM
SWA_KV = 8
SWA_REP = SWA_HEADS // SWA_KV

DIL_PATTERNS = ((128, 1), (512, 4), (2048, 16))
N_DIL = len(DIL_PATTERNS)
DIL_HEADS = D_MODEL // HEAD_DIM
DIL_KV = 8
DIL_REP = DIL_HEADS // DIL_KV
DIL_MAX_WINDOW = max(w for w, _ in DIL_PATTERNS)

MLP_HIDDEN = 4 * D_MODEL

E_SIZES = (SSD_INNER, CONV_DIM, SSD_HEADS, SWA_HEADS * HEAD_DIM, SWA_KV * HEAD_DIM, SWA_KV * HEAD_DIM)
E_IN = sum(E_SIZES)
E_OUT = SSD_INNER + SWA_HEADS * HEAD_DIM
O_SIZES = (N_DIL * DIL_HEADS * HEAD_DIM, DIL_KV * HEAD_DIM, DIL_KV * HEAD_DIM)
O_IN = sum(O_SIZES)
O_OUT = DIL_HEADS * HEAD_DIM

kernel_name = 'hybrid_ssd_swa_dilated_decoder_step'

F32 = jnp.float32


def split_cols(a, sizes):
    idx = np.cumsum(sizes)[:-1].tolist()
    return jnp.split(a, idx, axis=-1)


def rmsnorm(x, g):
    xf = x.astype(F32)
    y = xf * lax.rsqrt(jnp.mean(xf * xf, axis=-1, keepdims=True) + NORM_EPS)
    return (y * g.astype(F32)).astype(x.dtype)


def rope(x, pos):
    half = ROT_DIM // 2
    inv_freq = jnp.exp(jnp.arange(half, dtype=F32) * (-math.log(ROPE_THETA) / half))
    ang = pos.astype(F32)[:, None] * inv_freq[None, :]
    cos = jnp.cos(ang)[None, :, None, :]
    sin = jnp.sin(ang)[None, :, None, :]
    xr = x[..., :ROT_DIM].astype(F32)
    x1, x2 = xr[..., :half], xr[..., half:]
    rot = jnp.concatenate([x1 * cos - x2 * sin, x2 * cos + x1 * sin], axis=-1).astype(x.dtype)
    return jnp.concatenate([rot, x[..., ROT_DIM:]], axis=-1)


def attn_probs(s, mask, sink):
    s = jnp.where(mask, s, -jnp.inf)
    m = jnp.max(s, axis=-1, keepdims=True)
    if sink is not None:
        sk = sink.astype(F32)[:, :, None, None]
        m = jnp.maximum(m, sk)
    e = jnp.exp(s - m)
    denom = jnp.sum(e, axis=-1, keepdims=True)
    if sink is not None:
        denom = denom + jnp.exp(sk - m)
    return e / denom, (m + jnp.log(denom))[..., 0]


def band_attention(q, k, v, max_dist, sink=None):
    b, n, g, r, hd = q.shape
    nb = -(-n // BAND)
    pad = nb * BAND - n
    qb = jnp.pad(q, ((0, 0), (0, pad), (0, 0), (0, 0), (0, 0))).reshape(b, nb, BAND, g, r, hd)
    kp = jnp.pad(k, ((0, 0), (BAND, pad), (0, 0), (0, 0))).reshape(b, nb + 1, BAND, g, hd)
    vp = jnp.pad(v, ((0, 0), (BAND, pad), (0, 0), (0, 0))).reshape(b, nb + 1, BAND, g, hd)
    kb = jnp.concatenate([kp[:, :-1], kp[:, 1:]], axis=2)
    vb = jnp.concatenate([vp[:, :-1], vp[:, 1:]], axis=2)
    qi = jnp.arange(BAND)[:, None]
    kj = jnp.arange(2 * BAND)[None, :] - BAND
    dist = qi - kj
    start = (jnp.arange(nb) * BAND)[:, None, None]
    mask = (dist >= 0) & (dist <= max_dist) & (start + kj >= 0)
    s = jnp.einsum('bcqgrd,bckgd->bcgrqk', qb, kb).astype(F32) * (hd ** -0.5)
    p, lse = attn_probs(s, mask[None, :, None, None], sink)
    o = jnp.einsum('bcgrqk,bckgd->bcqgrd', p.astype(v.dtype), vb).reshape(b, nb * BAND, g, r, hd)[:, :n]
    lse = jnp.transpose(lse, (0, 1, 4, 2, 3)).reshape(b, nb * BAND, g, r)[:, :n]
    return o, lse


def window_attention_cached(q, k_new, v_new, k_prev, v_prev, start_pos, sink):
    T = q.shape[1]
    Lb = k_prev.shape[1]
    kc = jnp.concatenate([k_prev.astype(k_new.dtype), k_new], axis=1)
    vc = jnp.concatenate([v_prev.astype(v_new.dtype), v_new], axis=1)
    qpos = start_pos + jnp.arange(T)
    kpos = start_pos - Lb + jnp.arange(Lb + T)
    dist = qpos[:, None] - kpos[None, :]
    mask = (dist >= 0) & (dist < SWA_WINDOW)
    s = jnp.einsum('btgrd,bkgd->bgrtk', q, kc).astype(F32) * (HEAD_DIM ** -0.5)
    p, _ = attn_probs(s, mask, sink)
    return jnp.einsum('bgrtk,bkgd->btgrd', p.astype(vc.dtype), vc)


def dilated_prompt(q, k, v, dil, max_dist):
    b, L = q.shape[:2]
    n = L // dil
    def to_res(a):
        return jnp.swapaxes(a.reshape((b, n, dil) + a.shape[2:]), 1, 2).reshape((b * dil, n) + a.shape[2:])
    def from_res(a):
        return jnp.swapaxes(a.reshape((b, dil, n) + a.shape[2:]), 1, 2).reshape((b, L) + a.shape[2:])
    o, lse = band_attention(to_res(q), to_res(k), to_res(v), max_dist)
    return from_res(o), from_res(lse)


def dilated_cached(q, k_new, v_new, k_prev, v_prev, window, dil):
    T = q.shape[1]
    Lb = k_prev.shape[1]
    nk = window // dil + 1
    j = Lb + jnp.arange(T)[:, None] - dil * jnp.arange(nk)[None, :]
    valid = j >= 0
    sel = (j < Lb)[None, :, :, None, None]
    jp = jnp.clip(j, 0, Lb - 1)
    jn = jnp.clip(j - Lb, 0, T - 1)
    kg = jnp.where(sel, k_prev[:, jp].astype(k_new.dtype), k_new[:, jn])
    vg = jnp.where(sel, v_prev[:, jp].astype(v_new.dtype), v_new[:, jn])
    s = jnp.einsum('btgrd,btkgd->bgrtk', q, kg).astype(F32) * (HEAD_DIM ** -0.5)
    p, lse = attn_probs(s, valid[None, None, None], None)
    o = jnp.einsum('bgrtk,btkgd->btgrd', p.astype(vg.dtype), vg)
    return o, jnp.transpose(lse, (0, 3, 1, 2))


def ssd_scan(x, dt, a, bm, cm, h0):
    b, L, H, P = x.shape
    G, N = bm.shape[2], bm.shape[3]
    R = H // G
    Q = min(SSD_CHUNK, L)
    nc = -(-L // Q)
    pad = nc * Q - L
    def chunks(t, tail):
        t = jnp.pad(t.astype(F32), ((0, 0), (0, pad)) + ((0, 0),) * (t.ndim - 2))
        return t.reshape((b, nc, Q) + tail)
    xc = chunks(x, (G, R, P))
    dtc = chunks(dt, (G, R))
    bc = chunks(bm, (G, N))
    cc = chunks(cm, (G, N))
    la = dtc * a.reshape(G, R)
    acs = jnp.cumsum(la, axis=2)
    xdt = xc * dtc[..., None]
    acs_t = jnp.moveaxis(acs, 2, -1)
    seg = acs_t[..., :, None] - acs_t[..., None, :]
    causal = jnp.tril(jnp.ones((Q, Q), dtype=bool))
    decay = jnp.exp(jnp.where(causal, seg, -jnp.inf))
    cb = jnp.einsum('bclgn,bcsgn->bcgls', cc, bc)
    y_diag = jnp.einsum('bcgrls,bcsgrp->bclgrp', cb[:, :, :, None] * decay, xdt)
    to_end = jnp.exp(acs[:, :, -1:] - acs)
    states = jnp.einsum('bclgn,bclgr,bclgrp->bcgrpn', bc, to_end, xdt)
    chunk_decay = jnp.exp(acs[:, :, -1])
    def step(h, inp):
        st, dec = inp
        return h * dec[..., None, None] + st, h
    h_last, h_in = lax.scan(step, h0.astype(F32).reshape(b, G, R, P, N),
                            (jnp.moveaxis(states, 1, 0), jnp.moveaxis(chunk_decay, 1, 0)))
    h_in = jnp.moveaxis(h_in, 0, 1)
    y_off = jnp.einsum('bclgn,bcgrpn,bclgr->bclgrp', cc, h_in, jnp.exp(acs))
    y = (y_diag + y_off).reshape(b, nc * Q, H, P)[:, :L]
    return y, h_last.reshape(b, H, P, N)


def ssd_branch(z, xbc, dt_raw, conv_prev, ssm_prev, conv_w, conv_b, dt_bias, a_log, d_skip, gate_norm):
    b, L, _ = xbc.shape
    xp = jnp.concatenate([conv_prev.astype(xbc.dtype), xbc], axis=1)
    acc = conv_b
    for i in range(SSD_CONV):
        acc = acc + xp[:, i:i + L] * conv_w[i]
    xbc_c = jax.nn.silu(acc)
    new_conv = xp[:, L:]
    xs, bm, cm = split_cols(xbc_c, (SSD_INNER, SSD_GROUPS * SSD_STATE, SSD_GROUPS * SSD_STATE))
    dt = jax.nn.softplus(dt_raw.astype(F32) + dt_bias.astype(F32))
    a = -jnp.exp(a_log.astype(F32))
    xh = xs.reshape(b, L, SSD_HEADS, SSD_HEADDIM)
    y, h_last = ssd_scan(xh, dt, a, bm.reshape(b, L, SSD_GROUPS, SSD_STATE),
                         cm.reshape(b, L, SSD_GROUPS, SSD_STATE), ssm_prev)
    y = y + d_skip.astype(F32)[:, None] * xh.astype(F32)
    y = y.reshape(b, L, SSD_INNER) * jax.nn.silu(z.astype(F32))
    yg = y.reshape(b, L, SSD_GROUPS, SSD_INNER // SSD_GROUPS)
    yg = yg * lax.rsqrt(jnp.mean(yg * yg, axis=-1, keepdims=True) + NORM_EPS)
    y = yg.reshape(b, L, SSD_INNER) * gate_norm.astype(F32)
    return y.astype(z.dtype), new_conv, h_last


def even_mixer(x, start_pos, conv_prev, ssm_prev, k_prev, v_prev, w_in, conv_w, conv_b,
               dt_bias, a_log, d_skip, gate_norm, sinks, w_out):
    b, L, _ = x.shape
    z, xbc, dt_raw, q, k, v = split_cols(x @ w_in, E_SIZES)
    ssd_out, new_conv, new_ssm = ssd_branch(z, xbc, dt_raw, conv_prev, ssm_prev, conv_w, conv_b,
                                            dt_bias, a_log, d_skip, gate_norm)
    pos = start_pos + jnp.arange(L)
    q = rope(q.reshape(b, L, SWA_HEADS, HEAD_DIM), pos).reshape(b, L, SWA_KV, SWA_REP, HEAD_DIM)
    k = rope(k.reshape(b, L, SWA_KV, HEAD_DIM), pos)
    v = v.reshape(b, L, SWA_KV, HEAD_DIM)
    sink = sinks.reshape(SWA_KV, SWA_REP)
    if k_prev is None:
        att, _ = band_attention(q, k, v, SWA_WINDOW - 1, sink)
        keep = min(SWA_WINDOW, L)
        k_out, v_out = k[:, L - keep:], v[:, L - keep:]
    else:
        att = window_attention_cached(q, k, v, k_prev, v_prev, start_pos, sink)
        k_out, v_out = k, v
    mixed = jnp.concatenate([ssd_out, att.reshape(b, L, SWA_HEADS * HEAD_DIM).astype(ssd_out.dtype)], axis=-1)
    return mixed @ w_out, new_conv, new_ssm, k_out, v_out


def odd_mixer(x, start_pos, k_prev, v_prev, w_in, w_out):
    b, L, _ = x.shape
    q, k, v = split_cols(x @ w_in, O_SIZES)
    pos = start_pos + jnp.arange(L)
    q = rope(q.reshape(b, L, N_DIL * DIL_HEADS, HEAD_DIM), pos).reshape(b, L, N_DIL, DIL_KV, DIL_REP, HEAD_DIM)
    k = rope(k.reshape(b, L, DIL_KV, HEAD_DIM), pos)
    v = v.reshape(b, L, DIL_KV, HEAD_DIM)
    outs, lses = [], []
    for gi, (window, dil) in enumerate(DIL_PATTERNS):
        if k_prev is None:
            o, lse = dilated_prompt(q[:, :, gi], k, v, dil, window // dil)
        else:
            o, lse = dilated_cached(q[:, :, gi], k, v, k_prev, v_prev, window, dil)
        outs.append(o)
        lses.append(lse)
    wts = jax.nn.softmax(jnp.stack(lses), axis=0)
    merged = jnp.einsum('nbtgr,nbtgrd->btgrd', wts, jnp.stack(outs).astype(F32))
    y = merged.reshape(b, L, DIL_HEADS * HEAD_DIM).astype(x.dtype) @ w_out
    if k_prev is None:
        keep = min(DIL_MAX_WINDOW, L)
        k_out, v_out = k[:, L - keep:], v[:, L - keep:]
    else:
        k_out, v_out = k, v
    return y, k_out, v_out


def sq_relu_mlp(x, w1, w2):
    return jnp.square(jax.nn.relu(x @ w1)) @ w2


def run_trunk(x, start_pos, conv_prev, ssm_prev, swa_k_prev, swa_v_prev, dil_k_prev, dil_v_prev,
              norm_mix, norm_mlp, e_w_in, e_conv_w, e_conv_b, e_dt_bias, e_a_log, e_d_skip,
              e_gate_norm, e_sinks, e_w_out, o_w_in, o_w_out, mlp_w1, mlp_w2, norm_final):
    b = x.shape[0]
    cached = conv_prev is not None
    convs, ssms, swa_ks, swa_vs, dil_ks, dil_vs = [], [], [], [], [], []
    h = x
    for layer in range(DEPTH):
        hn = rmsnorm(h, norm_mix[layer])
        if layer % 2 == 0:
            e = layer // 2
            if cached:
                cp, sp, kp, vp = conv_prev[e], ssm_prev[e], swa_k_prev[e], swa_v_prev[e]
            else:
                cp = jnp.zeros((b, SSD_CONV - 1, CONV_DIM), x.dtype)
                sp = jnp.zeros((b, SSD_HEADS, SSD_HEADDIM, SSD_STATE), F32)
                kp, vp = None, None
            y, c_new, s_new, k_new, v_new = even_mixer(
                hn, start_pos, cp, sp, kp, vp, e_w_in[e], e_conv_w[e], e_conv_b[e], e_dt_bias[e],
                e_a_log[e], e_d_skip[e], e_gate_norm[e], e_sinks[e], e_w_out[e])
            convs.append(c_new)
            ssms.append(s_new)
            swa_ks.append(k_new)
            swa_vs.append(v_new)
        else:
            o = layer // 2
            kp = dil_k_prev[o] if cached else None
            vp = dil_v_prev[o] if cached else None
            y, k_new, v_new = odd_mixer(hn, start_pos, kp, vp, o_w_in[o], o_w_out[o])
            dil_ks.append(k_new)
            dil_vs.append(v_new)
        h = h + y
        h = h + sq_relu_mlp(rmsnorm(h, norm_mlp[layer]), mlp_w1[layer], mlp_w2[layer])
    return (rmsnorm(h, norm_final), jnp.stack(convs), jnp.stack(ssms), jnp.stack(swa_ks),
            jnp.stack(swa_vs), jnp.stack(dil_ks), jnp.stack(dil_vs))


def setup_inputs(seed: int = 0) -> dict:
    key = jax.random.key(seed)
    ks = jax.random.split(key, 24)
    def nrm(k, shape, scale):
        return jax.random.normal(k, shape, F32) * scale
    buf_swa = min(SWA_WINDOW, PAST_LEN)
    buf_dil = min(DIL_MAX_WINDOW, PAST_LEN)
    dt0 = jnp.exp(jax.random.uniform(ks[13], (N_EVEN, SSD_HEADS), F32, math.log(1e-3), math.log(1e-1)))
    dt_bias = dt0 + jnp.log(-jnp.expm1(-dt0))
    a_log = jnp.log(jax.random.uniform(ks[14], (N_EVEN, SSD_HEADS), F32, 1.0, 16.0))
    return {
        'x_prompt': nrm(ks[0], (BATCH, SEQ, D_MODEL), 1.0),
        'x_sample': nrm(ks[1], (DEC_BATCH, DEC_SEQ, D_MODEL), 1.0),
        'state_conv': nrm(ks[2], (N_EVEN, DEC_BATCH, SSD_CONV - 1, CONV_DIM), 1.0),
        'state_ssm': nrm(ks[3], (N_EVEN, DEC_BATCH, SSD_HEADS, SSD_HEADDIM, SSD_STATE), 0.1),
        'cache_swa_k': nrm(ks[4], (N_EVEN, DEC_BATCH, buf_swa, SWA_KV, HEAD_DIM), 1.0),
        'cache_swa_v': nrm(ks[5], (N_EVEN, DEC_BATCH, buf_swa, SWA_KV, HEAD_DIM), 1.0),
        'cache_dil_k': nrm(ks[6], (N_ODD, DEC_BATCH, buf_dil, DIL_KV, HEAD_DIM), 1.0),
        'cache_dil_v': nrm(ks[7], (N_ODD, DEC_BATCH, buf_dil, DIL_KV, HEAD_DIM), 1.0),
        'norm_mix': 1.0 + nrm(ks[8], (DEPTH, D_MODEL), 0.02),
        'norm_mlp': 1.0 + nrm(ks[9], (DEPTH, D_MODEL), 0.02),
        'e_w_in': nrm(ks[10], (N_EVEN, D_MODEL, E_IN), D_MODEL ** -0.5),
        'e_conv_w': nrm(ks[11], (N_EVEN, SSD_CONV, CONV_DIM), SSD_CONV ** -0.5),
        'e_conv_b': nrm(ks[12], (N_EVEN, CONV_DIM), 0.02),
        'e_dt_bias': dt_bias,
        'e_a_log': a_log,
        'e_d_skip': 1.0 + nrm(ks[15], (N_EVEN, SSD_HEADS), 0.02),
        'e_gate_norm': 1.0 + nrm(ks[16], (N_EVEN, SSD_INNER), 0.02),
        'e_sinks': nrm(ks[17], (N_EVEN, SWA_HEADS), 0.5),
        'e_w_out': nrm(ks[18], (N_EVEN, E_OUT, D_MODEL), E_OUT ** -0.5),
        'o_w_in': nrm(ks[19], (N_ODD, D_MODEL, O_IN), D_MODEL ** -0.5),
        'o_w_out': nrm(ks[20], (N_ODD, O_OUT, D_MODEL), O_OUT ** -0.5),
        'mlp_w1': nrm(ks[21], (DEPTH, D_MODEL, MLP_HIDDEN), D_MODEL ** -0.5),
        'mlp_w2': nrm(ks[22], (DEPTH, MLP_HIDDEN, D_MODEL), MLP_HIDDEN ** -0.5),
        'norm_final': 1.0 + nrm(ks[23], (D_MODEL,), 0.02),
    }


def reference(x_prompt, x_sample, state_conv, state_ssm, cache_swa_k, cache_swa_v, cache_dil_k, cache_dil_v,
              norm_mix, norm_mlp, e_w_in, e_conv_w, e_conv_b, e_dt_bias, e_a_log, e_d_skip, e_gate_norm,
              e_sinks, e_w_out, o_w_in, o_w_out, mlp_w1, mlp_w2, norm_final):
    weights = (norm_mix, norm_mlp, e_w_in, e_conv_w, e_conv_b, e_dt_bias, e_a_log, e_d_skip, e_gate_norm,
               e_sinks, e_w_out, o_w_in, o_w_out, mlp_w1, mlp_w2, norm_final)
    y_prompt, p_conv, p_ssm, p_swa_k, p_swa_v, p_dil_k, p_dil_v = run_trunk(
        x_prompt, 0, None, None, None, None, None, None, *weights)
    y_sample, s_conv, s_ssm, s_swa_k, s_swa_v, s_dil_k, s_dil_v = run_trunk(
        x_sample, PAST_LEN, state_conv, state_ssm, cache_swa_k, cache_swa_v, cache_dil_k, cache_dil_v, *weights)
    return (y_prompt, y_sample, p_conv, p_ssm, p_swa_k, p_swa_v, p_dil_k, p_dil_v,
            s_conv, s_ssm, s_swa_k, s_swa_v, s_dil_k, s_dil_v)
```

```python
import functools
import math

import jax
import jax.numpy as jnp
from jax import lax
from jax.experimental import pallas as pl
from jax.experimental.pallas import tpu as pltpu

F32 = jnp.float32
BF16 = jnp.bfloat16

NORM_EPS = 1e-5
HEAD_DIM = 64
ROT_HALF = 8
ROPE_THETA = 500000.0
PAST_LEN = 8192
N_HEADS = 32
N_KV = 8
REP = N_HEADS // N_KV
SSD_GROUPS = 4
SSD_STATE = 128
SSD_CONV = 4
SSD_CHUNK = 128
SWA_WINDOW = 128
DIL_PATTERNS = ((128, 1), (512, 4), (2048, 16))
BAND = 128
ATT_SCALE = HEAD_DIM ** -0.5
_LOG2_HEAD_DIM = 6

LANES = 128
VMEM_LIMIT_BYTES = 56 * 1024 * 1024

_NT = (((1,), (1,)), ((), ()))
_TN = (((0,), (0,)), ((), ()))


def _cparams(n_axes):
    return pltpu.CompilerParams(dimension_semantics=("arbitrary",) * n_axes,
                                vmem_limit_bytes=VMEM_LIMIT_BYTES)


def _silu(x):
    return x * jax.nn.sigmoid(x)


def _tile_lanes(t, width):
    k = width // t.shape[1]
    return t if k == 1 else jnp.concatenate([t] * k, axis=1)


def _rope(x, cos_t, sa_t, sb_t):
    w = x.shape[1]
    return (x * _tile_lanes(cos_t, w)
            + pltpu.roll(x, w - ROT_HALF, 1) * _tile_lanes(sa_t, w)
            + pltpu.roll(x, ROT_HALF, 1) * _tile_lanes(sb_t, w))


def _rope_table_kernel(cos_ref, sa_ref, sb_ref, *, tm, seq, dec_seq):
    i = pl.program_id(0)
    row = i * tm + lax.broadcasted_iota(jnp.int32, (tm, LANES), 0)
    lane = lax.broadcasted_iota(jnp.int32, (tm, LANES), 1)
    pos = jnp.where(row < seq, row, PAST_LEN + ((row - seq) & (dec_seq - 1)))
    c = lane & (HEAD_DIM - 1)
    f = (c & (ROT_HALF - 1)).astype(F32)
    inv_freq = jnp.exp(f * (-math.log(ROPE_THETA) / ROT_HALF))
    ang = pos.astype(F32) * inv_freq
    cs = jnp.cos(ang)
    sn = jnp.sin(ang)
    cos_ref[...] = jnp.where(c < 2 * ROT_HALF, cs, 1.0)
    sa_ref[...] = jnp.where(c < ROT_HALF, -sn, 0.0)
    sb_ref[...] = jnp.where((c >= ROT_HALF) & (c < 2 * ROT_HALF), sn, 0.0)


def _rope_tables(m, seq, dec_seq, tm):
    assert dec_seq & (dec_seq - 1) == 0 and m % tm == 0
    shp = jax.ShapeDtypeStruct((m, LANES), F32)
    spec = pl.BlockSpec((tm, LANES), lambda i: (i, 0))
    return pl.pallas_call(
        functools.partial(_rope_table_kernel, tm=tm, seq=seq, dec_seq=dec_seq),
        out_shape=(shp, shp, shp), grid=(m // tm,), out_specs=(spec, spec, spec),
        compiler_params=_cparams(1), name="rope_tables")()


def _norm_mm_kernel(x_ref, g_ref, w_ref, o_ref, xn_ref, *, act):
    @pl.when(pl.program_id(1) == 0)
    def _():
        x = x_ref[...]
        ms = jnp.mean(x * x, axis=-1, keepdims=True)
        xn_ref[...] = (x * lax.rsqrt(ms + NORM_EPS) * g_ref[...]).astype(BF16)

    acc = jnp.dot(xn_ref[...], w_ref[...], preferred_element_type=F32)
    if act == "relu2":
        acc = jnp.square(jnp.maximum(acc, 0.0))
    o_ref[...] = acc.astype(o_ref.dtype)


def _norm_matmul(x, g, w, *, tm, tn, act=None, out_dtype=F32, name):
    m, k = x.shape
    n = w.shape[1]
    assert m % tm == 0 and n % tn == 0 and w.shape[0] == k
    return pl.pallas_call(
        functools.partial(_norm_mm_kernel, act=act),
        out_shape=jax.ShapeDtypeStruct((m, n), out_dtype),
        grid=(m // tm, n // tn),
        in_specs=[pl.BlockSpec((tm, k), lambda i, j: (i, 0)),
                  pl.BlockSpec((1, k), lambda i, j: (0, 0)),
                  pl.BlockSpec((k, tn), lambda i, j: (0, j))],
        out_specs=pl.BlockSpec((tm, tn), lambda i, j: (i, j)),
        scratch_shapes=[pltpu.VMEM((tm, k), BF16)],
        compiler_params=_cparams(2), name=name)(x, g.reshape(1, k), w)


def _mm_resid_kernel(*refs, n_in):
    x_refs, w_refs = refs[:n_in], refs[n_in:2 * n_in]
    r_ref, o_ref = refs[2 * n_in], refs[2 * n_in + 1]
    acc = r_ref[...]
    for x_ref, w_ref in zip(x_refs, w_refs):
        acc = acc + jnp.dot(x_ref[...], w_ref[...], preferred_element_type=F32)
    o_ref[...] = acc


def _matmul_resid(xs, ws, resid, *, tm, tn, name):
    m, n = resid.shape
    assert m % tm == 0 and n % tn == 0
    in_specs = []
    for x in xs:
        in_specs.append(pl.BlockSpec((tm, x.shape[1]), lambda i, j: (i, 0)))
    for w in ws:
        in_specs.append(pl.BlockSpec((w.shape[0], tn), lambda i, j: (0, j)))
    in_specs.append(pl.BlockSpec((tm, tn), lambda i, j: (i, j)))
    return pl.pallas_call(
        functools.partial(_mm_resid_kernel, n_in=len(xs)),
        out_shape=jax.ShapeDtypeStruct((m, n), F32),
        grid=(m // tm, n // tn),
        in_specs=in_specs,
        out_specs=pl.BlockSpec((tm, tn), lambda i, j: (i, j)),
        compiler_params=_cparams(2), name=name)(*xs, *ws, resid)


def _final_norm_kernel(x_ref, g_ref, o_ref):
    x = x_ref[...]
    ms = jnp.mean(x * x, axis=-1, keepdims=True)
    o_ref[...] = x * lax.rsqrt(ms + NORM_EPS) * g_ref[...]


def _final_norm(x, g, *, tm):
    m, k = x.shape
    return pl.pallas_call(
        _final_norm_kernel, out_shape=jax.ShapeDtypeStruct((m, k), F32), grid=(m // tm,),
        in_specs=[pl.BlockSpec((tm, k), lambda i: (i, 0)), pl.BlockSpec((1, k), lambda i: (0, 0))],
        out_specs=pl.BlockSpec((tm, k), lambda i: (i, 0)),
        compiler_params=_cparams(1), name="final_norm")(x, g.reshape(1, k))


def _causal_conv(raw, tail, w_ref, b_ref):
    n = raw.shape[0]
    full = jnp.concatenate([tail, raw], axis=0)
    acc = b_ref[...]
    for i in range(SSD_CONV):
        s = SSD_CONV - 1 - i
        acc = acc + full[8 - s:8 - s + n] * w_ref[i:i + 1, :]
    return _silu(acc)


def _gate_groupnorm(y, z, gn):
    y = y * _silu(z)
    gw = y.shape[1] // SSD_GROUPS
    outs = []
    for g in range(SSD_GROUPS):
        yg = y[:, g * gw:(g + 1) * gw]
        ms = jnp.mean(yg * yg, axis=-1, keepdims=True)
        outs.append(yg * lax.rsqrt(ms + NORM_EPS))
    return jnp.concatenate(outs, axis=1) * gn


def _ssd_prompt_kernel(z_ref, xs_ref, bc_ref, dt_ref, wx_ref, wbc_ref, bx_ref, bbc_ref, dtb_ref, alog_ref,
                       dskip_ref, gn_ref, y_ref, tailx_ref, tailbc_ref, h_ref, y_s):
    c = pl.program_id(0)
    q = SSD_CHUNK
    gs = SSD_STATE
    hp = HEAD_DIM

    @pl.when(c == 0)
    def _():
        tailx_ref[...] = jnp.zeros_like(tailx_ref)
        tailbc_ref[...] = jnp.zeros_like(tailbc_ref)
        h_ref[...] = jnp.zeros_like(h_ref)

    xs_raw = xs_ref[...]
    bc_raw = bc_ref[...]
    xs_c = _causal_conv(xs_raw, tailx_ref[...], wx_ref, bx_ref)
    bc_c = _causal_conv(bc_raw, tailbc_ref[...], wbc_ref, bbc_ref)
    tailx_ref[...] = xs_raw[q - 8:q]
    tailbc_ref[...] = bc_raw[q - 8:q]

    lane = lax.broadcasted_iota(jnp.int32, (q, LANES), 1)
    row = lax.broadcasted_iota(jnp.int32, (q, LANES), 0)
    dt = jax.nn.softplus(dt_ref[:, :LANES] + dtb_ref[...])
    a = -jnp.exp(alog_ref[...])
    la = jnp.where(lane < N_HEADS, dt * a, 0.0)
    acs = la
    s = 1
    while s < q:
        acs = acs + jnp.where(row >= s, pltpu.roll(acs, s, 0), 0.0)
        s *= 2
    acs_t = acs.T
    acs_last = acs[q - 1:q, :]
    to_end = jnp.exp(acs_last - acs)
    eacs = jnp.exp(acs)
    cdec = jnp.exp(acs_last)
    causal = (lax.broadcasted_iota(jnp.int32, (q, q), 0) >= lax.broadcasted_iota(jnp.int32, (q, q), 1))

    rep = N_HEADS // SSD_GROUPS
    for g in range(SSD_GROUPS):
        bg = bc_c[:, g * gs:(g + 1) * gs].astype(BF16)
        cg = bc_c[:, SSD_GROUPS * gs + g * gs:SSD_GROUPS * gs + (g + 1) * gs].astype(BF16)
        cb = lax.dot_general(cg, bg, _NT, preferred_element_type=F32)
        for rp in range(rep // 2):
            pair = []
            for h in (g * rep + 2 * rp, g * rep + 2 * rp + 1):
                xh = xs_c[:, h * hp:(h + 1) * hp]
                xdt = xh * dt[:, h:h + 1]
                lmat = jnp.exp(jnp.where(causal, acs[:, h:h + 1] - acs_t[h:h + 1, :], -jnp.inf))
                y = jnp.dot((cb * lmat).astype(BF16), xdt.astype(BF16), preferred_element_type=F32)
                hs = h_ref[h]
                yoff = lax.dot_general(cg, hs.astype(BF16), _NT, preferred_element_type=F32)
                y = y + eacs[:, h:h + 1] * yoff
                w16 = (xdt * to_end[:, h:h + 1]).astype(BF16)
                st = lax.dot_general(w16, bg, _TN, preferred_element_type=F32)
                h_ref[h] = hs * cdec[:, h:h + 1] + st
                pair.append(y + dskip_ref[:, h * hp:(h + 1) * hp] * xh)
            h0 = g * rep + 2 * rp
            y_s[:, h0 * hp:(h0 + 2) * hp] = jnp.concatenate(pair, axis=1)

    y_ref[...] = _gate_groupnorm(y_s[...], z_ref[...], gn_ref[...]).astype(y_ref.dtype)


def _ssd_prompt(p0, off, seq, wx, wbc, bx, bbc, dtb, alog, dskip, gn):
    q = SSD_CHUNK
    d_in = wx.shape[1]
    d_bc = wbc.shape[1]
    assert seq % q == 0
    const = lambda shape: pl.BlockSpec(shape, lambda c: (0,) * len(shape))
    return pl.pallas_call(
        _ssd_prompt_kernel,
        out_shape=(jax.ShapeDtypeStruct((seq, d_in), BF16),
                   jax.ShapeDtypeStruct((8, d_in), F32),
                   jax.ShapeDtypeStruct((8, d_bc), F32),
                   jax.ShapeDtypeStruct((N_HEADS, HEAD_DIM, SSD_STATE), F32)),
        grid=(seq // q,),
        in_specs=[pl.BlockSpec((q, d_in), lambda c: (c, off["z"] // d_in)),
                  pl.BlockSpec((q, d_in), lambda c: (c, off["xs"] // d_in)),
                  pl.BlockSpec((q, d_bc), lambda c: (c, off["bc"] // d_bc)),
                  pl.BlockSpec((q, 256), lambda c: (c, off["dt"] // 256)),
                  const((SSD_CONV, d_in)), const((SSD_CONV, d_bc)), const((1, d_in)), const((1, d_bc)),
                  const((1, LANES)), const((1, LANES)), const((1, d_in)), const((1, d_in))],
        out_specs=(pl.BlockSpec((q, d_in), lambda c: (c, 0)),
                   const((8, d_in)), const((8, d_bc)), const((N_HEADS, HEAD_DIM, SSD_STATE))),
        scratch_shapes=[pltpu.VMEM((q, d_in), F32)],
        compiler_params=_cparams(1), name="ssd_prompt")(p0, p0, p0, p0, wx, wbc, bx, bbc, dtb, alog, dskip, gn)


def _split3(x):
    hi = x.astype(BF16)
    r1 = x - hi.astype(F32)
    mid = r1.astype(BF16)
    lo = (r1 - mid.astype(F32)).astype(BF16)
    return hi, mid, lo


def _ssd_sample_kernel(z_ref, xs_ref, bc_ref, dt_ref, ex_ref, ebc_ref, st_ref, wx_ref, wbc_ref, bx_ref, bbc_ref,
                       dtb_ref, alog_ref, dskip_ref, gn_ref, eexp_ref, y_ref, sto_ref, *, nb, t_len):
    rows = nb * t_len
    gs = SSD_STATE
    gw = xs_ref.shape[1] // SSD_GROUPS
    rep = N_HEADS // SSD_GROUPS

    def tcol(width):
        r = lax.broadcasted_iota(jnp.int32, (rows, width), 0)
        return r & (t_len - 1), r >> int(math.log2(t_len))

    def conv(raw, est, w_ref, b_ref):
        t, _ = tcol(raw.shape[1])
        acc = b_ref[...]
        for i in range(SSD_CONV):
            s = SSD_CONV - 1 - i
            if s == 0:
                sh = raw
            else:
                k = (rows - (SSD_CONV - 1) + s) % rows
                sh = jnp.where(t >= s, pltpu.roll(raw, s, 0), est if k == 0 else pltpu.roll(est, k, 0))
            acc = acc + sh * w_ref[i:i + 1, :]
        return _silu(acc)

    xs_c = conv(xs_ref[...], ex_ref[...], wx_ref, bx_ref)
    bc_c = conv(bc_ref[...], ebc_ref[...], wbc_ref, bbc_ref)
    bm = bc_c[:, :SSD_GROUPS * gs]
    cm = bc_c[:, SSD_GROUPS * gs:]

    t1, _ = tcol(LANES)
    lane = lax.broadcasted_iota(jnp.int32, (rows, LANES), 1)
    dt = jax.nn.softplus(dt_ref[:, :LANES] + dtb_ref[...])
    a = -jnp.exp(alog_ref[...])
    la = jnp.where(lane < N_HEADS, dt * a, 0.0)
    acs = la
    for s in range(1, t_len):
        acs = acs + jnp.where(t1 >= s, pltpu.roll(la, s, 0), 0.0)
    alast = jnp.where(t1 == t_len - 1, acs, 0.0)
    for u in range(1, t_len):
        alast = alast + jnp.where(t1 == t_len - 1 - u, pltpu.roll(acs, rows - u, 0), 0.0)
    parts = [dt, jnp.exp(acs), jnp.exp(alast - acs), jnp.exp(alast)]
    for k in range(1, t_len):
        parts.append(jnp.exp(acs - pltpu.roll(acs, k, 0)))
    stacked = jnp.concatenate(parts, axis=0)
    n_parts = len(parts)
    hi, mid, lo = _split3(stacked)
    e = eexp_ref[...]
    full = (jnp.dot(hi, e, preferred_element_type=F32) + jnp.dot(mid, e, preferred_element_type=F32)
            + jnp.dot(lo, e, preferred_element_type=F32))
    dt_f, eacs_f, toend_f, cdec_f = (full[i * rows:(i + 1) * rows] for i in range(4))
    dec_f = [None] + [full[(3 + k) * rows:(4 + k) * rows] for k in range(1, t_len)]

    tw, bw = tcol(xs_c.shape[1])
    xdt = xs_c * dt_f
    y = jnp.zeros_like(xs_c)
    for k in range(t_len):
        bmk = bm if k == 0 else pltpu.roll(bm, k, 0)
        prod = cm * bmk
        cbs = []
        for g in range(SSD_GROUPS):
            sg = jnp.sum(prod[:, g * gs:(g + 1) * gs], axis=-1, keepdims=True)
            cbs.append(jnp.broadcast_to(sg, (rows, gw)))
        cb_f = jnp.concatenate(cbs, axis=1)
        if k == 0:
            y = y + cb_f * xdt
        else:
            y = y + jnp.where(tw >= k, cb_f * dec_f[k] * pltpu.roll(xdt, k, 0), 0.0)

    cm16 = cm.astype(BF16)
    bm16 = bm.astype(BF16)
    w_f = (toend_f * xdt)
    tg, bg_ = tcol(gw)
    tn, _ = tcol(gs)
    ones_blk = jnp.ones((3, gs), BF16)
    yoffs = []
    for g in range(SSD_GROUPS):
        cg = cm16[:, g * gs:(g + 1) * gs]
        rhs_top = jnp.concatenate([bm16[:, g * gs:(g + 1) * gs], jnp.zeros((rows, gs), BF16)], axis=1)
        rhs_mid = jnp.concatenate([jnp.zeros((3, gs), BF16), ones_blk], axis=1)
        rhs = jnp.concatenate([rhs_top, rhs_mid, jnp.zeros((13, 2 * gs), BF16)], axis=0)
        wg = w_f[:, g * gw:(g + 1) * gw]
        cdg = cdec_f[:, g * gw:(g + 1) * gw]
        yg = jnp.zeros((rows, gw), F32)
        for b in range(nb):
            hb = st_ref[b, g * rep:(g + 1) * rep].reshape(gw, gs)
            yb = lax.dot_general(cg, hb.astype(BF16), _NT, preferred_element_type=F32)
            yg = jnp.where(bg_ == b, yb, yg)
            wb = jnp.where(bg_ == b, wg, 0.0).astype(BF16)
            d_hi, d_mid, d_lo = _split3(cdg[b * t_len:b * t_len + 1])
            lhs = jnp.concatenate([wb, d_hi, d_mid, d_lo, jnp.zeros((13, gw), BF16)], axis=0)
            sd = lax.dot_general(lhs, rhs, _TN, preferred_element_type=F32)
            sto_ref[b, g * rep:(g + 1) * rep] = (hb * sd[:, gs:] + sd[:, :gs]).reshape(rep, HEAD_DIM, gs)
        yoffs.append(yg)
    y = y + eacs_f * jnp.concatenate(yoffs, axis=1) + dskip_ref[...] * xs_c
    y_ref[...] = _gate_groupnorm(y, z_ref[...], gn_ref[...]).astype(y_ref.dtype)


def _ssd_sample(p0, off, seq, n_b, t_len, epad, state, wx, wbc, bx, bbc, dtb, alog, dskip, gn, eexp, *, nb):
    rows = nb * t_len
    d_in = wx.shape[1]
    d_bc = wbc.shape[1]
    assert n_b % nb == 0 and seq % rows == 0 and t_len & (t_len - 1) == 0
    r0 = seq // rows
    const = lambda shape: pl.BlockSpec(shape, lambda i: (0,) * len(shape))
    st_spec = pl.BlockSpec((nb, N_HEADS, HEAD_DIM, SSD_STATE), lambda i: (i, 0, 0, 0))
    return pl.pallas_call(
        functools.partial(_ssd_sample_kernel, nb=nb, t_len=t_len),
        out_shape=(jax.ShapeDtypeStruct((n_b * t_len, d_in), BF16),
                   jax.ShapeDtypeStruct(state.shape, F32)),
        grid=(n_b // nb,),
        in_specs=[pl.BlockSpec((rows, d_in), lambda i: (r0 + i, off["z"] // d_in)),
                  pl.BlockSpec((rows, d_in), lambda i: (r0 + i, off["xs"] // d_in)),
                  pl.BlockSpec((rows, d_bc), lambda i: (r0 + i, off["bc"] // d_bc)),
                  pl.BlockSpec((rows, 256), lambda i: (r0 + i, off["dt"] // 256)),
                  pl.BlockSpec((rows, d_in), lambda i: (i, 0)),
                  pl.BlockSpec((rows, d_bc), lambda i: (i, d_in // d_bc)),
                  st_spec,
                  const((SSD_CONV, d_in)), const((SSD_CONV, d_bc)), const((1, d_in)), const((1, d_bc)),
                  const((1, LANES)), const((1, LANES)), const((1, d_in)), const((1, d_in)),
                  const((LANES, d_in))],
        out_specs=(pl.BlockSpec((rows, d_in), lambda i: (i, 0)), st_spec),
        compiler_params=_cparams(1), name="ssd_sample")(
            p0, p0, p0, p0, epad, epad, state, wx, wbc, bx, bbc, dtb, alog, dskip, gn, eexp)


def _band_kernel(*refs, dil, max_dist, use_sink, want_lse, want_krot, nj):
    it = iter(refs)
    q_ref, kc_ref, kp_ref, vc_ref, vp_ref = (next(it) for _ in range(5))
    tc = [next(it) for _ in range(3)]
    tp = [next(it) for _ in range(3)]
    sink_ref = next(it) if use_sink else None
    o_ref = next(it)
    lse_ref = next(it) if want_lse else None
    krot_ref = next(it) if want_krot else None
    lse_s = next(it) if want_lse else None

    c = pl.program_id(0)
    j = pl.program_id(1)
    n_res = 1 if dil == 1 else dil

    def picks(rho):
        if dil == 1:
            return None
        if dil == 16:
            return [rho]
        return [rho + dil * u for u in range(16 // dil)]

    def load(ref, rho):
        p = picks(rho)
        if p is None:
            return ref[...]
        return jnp.concatenate([ref[:, s, :] for s in p], axis=0) if len(p) > 1 else ref[:, p[0], :]

    def store(ref, rho, val):
        p = picks(rho)
        if p is None:
            ref[...] = val
            return
        n = val.shape[0] // len(p)
        for u, s in enumerate(p):
            ref[:, s, :] = val[u * n:(u + 1) * n]

    def pos_of(i):
        if dil in (1, 16):
            return i
        n = BAND // (16 // dil)
        return (16 // dil) * (i % n) + i // n

    qi = lax.broadcasted_iota(jnp.int32, (REP * BAND, 2 * BAND), 0) & (BAND - 1)
    kj = lax.broadcasted_iota(jnp.int32, (REP * BAND, 2 * BAND), 1)
    kpos = pos_of(kj & (BAND - 1)) + (kj >> 7) * BAND - BAND
    dist = pos_of(qi) - kpos
    mask = (dist >= 0) & (dist <= max_dist) & ((kj >= BAND) | (c > 0))

    if want_lse:
        @pl.when(j == 0)
        def _():
            lse_s[...] = jnp.zeros_like(lse_s)
        lane = lax.broadcasted_iota(jnp.int32, (BAND, LANES), 1)

    for rho in range(n_res):
        tcs = [load(t, rho) for t in tc]
        tps = [load(t, rho) for t in tp]
        q = (_rope(load(q_ref, rho), *tcs) * ATT_SCALE).astype(BF16)
        kc = _rope(load(kc_ref, rho), *tcs)
        kp = _rope(load(kp_ref, rho), *tps)
        vc = load(vc_ref, rho)
        vp = load(vp_ref, rho)
        if want_krot:
            store(krot_ref, rho, kc)
        pieces = []
        lse_acc = None
        for gl in range(2):
            sl = slice(gl * HEAD_DIM, (gl + 1) * HEAD_DIM)
            kcat = jnp.concatenate([kp[:, sl], kc[:, sl]], axis=0).astype(BF16)
            vcat = jnp.concatenate([vp[:, sl], vc[:, sl]], axis=0).astype(BF16)
            qg = jnp.concatenate(
                [q[:, (gl * REP + r) * HEAD_DIM:(gl * REP + r + 1) * HEAD_DIM] for r in range(REP)], axis=0)
            s = lax.dot_general(qg, kcat, _NT, preferred_element_type=F32)
            s = jnp.where(mask, s, -jnp.inf)
            m = jnp.max(s, axis=-1, keepdims=True)
            if use_sink:
                sk = jnp.concatenate(
                    [jnp.full((BAND, 1), sink_ref[j * 2 * REP + gl * REP + r], F32) for r in range(REP)], axis=0)
                m = jnp.maximum(m, sk)
            e = jnp.exp(s - m)
            den = jnp.sum(e, axis=-1, keepdims=True)
            if use_sink:
                den = den + jnp.exp(sk - m)
            o = jnp.dot(e.astype(BF16), vcat, preferred_element_type=F32) / den
            for r in range(REP):
                pieces.append(o[r * BAND:(r + 1) * BAND])
            if want_lse:
                lse = m + jnp.log(den)
                for r in range(REP):
                    col = jnp.broadcast_to(lse[r * BAND:(r + 1) * BAND], (BAND, LANES))
                    hit = jnp.where(lane == j * 2 * REP + gl * REP + r, col, 0.0)
                    lse_acc = hit if lse_acc is None else lse_acc + hit
        store(o_ref, rho, jnp.concatenate(pieces, axis=1).astype(o_ref.dtype))
        if want_lse:
            store(lse_s, rho, load(lse_s, rho) + lse_acc)

    if want_lse:
        @pl.when(j == nj - 1)
        def _():
            lse_ref[...] = lse_s[...]


def _band_attention(p, seq, q_off, k_off, v_off, tables, sinks, *, dil, max_dist, want_lse, want_krot, name):
    m_all, w = p.shape
    nj = N_KV // 2
    qw, kw = 2 * REP * HEAD_DIM, 2 * HEAD_DIM
    use_sink = sinks is not None
    d_att = N_HEADS * HEAD_DIM
    if dil == 1:
        nblk = seq // BAND
        mk = lambda width, cb, prev: pl.BlockSpec(
            (BAND, width), (lambda c, j: (jnp.maximum(c - 1, 0), cb + j)) if prev else (lambda c, j: (c, cb + j)))
        mt = lambda prev: pl.BlockSpec(
            (BAND, LANES), (lambda c, j: (jnp.maximum(c - 1, 0), 0)) if prev else (lambda c, j: (c, 0)))
        pv = p
        tv = tables
        o_shape, o_spec = (seq, d_att), pl.BlockSpec((BAND, qw), lambda c, j: (c, j))
        l_shape, l_spec = (seq, LANES), pl.BlockSpec((BAND, LANES), lambda c, j: (c, 0))
        k_shape, k_spec = (seq, N_KV * HEAD_DIM), pl.BlockSpec((BAND, kw), lambda c, j: (c, j))
        lse_scr = (BAND, LANES)
    else:
        assert m_all % 16 == 0 and seq % (16 * BAND) == 0
        a = BAND * dil // 16
        nblk = seq // 16 // a
        mk = lambda width, cb, prev: pl.BlockSpec(
            (a, 16, width),
            (lambda c, j: (jnp.maximum(c - 1, 0), 0, cb + j)) if prev else (lambda c, j: (c, 0, cb + j)))
        mt = lambda prev: pl.BlockSpec(
            (a, 16, LANES), (lambda c, j: (jnp.maximum(c - 1, 0), 0, 0)) if prev else (lambda c, j: (c, 0, 0)))
        pv = p.reshape(m_all // 16, 16, w)
        tv = [t.reshape(m_all // 16, 16, LANES) for t in tables]
        o_shape, o_spec = (seq // 16, 16, d_att), pl.BlockSpec((a, 16, qw), lambda c, j: (c, 0, j))
        l_shape, l_spec = (seq // 16, 16, LANES), pl.BlockSpec((a, 16, LANES), lambda c, j: (c, 0, 0))
        k_shape, k_spec = (seq // 16, 16, N_KV * HEAD_DIM), pl.BlockSpec((a, 16, kw), lambda c, j: (c, 0, j))
        lse_scr = (a, 16, LANES)
    in_specs = [mk(qw, q_off // qw, False), mk(kw, k_off // kw, False), mk(kw, k_off // kw, True),
                mk(kw, v_off // kw, False), mk(kw, v_off // kw, True)]
    in_specs += [mt(False)] * 3 + [mt(True)] * 3
    args = [pv] * 5 + list(tv) + list(tv)
    if use_sink:
        in_specs.append(pl.BlockSpec(memory_space=pltpu.SMEM))
        args.append(sinks)
    out_shape, out_specs, scratch = [jax.ShapeDtypeStruct(o_shape, BF16)], [o_spec], []
    if want_lse:
        out_shape.append(jax.ShapeDtypeStruct(l_shape, F32))
        out_specs.append(l_spec)
        scratch.append(pltpu.VMEM(lse_scr, F32))
    if want_krot:
        out_shape.append(jax.ShapeDtypeStruct(k_shape, F32))
        out_specs.append(k_spec)
    outs = pl.pallas_call(
        functools.partial(_band_kernel, dil=dil, max_dist=max_dist, use_sink=use_sink, want_lse=want_lse,
                          want_krot=want_krot, nj=nj),
        out_shape=tuple(out_shape), grid=(nblk, nj), in_specs=in_specs, out_specs=tuple(out_specs),
        scratch_shapes=scratch, compiler_params=_cparams(2), name=name)(*args)
    outs = list(outs)
    res = [outs.pop(0).reshape(seq, d_att)]
    if want_lse:
        res.append(outs.pop(0).reshape(seq, LANES))
    if want_krot:
        res.append(outs.pop(0).reshape(seq, N_KV * HEAD_DIM))
    return res


def _merge_kernel(o1_ref, o2_ref, o3_ref, l1_ref, l2_ref, l3_ref, e_ref, out_ref):
    ls = [l1_ref[...], l2_ref[...], l3_ref[...]]
    mx = jnp.maximum(jnp.maximum(ls[0], ls[1]), ls[2])
    es = [jnp.exp(l - mx) for l in ls]
    inv = 1.0 / (es[0] + es[1] + es[2])
    e = e_ref[...]
    acc = None
    for en, o_ref in zip(es, (o1_ref, o2_ref, o3_ref)):
        w = en * inv
        hi = w.astype(BF16)
        lo = (w - hi.astype(F32)).astype(BF16)
        wf = jnp.dot(hi, e, preferred_element_type=F32) + jnp.dot(lo, e, preferred_element_type=F32)
        term = wf * o_ref[...].astype(F32)
        acc = term if acc is None else acc + term
    out_ref[...] = acc.astype(out_ref.dtype)


def _merge(os_, ls_, eexp, *, tm):
    seq, d = os_[0].shape
    o_spec = pl.BlockSpec((tm, d), lambda i: (i, 0))
    l_spec = pl.BlockSpec((tm, LANES), lambda i: (i, 0))
    return pl.pallas_call(
        _merge_kernel, out_shape=jax.ShapeDtypeStruct((seq, d), BF16), grid=(seq // tm,),
        in_specs=[o_spec] * 3 + [l_spec] * 3 + [pl.BlockSpec((LANES, d), lambda i: (0, 0))],
        out_specs=o_spec, compiler_params=_cparams(1), name="dil_merge")(*os_, *ls_, eexp)


def _expand_q(qb, t_len):
    gd = N_KV * HEAD_DIM
    sub = lax.broadcasted_iota(jnp.int32, (N_KV, gd), 0)
    lg = lax.broadcasted_iota(jnp.int32, (N_KV, gd), 1) >> _LOG2_HEAD_DIM
    diag = sub == lg
    tiles = []
    for r in range(REP):
        vr = jnp.concatenate(
            [qb[:, (g * REP + r) * HEAD_DIM:(g * REP + r + 1) * HEAD_DIM] for g in range(N_KV)], axis=1)
        for t in range(t_len):
            tiles.append(jnp.where(diag, jnp.broadcast_to(vr[t:t + 1], (N_KV, gd)), 0.0))
    return jnp.concatenate(tiles, axis=0).astype(BF16)


def _collapse_o(r_full, t_len):
    gd = N_KV * HEAD_DIM
    n = r_full.shape[0]
    sub = lax.broadcasted_iota(jnp.int32, (n, gd), 0) & (N_KV - 1)
    lg = lax.broadcasted_iota(jnp.int32, (n, gd), 1) >> _LOG2_HEAD_DIM
    masked = jnp.where(sub == lg, r_full, 0.0)
    red = jnp.sum(masked.reshape(n // N_KV, N_KV, gd), axis=1)
    pieces = []
    for g in range(N_KV):
        for r in range(REP):
            pieces.append(red[r * t_len:(r + 1) * t_len, g * HEAD_DIM:(g + 1) * HEAD_DIM])
    return jnp.concatenate(pieces, axis=1)


def _row_t(shape, t_len):
    r = lax.broadcasted_iota(jnp.int32, shape, 0)
    return (r >> 3) & (t_len - 1)


def _swa_cached_kernel(q_ref, kn_ref, vn_ref, cos_ref, sa_ref, sb_ref, ck_ref, cv_ref, sink_ref,
                       o_ref, krot_ref, *, nb, t_len):
    tabs = (cos_ref[...], sa_ref[...], sb_ref[...])
    q = _rope(q_ref[...], *tabs) * ATT_SCALE
    kn = _rope(kn_ref[...], *tabs)
    krot_ref[...] = kn
    kn16 = kn.astype(BF16)
    vn16 = vn_ref[...].astype(BF16)
    lb = ck_ref.shape[1]
    gd = N_KV * HEAD_DIM
    nk = lb + 8
    nrow = REP * t_len * N_KV
    tq = _row_t((nrow, nk), t_len)
    col = lax.broadcasted_iota(jnp.int32, (nrow, nk), 1)
    mask = ((col < lb) & (col >= tq + 1)) | ((col >= lb) & (col - lb <= tq))
    sk = sink_ref[...]
    pad = jnp.zeros((8 - t_len, gd), BF16)
    outs = []
    for b in range(nb):
        rs = slice(b * t_len, (b + 1) * t_len)
        qx = _expand_q(q[rs], t_len)
        kcat = jnp.concatenate([ck_ref[b].reshape(lb, gd).astype(BF16), kn16[rs], pad], axis=0)
        vcat = jnp.concatenate([cv_ref[b].reshape(lb, gd).astype(BF16), vn16[rs], pad], axis=0)
        s = lax.dot_general(qx, kcat, _NT, preferred_element_type=F32)
        s = jnp.where(mask, s, -jnp.inf)
        m = jnp.maximum(jnp.max(s, axis=-1, keepdims=True), sk)
        e = jnp.exp(s - m)
        den = jnp.sum(e, axis=-1, keepdims=True) + jnp.exp(sk - m)
        r_full = jnp.dot(e.astype(BF16), vcat, preferred_element_type=F32) / den
        outs.append(_collapse_o(r_full, t_len))
    o_ref[...] = jnp.concatenate(outs, axis=0).astype(o_ref.dtype)


def _swa_cached(p0, off, seq, n_b, t_len, tables, cache_k, cache_v, sink_col, *, nb):
    rows = nb * t_len
    d_att = N_HEADS * HEAD_DIM
    d_kv = N_KV * HEAD_DIM
    lb = cache_k.shape[1]
    r0 = seq // rows
    nrow = REP * t_len * N_KV
    c_spec = pl.BlockSpec((nb, lb, N_KV, HEAD_DIM), lambda i: (i, 0, 0, 0))
    t_spec = pl.BlockSpec((rows, LANES), lambda i: (r0 + i, 0))
    return pl.pallas_call(
        functools.partial(_swa_cached_kernel, nb=nb, t_len=t_len),
        out_shape=(jax.ShapeDtypeStruct((n_b * t_len, d_att), BF16),
                   jax.ShapeDtypeStruct((n_b * t_len, d_kv), F32)),
        grid=(n_b // nb,),
        in_specs=[pl.BlockSpec((rows, d_att), lambda i: (r0 + i, off["q"] // d_att)),
                  pl.BlockSpec((rows, d_kv), lambda i: (r0 + i, off["k"] // d_kv)),
                  pl.BlockSpec((rows, d_kv), lambda i: (r0 + i, off["v"] // d_kv)),
                  t_spec, t_spec, t_spec, c_spec, c_spec,
                  pl.BlockSpec((nrow, 1), lambda i: (0, 0))],
        out_specs=(pl.BlockSpec((rows, d_att), lambda i: (i, 0)),
                   pl.BlockSpec((rows, d_kv), lambda i: (i, 0))),
        compiler_params=_cparams(1), name="swa_cached")(p0, p0, p0, *tables, cache_k, cache_v, sink_col)


def _dil_cached_kernel(q1_ref, q2_ref, q3_ref, kn_ref, vn_ref, cos_ref, sa_ref, sb_ref,
                       k1_ref, k4_ref, k16_ref, v1_ref, v4_ref, v16_ref, o_ref, krot_ref, *, nb, t_len):
    tabs = (cos_ref[...], sa_ref[...], sb_ref[...])
    qs = [_rope(r[...], *tabs) * ATT_SCALE for r in (q1_ref, q2_ref, q3_ref)]
    kn = _rope(kn_ref[...], *tabs)
    krot_ref[...] = kn
    kn16 = kn.astype(BF16)
    vn16 = vn_ref[...].astype(BF16)
    lb = k1_ref.shape[1]
    gd = N_KV * HEAD_DIM
    nrow = REP * t_len * N_KV
    nk1 = lb + 8
    tq1 = _row_t((nrow, nk1), t_len)
    col1 = lax.broadcasted_iota(jnp.int32, (nrow, nk1), 1)
    mask1 = ((col1 < lb) & (col1 >= tq1)) | ((col1 >= lb) & (col1 - lb <= tq1))
    blk = lb + 8
    nks = t_len * blk
    tqs = _row_t((nrow, nks), t_len)
    cols = lax.broadcasted_iota(jnp.int32, (nrow, nks), 1)
    tks = sum((cols >= u * blk).astype(jnp.int32) for u in range(1, t_len))
    masks = (tks == tqs) & ((cols - tks * blk) <= lb)
    pad1 = jnp.zeros((8 - t_len, gd), BF16)
    pad7 = jnp.zeros((7, gd), BF16)

    def attend(qx, kcat, vcat, mask):
        s = lax.dot_general(qx, kcat, _NT, preferred_element_type=F32)
        s = jnp.where(mask, s, -jnp.inf)
        m = jnp.max(s, axis=-1, keepdims=True)
        e = jnp.exp(s - m)
        den = jnp.sum(e, axis=-1, keepdims=True)
        return jnp.dot(e.astype(BF16), vcat, preferred_element_type=F32) / den, m + jnp.log(den)

    outs = []
    for b in range(nb):
        rs = slice(b * t_len, (b + 1) * t_len)
        res = []
        kcat = jnp.concatenate([k1_ref[b].reshape(lb, gd).astype(BF16), kn16[rs], pad1], axis=0)
        vcat = jnp.concatenate([v1_ref[b].reshape(lb, gd).astype(BF16), vn16[rs], pad1], axis=0)
        res.append(attend(_expand_q(qs[0][rs], t_len), kcat, vcat, mask1))
        for kq, (k_ref, v_ref) in enumerate(((k4_ref, v4_ref), (k16_ref, v16_ref))):
            kparts, vparts = [], []
            for t in range(t_len):
                row = b * t_len + t
                kparts += [k_ref[b, :, t].reshape(lb, gd).astype(BF16), kn16[row:row + 1], pad7]
                vparts += [v_ref[b, :, t].reshape(lb, gd).astype(BF16), vn16[row:row + 1], pad7]
            res.append(attend(_expand_q(qs[kq + 1][rs], t_len), jnp.concatenate(kparts, axis=0),
                              jnp.concatenate(vparts, axis=0), masks))
        mx = jnp.maximum(jnp.maximum(res[0][1], res[1][1]), res[2][1])
        ws = [jnp.exp(l - mx) for _, l in res]
        inv = 1.0 / (ws[0] + ws[1] + ws[2])
        merged = (ws[0] * inv) * res[0][0] + (ws[1] * inv) * res[1][0] + (ws[2] * inv) * res[2][0]
        outs.append(_collapse_o(merged, t_len))
    o_ref[...] = jnp.concatenate(outs, axis=0).astype(o_ref.dtype)


def _dil_cached(p1, seq, n_b, t_len, tables, cache_k, cache_v, *, nb):
    rows = nb * t_len
    d_att = N_HEADS * HEAD_DIM
    d_kv = N_KV * HEAD_DIM
    lbuf = cache_k.shape[1]
    assert rows % 8 == 0 and seq % rows == 0
    r0 = seq // rows
    w1, d1 = DIL_PATTERNS[0]
    (w4, d4), (w16, d16) = DIL_PATTERNS[1], DIL_PATTERNS[2]
    lb = w1 // d1
    assert w4 // d4 == lb and w16 // d16 == lb and d4 >= t_len and lbuf >= w16 and lbuf % d16 == 0
    views, specs = [], []
    for c in (cache_k, cache_v):
        n1 = lbuf // lb
        views.append(c)
        specs.append(pl.BlockSpec((nb, lb, N_KV, HEAD_DIM), lambda i: (i, n1 - 1, 0, 0)))
        for dl in (d4, d16):
            nblk = lbuf // dl // lb
            views.append(c.reshape(n_b, lbuf // dl, dl, N_KV, HEAD_DIM))
            specs.append(pl.BlockSpec((nb, lb, t_len, N_KV, HEAD_DIM),
                                      functools.partial(lambda i, nblk: (i, nblk - 1, 0, 0, 0), nblk=nblk)))
    t_spec = pl.BlockSpec((rows, LANES), lambda i: (r0 + i, 0))
    nq = d_att * len(DIL_PATTERNS)
    return pl.pallas_call(
        functools.partial(_dil_cached_kernel, nb=nb, t_len=t_len),
        out_shape=(jax.ShapeDtypeStruct((n_b * t_len, d_att), F32),
                   jax.ShapeDtypeStruct((n_b * t_len, d_kv), F32)),
        grid=(n_b // nb,),
        in_specs=[pl.BlockSpec((rows, d_att), lambda i: (r0 + i, 0)),
                  pl.BlockSpec((rows, d_att), lambda i: (r0 + i, 1)),
                  pl.BlockSpec((rows, d_att), lambda i: (r0 + i, 2)),
                  pl.BlockSpec((rows, d_kv), lambda i: (r0 + i, nq // d_kv)),
                  pl.BlockSpec((rows, d_kv), lambda i: (r0 + i, nq // d_kv + 1)),
                  t_spec, t_spec, t_spec] + specs,
        out_specs=(pl.BlockSpec((rows, d_att), lambda i: (i, 0)),
                   pl.BlockSpec((rows, d_kv), lambda i: (i, 0))),
        compiler_params=_cparams(1), name="dil_cached")(p1, p1, p1, p1, p1, *tables, *views)


def _row_tile(m, pref):
    for t in (pref, 512, 256, 128, 64, 32, 16, 8):
        if t <= pref and m % t == 0:
            return t
    raise ValueError(m)


def kernel(x_prompt, x_sample, state_conv, state_ssm, cache_swa_k, cache_swa_v, cache_dil_k, cache_dil_v, norm_mix, norm_mlp, e_w_in, e_conv_w, e_conv_b, e_dt_bias, e_a_log, e_d_skip, e_gate_norm, e_sinks, e_w_out, o_w_in, o_w_out, mlp_w1, mlp_w2, norm_final):
    nbp, seq, d = x_prompt.shape
    n_b, t_len, _ = x_sample.shape
    assert nbp == 1 and d == N_HEADS * HEAD_DIM and norm_mix.shape[0] == 2
    ms = n_b * t_len
    m = seq + ms
    tm = _row_tile(m, 512)
    d_kv = N_KV * HEAD_DIM
    d_bc = 2 * SSD_GROUPS * SSD_STATE
    conv_dim = d + d_bc

    h = jnp.concatenate([x_prompt.reshape(seq, d), x_sample.reshape(ms, d)], axis=0)
    tables = _rope_tables(m, seq, t_len, tm)
    eexp = (lax.broadcasted_iota(jnp.int32, (LANES, d), 1) // HEAD_DIM
            == lax.broadcasted_iota(jnp.int32, (LANES, d), 0)).astype(BF16)

    wi = e_w_in[0]
    c_z, c_xbc, c_dt, c_q = 0, d, d + conv_dim, d + conv_dim + N_HEADS
    c_k, c_v = c_q + d, c_q + d + d_kv
    w0 = jnp.concatenate(
        [wi[:, c_z:c_z + 2 * d], wi[:, c_q:c_q + d], wi[:, c_xbc + d:c_xbc + conv_dim], wi[:, c_k:c_v + d_kv],
         wi[:, c_dt:c_dt + N_HEADS], jnp.zeros((d, 256 - N_HEADS), F32)], axis=1).astype(BF16)
    off0 = {"z": 0, "xs": d, "q": 2 * d, "bc": 3 * d, "k": 3 * d + d_bc, "v": 3 * d + d_bc + d_kv,
            "dt": 3 * d + d_bc + 2 * d_kv}
    p0 = _norm_matmul(h, norm_mix[0], w0, tm=tm, tn=768, name="l0_in_proj")

    cw, cb = e_conv_w[0], e_conv_b[0]
    wx, wbc = cw[:, :d], cw[:, d:]
    bx, bbc = cb[:d].reshape(1, d), cb[d:].reshape(1, d_bc)
    pad_h = lambda v: jnp.pad(v.reshape(1, N_HEADS), ((0, 0), (0, LANES - N_HEADS)))
    dtb, alog = pad_h(e_dt_bias[0]), pad_h(e_a_log[0])
    dskip = jnp.repeat(e_d_skip[0], HEAD_DIM).reshape(1, d)
    gn = e_gate_norm[0].reshape(1, d)

    ssd_p, tail_x, tail_bc, p_ssm = _ssd_prompt(p0, off0, seq, wx, wbc, bx, bbc, dtb, alog, dskip, gn)
    epad = jnp.pad(state_conv[0], ((0, 0), (0, 1), (0, 0))).reshape(ms, conv_dim)
    ssd_s, s_ssm = _ssd_sample(p0, off0, seq, n_b, t_len, epad, state_ssm[0], wx, wbc, bx, bbc, dtb, alog, dskip,
                               gn, eexp, nb=4)

    att_p, krot0_p = _band_attention(p0, seq, off0["q"], off0["k"], off0["v"], tables, e_sinks[0],
                                     dil=1, max_dist=SWA_WINDOW - 1, want_lse=False, want_krot=True, name="swa_prompt")
    sink_col = jnp.broadcast_to(e_sinks[0].reshape(N_KV, REP).T[:, None, :], (REP, t_len, N_KV)).reshape(-1, 1)
    att_s, krot0_s = _swa_cached(p0, off0, seq, n_b, t_len, tables, cache_swa_k[0], cache_swa_v[0], sink_col, nb=4)

    ssd_all = jnp.concatenate([ssd_p, ssd_s], axis=0)
    att_all = jnp.concatenate([att_p, att_s], axis=0)
    wo = e_w_out[0].astype(BF16)
    h = _matmul_resid([ssd_all, att_all], [wo[:d], wo[d:]], h, tm=tm, tn=1024, name="l0_out_proj")
    u = _norm_matmul(h, norm_mlp[0], mlp_w1[0].astype(BF16), tm=tm, tn=1024, act="relu2", out_dtype=BF16,
                     name="l0_mlp_up")
    h = _matmul_resid([u], [mlp_w2[0].astype(BF16)], h, tm=tm, tn=512, name="l0_mlp_down")

    n_pat = len(DIL_PATTERNS)
    p1 = _norm_matmul(h, norm_mix[1], o_w_in[0].astype(BF16), tm=tm, tn=1024, name="l1_in_proj")
    k_off, v_off = n_pat * d, n_pat * d + d_kv
    os_, ls_ = [], []
    krot1_p = None
    for gi, (window, dil) in enumerate(DIL_PATTERNS):
        res = _band_attention(p1, seq, gi * d, k_off, v_off, tables, None, dil=dil, max_dist=window // dil,
                              want_lse=True, want_krot=(gi == 0), name=f"dil_prompt_{dil}")
        os_.append(res[0])
        ls_.append(res[1])
        if gi == 0:
            krot1_p = res[2]
    merged_p = _merge(os_, ls_, eexp, tm=_row_tile(seq, 256))
    dil_s, krot1_s = _dil_cached(p1, seq, n_b, t_len, tables, cache_dil_k[0], cache_dil_v[0], nb=2)
    merged = jnp.concatenate([merged_p, dil_s.astype(BF16)], axis=0)
    h = _matmul_resid([merged], [o_w_out[0].astype(BF16)], h, tm=tm, tn=1024, name="l1_out_proj")
    u = _norm_matmul(h, norm_mlp[1], mlp_w1[1].astype(BF16), tm=tm, tn=1024, act="relu2", out_dtype=BF16,
                     name="l1_mlp_up")
    h = _matmul_resid([u], [mlp_w2[1].astype(BF16)], h, tm=tm, tn=512, name="l1_mlp_down")
    y = _final_norm(h, norm_final, tm=tm)

    keep_swa = min(SWA_WINDOW, seq)
    keep_dil = min(max(w for w, _ in DIL_PATTERNS), seq)
    kv4 = lambda a: a.reshape(a.shape[0], N_KV, HEAD_DIM)
    y_prompt = y[:seq].reshape(1, seq, d)
    y_sample = y[seq:].reshape(n_b, t_len, d)
    p_conv = jnp.concatenate([tail_x[8 - (SSD_CONV - 1):], tail_bc[8 - (SSD_CONV - 1):]], axis=1)[None, None]
    p_swa_k = kv4(krot0_p[seq - keep_swa:])[None, None]
    p_swa_v = kv4(p0[seq - keep_swa:seq, off0["v"]:off0["v"] + d_kv])[None, None]
    p_dil_k = kv4(krot1_p[seq - keep_dil:])[None, None]
    p_dil_v = kv4(p1[seq - keep_dil:seq, v_off:v_off + d_kv])[None, None]
    xbc_s = jnp.concatenate([p0[seq:, off0["xs"]:off0["xs"] + d], p0[seq:, off0["bc"]:off0["bc"] + d_bc]], axis=1)
    s_conv = xbc_s.reshape(n_b, t_len, conv_dim)[:, t_len - (SSD_CONV - 1):][None]
    bt = lambda a: a.reshape(n_b, t_len, N_KV, HEAD_DIM)[None]
    s_swa_k = bt(krot0_s)
    s_swa_v = bt(p0[seq:, off0["v"]:off0["v"] + d_kv])
    s_dil_k = bt(krot1_s)
    s_dil_v = bt(p1[seq:, v_off:v_off + d_kv])
    return (y_prompt, y_sample, p_conv, p_ssm[None, None], p_swa_k, p_swa_v, p_dil_k, p_dil_v,
            s_conv, s_ssm[None], s_swa_k, s_swa_v, s_dil_k, s_dil_v)
```

```python
import functools
import math

import jax
import jax.numpy as jnp
from jax import lax
from jax.experimental import pallas as pl
from jax.experimental.pallas import tpu as pltpu

F32 = jnp.float32
BF16 = jnp.bfloat16

NORM_EPS = 1e-5
HEAD_DIM = 64
ROT_HALF = 8
ROPE_THETA = 500000.0
PAST_LEN = 8192
N_HEADS = 32
N_KV = 8
REP = N_HEADS // N_KV
SSD_GROUPS = 4
SSD_STATE = 128
SSD_CONV = 4
SSD_CHUNK = 128
SWA_WINDOW = 128
DIL_PATTERNS = ((128, 1), (512, 4), (2048, 16))
BAND = 128
ATT_SCALE = HEAD_DIM ** -0.5
LOG2E = math.log2(math.e)
LN2 = math.log(2.0)
NEG = -1e30
_LOG2_HEAD_DIM = 6

LANES = 128
BF16_ROWS = 16
PERM_CHUNK = 256
VMEM_LIMIT_BYTES = 56 * 1024 * 1024

_NT = (((1,), (1,)), ((), ()))
_TN = (((0,), (0,)), ((), ()))


def _cparams(n_axes):
    return pltpu.CompilerParams(dimension_semantics=("arbitrary",) * n_axes,
                                vmem_limit_bytes=VMEM_LIMIT_BYTES)


def _silu(x):
    return x * jax.nn.sigmoid(x)


def _tile_lanes(t, width):
    k = width // t.shape[1]
    return t if k == 1 else jnp.concatenate([t] * k, axis=1)


def _rope(x, cos_t, sa_t, sb_t):
    w = x.shape[1]
    return (x * _tile_lanes(cos_t, w)
            + pltpu.roll(x, w - ROT_HALF, 1) * _tile_lanes(sa_t, w)
            + pltpu.roll(x, ROT_HALF, 1) * _tile_lanes(sb_t, w))


def _low_half(shape):
    return (lax.broadcasted_iota(jnp.int32, shape, 1) & (LANES - 1)) < HEAD_DIM


def _rope_table_kernel(cos_ref, sa_ref, sb_ref, *, tm, seq, dec_seq):
    i = pl.program_id(0)
    row = i * tm + lax.broadcasted_iota(jnp.int32, (tm, LANES), 0)
    lane = lax.broadcasted_iota(jnp.int32, (tm, LANES), 1)
    pos = jnp.where(row < seq, row, PAST_LEN + ((row - seq) & (dec_seq - 1)))
    c = lane & (HEAD_DIM - 1)
    f = (c & (ROT_HALF - 1)).astype(F32)
    inv_freq = jnp.exp(f * (-math.log(ROPE_THETA) / ROT_HALF))
    ang = pos.astype(F32) * inv_freq
    cs = jnp.cos(ang)
    sn = jnp.sin(ang)
    cos_ref[...] = jnp.where(c < 2 * ROT_HALF, cs, 1.0)
    sa_ref[...] = jnp.where(c < ROT_HALF, -sn, 0.0)
    sb_ref[...] = jnp.where((c >= ROT_HALF) & (c < 2 * ROT_HALF), sn, 0.0)


def _rope_tables(m, seq, dec_seq, tm):
    assert dec_seq & (dec_seq - 1) == 0 and m % tm == 0
    shp = jax.ShapeDtypeStruct((m, LANES), F32)
    spec = pl.BlockSpec((tm, LANES), lambda i: (i, 0))
    return pl.pallas_call(
        functools.partial(_rope_table_kernel, tm=tm, seq=seq, dec_seq=dec_seq),
        out_shape=(shp, shp, shp), grid=(m // tm,), out_specs=(spec, spec, spec),
        compiler_params=_cparams(1), name="rope_tables")()


def _as_pair(a):
    return tuple(a) if isinstance(a, (tuple, list)) else (a,)


def _row_specs(parts, tm, width, n_p, col):
    if len(parts) == 1:
        return [pl.BlockSpec((tm, width), lambda i, j: (i, col(j)))]
    return [pl.BlockSpec((tm, width), lambda i, j: (jnp.minimum(i, n_p - 1), col(j))),
            pl.BlockSpec((tm, width), lambda i, j: (jnp.maximum(i - n_p, 0), col(j)))]


def _by_source(n_p, any_pair, compute):
    if not any_pair:
        compute(0)
        return
    i = pl.program_id(0)
    pl.when(i < n_p)(lambda: compute(0))
    pl.when(i >= n_p)(lambda: compute(1))


def _norm_mm_kernel(*refs, act, n_x, n_p):
    x_refs = refs[:n_x]
    g_ref, w_ref, o_ref, xn_ref = refs[n_x:]

    def normalize(sel):
        x = x_refs[min(sel, n_x - 1)][...]
        ms = jnp.mean(x * x, axis=-1, keepdims=True)
        xn_ref[...] = (x * lax.rsqrt(ms + NORM_EPS) * g_ref[...]).astype(BF16)

    @pl.when(pl.program_id(1) == 0)
    def _():
        _by_source(n_p, n_x > 1, normalize)

    acc = jnp.dot(xn_ref[...], w_ref[...], preferred_element_type=F32)
    if act == "relu2":
        acc = jnp.square(jnp.maximum(acc, 0.0))
    o_ref[...] = acc.astype(o_ref.dtype)


def _norm_matmul(x, g, w, *, tm, tn, act=None, out_dtype=F32, name):
    xs = _as_pair(x)
    m = sum(a.shape[0] for a in xs)
    k = xs[0].shape[1]
    n = w.shape[1]
    n_p = xs[0].shape[0] // tm
    assert all(a.shape[0] % tm == 0 for a in xs) and n % tn == 0 and w.shape[0] == k
    return pl.pallas_call(
        functools.partial(_norm_mm_kernel, act=act, n_x=len(xs), n_p=n_p),
        out_shape=jax.ShapeDtypeStruct((m, n), out_dtype),
        grid=(m // tm, n // tn),
        in_specs=_row_specs(xs, tm, k, n_p, lambda j: 0) + [
            pl.BlockSpec((1, k), lambda i, j: (0, 0)),
            pl.BlockSpec((k, tn), lambda i, j: (0, j))],
        out_specs=pl.BlockSpec((tm, tn), lambda i, j: (i, j)),
        scratch_shapes=[pltpu.VMEM((tm, k), BF16)],
        compiler_params=_cparams(2), name=name)(*xs, g.reshape(1, k), w)


def _mm_resid_kernel(*refs, kinds, n_p):
    it = iter(refs)
    x_ops = [tuple(next(it) for _ in range(kd)) for kd in kinds[:-1]]
    w_refs = [next(it) for _ in kinds[:-1]]
    r_op = tuple(next(it) for _ in range(kinds[-1]))
    o_ref = next(it)

    def compute(sel):
        acc = r_op[min(sel, len(r_op) - 1)][...]
        for x_op, w_ref in zip(x_ops, w_refs):
            acc = acc + jnp.dot(x_op[min(sel, len(x_op) - 1)][...], w_ref[...], preferred_element_type=F32)
        o_ref[...] = acc

    _by_source(n_p, max(kinds) > 1, compute)


def _matmul_resid(xs, ws, resid, *, tm, tn, name):
    xs = [_as_pair(x) for x in xs]
    rs = _as_pair(resid)
    m = sum(a.shape[0] for a in rs)
    n = rs[0].shape[1]
    pairs = [p for p in xs + [rs] if len(p) > 1]
    n_p = pairs[0][0].shape[0] // tm if pairs else 0
    assert m % tm == 0 and n % tn == 0 and all(a.shape[0] % tm == 0 for p in pairs for a in p)
    in_specs = []
    for x in xs:
        in_specs += _row_specs(x, tm, x[0].shape[1], n_p, lambda j: 0)
    for w in ws:
        in_specs.append(pl.BlockSpec((w.shape[0], tn), lambda i, j: (0, j)))
    in_specs += _row_specs(rs, tm, tn, n_p, lambda j: j)
    args = [a for x in xs for a in x] + list(ws) + list(rs)
    return pl.pallas_call(
        functools.partial(_mm_resid_kernel, kinds=tuple(len(x) for x in xs) + (len(rs),), n_p=n_p),
        out_shape=jax.ShapeDtypeStruct((m, n), F32),
        grid=(m // tm, n // tn),
        in_specs=in_specs,
        out_specs=pl.BlockSpec((tm, tn), lambda i, j: (i, j)),
        compiler_params=_cparams(2), name=name)(*args)


def _final_norm_kernel(x_ref, g_ref, op_ref, os_ref, *, n_p):
    x = x_ref[...]
    ms = jnp.mean(x * x, axis=-1, keepdims=True)
    y = x * lax.rsqrt(ms + NORM_EPS) * g_ref[...]
    i = pl.program_id(0)

    @pl.when(i < n_p)
    def _():
        op_ref[...] = y

    @pl.when(i >= n_p)
    def _():
        os_ref[...] = y


def _final_norm(x, g, seq, *, tm):
    m, k = x.shape
    n_p = seq // tm
    assert seq % tm == 0 and (m - seq) % tm == 0
    return pl.pallas_call(
        functools.partial(_final_norm_kernel, n_p=n_p),
        out_shape=(jax.ShapeDtypeStruct((seq, k), F32), jax.ShapeDtypeStruct((m - seq, k), F32)),
        grid=(m // tm,),
        in_specs=[pl.BlockSpec((tm, k), lambda i: (i, 0)), pl.BlockSpec((1, k), lambda i: (0, 0))],
        out_specs=(pl.BlockSpec((tm, k), lambda i: (jnp.minimum(i, n_p - 1), 0)),
                   pl.BlockSpec((tm, k), lambda i: (jnp.maximum(i - n_p, 0), 0))),
        compiler_params=_cparams(1), name="final_norm")(x, g.reshape(1, k))


def _causal_conv(raw, tail, w_ref, b_ref):
    n = raw.shape[0]
    full = jnp.concatenate([tail, raw], axis=0)
    acc = b_ref[...]
    for i in range(SSD_CONV):
        s = SSD_CONV - 1 - i
        acc = acc + full[8 - s:8 - s + n] * w_ref[i:i + 1, :]
    return _silu(acc)


def _gate_groupnorm(y, z, gn):
    y = y * _silu(z)
    gw = y.shape[1] // SSD_GROUPS
    outs = []
    for g in range(SSD_GROUPS):
        yg = y[:, g * gw:(g + 1) * gw]
        ms = jnp.mean(yg * yg, axis=-1, keepdims=True)
        outs.append(yg * lax.rsqrt(ms + NORM_EPS))
    return jnp.concatenate(outs, axis=1) * gn


def _ssd_prompt_kernel(z_ref, xs_ref, bc_ref, dt_ref, wx_ref, wbc_ref, bx_ref, bbc_ref, dtb_ref, alog_ref,
                       dskip_ref, gn_ref, y_ref, tailx_ref, tailbc_ref, h_ref, y_s):
    c = pl.program_id(0)
    q = SSD_CHUNK
    gs = SSD_STATE
    hp = HEAD_DIM

    @pl.when(c == 0)
    def _():
        tailx_ref[...] = jnp.zeros_like(tailx_ref)
        tailbc_ref[...] = jnp.zeros_like(tailbc_ref)
        h_ref[...] = jnp.zeros_like(h_ref)

    xs_raw = xs_ref[...]
    bc_raw = bc_ref[...]
    xs_c = _causal_conv(xs_raw, tailx_ref[...], wx_ref, bx_ref)
    bc_c = _causal_conv(bc_raw, tailbc_ref[...], wbc_ref, bbc_ref)
    tailx_ref[...] = xs_raw[q - 8:q]
    tailbc_ref[...] = bc_raw[q - 8:q]

    lane = lax.broadcasted_iota(jnp.int32, (q, LANES), 1)
    row = lax.broadcasted_iota(jnp.int32, (q, LANES), 0)
    dt = jax.nn.softplus(dt_ref[:, :LANES] + dtb_ref[...])
    a = -jnp.exp(alog_ref[...])
    la = jnp.where(lane < N_HEADS, dt * a, 0.0)
    acs = la
    s = 1
    while s < q:
        acs = acs + jnp.where(row >= s, pltpu.roll(acs, s, 0), 0.0)
        s *= 2
    acs_t = acs.T
    acs_last = acs[q - 1:q, :]
    to_end = jnp.exp(acs_last - acs)
    eacs = jnp.exp(acs)
    cdec = jnp.exp(acs_last)
    causal = (lax.broadcasted_iota(jnp.int32, (q, q), 0) >= lax.broadcasted_iota(jnp.int32, (q, q), 1))

    rep = N_HEADS // SSD_GROUPS
    for g in range(SSD_GROUPS):
        bg = bc_c[:, g * gs:(g + 1) * gs].astype(BF16)
        cg = bc_c[:, SSD_GROUPS * gs + g * gs:SSD_GROUPS * gs + (g + 1) * gs].astype(BF16)
        cb = lax.dot_general(cg, bg, _NT, preferred_element_type=F32)
        for rp in range(rep // 2):
            pair = []
            for h in (g * rep + 2 * rp, g * rep + 2 * rp + 1):
                xh = xs_c[:, h * hp:(h + 1) * hp]
                xdt = xh * dt[:, h:h + 1]
                lmat = jnp.exp(jnp.where(causal, acs[:, h:h + 1] - acs_t[h:h + 1, :], -jnp.inf))
                y = jnp.dot((cb * lmat).astype(BF16), xdt.astype(BF16), preferred_element_type=F32)
                hs = h_ref[h]
                yoff = lax.dot_general(cg, hs.astype(BF16), _NT, preferred_element_type=F32)
                y = y + eacs[:, h:h + 1] * yoff
                w16 = (xdt * to_end[:, h:h + 1]).astype(BF16)
                st = lax.dot_general(w16, bg, _TN, preferred_element_type=F32)
                h_ref[h] = hs * cdec[:, h:h + 1] + st
                pair.append(y + dskip_ref[:, h * hp:(h + 1) * hp] * xh)
            h0 = g * rep + 2 * rp
            y_s[:, h0 * hp:(h0 + 2) * hp] = jnp.concatenate(pair, axis=1)

    y_ref[...] = _gate_groupnorm(y_s[...], z_ref[...], gn_ref[...]).astype(y_ref.dtype)


def _ssd_prompt(p0, off, seq, wx, wbc, bx, bbc, dtb, alog, dskip, gn):
    q = SSD_CHUNK
    d_in = wx.shape[1]
    d_bc = wbc.shape[1]
    assert seq % q == 0
    const = lambda shape: pl.BlockSpec(shape, lambda c: (0,) * len(shape))
    return pl.pallas_call(
        _ssd_prompt_kernel,
        out_shape=(jax.ShapeDtypeStruct((seq, d_in), BF16),
                   jax.ShapeDtypeStruct((8, d_in), F32),
                   jax.ShapeDtypeStruct((8, d_bc), F32),
                   jax.ShapeDtypeStruct((N_HEADS, HEAD_DIM, SSD_STATE), F32)),
        grid=(seq // q,),
        in_specs=[pl.BlockSpec((q, d_in), lambda c: (c, off["z"] // d_in)),
                  pl.BlockSpec((q, d_in), lambda c: (c, off["xs"] // d_in)),
                  pl.BlockSpec((q, d_bc), lambda c: (c, off["bc"] // d_bc)),
                  pl.BlockSpec((q, 256), lambda c: (c, off["dt"] // 256)),
                  const((SSD_CONV, d_in)), const((SSD_CONV, d_bc)), const((1, d_in)), const((1, d_bc)),
                  const((1, LANES)), const((1, LANES)), const((1, d_in)), const((1, d_in))],
        out_specs=(pl.BlockSpec((q, d_in), lambda c: (c, 0)),
                   const((8, d_in)), const((8, d_bc)), const((N_HEADS, HEAD_DIM, SSD_STATE))),
        scratch_shapes=[pltpu.VMEM((q, d_in), F32)],
        compiler_params=_cparams(1), name="ssd_prompt")(p0, p0, p0, p0, wx, wbc, bx, bbc, dtb, alog, dskip, gn)


def _split3(x):
    hi = x.astype(BF16)
    r1 = x - hi.astype(F32)
    mid = r1.astype(BF16)
    lo = (r1 - mid.astype(F32)).astype(BF16)
    return hi, mid, lo


def _ssd_sample_kernel(z_ref, xs_ref, bc_ref, dt_ref, ex_ref, ebc_ref, st_ref, wx_ref, wbc_ref, bx_ref, bbc_ref,
                       dtb_ref, alog_ref, dskip_ref, gn_ref, eexp_ref, y_ref, sto_ref, *, nb, t_len):
    rows = nb * t_len
    gs = SSD_STATE
    gw = xs_ref.shape[1] // SSD_GROUPS
    rep = N_HEADS // SSD_GROUPS

    def tcol(width):
        r = lax.broadcasted_iota(jnp.int32, (rows, width), 0)
        return r & (t_len - 1), r >> int(math.log2(t_len))

    def conv(raw, est, w_ref, b_ref):
        t, _ = tcol(raw.shape[1])
        acc = b_ref[...]
        for i in range(SSD_CONV):
            s = SSD_CONV - 1 - i
            if s == 0:
                sh = raw
            else:
                k = (rows - (SSD_CONV - 1) + s) % rows
                sh = jnp.where(t >= s, pltpu.roll(raw, s, 0), est if k == 0 else pltpu.roll(est, k, 0))
            acc = acc + sh * w_ref[i:i + 1, :]
        return _silu(acc)

    xs_c = conv(xs_ref[...], ex_ref[...], wx_ref, bx_ref)
    bc_c = conv(bc_ref[...], ebc_ref[...], wbc_ref, bbc_ref)
    bm = bc_c[:, :SSD_GROUPS * gs]
    cm = bc_c[:, SSD_GROUPS * gs:]

    t1, _ = tcol(LANES)
    lane = lax.broadcasted_iota(jnp.int32, (rows, LANES), 1)
    dt = jax.nn.softplus(dt_ref[:, :LANES] + dtb_ref[...])
    a = -jnp.exp(alog_ref[...])
    la = jnp.where(lane < N_HEADS, dt * a, 0.0)
    acs = la
    for s in range(1, t_len):
        acs = acs + jnp.where(t1 >= s, pltpu.roll(la, s, 0), 0.0)
    alast = jnp.where(t1 == t_len - 1, acs, 0.0)
    for u in range(1, t_len):
        alast = alast + jnp.where(t1 == t_len - 1 - u, pltpu.roll(acs, rows - u, 0), 0.0)
    parts = [dt, jnp.exp(acs), jnp.exp(alast - acs), jnp.exp(alast)]
    for k in range(1, t_len):
        parts.append(jnp.exp(acs - pltpu.roll(acs, k, 0)))
    stacked = jnp.concatenate(parts, axis=0)
    hi, mid, lo = _split3(stacked)
    e = eexp_ref[...]
    full = (jnp.dot(hi, e, preferred_element_type=F32) + jnp.dot(mid, e, preferred_element_type=F32)
            + jnp.dot(lo, e, preferred_element_type=F32))
    dt_f, eacs_f, toend_f, cdec_f = (full[i * rows:(i + 1) * rows] for i in range(4))
    dec_f = [None] + [full[(3 + k) * rows:(4 + k) * rows] for k in range(1, t_len)]

    tw, _ = tcol(xs_c.shape[1])
    xdt = xs_c * dt_f
    y = jnp.zeros_like(xs_c)
    for k in range(t_len):
        bmk = bm if k == 0 else pltpu.roll(bm, k, 0)
        prod = cm * bmk
        cbs = []
        for g in range(SSD_GROUPS):
            sg = jnp.sum(prod[:, g * gs:(g + 1) * gs], axis=-1, keepdims=True)
            cbs.append(jnp.broadcast_to(sg, (rows, gw)))
        cb_f = jnp.concatenate(cbs, axis=1)
        if k == 0:
            y = y + cb_f * xdt
        else:
            y = y + jnp.where(tw >= k, cb_f * dec_f[k] * pltpu.roll(xdt, k, 0), 0.0)

    cm16 = cm.astype(BF16)
    bm16 = bm.astype(BF16)
    w_f = (toend_f * xdt)
    _, bg_ = tcol(gw)
    ones_blk = jnp.ones((3, gs), BF16)
    yoffs = []
    for g in range(SSD_GROUPS):
        cg = cm16[:, g * gs:(g + 1) * gs]
        rhs_top = jnp.concatenate([bm16[:, g * gs:(g + 1) * gs], jnp.zeros((rows, gs), BF16)], axis=1)
        rhs_mid = jnp.concatenate([jnp.zeros((3, gs), BF16), ones_blk], axis=1)
        rhs = jnp.concatenate([rhs_top, rhs_mid, jnp.zeros((13, 2 * gs), BF16)], axis=0)
        wg = w_f[:, g * gw:(g + 1) * gw]
        cdg = cdec_f[:, g * gw:(g + 1) * gw]
        yg = jnp.zeros((rows, gw), F32)
        for b in range(nb):
            hb = st_ref[b, g * rep:(g + 1) * rep].reshape(gw, gs)
            yb = lax.dot_general(cg, hb.astype(BF16), _NT, preferred_element_type=F32)
            yg = jnp.where(bg_ == b, yb, yg)
            wb = jnp.where(bg_ == b, wg, 0.0).astype(BF16)
            d_hi, d_mid, d_lo = _split3(cdg[b * t_len:b * t_len + 1])
            lhs = jnp.concatenate([wb, d_hi, d_mid, d_lo, jnp.zeros((13, gw), BF16)], axis=0)
            sd = lax.dot_general(lhs, rhs, _TN, preferred_element_type=F32)
            sto_ref[b, g * rep:(g + 1) * rep] = (hb * sd[:, gs:] + sd[:, :gs]).reshape(rep, HEAD_DIM, gs)
        yoffs.append(yg)
    y = y + eacs_f * jnp.concatenate(yoffs, axis=1) + dskip_ref[...] * xs_c
    y_ref[...] = _gate_groupnorm(y, z_ref[...], gn_ref[...]).astype(y_ref.dtype)


def _ssd_sample(p0, off, seq, n_b, t_len, epad, state, wx, wbc, bx, bbc, dtb, alog, dskip, gn, eexp, *, nb):
    rows = nb * t_len
    d_in = wx.shape[1]
    d_bc = wbc.shape[1]
    assert n_b % nb == 0 and seq % rows == 0 and t_len & (t_len - 1) == 0
    r0 = seq // rows
    const = lambda shape: pl.BlockSpec(shape, lambda i: (0,) * len(shape))
    st_spec = pl.BlockSpec((nb, N_HEADS, HEAD_DIM, SSD_STATE), lambda i: (i, 0, 0, 0))
    return pl.pallas_call(
        functools.partial(_ssd_sample_kernel, nb=nb, t_len=t_len),
        out_shape=(jax.ShapeDtypeStruct((n_b * t_len, d_in), BF16),
                   jax.ShapeDtypeStruct(state.shape, F32)),
        grid=(n_b // nb,),
        in_specs=[pl.BlockSpec((rows, d_in), lambda i: (r0 + i, off["z"] // d_in)),
                  pl.BlockSpec((rows, d_in), lambda i: (r0 + i, off["xs"] // d_in)),
                  pl.BlockSpec((rows, d_bc), lambda i: (r0 + i, off["bc"] // d_bc)),
                  pl.BlockSpec((rows, 256), lambda i: (r0 + i, off["dt"] // 256)),
                  pl.BlockSpec((rows, d_in), lambda i: (i, 0)),
                  pl.BlockSpec((rows, d_bc), lambda i: (i, d_in // d_bc)),
                  st_spec,
                  const((SSD_CONV, d_in)), const((SSD_CONV, d_bc)), const((1, d_in)), const((1, d_bc)),
                  const((1, LANES)), const((1, LANES)), const((1, d_in)), const((1, d_in)),
                  const((LANES, d_in))],
        out_specs=(pl.BlockSpec((rows, d_in), lambda i: (i, 0)), st_spec),
        compiler_params=_cparams(1), name="ssd_sample")(
            p0, p0, p0, p0, epad, epad, state, wx, wbc, bx, bbc, dtb, alog, dskip, gn, eexp)


def _perm_matrix(dil):
    i = jnp.arange(PERM_CHUNK)
    blk = BF16_ROWS * dil
    src = (i // blk) * blk + (i % BF16_ROWS) * dil + (i % blk) // BF16_ROWS
    return (src[:, None] == jnp.arange(PERM_CHUNK)[None, :]).astype(BF16)


def _band_kernel(*refs, dil, max_dist, use_sink, want_lse, want_krot, nkv, nj):
    it = iter(refs)
    q_ref, k_ref, v_ref, cos_ref, sa_ref, sb_ref = (next(it) for _ in range(6))
    p_ref, pt_ref = (next(it), next(it)) if dil > 1 else (None, None)
    sink_ref = next(it) if use_sink else None
    o_ref = next(it)
    lse_ref = next(it) if want_lse else None
    krot_ref = next(it) if want_krot else None
    qlo_s, qhi_s, kd_s, vd_s, kdp_s, vdp_s, op_s = (next(it) for _ in range(7))
    lsep_s = next(it) if want_lse else None
    lse_s = next(it) if (want_lse and nj > 1) else None

    c = pl.program_id(0)
    j = pl.program_id(1)
    rows, wq = q_ref.shape
    wk = k_ref.shape[1]
    n_chunk = rows // PERM_CHUNK if dil > 1 else 0
    n_grp = BAND // BF16_ROWS

    def perm(x):
        if dil == 1:
            return x
        x16 = x.astype(BF16)
        return jnp.concatenate(
            [jnp.dot(p_ref[...], x16[k * PERM_CHUNK:(k + 1) * PERM_CHUNK], preferred_element_type=F32)
             for k in range(n_chunk)], axis=0)

    def unperm(xp16):
        if dil == 1:
            return [xp16]
        return [jnp.dot(pt_ref[...], xp16[k * PERM_CHUNK:(k + 1) * PERM_CHUNK], preferred_element_type=F32)
                for k in range(n_chunk)]

    tabs = (cos_ref[...], sa_ref[...], sb_ref[...])
    q = perm(_rope(q_ref[...], *tabs) * (ATT_SCALE * LOG2E))
    lowq = _low_half((rows, wq))
    qlo_s[...] = jnp.where(lowq, q, 0.0).astype(BF16)
    qhi_s[...] = jnp.where(lowq, 0.0, q).astype(BF16)
    kr = _rope(k_ref[...], *tabs)
    if want_krot:
        krot_ref[...] = kr
    lowk = _low_half((rows, LANES))
    for x, d_s in ((perm(kr), kd_s), (perm(v_ref[...]), vd_s)):
        up = pltpu.roll(x, HEAD_DIM, 1)
        dn = pltpu.roll(x, wk - HEAD_DIM, 1)
        for t in range(wk // LANES):
            sl = slice(t * LANES, (t + 1) * LANES)
            d_s[2 * t] = jnp.where(lowk, x[:, sl], up[:, sl]).astype(BF16)
            d_s[2 * t + 1] = jnp.where(lowk, dn[:, sl], x[:, sl]).astype(BF16)

    @pl.when(c == 0)
    def _():
        kdp_s[j] = jnp.zeros(kdp_s.shape[1:], BF16)
        vdp_s[j] = jnp.zeros(vdp_s.shape[1:], BF16)

    qi = lax.broadcasted_iota(jnp.int32, (REP * BAND, 2 * BAND), 0) & (BAND - 1)
    kj = lax.broadcasted_iota(jnp.int32, (REP * BAND, 2 * BAND), 1)
    dist = qi - kj + BAND
    bias = jnp.where((dist >= 0) & (dist <= max_dist) & ((kj >= BAND) | (c > 0)), 0.0, NEG)
    low = _low_half((BAND, LANES))
    lane = lax.broadcasted_iota(jnp.int32, (BAND, LANES), 1)
    step = BF16_ROWS * dil

    def residue(rho, carry):
        starts = [pl.multiple_of(b * step + rho * BF16_ROWS, BF16_ROWS) for b in range(n_grp)]

        def gather(get):
            return jnp.concatenate([get(s) for s in starts], axis=0)

        tiles = []
        lse_acc = jnp.zeros((BAND, LANES), F32)
        for g in range(nkv):
            ca, cb = slice(2 * g * LANES, (2 * g + 1) * LANES), slice((2 * g + 1) * LANES, (2 * g + 2) * LANES)
            lhs = jnp.concatenate([gather(lambda s: qlo_s[pl.ds(s, BF16_ROWS), ca]),
                                   gather(lambda s: qhi_s[pl.ds(s, BF16_ROWS), ca]),
                                   gather(lambda s: qlo_s[pl.ds(s, BF16_ROWS), cb]),
                                   gather(lambda s: qhi_s[pl.ds(s, BF16_ROWS), cb])], axis=0)
            kcat = jnp.concatenate([gather(lambda s: kdp_s[j, g, pl.ds(s, BF16_ROWS), :]),
                                    gather(lambda s: kd_s[g, pl.ds(s, BF16_ROWS), :])], axis=0)
            vcat = jnp.concatenate([gather(lambda s: vdp_s[j, g, pl.ds(s, BF16_ROWS), :]),
                                    gather(lambda s: vd_s[g, pl.ds(s, BF16_ROWS), :])], axis=0)
            s = lax.dot_general(lhs, kcat, _NT, preferred_element_type=F32) + bias
            m = jnp.max(s, axis=-1, keepdims=True)
            if use_sink:
                sk = jnp.concatenate(
                    [jnp.full((BAND, 1), sink_ref[j * REP * nkv + g * REP + r] * LOG2E, F32) for r in range(REP)],
                    axis=0)
                m = jnp.maximum(m, sk)
            e = jnp.exp2(s - m)
            den = jnp.sum(e, axis=-1, keepdims=True)
            if use_sink:
                den = den + jnp.exp2(sk - m)
            o = jnp.dot(e.astype(BF16), vcat, preferred_element_type=F32) * (1.0 / den)
            tiles.append(jnp.where(low, o[0:BAND], o[BAND:2 * BAND]))
            tiles.append(jnp.where(low, o[2 * BAND:3 * BAND], o[3 * BAND:4 * BAND]))
            if want_lse:
                lse = (m + jnp.log2(den)) * LN2
                for r in range(REP):
                    col = jnp.broadcast_to(lse[r * BAND:(r + 1) * BAND], (BAND, LANES))
                    lse_acc = lse_acc + jnp.where(lane == j * REP * nkv + g * REP + r, col, 0.0)
        o_res = jnp.concatenate(tiles, axis=1)
        for b, s0 in enumerate(starts):
            op_s[pl.ds(s0, BF16_ROWS), :] = o_res[b * BF16_ROWS:(b + 1) * BF16_ROWS]
            if want_lse:
                lsep_s[pl.ds(s0, BF16_ROWS), :] = lse_acc[b * BF16_ROWS:(b + 1) * BF16_ROWS]
        return carry

    if dil == 1:
        residue(0, 0)
    else:
        lax.fori_loop(0, dil, residue, 0, unroll=2 if nkv <= 2 else 1)

    for k, blk in enumerate(unperm(op_s[...].astype(BF16))):
        if dil == 1:
            o_ref[...] = blk
        else:
            o_ref[k * PERM_CHUNK:(k + 1) * PERM_CHUNK, :] = blk.astype(o_ref.dtype)
    if want_lse:
        if dil == 1:
            lse_nat = lsep_s[...]
        else:
            parts = [unperm(x) for x in _split3(lsep_s[...])]
            lse_nat = jnp.concatenate([a + b + d for a, b, d in zip(*parts)], axis=0)
        if nj == 1:
            lse_ref[...] = lse_nat
        else:
            @pl.when(j == 0)
            def _():
                lse_s[...] = lse_nat

            @pl.when(j > 0)
            def _():
                lse_s[...] = lse_s[...] + lse_nat

            @pl.when(j == nj - 1)
            def _():
                lse_ref[...] = lse_s[...]
    kdp_s[j] = kd_s[...]
    vdp_s[j] = vd_s[...]


def _band_attention(p, seq, q_off, k_off, v_off, tables, sinks, *, dil, max_dist, nkv, want_lse, want_krot, name):
    rows = BAND * dil
    nj = N_KV // nkv
    qw, kw = nkv * REP * HEAD_DIM, nkv * HEAD_DIM
    assert seq % rows == 0 and kw % LANES == 0 and q_off % qw == 0 and k_off % kw == 0 and v_off % kw == 0
    assert dil == 1 or (rows % PERM_CHUNK == 0 and PERM_CHUNK % (BF16_ROWS * dil) == 0)
    use_sink = sinks is not None
    d_att = N_HEADS * HEAD_DIM
    blk = lambda width, cb: pl.BlockSpec((rows, width), lambda c, j: (c, cb + j))
    t_spec = pl.BlockSpec((rows, LANES), lambda c, j: (c, 0))
    in_specs = [blk(qw, q_off // qw), blk(kw, k_off // kw), blk(kw, v_off // kw), t_spec, t_spec, t_spec]
    args = [p, p, p] + list(tables)
    if dil > 1:
        pm = _perm_matrix(dil)
        in_specs += [pl.BlockSpec((PERM_CHUNK, PERM_CHUNK), lambda c, j: (0, 0))] * 2
        args += [pm, pm.T]
    if use_sink:
        in_specs.append(pl.BlockSpec(memory_space=pltpu.SMEM))
        args.append(sinks)
    out_shape = [jax.ShapeDtypeStruct((seq, d_att), BF16)]
    out_specs = [pl.BlockSpec((rows, qw), lambda c, j: (c, j))]
    scratch = [pltpu.VMEM((rows, qw), BF16), pltpu.VMEM((rows, qw), BF16),
               pltpu.VMEM((nkv, rows, LANES), BF16), pltpu.VMEM((nkv, rows, LANES), BF16),
               pltpu.VMEM((nj, nkv, rows, LANES), BF16), pltpu.VMEM((nj, nkv, rows, LANES), BF16),
               pltpu.VMEM((rows, qw), F32)]
    if want_lse:
        out_shape.append(jax.ShapeDtypeStruct((seq, LANES), F32))
        out_specs.append(pl.BlockSpec((rows, LANES), lambda c, j: (c, 0)))
        scratch.append(pltpu.VMEM((rows, LANES), F32))
        if nj > 1:
            scratch.append(pltpu.VMEM((rows, LANES), F32))
    if want_krot:
        out_shape.append(jax.ShapeDtypeStruct((seq, N_KV * HEAD_DIM), F32))
        out_specs.append(pl.BlockSpec((rows, kw), lambda c, j: (c, j)))
    return pl.pallas_call(
        functools.partial(_band_kernel, dil=dil, max_dist=max_dist, use_sink=use_sink, want_lse=want_lse,
                          want_krot=want_krot, nkv=nkv, nj=nj),
        out_shape=tuple(out_shape), grid=(seq // rows, nj), in_specs=in_specs, out_specs=tuple(out_specs),
        scratch_shapes=scratch, compiler_params=_cparams(2), name=name)(*args)


def _merge_kernel(o1_ref, o2_ref, o3_ref, l1_ref, l2_ref, l3_ref, e_ref, out_ref):
    ls = [l1_ref[...], l2_ref[...], l3_ref[...]]
    mx = jnp.maximum(jnp.maximum(ls[0], ls[1]), ls[2])
    es = [jnp.exp(l - mx) for l in ls]
    inv = 1.0 / (es[0] + es[1] + es[2])
    e = e_ref[...]
    acc = None
    for en, o_ref in zip(es, (o1_ref, o2_ref, o3_ref)):
        w = en * inv
        hi = w.astype(BF16)
        lo = (w - hi.astype(F32)).astype(BF16)
        wf = jnp.dot(hi, e, preferred_element_type=F32) + jnp.dot(lo, e, preferred_element_type=F32)
        term = wf * o_ref[...].astype(F32)
        acc = term if acc is None else acc + term
    out_ref[...] = acc.astype(out_ref.dtype)


def _merge(os_, ls_, eexp, *, tm):
    seq, d = os_[0].shape
    o_spec = pl.BlockSpec((tm, d), lambda i: (i, 0))
    l_spec = pl.BlockSpec((tm, LANES), lambda i: (i, 0))
    return pl.pallas_call(
        _merge_kernel, out_shape=jax.ShapeDtypeStruct((seq, d), BF16), grid=(seq // tm,),
        in_specs=[o_spec] * 3 + [l_spec] * 3 + [pl.BlockSpec((LANES, d), lambda i: (0, 0))],
        out_specs=o_spec, compiler_params=_cparams(1), name="dil_merge")(*os_, *ls_, eexp)


def _expand_q(qb, t_len):
    gd = N_KV * HEAD_DIM
    sub = lax.broadcasted_iota(jnp.int32, (N_KV, gd), 0)
    lg = lax.broadcasted_iota(jnp.int32, (N_KV, gd), 1) >> _LOG2_HEAD_DIM
    diag = sub == lg
    tiles = []
    for r in range(REP):
        vr = jnp.concatenate(
            [qb[:, (g * REP + r) * HEAD_DIM:(g * REP + r + 1) * HEAD_DIM] for g in range(N_KV)], axis=1)
        for t in range(t_len):
            tiles.append(jnp.where(diag, jnp.broadcast_to(vr[t:t + 1], (N_KV, gd)), 0.0))
    return jnp.concatenate(tiles, axis=0).astype(BF16)


def _collapse_o(r_full, t_len):
    gd = N_KV * HEAD_DIM
    n = r_full.shape[0]
    sub = lax.broadcasted_iota(jnp.int32, (n, gd), 0) & (N_KV - 1)
    lg = lax.broadcasted_iota(jnp.int32, (n, gd), 1) >> _LOG2_HEAD_DIM
    masked = jnp.where(sub == lg, r_full, 0.0)
    red = jnp.sum(masked.reshape(n // N_KV, N_KV, gd), axis=1)
    pieces = []
    for g in range(N_KV):
        for r in range(REP):
            pieces.append(red[r * t_len:(r + 1) * t_len, g * HEAD_DIM:(g + 1) * HEAD_DIM])
    return jnp.concatenate(pieces, axis=1)


def _row_t(shape, t_len):
    r = lax.broadcasted_iota(jnp.int32, shape, 0)
    return (r >> 3) & (t_len - 1)


def _cached_attend(qx, kt, vt, mask_w, kn, vn, mask_n, sink):
    s_w = jnp.where(mask_w, jnp.dot(qx, kt, preferred_element_type=F32), NEG)
    s_n = jnp.where(mask_n, lax.dot_general(qx, kn, _NT, preferred_element_type=F32), NEG)
    m = jnp.maximum(jnp.max(s_w, axis=-1, keepdims=True), jnp.max(s_n, axis=-1, keepdims=True))
    if sink is not None:
        m = jnp.maximum(m, sink)
    e_w = jnp.exp(s_w - m)
    e_n = jnp.exp(s_n - m)
    den = jnp.sum(e_w, axis=-1, keepdims=True) + jnp.sum(e_n, axis=-1, keepdims=True)
    if sink is not None:
        den = den + jnp.exp(sink - m)
    r = (lax.dot_general(e_w.astype(BF16), vt, _NT, preferred_element_type=F32)
         + jnp.dot(e_n.astype(BF16), vn, preferred_element_type=F32))
    return r / den, m + jnp.log(den)


def _swa_cached_kernel(q_ref, kn_ref, vn_ref, cos_ref, sa_ref, sb_ref, ck_ref, cv_ref, sink_ref,
                       o_ref, krot_ref, *, nb, t_len):
    tabs = (cos_ref[...], sa_ref[...], sb_ref[...])
    q = _rope(q_ref[...], *tabs) * ATT_SCALE
    kn = _rope(kn_ref[...], *tabs)
    krot_ref[...] = kn
    kn16 = kn.astype(BF16)
    vn16 = vn_ref[...].astype(BF16)
    lb = ck_ref.shape[3]
    gd = N_KV * HEAD_DIM
    nrow = REP * t_len * N_KV
    tq = _row_t((nrow, lb), t_len)
    mask_w = lax.broadcasted_iota(jnp.int32, (nrow, lb), 1) >= tq + 1
    mask_n = lax.broadcasted_iota(jnp.int32, (nrow, BF16_ROWS), 1) <= _row_t((nrow, BF16_ROWS), t_len)
    sk = sink_ref[...]
    pad = jnp.zeros((BF16_ROWS - t_len, gd), BF16)
    outs = []
    for b in range(nb):
        rs = slice(b * t_len, (b + 1) * t_len)
        r_full, _ = _cached_attend(
            _expand_q(q[rs], t_len), ck_ref[b].reshape(gd, lb).astype(BF16), cv_ref[b].reshape(gd, lb).astype(BF16),
            mask_w, jnp.concatenate([kn16[rs], pad], axis=0), jnp.concatenate([vn16[rs], pad], axis=0), mask_n, sk)
        outs.append(_collapse_o(r_full, t_len))
    o_ref[...] = jnp.concatenate(outs, axis=0).astype(o_ref.dtype)


def _swa_cached(p0, off, seq, n_b, t_len, tables, cache_kt, cache_vt, sink_col, *, nb):
    rows = nb * t_len
    d_att = N_HEADS * HEAD_DIM
    d_kv = N_KV * HEAD_DIM
    lb = cache_kt.shape[3]
    assert lb == SWA_WINDOW and t_len <= BF16_ROWS
    r0 = seq // rows
    nrow = REP * t_len * N_KV
    c_spec = pl.BlockSpec((nb, N_KV, HEAD_DIM, lb), lambda i: (i, 0, 0, 0))
    t_spec = pl.BlockSpec((rows, LANES), lambda i: (r0 + i, 0))
    return pl.pallas_call(
        functools.partial(_swa_cached_kernel, nb=nb, t_len=t_len),
        out_shape=(jax.ShapeDtypeStruct((n_b * t_len, d_att), BF16),
                   jax.ShapeDtypeStruct((n_b * t_len, d_kv), F32)),
        grid=(n_b // nb,),
        in_specs=[pl.BlockSpec((rows, d_att), lambda i: (r0 + i, off["q"] // d_att)),
                  pl.BlockSpec((rows, d_kv), lambda i: (r0 + i, off["k"] // d_kv)),
                  pl.BlockSpec((rows, d_kv), lambda i: (r0 + i, off["v"] // d_kv)),
                  t_spec, t_spec, t_spec, c_spec, c_spec,
                  pl.BlockSpec((nrow, 1), lambda i: (0, 0))],
        out_specs=(pl.BlockSpec((rows, d_att), lambda i: (i, 0)),
                   pl.BlockSpec((rows, d_kv), lambda i: (i, 0))),
        compiler_params=_cparams(1), name="swa_cached")(p0, p0, p0, *tables, cache_kt, cache_vt, sink_col)


def _dil_cached_kernel(q1_ref, q2_ref, q3_ref, kn_ref, vn_ref, cos_ref, sa_ref, sb_ref, ck_ref, cv_ref,
                       o_ref, krot_ref, *, nb, t_len):
    tabs = (cos_ref[...], sa_ref[...], sb_ref[...])
    qs = [_rope(r[...], *tabs) * ATT_SCALE for r in (q1_ref, q2_ref, q3_ref)]
    kn = _rope(kn_ref[...], *tabs)
    krot_ref[...] = kn
    kn16 = kn.astype(BF16)
    vn16 = vn_ref[...].astype(BF16)
    lbuf = ck_ref.shape[3]
    gd = N_KV * HEAD_DIM
    nrow = REP * t_len * N_KV
    tqn = _row_t((nrow, BF16_ROWS), t_len)
    coln = lax.broadcasted_iota(jnp.int32, (nrow, BF16_ROWS), 1)
    masks = []
    for window, dil in DIL_PATTERNS:
        tq = _row_t((nrow, window), t_len)
        col = lax.broadcasted_iota(jnp.int32, (nrow, window), 1)
        if dil == 1:
            masks.append((col >= tq, coln <= tqn))
        else:
            masks.append(((col & (dil - 1)) == tq, coln == tqn))
    pad = jnp.zeros((BF16_ROWS - t_len, gd), BF16)
    outs = []
    for b in range(nb):
        rs = slice(b * t_len, (b + 1) * t_len)
        kt = ck_ref[b].reshape(gd, lbuf).astype(BF16)
        vt = cv_ref[b].reshape(gd, lbuf).astype(BF16)
        knp = jnp.concatenate([kn16[rs], pad], axis=0)
        vnp = jnp.concatenate([vn16[rs], pad], axis=0)
        res = []
        for gi, (window, dil) in enumerate(DIL_PATTERNS):
            res.append(_cached_attend(_expand_q(qs[gi][rs], t_len), kt[:, lbuf - window:], vt[:, lbuf - window:],
                                      masks[gi][0], knp, vnp, masks[gi][1], None))
        mx = jnp.maximum(jnp.maximum(res[0][1], res[1][1]), res[2][1])
        ws = [jnp.exp(l - mx) for _, l in res]
        inv = 1.0 / (ws[0] + ws[1] + ws[2])
        merged = (ws[0] * inv) * res[0][0] + (ws[1] * inv) * res[1][0] + (ws[2] * inv) * res[2][0]
        outs.append(_collapse_o(merged, t_len))
    o_ref[...] = jnp.concatenate(outs, axis=0).astype(o_ref.dtype)


def _dil_cached(p1, seq, n_b, t_len, tables, cache_kt, cache_vt, *, nb):
    rows = nb * t_len
    d_att = N_HEADS * HEAD_DIM
    d_kv = N_KV * HEAD_DIM
    lbuf = cache_kt.shape[3]
    assert rows % 8 == 0 and seq % rows == 0 and t_len <= BF16_ROWS
    for window, dil in DIL_PATTERNS:
        assert window <= lbuf and (PAST_LEN - lbuf) % dil == 0 and lbuf % dil == 0 and (dil == 1 or dil >= t_len)
    r0 = seq // rows
    c_spec = pl.BlockSpec((nb, N_KV, HEAD_DIM, lbuf), lambda i: (i, 0, 0, 0))
    t_spec = pl.BlockSpec((rows, LANES), lambda i: (r0 + i, 0))
    nq = d_att * len(DIL_PATTERNS)
    return pl.pallas_call(
        functools.partial(_dil_cached_kernel, nb=nb, t_len=t_len),
        out_shape=(jax.ShapeDtypeStruct((n_b * t_len, d_att), F32),
                   jax.ShapeDtypeStruct((n_b * t_len, d_kv), F32)),
        grid=(n_b // nb,),
        in_specs=[pl.BlockSpec((rows, d_att), lambda i: (r0 + i, 0)),
                  pl.BlockSpec((rows, d_att), lambda i: (r0 + i, 1)),
                  pl.BlockSpec((rows, d_att), lambda i: (r0 + i, 2)),
                  pl.BlockSpec((rows, d_kv), lambda i: (r0 + i, nq // d_kv)),
                  pl.BlockSpec((rows, d_kv), lambda i: (r0 + i, nq // d_kv + 1)),
                  t_spec, t_spec, t_spec, c_spec, c_spec],
        out_specs=(pl.BlockSpec((rows, d_att), lambda i: (i, 0)),
                   pl.BlockSpec((rows, d_kv), lambda i: (i, 0))),
        compiler_params=_cparams(1), name="dil_cached")(p1, p1, p1, p1, p1, *tables, cache_kt, cache_vt)


def _row_tile(m, pref):
    for t in (512, 256, 128, 64, 32, 16, 8):
        if t <= pref and m % t == 0:
            return t
    raise ValueError(m)


def kernel(x_prompt, x_sample, state_conv, state_ssm, cache_swa_k, cache_swa_v, cache_dil_k, cache_dil_v, norm_mix, norm_mlp, e_w_in, e_conv_w, e_conv_b, e_dt_bias, e_a_log, e_d_skip, e_gate_norm, e_sinks, e_w_out, o_w_in, o_w_out, mlp_w1, mlp_w2, norm_final):
    nbp, seq, d = x_prompt.shape
    n_b, t_len, _ = x_sample.shape
    assert nbp == 1 and d == N_HEADS * HEAD_DIM and norm_mix.shape[0] == 2
    ms = n_b * t_len
    m = seq + ms
    tm = _row_tile(math.gcd(seq, ms), 512)
    d_kv = N_KV * HEAD_DIM
    d_bc = 2 * SSD_GROUPS * SSD_STATE
    conv_dim = d + d_bc

    x_pair = (x_prompt.reshape(seq, d), x_sample.reshape(ms, d))
    tables = _rope_tables(m, seq, t_len, tm)
    eexp = (lax.broadcasted_iota(jnp.int32, (LANES, d), 1) // HEAD_DIM
            == lax.broadcasted_iota(jnp.int32, (LANES, d), 0)).astype(BF16)
    keys_minor = lambda c: jnp.transpose(c[0], (0, 2, 3, 1))

    wi = e_w_in[0]
    c_z, c_xbc, c_dt, c_q = 0, d, d + conv_dim, d + conv_dim + N_HEADS
    c_k, c_v = c_q + d, c_q + d + d_kv
    w0 = jnp.concatenate(
        [wi[:, c_z:c_z + 2 * d], wi[:, c_q:c_q + d], wi[:, c_xbc + d:c_xbc + conv_dim], wi[:, c_k:c_v + d_kv],
         wi[:, c_dt:c_dt + N_HEADS], jnp.zeros((d, 256 - N_HEADS), F32)], axis=1).astype(BF16)
    off0 = {"z": 0, "xs": d, "q": 2 * d, "bc": 3 * d, "k": 3 * d + d_bc, "v": 3 * d + d_bc + d_kv,
            "dt": 3 * d + d_bc + 2 * d_kv}
    p0 = _norm_matmul(x_pair, norm_mix[0], w0, tm=tm, tn=768, name="l0_in_proj")

    cw, cb = e_conv_w[0], e_conv_b[0]
    wx, wbc = cw[:, :d], cw[:, d:]
    bx, bbc = cb[:d].reshape(1, d), cb[d:].reshape(1, d_bc)
    pad_h = lambda v: jnp.pad(v.reshape(1, N_HEADS), ((0, 0), (0, LANES - N_HEADS)))
    dtb, alog = pad_h(e_dt_bias[0]), pad_h(e_a_log[0])
    dskip = jnp.repeat(e_d_skip[0], HEAD_DIM).reshape(1, d)
    gn = e_gate_norm[0].reshape(1, d)

    ssd_p, tail_x, tail_bc, p_ssm = _ssd_prompt(p0, off0, seq, wx, wbc, bx, bbc, dtb, alog, dskip, gn)
    epad = jnp.pad(state_conv[0], ((0, 0), (0, 1), (0, 0))).reshape(ms, conv_dim)
    ssd_s, s_ssm = _ssd_sample(p0, off0, seq, n_b, t_len, epad, state_ssm[0], wx, wbc, bx, bbc, dtb, alog, dskip,
                               gn, eexp, nb=4)

    att_p, krot0_p = _band_attention(p0, seq, off0["q"], off0["k"], off0["v"], tables, e_sinks[0], dil=1,
                                     max_dist=SWA_WINDOW - 1, nkv=N_KV, want_lse=False, want_krot=True,
                                     name="swa_prompt")
    sink_col = jnp.broadcast_to(e_sinks[0].reshape(N_KV, REP).T[:, None, :], (REP, t_len, N_KV)).reshape(-1, 1)
    att_s, krot0_s = _swa_cached(p0, off0, seq, n_b, t_len, tables, keys_minor(cache_swa_k),
                                 keys_minor(cache_swa_v), sink_col, nb=4)

    wo = e_w_out[0].astype(BF16)
    h = _matmul_resid([(ssd_p, ssd_s), (att_p, att_s)], [wo[:d], wo[d:]], x_pair, tm=tm, tn=1024, name="l0_out_proj")
    u = _norm_matmul(h, norm_mlp[0], mlp_w1[0].astype(BF16), tm=tm, tn=1024, act="relu2", out_dtype=BF16,
                     name="l0_mlp_up")
    h = _matmul_resid([u], [mlp_w2[0].astype(BF16)], h, tm=tm, tn=512, name="l0_mlp_down")

    n_pat = len(DIL_PATTERNS)
    p1 = _norm_matmul(h, norm_mix[1], o_w_in[0].astype(BF16), tm=tm, tn=1024, name="l1_in_proj")
    k_off, v_off = n_pat * d, n_pat * d + d_kv
    os_, ls_ = [], []
    krot1_p = None
    for gi, (window, dil) in enumerate(DIL_PATTERNS):
        res = _band_attention(p1, seq, gi * d, k_off, v_off, tables, None, dil=dil, max_dist=window // dil,
                              nkv=(2 if BAND * dil * N_HEADS * HEAD_DIM * 4 > 8 * 2 ** 20 else N_KV),
                              want_lse=True, want_krot=(gi == 0), name=f"dil_prompt_{dil}")
        os_.append(res[0])
        ls_.append(res[1])
        if gi == 0:
            krot1_p = res[2]
    merged_p = _merge(os_, ls_, eexp, tm=_row_tile(seq, 256))
    dil_s, krot1_s = _dil_cached(p1, seq, n_b, t_len, tables, keys_minor(cache_dil_k), keys_minor(cache_dil_v), nb=2)
    h = _matmul_resid([(merged_p, dil_s.astype(BF16))], [o_w_out[0].astype(BF16)], h, tm=tm, tn=1024,
                      name="l1_out_proj")
    u = _norm_matmul(h, norm_mlp[1], mlp_w1[1].astype(BF16), tm=tm, tn=1024, act="relu2", out_dtype=BF16,
                     name="l1_mlp_up")
    h = _matmul_resid([u], [mlp_w2[1].astype(BF16)], h, tm=tm, tn=512, name="l1_mlp_down")
    y_p, y_s = _final_norm(h, norm_final, seq, tm=tm)

    keep_swa = min(SWA_WINDOW, seq)
    keep_dil = min(max(w for w, _ in DIL_PATTERNS), seq)
    kv4 = lambda a: a.reshape(a.shape[0], N_KV, HEAD_DIM)
    y_prompt = y_p.reshape(1, seq, d)
    y_sample = y_s.reshape(n_b, t_len, d)
    p_conv = jnp.concatenate([tail_x[8 - (SSD_CONV - 1):], tail_bc[8 - (SSD_CONV - 1):]], axis=1)[None, None]
    p_swa_k = kv4(krot0_p[seq - keep_swa:])[None, None]
    p_swa_v = kv4(p0[seq - keep_swa:seq, off0["v"]:off0["v"] + d_kv])[None, None]
    p_dil_k = kv4(krot1_p[seq - keep_dil:])[None, None]
    p_dil_v = kv4(p1[seq - keep_dil:seq, v_off:v_off + d_kv])[None, None]
    xbc_s = jnp.concatenate([p0[seq:, off0["xs"]:off0["xs"] + d], p0[seq:, off0["bc"]:off0["bc"] + d_bc]], axis=1)
    s_conv = xbc_s.reshape(n_b, t_len, conv_dim)[:, t_len - (SSD_CONV - 1):][None]
    bt = lambda a: a.reshape(n_b, t_len, N_KV, HEAD_DIM)[None]
    s_swa_k = bt(krot0_s)
    s_swa_v = bt(p0[seq:, off0["v"]:off0["v"] + d_kv])
    s_dil_k = bt(krot1_s)
    s_dil_v = bt(p1[seq:, v_off:v_off + d_kv])
    return (y_prompt, y_sample, p_conv, p_ssm[None, None], p_swa_k, p_swa_v, p_dil_k, p_dil_v,
            s_conv, s_ssm[None], s_swa_k, s_swa_v, s_dil_k, s_dil_v)
```

```python
import functools
import math

import jax
import jax.numpy as jnp
from jax import lax
from jax.experimental import pallas as pl
from jax.experimental.pallas import tpu as pltpu

F32 = jnp.float32
BF16 = jnp.bfloat16

NORM_EPS = 1e-5
HEAD_DIM = 64
ROT_HALF = 8
ROPE_THETA = 500000.0
PAST_LEN = 8192
N_HEADS = 32
N_KV = 8
REP = N_HEADS // N_KV
SSD_GROUPS = 4
SSD_STATE = 128
SSD_CONV = 4
SSD_CHUNK = 128
SWA_WINDOW = 128
DIL_PATTERNS = ((128, 1), (512, 4), (2048, 16))
BAND = 128
ATT_SCALE = HEAD_DIM ** -0.5
LOG2E = math.log2(math.e)
LN2 = math.log(2.0)
NEG = -1e30
SOFTMAX_ROWS = 32
UNITS_IN_FLIGHT = 4
_LOG2_HEAD_DIM = 6

LANES = 128
BF16_ROWS = 16
PERM_CHUNK = 256
VMEM_LIMIT_BYTES = 56 * 1024 * 1024

_NT = (((1,), (1,)), ((), ()))
_TN = (((0,), (0,)), ((), ()))


def _cparams(n_axes):
    return pltpu.CompilerParams(dimension_semantics=("arbitrary",) * n_axes,
                                vmem_limit_bytes=VMEM_LIMIT_BYTES)


def _silu(x):
    return x * jax.nn.sigmoid(x)


def _tile_lanes(t, width):
    k = width // t.shape[1]
    return t if k == 1 else jnp.concatenate([t] * k, axis=1)


def _rope(x, cos_t, sa_t, sb_t):
    w = x.shape[1]
    return (x * _tile_lanes(cos_t, w)
            + pltpu.roll(x, w - ROT_HALF, 1) * _tile_lanes(sa_t, w)
            + pltpu.roll(x, ROT_HALF, 1) * _tile_lanes(sb_t, w))


def _low_half(shape):
    return (lax.broadcasted_iota(jnp.int32, shape, 1) & (LANES - 1)) < HEAD_DIM


def _rope_table_kernel(cos_ref, sa_ref, sb_ref, *, tm, seq, dec_seq):
    i = pl.program_id(0)
    row = i * tm + lax.broadcasted_iota(jnp.int32, (tm, LANES), 0)
    lane = lax.broadcasted_iota(jnp.int32, (tm, LANES), 1)
    pos = jnp.where(row < seq, row, PAST_LEN + ((row - seq) & (dec_seq - 1)))
    c = lane & (HEAD_DIM - 1)
    f = (c & (ROT_HALF - 1)).astype(F32)
    inv_freq = jnp.exp(f * (-math.log(ROPE_THETA) / ROT_HALF))
    ang = pos.astype(F32) * inv_freq
    cs = jnp.cos(ang)
    sn = jnp.sin(ang)
    cos_ref[...] = jnp.where(c < 2 * ROT_HALF, cs, 1.0)
    sa_ref[...] = jnp.where(c < ROT_HALF, -sn, 0.0)
    sb_ref[...] = jnp.where((c >= ROT_HALF) & (c < 2 * ROT_HALF), sn, 0.0)


def _rope_tables(m, seq, dec_seq, tm):
    assert dec_seq & (dec_seq - 1) == 0 and m % tm == 0
    shp = jax.ShapeDtypeStruct((m, LANES), F32)
    spec = pl.BlockSpec((tm, LANES), lambda i: (i, 0))
    return pl.pallas_call(
        functools.partial(_rope_table_kernel, tm=tm, seq=seq, dec_seq=dec_seq),
        out_shape=(shp, shp, shp), grid=(m // tm,), out_specs=(spec, spec, spec),
        compiler_params=_cparams(1), name="rope_tables")()


def _rmsnorm_bf16(x, g):
    ms = jnp.mean(x * x, axis=-1, keepdims=True)
    return (x * lax.rsqrt(ms + NORM_EPS) * g).astype(BF16)


def _norm_mm_kernel(x_ref, g_ref, w_ref, o_ref, xn_ref):
    @pl.when(pl.program_id(1) == 0)
    def _():
        xn_ref[...] = _rmsnorm_bf16(x_ref[...], g_ref[...])

    o_ref[...] = jnp.dot(xn_ref[...], w_ref[...], preferred_element_type=F32).astype(o_ref.dtype)


def _norm_matmul(x, g, w, *, tm, tn, name):
    m, k = x.shape
    n = w.shape[1]
    assert m % tm == 0 and n % tn == 0 and w.shape[0] == k
    return pl.pallas_call(
        _norm_mm_kernel,
        out_shape=jax.ShapeDtypeStruct((m, n), F32),
        grid=(m // tm, n // tn),
        in_specs=[pl.BlockSpec((tm, k), lambda i, j: (i, 0)),
                  pl.BlockSpec((1, k), lambda i, j: (0, 0)),
                  pl.BlockSpec((k, tn), lambda i, j: (0, j))],
        out_specs=pl.BlockSpec((tm, tn), lambda i, j: (i, j)),
        scratch_shapes=[pltpu.VMEM((tm, k), BF16)],
        compiler_params=_cparams(2), name=name)(x, g.reshape(1, k), w)


def _mm_resid_kernel(*refs, n_in):
    x_refs, w_refs = refs[:n_in], refs[n_in:2 * n_in]
    r_ref, o_ref = refs[2 * n_in], refs[2 * n_in + 1]
    acc = r_ref[...]
    for x_ref, w_ref in zip(x_refs, w_refs):
        acc = acc + jnp.dot(x_ref[...], w_ref[...], preferred_element_type=F32)
    o_ref[...] = acc


def _matmul_resid(xs, ws, resid, *, tm, tn, name):
    m, n = resid.shape
    assert m % tm == 0 and n % tn == 0
    in_specs = [pl.BlockSpec((tm, x.shape[1]), lambda i, j: (i, 0)) for x in xs]
    in_specs += [pl.BlockSpec((w.shape[0], tn), lambda i, j: (0, j)) for w in ws]
    in_specs.append(pl.BlockSpec((tm, tn), lambda i, j: (i, j)))
    return pl.pallas_call(
        functools.partial(_mm_resid_kernel, n_in=len(xs)),
        out_shape=jax.ShapeDtypeStruct((m, n), F32),
        grid=(m // tm, n // tn),
        in_specs=in_specs,
        out_specs=pl.BlockSpec((tm, tn), lambda i, j: (i, j)),
        compiler_params=_cparams(2), name=name)(*xs, *ws, resid)


def _mlp_kernel(x_ref, g_ref, w1_ref, w2_ref, o_ref, xn_ref):
    @pl.when(pl.program_id(1) == 0)
    def _():
        x = x_ref[...]
        xn_ref[...] = _rmsnorm_bf16(x, g_ref[...])
        o_ref[...] = x

    u = jnp.dot(xn_ref[...], w1_ref[...], preferred_element_type=F32)
    u = jnp.square(jnp.maximum(u, 0.0)).astype(BF16)
    o_ref[...] += jnp.dot(u, w2_ref[...], preferred_element_type=F32)


def _mlp(x, g, w1, w2, *, tm, tk, name):
    m, d = x.shape
    hid = w1.shape[1]
    assert m % tm == 0 and hid % tk == 0
    return pl.pallas_call(
        _mlp_kernel,
        out_shape=jax.ShapeDtypeStruct((m, d), F32),
        grid=(m // tm, hid // tk),
        in_specs=[pl.BlockSpec((tm, d), lambda i, k: (i, 0)),
                  pl.BlockSpec((1, d), lambda i, k: (0, 0)),
                  pl.BlockSpec((d, tk), lambda i, k: (0, k)),
                  pl.BlockSpec((tk, d), lambda i, k: (k, 0))],
        out_specs=pl.BlockSpec((tm, d), lambda i, k: (i, 0)),
        scratch_shapes=[pltpu.VMEM((tm, d), BF16)],
        compiler_params=_cparams(2), name=name)(x, g.reshape(1, d), w1, w2)


def _final_norm_kernel(x_ref, g_ref, o_ref):
    x = x_ref[...]
    ms = jnp.mean(x * x, axis=-1, keepdims=True)
    o_ref[...] = x * lax.rsqrt(ms + NORM_EPS) * g_ref[...]


def _final_norm(x, g, *, tm, name):
    m, k = x.shape
    return pl.pallas_call(
        _final_norm_kernel, out_shape=jax.ShapeDtypeStruct((m, k), F32), grid=(m // tm,),
        in_specs=[pl.BlockSpec((tm, k), lambda i: (i, 0)), pl.BlockSpec((1, k), lambda i: (0, 0))],
        out_specs=pl.BlockSpec((tm, k), lambda i: (i, 0)),
        compiler_params=_cparams(1), name=name)(x, g.reshape(1, k))


def _causal_conv(raw, tail, w_ref, b_ref):
    n = raw.shape[0]
    full = jnp.concatenate([tail, raw], axis=0)
    acc = b_ref[...]
    for i in range(SSD_CONV):
        s = SSD_CONV - 1 - i
        acc = acc + full[8 - s:8 - s + n] * w_ref[i:i + 1, :]
    return _silu(acc)


def _gate_groupnorm(y, z, gn):
    y = y * _silu(z)
    gw = y.shape[1] // SSD_GROUPS
    outs = []
    for g in range(SSD_GROUPS):
        yg = y[:, g * gw:(g + 1) * gw]
        ms = jnp.mean(yg * yg, axis=-1, keepdims=True)
        outs.append(yg * lax.rsqrt(ms + NORM_EPS))
    return jnp.concatenate(outs, axis=1) * gn


def _ssd_prompt_kernel(z_ref, xs_ref, bc_ref, dt_ref, wx_ref, wbc_ref, bx_ref, bbc_ref, dtb_ref, alog_ref,
                       dskip_ref, gn_ref, y_ref, tailx_ref, tailbc_ref, h_ref, y_s):
    c = pl.program_id(0)
    q = SSD_CHUNK
    gs = SSD_STATE
    hp = HEAD_DIM

    @pl.when(c == 0)
    def _():
        tailx_ref[...] = jnp.zeros_like(tailx_ref)
        tailbc_ref[...] = jnp.zeros_like(tailbc_ref)
        h_ref[...] = jnp.zeros_like(h_ref)

    xs_raw = xs_ref[...]
    bc_raw = bc_ref[...]
    xs_c = _causal_conv(xs_raw, tailx_ref[...], wx_ref, bx_ref)
    bc_c = _causal_conv(bc_raw, tailbc_ref[...], wbc_ref, bbc_ref)
    tailx_ref[...] = xs_raw[q - 8:q]
    tailbc_ref[...] = bc_raw[q - 8:q]

    lane = lax.broadcasted_iota(jnp.int32, (q, LANES), 1)
    row = lax.broadcasted_iota(jnp.int32, (q, LANES), 0)
    dt = jax.nn.softplus(dt_ref[:, :LANES] + dtb_ref[...])
    a = -jnp.exp(alog_ref[...])
    la = jnp.where(lane < N_HEADS, dt * a, 0.0)
    acs = la
    s = 1
    while s < q:
        acs = acs + jnp.where(row >= s, pltpu.roll(acs, s, 0), 0.0)
        s *= 2
    acs_t = acs.T
    acs_last = acs[q - 1:q, :]
    to_end = jnp.exp(acs_last - acs)
    eacs = jnp.exp(acs)
    cdec = jnp.exp(acs_last)
    causal = (lax.broadcasted_iota(jnp.int32, (q, q), 0) >= lax.broadcasted_iota(jnp.int32, (q, q), 1))

    rep = N_HEADS // SSD_GROUPS
    for g in range(SSD_GROUPS):
        bg = bc_c[:, g * gs:(g + 1) * gs].astype(BF16)
        cg = bc_c[:, SSD_GROUPS * gs + g * gs:SSD_GROUPS * gs + (g + 1) * gs].astype(BF16)
        cb = lax.dot_general(cg, bg, _NT, preferred_element_type=F32)
        for rp in range(rep // 2):
            pair = []
            for h in (g * rep + 2 * rp, g * rep + 2 * rp + 1):
                xh = xs_c[:, h * hp:(h + 1) * hp]
                xdt = xh * dt[:, h:h + 1]
                lmat = jnp.exp(jnp.where(causal, acs[:, h:h + 1] - acs_t[h:h + 1, :], -jnp.inf))
                y = jnp.dot((cb * lmat).astype(BF16), xdt.astype(BF16), preferred_element_type=F32)
                hs = h_ref[h]
                yoff = lax.dot_general(cg, hs.astype(BF16), _NT, preferred_element_type=F32)
                y = y + eacs[:, h:h + 1] * yoff
                w16 = (xdt * to_end[:, h:h + 1]).astype(BF16)
                st = lax.dot_general(w16, bg, _TN, preferred_element_type=F32)
                h_ref[h] = hs * cdec[:, h:h + 1] + st
                pair.append(y + dskip_ref[:, h * hp:(h + 1) * hp] * xh)
            h0 = g * rep + 2 * rp
            y_s[:, h0 * hp:(h0 + 2) * hp] = jnp.concatenate(pair, axis=1)

    y_ref[...] = _gate_groupnorm(y_s[...], z_ref[...], gn_ref[...]).astype(y_ref.dtype)


def _ssd_prompt(p0, off, seq, wx, wbc, bx, bbc, dtb, alog, dskip, gn):
    q = SSD_CHUNK
    d_in = wx.shape[1]
    d_bc = wbc.shape[1]
    assert seq % q == 0
    const = lambda shape: pl.BlockSpec(shape, lambda c: (0,) * len(shape))
    return pl.pallas_call(
        _ssd_prompt_kernel,
        out_shape=(jax.ShapeDtypeStruct((seq, d_in), BF16),
                   jax.ShapeDtypeStruct((8, d_in), F32),
                   jax.ShapeDtypeStruct((8, d_bc), F32),
                   jax.ShapeDtypeStruct((N_HEADS, HEAD_DIM, SSD_STATE), F32)),
        grid=(seq // q,),
        in_specs=[pl.BlockSpec((q, d_in), lambda c: (c, off["z"] // d_in)),
                  pl.BlockSpec((q, d_in), lambda c: (c, off["xs"] // d_in)),
                  pl.BlockSpec((q, d_bc), lambda c: (c, off["bc"] // d_bc)),
                  pl.BlockSpec((q, 256), lambda c: (c, off["dt"] // 256)),
                  const((SSD_CONV, d_in)), const((SSD_CONV, d_bc)), const((1, d_in)), const((1, d_bc)),
                  const((1, LANES)), const((1, LANES)), const((1, d_in)), const((1, d_in))],
        out_specs=(pl.BlockSpec((q, d_in), lambda c: (c, 0)),
                   const((8, d_in)), const((8, d_bc)), const((N_HEADS, HEAD_DIM, SSD_STATE))),
        scratch_shapes=[pltpu.VMEM((q, d_in), F32)],
        compiler_params=_cparams(1), name="ssd_prompt")(p0, p0, p0, p0, wx, wbc, bx, bbc, dtb, alog, dskip, gn)


def _split3(x):
    hi = x.astype(BF16)
    r1 = x - hi.astype(F32)
    mid = r1.astype(BF16)
    lo = (r1 - mid.astype(F32)).astype(BF16)
    return hi, mid, lo


def _ssd_sample_kernel(z_ref, xs_ref, bc_ref, dt_ref, ex_ref, ebc_ref, st_ref, wx_ref, wbc_ref, bx_ref, bbc_ref,
                       dtb_ref, alog_ref, dskip_ref, gn_ref, eexp_ref, y_ref, sto_ref, *, nb, t_len):
    rows = nb * t_len
    gs = SSD_STATE
    gw = xs_ref.shape[1] // SSD_GROUPS
    rep = N_HEADS // SSD_GROUPS

    def tcol(width):
        r = lax.broadcasted_iota(jnp.int32, (rows, width), 0)
        return r & (t_len - 1), r >> int(math.log2(t_len))

    def conv(raw, est, w_ref, b_ref):
        t, _ = tcol(raw.shape[1])
        acc = b_ref[...]
        for i in range(SSD_CONV):
            s = SSD_CONV - 1 - i
            if s == 0:
                sh = raw
            else:
                k = (rows - (SSD_CONV - 1) + s) % rows
                sh = jnp.where(t >= s, pltpu.roll(raw, s, 0), est if k == 0 else pltpu.roll(est, k, 0))
            acc = acc + sh * w_ref[i:i + 1, :]
        return _silu(acc)

    xs_c = conv(xs_ref[...], ex_ref[...], wx_ref, bx_ref)
    bc_c = conv(bc_ref[...], ebc_ref[...], wbc_ref, bbc_ref)
    bm = bc_c[:, :SSD_GROUPS * gs]
    cm = bc_c[:, SSD_GROUPS * gs:]

    t1, _ = tcol(LANES)
    lane = lax.broadcasted_iota(jnp.int32, (rows, LANES), 1)
    dt = jax.nn.softplus(dt_ref[:, :LANES] + dtb_ref[...])
    a = -jnp.exp(alog_ref[...])
    la = jnp.where(lane < N_HEADS, dt * a, 0.0)
    acs = la
    for s in range(1, t_len):
        acs = acs + jnp.where(t1 >= s, pltpu.roll(la, s, 0), 0.0)
    alast = jnp.where(t1 == t_len - 1, acs, 0.0)
    for u in range(1, t_len):
        alast = alast + jnp.where(t1 == t_len - 1 - u, pltpu.roll(acs, rows - u, 0), 0.0)
    parts = [dt, jnp.exp(acs), jnp.exp(alast - acs), jnp.exp(alast)]
    for k in range(1, t_len):
        parts.append(jnp.exp(acs - pltpu.roll(acs, k, 0)))
    stacked = jnp.concatenate(parts, axis=0)
    hi, mid, lo = _split3(stacked)
    e = eexp_ref[...]
    full = (jnp.dot(hi, e, preferred_element_type=F32) + jnp.dot(mid, e, preferred_element_type=F32)
            + jnp.dot(lo, e, preferred_element_type=F32))
    dt_f, eacs_f, toend_f, cdec_f = (full[i * rows:(i + 1) * rows] for i in range(4))
    dec_f = [None] + [full[(3 + k) * rows:(4 + k) * rows] for k in range(1, t_len)]

    tw, _ = tcol(xs_c.shape[1])
    xdt = xs_c * dt_f
    y = jnp.zeros_like(xs_c)
    for k in range(t_len):
        bmk = bm if k == 0 else pltpu.roll(bm, k, 0)
        prod = cm * bmk
        cbs = []
        for g in range(SSD_GROUPS):
            sg = jnp.sum(prod[:, g * gs:(g + 1) * gs], axis=-1, keepdims=True)
            cbs.append(jnp.broadcast_to(sg, (rows, gw)))
        cb_f = jnp.concatenate(cbs, axis=1)
        if k == 0:
            y = y + cb_f * xdt
        else:
            y = y + jnp.where(tw >= k, cb_f * dec_f[k] * pltpu.roll(xdt, k, 0), 0.0)

    cm16 = cm.astype(BF16)
    bm16 = bm.astype(BF16)
    w_f = (toend_f * xdt)
    _, bg_ = tcol(gw)
    ones_blk = jnp.ones((3, gs), BF16)
    yoffs = []
    for g in range(SSD_GROUPS):
        cg = cm16[:, g * gs:(g + 1) * gs]
        rhs_top = jnp.concatenate([bm16[:, g * gs:(g + 1) * gs], jnp.zeros((rows, gs), BF16)], axis=1)
        rhs_mid = jnp.concatenate([jnp.zeros((3, gs), BF16), ones_blk], axis=1)
        rhs = jnp.concatenate([rhs_top, rhs_mid, jnp.zeros((13, 2 * gs), BF16)], axis=0)
        wg = w_f[:, g * gw:(g + 1) * gw]
        cdg = cdec_f[:, g * gw:(g + 1) * gw]
        yg = jnp.zeros((rows, gw), F32)
        for b in range(nb):
            hb = st_ref[b, g * rep:(g + 1) * rep].reshape(gw, gs)
            yb = lax.dot_general(cg, hb.astype(BF16), _NT, preferred_element_type=F32)
            yg = jnp.where(bg_ == b, yb, yg)
            wb = jnp.where(bg_ == b, wg, 0.0).astype(BF16)
            d_hi, d_mid, d_lo = _split3(cdg[b * t_len:b * t_len + 1])
            lhs = jnp.concatenate([wb, d_hi, d_mid, d_lo, jnp.zeros((13, gw), BF16)], axis=0)
            sd = lax.dot_general(lhs, rhs, _TN, preferred_element_type=F32)
            sto_ref[b, g * rep:(g + 1) * rep] = (hb * sd[:, gs:] + sd[:, :gs]).reshape(rep, HEAD_DIM, gs)
        yoffs.append(yg)
    y = y + eacs_f * jnp.concatenate(yoffs, axis=1) + dskip_ref[...] * xs_c
    y_ref[...] = _gate_groupnorm(y, z_ref[...], gn_ref[...]).astype(y_ref.dtype)


def _ssd_sample(p0, off, seq, n_b, t_len, epad, state, wx, wbc, bx, bbc, dtb, alog, dskip, gn, eexp, *, nb):
    rows = nb * t_len
    d_in = wx.shape[1]
    d_bc = wbc.shape[1]
    assert n_b % nb == 0 and t_len & (t_len - 1) == 0
    const = lambda shape: pl.BlockSpec(shape, lambda i: (0,) * len(shape))
    st_spec = pl.BlockSpec((nb, N_HEADS, HEAD_DIM, SSD_STATE), lambda i: (i, 0, 0, 0))
    return pl.pallas_call(
        functools.partial(_ssd_sample_kernel, nb=nb, t_len=t_len),
        out_shape=(jax.ShapeDtypeStruct((n_b * t_len, d_in), BF16),
                   jax.ShapeDtypeStruct(state.shape, F32)),
        grid=(n_b // nb,),
        in_specs=[pl.BlockSpec((rows, d_in), lambda i: (i, off["z"] // d_in)),
                  pl.BlockSpec((rows, d_in), lambda i: (i, off["xs"] // d_in)),
                  pl.BlockSpec((rows, d_bc), lambda i: (i, off["bc"] // d_bc)),
                  pl.BlockSpec((rows, 256), lambda i: (i, off["dt"] // 256)),
                  pl.BlockSpec((rows, d_in), lambda i: (i, 0)),
                  pl.BlockSpec((rows, d_bc), lambda i: (i, d_in // d_bc)),
                  st_spec,
                  const((SSD_CONV, d_in)), const((SSD_CONV, d_bc)), const((1, d_in)), const((1, d_bc)),
                  const((1, LANES)), const((1, LANES)), const((1, d_in)), const((1, d_in)),
                  const((LANES, d_in))],
        out_specs=(pl.BlockSpec((rows, d_in), lambda i: (i, 0)), st_spec),
        compiler_params=_cparams(1), name="ssd_sample")(
            p0, p0, p0, p0, epad, epad, state, wx, wbc, bx, bbc, dtb, alog, dskip, gn, eexp)


def _perm_matrix(dil):
    i = jnp.arange(PERM_CHUNK)
    blk = BF16_ROWS * dil
    src = (i // blk) * blk + (i % BF16_ROWS) * dil + (i % blk) // BF16_ROWS
    return (src[:, None] == jnp.arange(PERM_CHUNK)[None, :]).astype(BF16)


def _band_kernel(*refs, dil, max_dist, use_sink, want_lse, want_krot, nkv, nj):
    it = iter(refs)
    q_ref, k_ref, v_ref, cos_ref, sa_ref, sb_ref = (next(it) for _ in range(6))
    p_ref, pt_ref = (next(it), next(it)) if dil > 1 else (None, None)
    sink_ref = next(it) if use_sink else None
    o_ref = next(it)
    lse_ref = next(it) if want_lse else None
    krot_ref = next(it) if want_krot else None
    qlo_s, qhi_s, kd_s, vd_s, kdp_s, vdp_s, op_s, bias_s, s_s, p_s = (next(it) for _ in range(10))
    res_per_iter = s_s.shape[0] // nkv
    lsep_s = next(it) if want_lse else None
    lse_s = next(it) if (want_lse and nj > 1) else None

    c = pl.program_id(0)
    j = pl.program_id(1)
    rows, wq = q_ref.shape
    wk = k_ref.shape[1]
    n_chunk = rows // PERM_CHUNK if dil > 1 else 0
    n_grp = BAND // BF16_ROWS

    def perm(x):
        if dil == 1:
            return x
        x16 = x.astype(BF16)
        return jnp.concatenate(
            [jnp.dot(p_ref[...], x16[k * PERM_CHUNK:(k + 1) * PERM_CHUNK], preferred_element_type=F32)
             for k in range(n_chunk)], axis=0)

    def unperm(xp16):
        if dil == 1:
            return [xp16]
        return [jnp.dot(pt_ref[...], xp16[k * PERM_CHUNK:(k + 1) * PERM_CHUNK], preferred_element_type=F32)
                for k in range(n_chunk)]

    tabs = (cos_ref[...], sa_ref[...], sb_ref[...])
    q = perm(_rope(q_ref[...], *tabs) * (ATT_SCALE * LOG2E))
    lowq = _low_half((rows, wq))
    qlo_s[...] = jnp.where(lowq, q, 0.0).astype(BF16)
    qhi_s[...] = jnp.where(lowq, 0.0, q).astype(BF16)
    kr = _rope(k_ref[...], *tabs)
    if want_krot:
        krot_ref[...] = kr
    lowk = _low_half((rows, LANES))
    for x, d_s in ((perm(kr), kd_s), (perm(v_ref[...]), vd_s)):
        up = pltpu.roll(x, HEAD_DIM, 1)
        dn = pltpu.roll(x, wk - HEAD_DIM, 1)
        for t in range(wk // LANES):
            sl = slice(t * LANES, (t + 1) * LANES)
            d_s[2 * t] = jnp.where(lowk, x[:, sl], up[:, sl]).astype(BF16)
            d_s[2 * t + 1] = jnp.where(lowk, dn[:, sl], x[:, sl]).astype(BF16)

    @pl.when(c == 0)
    def _():
        kdp_s[j] = jnp.zeros(kdp_s.shape[1:], BF16)
        vdp_s[j] = jnp.zeros(vdp_s.shape[1:], BF16)

    qi = lax.broadcasted_iota(jnp.int32, (BAND, 2 * BAND), 0)
    kj = lax.broadcasted_iota(jnp.int32, (BAND, 2 * BAND), 1)
    dist = qi - kj + BAND
    bias_s[...] = jnp.where((dist >= 0) & (dist <= max_dist) & ((kj >= BAND) | (c > 0)), 0.0, NEG)
    low = _low_half((BAND, LANES))
    lane = lax.broadcasted_iota(jnp.int32, (BAND, LANES), 1)
    step = BF16_ROWS * dil
    n_sub = BAND // SOFTMAX_ROWS

    def residue(rho, set0):
        starts = [pl.multiple_of(b * step + rho * BF16_ROWS, BF16_ROWS) for b in range(n_grp)]

        def gather(get):
            return jnp.concatenate([get(s) for s in starts], axis=0)

        tiles = []
        lse_acc = jnp.zeros((BAND, LANES), F32)
        for g in range(nkv):
            ca, cb = slice(2 * g * LANES, (2 * g + 1) * LANES), slice((2 * g + 1) * LANES, (2 * g + 2) * LANES)
            lhs = jnp.concatenate([gather(lambda s: qlo_s[pl.ds(s, BF16_ROWS), ca]),
                                   gather(lambda s: qhi_s[pl.ds(s, BF16_ROWS), ca]),
                                   gather(lambda s: qlo_s[pl.ds(s, BF16_ROWS), cb]),
                                   gather(lambda s: qhi_s[pl.ds(s, BF16_ROWS), cb])], axis=0)
            kcat = jnp.concatenate([gather(lambda s: kdp_s[j, g, pl.ds(s, BF16_ROWS), :]),
                                    gather(lambda s: kd_s[g, pl.ds(s, BF16_ROWS), :])], axis=0)
            vcat = jnp.concatenate([gather(lambda s: vdp_s[j, g, pl.ds(s, BF16_ROWS), :]),
                                    gather(lambda s: vd_s[g, pl.ds(s, BF16_ROWS), :])], axis=0)
            st = set0 + g
            s_s[st] = lax.dot_general(lhs, kcat, _NT, preferred_element_type=F32)
            lse_cols = []
            for ch in range(REP * n_sub):
                rs = slice(ch * SOFTMAX_ROWS, (ch + 1) * SOFTMAX_ROWS)
                bs = slice((ch % n_sub) * SOFTMAX_ROWS, (ch % n_sub + 1) * SOFTMAX_ROWS)
                sc = s_s[st, rs, :] + bias_s[bs, :]
                m = jnp.max(sc, axis=-1, keepdims=True)
                if use_sink:
                    sk = sink_ref[j * REP * nkv + g * REP + ch // n_sub] * LOG2E
                    m = jnp.maximum(m, sk)
                e = jnp.exp2(sc - m)
                den = jnp.sum(e, axis=-1, keepdims=True)
                if use_sink:
                    den = den + jnp.exp2(sk - m)
                p_s[st, rs, :] = (e * (1.0 / den)).astype(BF16)
                if want_lse:
                    lse_cols.append((m + jnp.log2(den)) * LN2)
            o = jnp.dot(p_s[st], vcat, preferred_element_type=F32)
            tiles.append(jnp.where(low, o[0:BAND], o[BAND:2 * BAND]))
            tiles.append(jnp.where(low, o[2 * BAND:3 * BAND], o[3 * BAND:4 * BAND]))
            if want_lse:
                for r in range(REP):
                    col = jnp.concatenate(lse_cols[r * n_sub:(r + 1) * n_sub], axis=0)
                    lse_acc = lse_acc + jnp.where(lane == j * REP * nkv + g * REP + r,
                                                  jnp.broadcast_to(col, (BAND, LANES)), 0.0)
        o_res = jnp.concatenate(tiles, axis=1)
        for b, s0 in enumerate(starts):
            op_s[pl.ds(s0, BF16_ROWS), :] = o_res[b * BF16_ROWS:(b + 1) * BF16_ROWS]
            if want_lse:
                lsep_s[pl.ds(s0, BF16_ROWS), :] = lse_acc[b * BF16_ROWS:(b + 1) * BF16_ROWS]

    if dil == 1:
        residue(0, 0)
    else:
        def body(i, carry):
            for u in range(res_per_iter):
                residue(i * res_per_iter + u, u * nkv)
            return carry

        lax.fori_loop(0, dil // res_per_iter, body, 0)

    for k, blk in enumerate(unperm(op_s[...].astype(BF16))):
        if dil == 1:
            o_ref[...] = blk
        else:
            o_ref[k * PERM_CHUNK:(k + 1) * PERM_CHUNK, :] = blk.astype(o_ref.dtype)
    if want_lse:
        if dil == 1:
            lse_nat = lsep_s[...]
        else:
            parts = [unperm(x) for x in _split3(lsep_s[...])]
            lse_nat = jnp.concatenate([a + b + d for a, b, d in zip(*parts)], axis=0)
        if nj == 1:
            lse_ref[...] = lse_nat
        else:
            @pl.when(j == 0)
            def _():
                lse_s[...] = lse_nat

            @pl.when(j > 0)
            def _():
                lse_s[...] = lse_s[...] + lse_nat

            @pl.when(j == nj - 1)
            def _():
                lse_ref[...] = lse_s[...]
    kdp_s[j] = kd_s[...]
    vdp_s[j] = vd_s[...]


def _band_attention(p, seq, q_off, k_off, v_off, tables, sinks, *, dil, max_dist, nkv, want_lse, want_krot, name):
    rows = BAND * dil
    nj = N_KV // nkv
    qw, kw = nkv * REP * HEAD_DIM, nkv * HEAD_DIM
    assert seq % rows == 0 and kw % LANES == 0 and q_off % qw == 0 and k_off % kw == 0 and v_off % kw == 0
    assert dil == 1 or (rows % PERM_CHUNK == 0 and PERM_CHUNK % (BF16_ROWS * dil) == 0)
    use_sink = sinks is not None
    d_att = N_HEADS * HEAD_DIM
    n_sets = nkv * (1 if dil == 1 else max(1, UNITS_IN_FLIGHT // nkv))
    assert dil == 1 or dil % (n_sets // nkv) == 0
    blk = lambda width, cb: pl.BlockSpec((rows, width), lambda c, j: (c, cb + j))
    t_spec = pl.BlockSpec((rows, LANES), lambda c, j: (c, 0))
    in_specs = [blk(qw, q_off // qw), blk(kw, k_off // kw), blk(kw, v_off // kw), t_spec, t_spec, t_spec]
    args = [p, p, p] + list(tables)
    if dil > 1:
        pm = _perm_matrix(dil)
        in_specs += [pl.BlockSpec((PERM_CHUNK, PERM_CHUNK), lambda c, j: (0, 0))] * 2
        args += [pm, pm.T]
    if use_sink:
        in_specs.append(pl.BlockSpec(memory_space=pltpu.SMEM))
        args.append(sinks)
    out_shape = [jax.ShapeDtypeStruct((seq, d_att), BF16)]
    out_specs = [pl.BlockSpec((rows, qw), lambda c, j: (c, j))]
    scratch = [pltpu.VMEM((rows, qw), BF16), pltpu.VMEM((rows, qw), BF16),
               pltpu.VMEM((nkv, rows, LANES), BF16), pltpu.VMEM((nkv, rows, LANES), BF16),
               pltpu.VMEM((nj, nkv, rows, LANES), BF16), pltpu.VMEM((nj, nkv, rows, LANES), BF16),
               pltpu.VMEM((rows, qw), F32),
               pltpu.VMEM((BAND, 2 * BAND), F32),
               pltpu.VMEM((n_sets, REP * BAND, 2 * BAND), F32), pltpu.VMEM((n_sets, REP * BAND, 2 * BAND), BF16)]
    if want_lse:
        out_shape.append(jax.ShapeDtypeStruct((seq, LANES), F32))
        out_specs.append(pl.BlockSpec((rows, LANES), lambda c, j: (c, 0)))
        scratch.append(pltpu.VMEM((rows, LANES), F32))
        if nj > 1:
            scratch.append(pltpu.VMEM((rows, LANES), F32))
    if want_krot:
        out_shape.append(jax.ShapeDtypeStruct((seq, N_KV * HEAD_DIM), F32))
        out_specs.append(pl.BlockSpec((rows, kw), lambda c, j: (c, j)))
    return pl.pallas_call(
        functools.partial(_band_kernel, dil=dil, max_dist=max_dist, use_sink=use_sink, want_lse=want_lse,
                          want_krot=want_krot, nkv=nkv, nj=nj),
        out_shape=tuple(out_shape), grid=(seq // rows, nj), in_specs=in_specs, out_specs=tuple(out_specs),
        scratch_shapes=scratch, compiler_params=_cparams(2), name=name)(*args)


def _merge_kernel(o1_ref, o2_ref, o3_ref, l1_ref, l2_ref, l3_ref, e_ref, out_ref):
    ls = [l1_ref[...], l2_ref[...], l3_ref[...]]
    mx = jnp.maximum(jnp.maximum(ls[0], ls[1]), ls[2])
    es = [jnp.exp(l - mx) for l in ls]
    inv = 1.0 / (es[0] + es[1] + es[2])
    e = e_ref[...]
    acc = None
    for en, o_ref in zip(es, (o1_ref, o2_ref, o3_ref)):
        w = en * inv
        hi = w.astype(BF16)
        lo = (w - hi.astype(F32)).astype(BF16)
        wf = jnp.dot(hi, e, preferred_element_type=F32) + jnp.dot(lo, e, preferred_element_type=F32)
        term = wf * o_ref[...].astype(F32)
        acc = term if acc is None else acc + term
    out_ref[...] = acc.astype(out_ref.dtype)


def _merge(os_, ls_, eexp, *, tm):
    seq, d = os_[0].shape
    o_spec = pl.BlockSpec((tm, d), lambda i: (i, 0))
    l_spec = pl.BlockSpec((tm, LANES), lambda i: (i, 0))
    return pl.pallas_call(
        _merge_kernel, out_shape=jax.ShapeDtypeStruct((seq, d), BF16), grid=(seq // tm,),
        in_specs=[o_spec] * 3 + [l_spec] * 3 + [pl.BlockSpec((LANES, d), lambda i: (0, 0))],
        out_specs=o_spec, compiler_params=_cparams(1), name="dil_merge")(*os_, *ls_, eexp)


def _expand_q(qb, t_len):
    gd = N_KV * HEAD_DIM
    sub = lax.broadcasted_iota(jnp.int32, (N_KV, gd), 0)
    lg = lax.broadcasted_iota(jnp.int32, (N_KV, gd), 1) >> _LOG2_HEAD_DIM
    diag = sub == lg
    tiles = []
    for r in range(REP):
        vr = jnp.concatenate(
            [qb[:, (g * REP + r) * HEAD_DIM:(g * REP + r + 1) * HEAD_DIM] for g in range(N_KV)], axis=1)
        for t in range(t_len):
            tiles.append(jnp.where(diag, jnp.broadcast_to(vr[t:t + 1], (N_KV, gd)), 0.0))
    return jnp.concatenate(tiles, axis=0).astype(BF16)


def _collapse_o(r_full, t_len):
    gd = N_KV * HEAD_DIM
    n = r_full.shape[0]
    sub = lax.broadcasted_iota(jnp.int32, (n, gd), 0) & (N_KV - 1)
    lg = lax.broadcasted_iota(jnp.int32, (n, gd), 1) >> _LOG2_HEAD_DIM
    masked = jnp.where(sub == lg, r_full, 0.0)
    red = jnp.sum(masked.reshape(n // N_KV, N_KV, gd), axis=1)
    pieces = []
    for g in range(N_KV):
        for r in range(REP):
            pieces.append(red[r * t_len:(r + 1) * t_len, g * HEAD_DIM:(g + 1) * HEAD_DIM])
    return jnp.concatenate(pieces, axis=1)


def _row_t(shape, t_len):
    r = lax.broadcasted_iota(jnp.int32, shape, 0)
    return (r >> 3) & (t_len - 1)


def _cached_attend(qx, kt, vt, mask_w, kn, vn, mask_n, sink):
    s_w = jnp.where(mask_w, jnp.dot(qx, kt, preferred_element_type=F32), NEG)
    s_n = jnp.where(mask_n, lax.dot_general(qx, kn, _NT, preferred_element_type=F32), NEG)
    m = jnp.maximum(jnp.max(s_w, axis=-1, keepdims=True), jnp.max(s_n, axis=-1, keepdims=True))
    if sink is not None:
        m = jnp.maximum(m, sink)
    e_w = jnp.exp(s_w - m)
    e_n = jnp.exp(s_n - m)
    den = jnp.sum(e_w, axis=-1, keepdims=True) + jnp.sum(e_n, axis=-1, keepdims=True)
    if sink is not None:
        den = den + jnp.exp(sink - m)
    r = (lax.dot_general(e_w.astype(BF16), vt, _NT, preferred_element_type=F32)
         + jnp.dot(e_n.astype(BF16), vn, preferred_element_type=F32))
    return r / den, m + jnp.log(den)


def _swa_cached_kernel(q_ref, kn_ref, vn_ref, cos_ref, sa_ref, sb_ref, ck_ref, cv_ref, sink_ref,
                       o_ref, krot_ref, *, nb, t_len):
    tabs = (cos_ref[...], sa_ref[...], sb_ref[...])
    q = _rope(q_ref[...], *tabs) * ATT_SCALE
    kn = _rope(kn_ref[...], *tabs)
    krot_ref[...] = kn
    kn16 = kn.astype(BF16)
    vn16 = vn_ref[...].astype(BF16)
    lb = ck_ref.shape[3]
    gd = N_KV * HEAD_DIM
    nrow = REP * t_len * N_KV
    tq = _row_t((nrow, lb), t_len)
    mask_w = lax.broadcasted_iota(jnp.int32, (nrow, lb), 1) >= tq + 1
    mask_n = lax.broadcasted_iota(jnp.int32, (nrow, BF16_ROWS), 1) <= _row_t((nrow, BF16_ROWS), t_len)
    sk = sink_ref[...]
    pad = jnp.zeros((BF16_ROWS - t_len, gd), BF16)
    outs = []
    for b in range(nb):
        rs = slice(b * t_len, (b + 1) * t_len)
        r_full, _ = _cached_attend(
            _expand_q(q[rs], t_len), ck_ref[b].reshape(gd, lb).astype(BF16), cv_ref[b].reshape(gd, lb).astype(BF16),
            mask_w, jnp.concatenate([kn16[rs], pad], axis=0), jnp.concatenate([vn16[rs], pad], axis=0), mask_n, sk)
        outs.append(_collapse_o(r_full, t_len))
    o_ref[...] = jnp.concatenate(outs, axis=0).astype(o_ref.dtype)


def _swa_cached(p0, off, seq, n_b, t_len, tables, cache_kt, cache_vt, sink_col, *, nb):
    rows = nb * t_len
    d_att = N_HEADS * HEAD_DIM
    d_kv = N_KV * HEAD_DIM
    lb = cache_kt.shape[3]
    assert lb == SWA_WINDOW and t_len <= BF16_ROWS
    r0 = seq // rows
    nrow = REP * t_len * N_KV
    c_spec = pl.BlockSpec((nb, N_KV, HEAD_DIM, lb), lambda i: (i, 0, 0, 0))
    t_spec = pl.BlockSpec((rows, LANES), lambda i: (r0 + i, 0))
    return pl.pallas_call(
        functools.partial(_swa_cached_kernel, nb=nb, t_len=t_len),
        out_shape=(jax.ShapeDtypeStruct((n_b * t_len, d_att), BF16),
                   jax.ShapeDtypeStruct((n_b * t_len, d_kv), F32)),
        grid=(n_b // nb,),
        in_specs=[pl.BlockSpec((rows, d_att), lambda i: (i, off["q"] // d_att)),
                  pl.BlockSpec((rows, d_kv), lambda i: (i, off["k"] // d_kv)),
                  pl.BlockSpec((rows, d_kv), lambda i: (i, off["v"] // d_kv)),
                  t_spec, t_spec, t_spec, c_spec, c_spec,
                  pl.BlockSpec((nrow, 1), lambda i: (0, 0))],
        out_specs=(pl.BlockSpec((rows, d_att), lambda i: (i, 0)),
                   pl.BlockSpec((rows, d_kv), lambda i: (i, 0))),
        compiler_params=_cparams(1), name="swa_cached")(p0, p0, p0, *tables, cache_kt, cache_vt, sink_col)


def _dil_cached_kernel(q1_ref, q2_ref, q3_ref, kn_ref, vn_ref, cos_ref, sa_ref, sb_ref, ck_ref, cv_ref,
                       o_ref, krot_ref, *, nb, t_len):
    tabs = (cos_ref[...], sa_ref[...], sb_ref[...])
    qs = [_rope(r[...], *tabs) * ATT_SCALE for r in (q1_ref, q2_ref, q3_ref)]
    kn = _rope(kn_ref[...], *tabs)
    krot_ref[...] = kn
    kn16 = kn.astype(BF16)
    vn16 = vn_ref[...].astype(BF16)
    lbuf = ck_ref.shape[3]
    gd = N_KV * HEAD_DIM
    nrow = REP * t_len * N_KV
    tqn = _row_t((nrow, BF16_ROWS), t_len)
    coln = lax.broadcasted_iota(jnp.int32, (nrow, BF16_ROWS), 1)
    masks = []
    for window, dil in DIL_PATTERNS:
        tq = _row_t((nrow, window), t_len)
        col = lax.broadcasted_iota(jnp.int32, (nrow, window), 1)
        if dil == 1:
            masks.append((col >= tq, coln <= tqn))
        else:
            masks.append(((col & (dil - 1)) == tq, coln == tqn))
    pad = jnp.zeros((BF16_ROWS - t_len, gd), BF16)
    outs = []
    for b in range(nb):
        rs = slice(b * t_len, (b + 1) * t_len)
        kt = ck_ref[b].reshape(gd, lbuf).astype(BF16)
        vt = cv_ref[b].reshape(gd, lbuf).astype(BF16)
        knp = jnp.concatenate([kn16[rs], pad], axis=0)
        vnp = jnp.concatenate([vn16[rs], pad], axis=0)
        res = []
        for gi, (window, dil) in enumerate(DIL_PATTERNS):
            res.append(_cached_attend(_expand_q(qs[gi][rs], t_len), kt[:, lbuf - window:], vt[:, lbuf - window:],
                                      masks[gi][0], knp, vnp, masks[gi][1], None))
        mx = jnp.maximum(jnp.maximum(res[0][1], res[1][1]), res[2][1])
        ws = [jnp.exp(l - mx) for _, l in res]
        inv = 1.0 / (ws[0] + ws[1] + ws[2])
        merged = (ws[0] * inv) * res[0][0] + (ws[1] * inv) * res[1][0] + (ws[2] * inv) * res[2][0]
        outs.append(_collapse_o(merged, t_len))
    o_ref[...] = jnp.concatenate(outs, axis=0).astype(o_ref.dtype)


def _dil_cached(p1, seq, n_b, t_len, tables, cache_kt, cache_vt, *, nb):
    rows = nb * t_len
    d_att = N_HEADS * HEAD_DIM
    d_kv = N_KV * HEAD_DIM
    lbuf = cache_kt.shape[3]
    assert rows % 8 == 0 and seq % rows == 0 and t_len <= BF16_ROWS
    for window, dil in DIL_PATTERNS:
        assert window <= lbuf and (PAST_LEN - lbuf) % dil == 0 and lbuf % dil == 0 and (dil == 1 or dil >= t_len)
    r0 = seq // rows
    c_spec = pl.BlockSpec((nb, N_KV, HEAD_DIM, lbuf), lambda i: (i, 0, 0, 0))
    t_spec = pl.BlockSpec((rows, LANES), lambda i: (r0 + i, 0))
    nq = d_att * len(DIL_PATTERNS)
    return pl.pallas_call(
        functools.partial(_dil_cached_kernel, nb=nb, t_len=t_len),
        out_shape=(jax.ShapeDtypeStruct((n_b * t_len, d_att), F32),
                   jax.ShapeDtypeStruct((n_b * t_len, d_kv), F32)),
        grid=(n_b // nb,),
        in_specs=[pl.BlockSpec((rows, d_att), lambda i: (i, 0)),
                  pl.BlockSpec((rows, d_att), lambda i: (i, 1)),
                  pl.BlockSpec((rows, d_att), lambda i: (i, 2)),
                  pl.BlockSpec((rows, d_kv), lambda i: (i, nq // d_kv)),
                  pl.BlockSpec((rows, d_kv), lambda i: (i, nq // d_kv + 1)),
                  t_spec, t_spec, t_spec, c_spec, c_spec],
        out_specs=(pl.BlockSpec((rows, d_att), lambda i: (i, 0)),
                   pl.BlockSpec((rows, d_kv), lambda i: (i, 0))),
        compiler_params=_cparams(1), name="dil_cached")(p1, p1, p1, p1, p1, *tables, cache_kt, cache_vt)


def _row_tile(m, pref):
    for t in (1024, 512, 256, 128, 64, 32, 16, 8):
        if t <= pref and m % t == 0:
            return t
    raise ValueError(m)


def kernel(x_prompt, x_sample, state_conv, state_ssm, cache_swa_k, cache_swa_v, cache_dil_k, cache_dil_v, norm_mix, norm_mlp, e_w_in, e_conv_w, e_conv_b, e_dt_bias, e_a_log, e_d_skip, e_gate_norm, e_sinks, e_w_out, o_w_in, o_w_out, mlp_w1, mlp_w2, norm_final):
    nbp, seq, d = x_prompt.shape
    n_b, t_len, _ = x_sample.shape
    assert nbp == 1 and d == N_HEADS * HEAD_DIM and norm_mix.shape[0] == 2
    ms = n_b * t_len
    m = seq + ms
    tms = (_row_tile(seq, 1024), _row_tile(ms, 512))
    d_kv = N_KV * HEAD_DIM
    d_bc = 2 * SSD_GROUPS * SSD_STATE
    conv_dim = d + d_bc
    both = lambda fn, *rows, **kw: tuple(
        fn(*(r[s] for r in rows), tm=tms[s], name=kw["name"] + ("_p", "_s")[s], **{k: v for k, v in kw.items() if k != "name"})
        for s in range(2))

    h = (x_prompt.reshape(seq, d), x_sample.reshape(ms, d))
    tables = _rope_tables(m, seq, t_len, _row_tile(math.gcd(seq, ms), 512))
    eexp = (lax.broadcasted_iota(jnp.int32, (LANES, d), 1) // HEAD_DIM
            == lax.broadcasted_iota(jnp.int32, (LANES, d), 0)).astype(BF16)
    keys_minor = lambda c: jnp.transpose(c[0], (0, 2, 3, 1))

    wi = e_w_in[0]
    c_z, c_xbc, c_dt, c_q = 0, d, d + conv_dim, d + conv_dim + N_HEADS
    c_k, c_v = c_q + d, c_q + d + d_kv
    w0 = jnp.concatenate(
        [wi[:, c_z:c_z + 2 * d], wi[:, c_q:c_q + d], wi[:, c_xbc + d:c_xbc + conv_dim], wi[:, c_k:c_v + d_kv],
         wi[:, c_dt:c_dt + N_HEADS], jnp.zeros((d, 256 - N_HEADS), F32)], axis=1).astype(BF16)
    off0 = {"z": 0, "xs": d, "q": 2 * d, "bc": 3 * d, "k": 3 * d + d_bc, "v": 3 * d + d_bc + d_kv,
            "dt": 3 * d + d_bc + 2 * d_kv}
    g0 = norm_mix[0]
    p0_p, p0_s = both(lambda x, tm, name: _norm_matmul(x, g0, w0, tm=tm, tn=768, name=name), h, name="l0_in_proj")

    cw, cb = e_conv_w[0], e_conv_b[0]
    wx, wbc = cw[:, :d], cw[:, d:]
    bx, bbc = cb[:d].reshape(1, d), cb[d:].reshape(1, d_bc)
    pad_h = lambda v: jnp.pad(v.reshape(1, N_HEADS), ((0, 0), (0, LANES - N_HEADS)))
    dtb, alog = pad_h(e_dt_bias[0]), pad_h(e_a_log[0])
    dskip = jnp.repeat(e_d_skip[0], HEAD_DIM).reshape(1, d)
    gn = e_gate_norm[0].reshape(1, d)

    ssd_p, tail_x, tail_bc, p_ssm = _ssd_prompt(p0_p, off0, seq, wx, wbc, bx, bbc, dtb, alog, dskip, gn)
    epad = jnp.pad(state_conv[0], ((0, 0), (0, 1), (0, 0))).reshape(ms, conv_dim)
    ssd_s, s_ssm = _ssd_sample(p0_s, off0, seq, n_b, t_len, epad, state_ssm[0], wx, wbc, bx, bbc, dtb, alog, dskip,
                               gn, eexp, nb=4)

    att_p, krot0_p = _band_attention(p0_p, seq, off0["q"], off0["k"], off0["v"], tables, e_sinks[0], dil=1,
                                     max_dist=SWA_WINDOW - 1, nkv=N_KV, want_lse=False, want_krot=True,
                                     name="swa_prompt")
    sink_col = jnp.broadcast_to(e_sinks[0].reshape(N_KV, REP).T[:, None, :], (REP, t_len, N_KV)).reshape(-1, 1)
    att_s, krot0_s = _swa_cached(p0_s, off0, seq, n_b, t_len, tables, keys_minor(cache_swa_k),
                                 keys_minor(cache_swa_v), sink_col, nb=4)

    def mlp(hh, layer):
        g, w1, w2 = norm_mlp[layer], mlp_w1[layer].astype(BF16), mlp_w2[layer].astype(BF16)
        return both(lambda x, tm, name: _mlp(x, g, w1, w2, tm=tm, tk=512, name=name), hh, name=f"l{layer}_mlp")

    wo = e_w_out[0].astype(BF16)
    wo_a, wo_b = wo[:d], wo[d:]
    h = both(lambda a, b, r, tm, name: _matmul_resid([a, b], [wo_a, wo_b], r, tm=tm, tn=1024, name=name),
             (ssd_p, ssd_s), (att_p, att_s), h, name="l0_out_proj")
    h = mlp(h, 0)

    n_pat = len(DIL_PATTERNS)
    g1, w1i = norm_mix[1], o_w_in[0].astype(BF16)
    p1_p, p1_s = both(lambda x, tm, name: _norm_matmul(x, g1, w1i, tm=tm, tn=1024, name=name), h, name="l1_in_proj")
    k_off, v_off = n_pat * d, n_pat * d + d_kv
    os_, ls_ = [], []
    krot1_p = None
    for gi, (window, dil) in enumerate(DIL_PATTERNS):
        res = _band_attention(p1_p, seq, gi * d, k_off, v_off, tables, None, dil=dil, max_dist=window // dil,
                              nkv=(2 if BAND * dil * N_HEADS * HEAD_DIM * 4 > 8 * 2 ** 20 else N_KV),
                              want_lse=True, want_krot=(gi == 0), name=f"dil_prompt_{dil}")
        os_.append(res[0])
        ls_.append(res[1])
        if gi == 0:
            krot1_p = res[2]
    merged_p = _merge(os_, ls_, eexp, tm=_row_tile(seq, 256))
    dil_s, krot1_s = _dil_cached(p1_s, seq, n_b, t_len, tables, keys_minor(cache_dil_k), keys_minor(cache_dil_v),
                                 nb=2)
    wo1 = o_w_out[0].astype(BF16)
    h = both(lambda a, r, tm, name: _matmul_resid([a], [wo1], r, tm=tm, tn=1024, name=name),
             (merged_p, dil_s.astype(BF16)), h, name="l1_out_proj")
    h = mlp(h, 1)
    y_p, y_s = both(lambda x, tm, name: _final_norm(x, norm_final, tm=tm, name=name), h, name="final_norm")

    keep_swa = min(SWA_WINDOW, seq)
    keep_dil = min(max(w for w, _ in DIL_PATTERNS), seq)
    kv4 = lambda a: a.reshape(a.shape[0], N_KV, HEAD_DIM)
    y_prompt = y_p.reshape(1, seq, d)
    y_sample = y_s.reshape(n_b, t_len, d)
    p_conv = jnp.concatenate([tail_x[8 - (SSD_CONV - 1):], tail_bc[8 - (SSD_CONV - 1):]], axis=1)[None, None]
    p_swa_k = kv4(krot0_p[seq - keep_swa:])[None, None]
    p_swa_v = kv4(p0_p[seq - keep_swa:, off0["v"]:off0["v"] + d_kv])[None, None]
    p_dil_k = kv4(krot1_p[seq - keep_dil:])[None, None]
    p_dil_v = kv4(p1_p[seq - keep_dil:, v_off:v_off + d_kv])[None, None]
    xbc_s = jnp.concatenate([p0_s[:, off0["xs"]:off0["xs"] + d], p0_s[:, off0["bc"]:off0["bc"] + d_bc]], axis=1)
    s_conv = xbc_s.reshape(n_b, t_len, conv_dim)[:, t_len - (SSD_CONV - 1):][None]
    bt = lambda a: a.reshape(n_b, t_len, N_KV, HEAD_DIM)[None]
    s_swa_k = bt(krot0_s)
    s_swa_v = bt(p0_s[:, off0["v"]:off0["v"] + d_kv])
    s_dil_k = bt(krot1_s)
    s_dil_v = bt(p1_s[:, v_off:v_off + d_kv])
    return (y_prompt, y_sample, p_conv, p_ssm[None, None], p_swa_k, p_swa_v, p_dil_k, p_dil_v,
            s_conv, s_ssm[None], s_swa_k, s_swa_v, s_dil_k, s_dil_v)
```

```python
import functools
import math

import jax
import jax.numpy as jnp
from jax import lax
from jax.experimental import pallas as pl
from jax.experimental.pallas import tpu as pltpu

F32 = jnp.float32
BF16 = jnp.bfloat16

NORM_EPS = 1e-5
HEAD_DIM = 64
ROT_HALF = 8
ROPE_THETA = 500000.0
PAST_LEN = 8192
N_HEADS = 32
N_KV = 8
REP = N_HEADS // N_KV
SSD_GROUPS = 4
SSD_STATE = 128
SSD_CONV = 4
SSD_CHUNK = 128
SWA_WINDOW = 128
DIL_PATTERNS = ((128, 1), (512, 4), (2048, 16))
BAND = 128
ATT_SCALE = HEAD_DIM ** -0.5
LOG2E = math.log2(math.e)
LN2 = math.log(2.0)
NEG = -1e30
UNITS_IN_FLIGHT = 4
_LOG2_HEAD_DIM = 6

LANES = 128
BF16_ROWS = 16
PERM_CHUNK = 256
VMEM_LIMIT_BYTES = 56 * 1024 * 1024

_NT = (((1,), (1,)), ((), ()))
_TN = (((0,), (0,)), ((), ()))


def _cparams(n_axes):
    return pltpu.CompilerParams(dimension_semantics=("arbitrary",) * n_axes,
                                vmem_limit_bytes=VMEM_LIMIT_BYTES)


def _silu(x):
    return x * jax.nn.sigmoid(x)


def _tile_lanes(t, width):
    k = width // t.shape[1]
    return t if k == 1 else jnp.concatenate([t] * k, axis=1)


def _rope(x, cos_t, sa_t, sb_t):
    w = x.shape[1]
    return (x * _tile_lanes(cos_t, w)
            + pltpu.roll(x, w - ROT_HALF, 1) * _tile_lanes(sa_t, w)
            + pltpu.roll(x, ROT_HALF, 1) * _tile_lanes(sb_t, w))


def _low_half(shape):
    return (lax.broadcasted_iota(jnp.int32, shape, 1) & (LANES - 1)) < HEAD_DIM


def _rope_table_kernel(cos_ref, sa_ref, sb_ref, *, tm, seq, dec_seq):
    i = pl.program_id(0)
    row = i * tm + lax.broadcasted_iota(jnp.int32, (tm, LANES), 0)
    lane = lax.broadcasted_iota(jnp.int32, (tm, LANES), 1)
    pos = jnp.where(row < seq, row, PAST_LEN + ((row - seq) & (dec_seq - 1)))
    c = lane & (HEAD_DIM - 1)
    f = (c & (ROT_HALF - 1)).astype(F32)
    inv_freq = jnp.exp(f * (-math.log(ROPE_THETA) / ROT_HALF))
    ang = pos.astype(F32) * inv_freq
    cs = jnp.cos(ang)
    sn = jnp.sin(ang)
    cos_ref[...] = jnp.where(c < 2 * ROT_HALF, cs, 1.0)
    sa_ref[...] = jnp.where(c < ROT_HALF, -sn, 0.0)
    sb_ref[...] = jnp.where((c >= ROT_HALF) & (c < 2 * ROT_HALF), sn, 0.0)


def _rope_tables(m, seq, dec_seq, tm):
    assert dec_seq & (dec_seq - 1) == 0 and m % tm == 0
    shp = jax.ShapeDtypeStruct((m, LANES), F32)
    spec = pl.BlockSpec((tm, LANES), lambda i: (i, 0))
    return pl.pallas_call(
        functools.partial(_rope_table_kernel, tm=tm, seq=seq, dec_seq=dec_seq),
        out_shape=(shp, shp, shp), grid=(m // tm,), out_specs=(spec, spec, spec),
        compiler_params=_cparams(1), name="rope_tables")()


def _rmsnorm_bf16(x, g):
    ms = jnp.mean(x * x, axis=-1, keepdims=True)
    return (x * lax.rsqrt(ms + NORM_EPS) * g).astype(BF16)


def _norm_mm_kernel(x_ref, g_ref, w_ref, o_ref, xn_ref):
    @pl.when(pl.program_id(1) == 0)
    def _():
        xn_ref[...] = _rmsnorm_bf16(x_ref[...], g_ref[...])

    o_ref[...] = jnp.dot(xn_ref[...], w_ref[...], preferred_element_type=F32).astype(o_ref.dtype)


def _norm_matmul(x, g, w, *, tm, tn, name):
    m, k = x.shape
    n = w.shape[1]
    assert m % tm == 0 and n % tn == 0 and w.shape[0] == k
    return pl.pallas_call(
        _norm_mm_kernel,
        out_shape=jax.ShapeDtypeStruct((m, n), F32),
        grid=(m // tm, n // tn),
        in_specs=[pl.BlockSpec((tm, k), lambda i, j: (i, 0)),
                  pl.BlockSpec((1, k), lambda i, j: (0, 0)),
                  pl.BlockSpec((k, tn), lambda i, j: (0, j))],
        out_specs=pl.BlockSpec((tm, tn), lambda i, j: (i, j)),
        scratch_shapes=[pltpu.VMEM((tm, k), BF16)],
        compiler_params=_cparams(2), name=name)(x, g.reshape(1, k), w)


def _norm_mm_multi_kernel(*refs, bounds):
    n_w = len(bounds)
    x_ref, g_ref = refs[:2]
    w_refs = refs[2:2 + n_w]
    ws_ref, o_ref, os_ref, xn_ref = refs[2 + n_w:]
    j = pl.program_id(1)

    @pl.when(j == 0)
    def _():
        xn_ref[...] = _rmsnorm_bf16(x_ref[...], g_ref[...])
        os_ref[...] = jnp.dot(xn_ref[...], ws_ref[...], preferred_element_type=F32)

    for w_ref, (lo, hi) in zip(w_refs, bounds):
        @pl.when((j >= lo) & (j < hi))
        def _():
            o_ref[...] = jnp.dot(xn_ref[...], w_ref[...], preferred_element_type=F32)


def _norm_matmul_multi(x, g, ws, w_small, *, tm, tn, name):
    m, k = x.shape
    assert m % tm == 0 and all(w.shape[1] % tn == 0 and w.shape[0] == k for w in ws)
    counts = [w.shape[1] // tn for w in ws]
    starts = [sum(counts[:s]) for s in range(len(ws))]
    bounds = tuple((a, a + c) for a, c in zip(starts, counts))
    w_specs = [pl.BlockSpec((k, tn), functools.partial(lambda i, j, a, c: (0, jnp.clip(j - a, 0, c - 1)), a=a, c=c))
               for a, c in zip(starts, counts)]
    n_small = w_small.shape[1]
    return pl.pallas_call(
        functools.partial(_norm_mm_multi_kernel, bounds=bounds),
        out_shape=(jax.ShapeDtypeStruct((m, sum(counts) * tn), F32), jax.ShapeDtypeStruct((m, n_small), F32)),
        grid=(m // tm, sum(counts)),
        in_specs=[pl.BlockSpec((tm, k), lambda i, j: (i, 0)), pl.BlockSpec((1, k), lambda i, j: (0, 0))] + w_specs
        + [pl.BlockSpec((k, n_small), lambda i, j: (0, 0))],
        out_specs=(pl.BlockSpec((tm, tn), lambda i, j: (i, j)), pl.BlockSpec((tm, n_small), lambda i, j: (i, 0))),
        scratch_shapes=[pltpu.VMEM((tm, k), BF16)],
        compiler_params=_cparams(2), name=name)(x, g.reshape(1, k), *ws, w_small)


def _mm_resid_kernel(*refs, n_in):
    x_refs, w_refs = refs[:n_in], refs[n_in:2 * n_in]
    r_ref, o_ref = refs[2 * n_in], refs[2 * n_in + 1]
    acc = r_ref[...]
    for x_ref, w_ref in zip(x_refs, w_refs):
        acc = acc + jnp.dot(x_ref[...], w_ref[...], preferred_element_type=F32)
    o_ref[...] = acc


def _matmul_resid(xs, ws, resid, *, tm, tn, name):
    m, n = resid.shape
    assert m % tm == 0 and n % tn == 0
    in_specs = [pl.BlockSpec((tm, x.shape[1]), lambda i, j: (i, 0)) for x in xs]
    in_specs += [pl.BlockSpec((w.shape[0], tn), lambda i, j: (0, j)) for w in ws]
    in_specs.append(pl.BlockSpec((tm, tn), lambda i, j: (i, j)))
    return pl.pallas_call(
        functools.partial(_mm_resid_kernel, n_in=len(xs)),
        out_shape=jax.ShapeDtypeStruct((m, n), F32),
        grid=(m // tm, n // tn),
        in_specs=in_specs,
        out_specs=pl.BlockSpec((tm, tn), lambda i, j: (i, j)),
        compiler_params=_cparams(2), name=name)(*xs, *ws, resid)


def _mlp_kernel(x_ref, g_ref, w1_ref, w2_ref, o_ref, xn_ref):
    @pl.when(pl.program_id(1) == 0)
    def _():
        x = x_ref[...]
        xn_ref[...] = _rmsnorm_bf16(x, g_ref[...])
        o_ref[...] = x

    u = jnp.dot(xn_ref[...], w1_ref[...], preferred_element_type=F32)
    u = jnp.square(jnp.maximum(u, 0.0)).astype(BF16)
    o_ref[...] += jnp.dot(u, w2_ref[...], preferred_element_type=F32)


def _mlp(x, g, w1, w2, *, tm, tk, name):
    m, d = x.shape
    hid = w1.shape[1]
    assert m % tm == 0 and hid % tk == 0
    return pl.pallas_call(
        _mlp_kernel,
        out_shape=jax.ShapeDtypeStruct((m, d), F32),
        grid=(m // tm, hid // tk),
        in_specs=[pl.BlockSpec((tm, d), lambda i, k: (i, 0)),
                  pl.BlockSpec((1, d), lambda i, k: (0, 0)),
                  pl.BlockSpec((d, tk), lambda i, k: (0, k)),
                  pl.BlockSpec((tk, d), lambda i, k: (k, 0))],
        out_specs=pl.BlockSpec((tm, d), lambda i, k: (i, 0)),
        scratch_shapes=[pltpu.VMEM((tm, d), BF16)],
        compiler_params=_cparams(2), name=name)(x, g.reshape(1, d), w1, w2)


def _final_norm_kernel(x_ref, g_ref, o_ref):
    x = x_ref[...]
    ms = jnp.mean(x * x, axis=-1, keepdims=True)
    o_ref[...] = x * lax.rsqrt(ms + NORM_EPS) * g_ref[...]


def _final_norm(x, g, *, tm, name):
    m, k = x.shape
    return pl.pallas_call(
        _final_norm_kernel, out_shape=jax.ShapeDtypeStruct((m, k), F32), grid=(m // tm,),
        in_specs=[pl.BlockSpec((tm, k), lambda i: (i, 0)), pl.BlockSpec((1, k), lambda i: (0, 0))],
        out_specs=pl.BlockSpec((tm, k), lambda i: (i, 0)),
        compiler_params=_cparams(1), name=name)(x, g.reshape(1, k))


def _causal_conv(raw, tail, w_ref, b_ref):
    n = raw.shape[0]
    full = jnp.concatenate([tail, raw], axis=0)
    acc = b_ref[...]
    for i in range(SSD_CONV):
        s = SSD_CONV - 1 - i
        acc = acc + full[8 - s:8 - s + n] * w_ref[i:i + 1, :]
    return _silu(acc)


def _gate_groupnorm(y, z, gn):
    y = y * _silu(z)
    gw = y.shape[1] // SSD_GROUPS
    outs = []
    for g in range(SSD_GROUPS):
        yg = y[:, g * gw:(g + 1) * gw]
        ms = jnp.mean(yg * yg, axis=-1, keepdims=True)
        outs.append(yg * lax.rsqrt(ms + NORM_EPS))
    return jnp.concatenate(outs, axis=1) * gn


def _ssd_prompt_kernel(z_ref, xs_ref, bc_ref, dt_ref, wx_ref, wbc_ref, bx_ref, bbc_ref, dtb_ref, alog_ref,
                       dskip_ref, gn_ref, eexp_ref, y_ref, tailx_ref, tailbc_ref, h_ref, y_s):
    c = pl.program_id(0)
    q = SSD_CHUNK
    gs = SSD_STATE
    hp = HEAD_DIM

    @pl.when(c == 0)
    def _():
        tailx_ref[...] = jnp.zeros_like(tailx_ref)
        tailbc_ref[...] = jnp.zeros_like(tailbc_ref)
        h_ref[...] = jnp.zeros_like(h_ref)

    xs_raw = xs_ref[...]
    bc_raw = bc_ref[...]
    xs_c = _causal_conv(xs_raw, tailx_ref[...], wx_ref, bx_ref)
    bc_c = _causal_conv(bc_raw, tailbc_ref[...], wbc_ref, bbc_ref)
    tailx_ref[...] = xs_raw[q - 8:q]
    tailbc_ref[...] = bc_raw[q - 8:q]

    lane = lax.broadcasted_iota(jnp.int32, (q, LANES), 1)
    row = lax.broadcasted_iota(jnp.int32, (q, LANES), 0)
    dt = jax.nn.softplus(dt_ref[...] + dtb_ref[...])
    a = -jnp.exp(alog_ref[...])
    la = jnp.where(lane < N_HEADS, dt * a, 0.0)
    acs = la
    s = 1
    while s < q:
        acs = acs + jnp.where(row >= s, pltpu.roll(acs, s, 0), 0.0)
        s *= 2
    acs_t = acs.T
    acs_last = acs[q - 1:q, :]
    to_end = jnp.exp(acs_last - acs)
    eacs = jnp.exp(acs)
    cdec = jnp.exp(acs_last)
    causal = (lax.broadcasted_iota(jnp.int32, (q, q), 0) >= lax.broadcasted_iota(jnp.int32, (q, q), 1))

    stacked = jnp.concatenate([dt, eacs, to_end], axis=0)
    hi = stacked.astype(BF16)
    lo = (stacked - hi.astype(F32)).astype(BF16)
    e = eexp_ref[...]
    full = jnp.dot(hi, e, preferred_element_type=F32) + jnp.dot(lo, e, preferred_element_type=F32)
    xdt = xs_c * full[0:q]
    eacs_f = full[q:2 * q]
    w_f = xdt * full[2 * q:3 * q]
    low = _low_half((q, LANES))
    top = lax.broadcasted_iota(jnp.int32, (2 * hp, gs), 0) < hp

    rep = N_HEADS // SSD_GROUPS
    for g in range(SSD_GROUPS):
        bg = bc_c[:, g * gs:(g + 1) * gs].astype(BF16)
        cg = bc_c[:, SSD_GROUPS * gs + g * gs:SSD_GROUPS * gs + (g + 1) * gs].astype(BF16)
        cb = lax.dot_general(cg, bg, _NT, preferred_element_type=F32)
        for rp in range(rep // 2):
            h0 = g * rep + 2 * rp
            sl = slice(h0 * hp, (h0 + 2) * hp)
            ms = []
            for h in (h0, h0 + 1):
                lmat = jnp.exp(jnp.where(causal, acs[:, h:h + 1] - acs_t[h:h + 1, :], -jnp.inf))
                ms.append((cb * lmat).astype(BF16))
            xp = xdt[:, sl]
            rhs = jnp.concatenate([jnp.where(low, xp, 0.0), jnp.where(low, 0.0, xp)], axis=0).astype(BF16)
            y = jnp.dot(jnp.concatenate(ms, axis=1), rhs, preferred_element_type=F32)
            hs = h_ref[h0:h0 + 2].reshape(2 * hp, gs)
            y = y + eacs_f[:, sl] * lax.dot_general(cg, hs.astype(BF16), _NT, preferred_element_type=F32)
            st = lax.dot_general(w_f[:, sl].astype(BF16), bg, _TN, preferred_element_type=F32)
            cd = jnp.where(top, cdec[:, h0:h0 + 1], cdec[:, h0 + 1:h0 + 2])
            h_ref[h0:h0 + 2] = (hs * cd + st).reshape(2, hp, gs)
            y_s[:, sl] = y + dskip_ref[:, sl] * xs_c[:, sl]

    y_ref[...] = _gate_groupnorm(y_s[...], z_ref[...], gn_ref[...]).astype(y_ref.dtype)


def _ssd_prompt(p0, dtr, off, seq, wx, wbc, bx, bbc, dtb, alog, dskip, gn, eexp):
    q = SSD_CHUNK
    d_in = wx.shape[1]
    d_bc = wbc.shape[1]
    assert seq % q == 0
    const = lambda shape: pl.BlockSpec(shape, lambda c: (0,) * len(shape))
    return pl.pallas_call(
        _ssd_prompt_kernel,
        out_shape=(jax.ShapeDtypeStruct((seq, d_in), BF16),
                   jax.ShapeDtypeStruct((8, d_in), F32),
                   jax.ShapeDtypeStruct((8, d_bc), F32),
                   jax.ShapeDtypeStruct((N_HEADS, HEAD_DIM, SSD_STATE), F32)),
        grid=(seq // q,),
        in_specs=[pl.BlockSpec((q, d_in), lambda c: (c, off["z"] // d_in)),
                  pl.BlockSpec((q, d_in), lambda c: (c, off["xs"] // d_in)),
                  pl.BlockSpec((q, d_bc), lambda c: (c, off["bc"] // d_bc)),
                  pl.BlockSpec((q, LANES), lambda c: (c, 0)),
                  const((SSD_CONV, d_in)), const((SSD_CONV, d_bc)), const((1, d_in)), const((1, d_bc)),
                  const((1, LANES)), const((1, LANES)), const((1, d_in)), const((1, d_in)), const((LANES, d_in))],
        out_specs=(pl.BlockSpec((q, d_in), lambda c: (c, 0)),
                   const((8, d_in)), const((8, d_bc)), const((N_HEADS, HEAD_DIM, SSD_STATE))),
        scratch_shapes=[pltpu.VMEM((q, d_in), F32)],
        compiler_params=_cparams(1), name="ssd_prompt")(p0, p0, p0, dtr, wx, wbc, bx, bbc, dtb, alog, dskip, gn, eexp)


def _split3(x):
    hi = x.astype(BF16)
    r1 = x - hi.astype(F32)
    mid = r1.astype(BF16)
    lo = (r1 - mid.astype(F32)).astype(BF16)
    return hi, mid, lo


def _ssd_sample_kernel(z_ref, xs_ref, bc_ref, dt_ref, ex_ref, ebc_ref, st_ref, wx_ref, wbc_ref, bx_ref, bbc_ref,
                       dtb_ref, alog_ref, dskip_ref, gn_ref, eexp_ref, y_ref, sto_ref, *, nb, t_len):
    rows = nb * t_len
    gs = SSD_STATE
    gw = xs_ref.shape[1] // SSD_GROUPS
    rep = N_HEADS // SSD_GROUPS

    def tcol(width):
        r = lax.broadcasted_iota(jnp.int32, (rows, width), 0)
        return r & (t_len - 1), r >> int(math.log2(t_len))

    def conv(raw, est, w_ref, b_ref):
        t, _ = tcol(raw.shape[1])
        acc = b_ref[...]
        for i in range(SSD_CONV):
            s = SSD_CONV - 1 - i
            if s == 0:
                sh = raw
            else:
                k = (rows - (SSD_CONV - 1) + s) % rows
                sh = jnp.where(t >= s, pltpu.roll(raw, s, 0), est if k == 0 else pltpu.roll(est, k, 0))
            acc = acc + sh * w_ref[i:i + 1, :]
        return _silu(acc)

    xs_c = conv(xs_ref[...], ex_ref[...], wx_ref, bx_ref)
    bc_c = conv(bc_ref[...], ebc_ref[...], wbc_ref, bbc_ref)
    bm = bc_c[:, :SSD_GROUPS * gs]
    cm = bc_c[:, SSD_GROUPS * gs:]

    t1, _ = tcol(LANES)
    lane = lax.broadcasted_iota(jnp.int32, (rows, LANES), 1)
    dt = jax.nn.softplus(dt_ref[...] + dtb_ref[...])
    a = -jnp.exp(alog_ref[...])
    la = jnp.where(lane < N_HEADS, dt * a, 0.0)
    acs = la
    for s in range(1, t_len):
        acs = acs + jnp.where(t1 >= s, pltpu.roll(la, s, 0), 0.0)
    alast = jnp.where(t1 == t_len - 1, acs, 0.0)
    for u in range(1, t_len):
        alast = alast + jnp.where(t1 == t_len - 1 - u, pltpu.roll(acs, rows - u, 0), 0.0)
    parts = [dt, jnp.exp(acs), jnp.exp(alast - acs), jnp.exp(alast)]
    for k in range(1, t_len):
        parts.append(jnp.exp(acs - pltpu.roll(acs, k, 0)))
    stacked = jnp.concatenate(parts, axis=0)
    hi, mid, lo = _split3(stacked)
    e = eexp_ref[...]
    full = (jnp.dot(hi, e, preferred_element_type=F32) + jnp.dot(mid, e, preferred_element_type=F32)
            + jnp.dot(lo, e, preferred_element_type=F32))
    dt_f, eacs_f, toend_f, cdec_f = (full[i * rows:(i + 1) * rows] for i in range(4))
    dec_f = [None] + [full[(3 + k) * rows:(4 + k) * rows] for k in range(1, t_len)]

    tw, _ = tcol(xs_c.shape[1])
    xdt = xs_c * dt_f
    y = jnp.zeros_like(xs_c)
    for k in range(t_len):
        bmk = bm if k == 0 else pltpu.roll(bm, k, 0)
        prod = cm * bmk
        cbs = []
        for g in range(SSD_GROUPS):
            sg = jnp.sum(prod[:, g * gs:(g + 1) * gs], axis=-1, keepdims=True)
            cbs.append(jnp.broadcast_to(sg, (rows, gw)))
        cb_f = jnp.concatenate(cbs, axis=1)
        if k == 0:
            y = y + cb_f * xdt
        else:
            y = y + jnp.where(tw >= k, cb_f * dec_f[k] * pltpu.roll(xdt, k, 0), 0.0)

    cm16 = cm.astype(BF16)
    bm16 = bm.astype(BF16)
    w_f = (toend_f * xdt)
    _, bg_ = tcol(gw)
    ones_blk = jnp.ones((3, gs), BF16)
    yoffs = []
    for g in range(SSD_GROUPS):
        cg = cm16[:, g * gs:(g + 1) * gs]
        rhs_top = jnp.concatenate([bm16[:, g * gs:(g + 1) * gs], jnp.zeros((rows, gs), BF16)], axis=1)
        rhs_mid = jnp.concatenate([jnp.zeros((3, gs), BF16), ones_blk], axis=1)
        rhs = jnp.concatenate([rhs_top, rhs_mid, jnp.zeros((13, 2 * gs), BF16)], axis=0)
        wg = w_f[:, g * gw:(g + 1) * gw]
        cdg = cdec_f[:, g * gw:(g + 1) * gw]
        yg = jnp.zeros((rows, gw), F32)
        for b in range(nb):
            hb = st_ref[b, g * rep:(g + 1) * rep].reshape(gw, gs)
            yb = lax.dot_general(cg, hb.astype(BF16), _NT, preferred_element_type=F32)
            yg = jnp.where(bg_ == b, yb, yg)
            wb = jnp.where(bg_ == b, wg, 0.0).astype(BF16)
            d_hi, d_mid, d_lo = _split3(cdg[b * t_len:b * t_len + 1])
            lhs = jnp.concatenate([wb, d_hi, d_mid, d_lo, jnp.zeros((13, gw), BF16)], axis=0)
            sd = lax.dot_general(lhs, rhs, _TN, preferred_element_type=F32)
            sto_ref[b, g * rep:(g + 1) * rep] = (hb * sd[:, gs:] + sd[:, :gs]).reshape(rep, HEAD_DIM, gs)
        yoffs.append(yg)
    y = y + eacs_f * jnp.concatenate(yoffs, axis=1) + dskip_ref[...] * xs_c
    y_ref[...] = _gate_groupnorm(y, z_ref[...], gn_ref[...]).astype(y_ref.dtype)


def _ssd_sample(p0, dtr, off, seq, n_b, t_len, epad, state, wx, wbc, bx, bbc, dtb, alog, dskip, gn, eexp, *, nb):
    rows = nb * t_len
    d_in = wx.shape[1]
    d_bc = wbc.shape[1]
    assert n_b % nb == 0 and t_len & (t_len - 1) == 0
    const = lambda shape: pl.BlockSpec(shape, lambda i: (0,) * len(shape))
    st_spec = pl.BlockSpec((nb, N_HEADS, HEAD_DIM, SSD_STATE), lambda i: (i, 0, 0, 0))
    return pl.pallas_call(
        functools.partial(_ssd_sample_kernel, nb=nb, t_len=t_len),
        out_shape=(jax.ShapeDtypeStruct((n_b * t_len, d_in), BF16),
                   jax.ShapeDtypeStruct(state.shape, F32)),
        grid=(n_b // nb,),
        in_specs=[pl.BlockSpec((rows, d_in), lambda i: (i, off["z"] // d_in)),
                  pl.BlockSpec((rows, d_in), lambda i: (i, off["xs"] // d_in)),
                  pl.BlockSpec((rows, d_bc), lambda i: (i, off["bc"] // d_bc)),
                  pl.BlockSpec((rows, LANES), lambda i: (i, 0)),
                  pl.BlockSpec((rows, d_in), lambda i: (i, 0)),
                  pl.BlockSpec((rows, d_bc), lambda i: (i, d_in // d_bc)),
                  st_spec,
                  const((SSD_CONV, d_in)), const((SSD_CONV, d_bc)), const((1, d_in)), const((1, d_bc)),
                  const((1, LANES)), const((1, LANES)), const((1, d_in)), const((1, d_in)),
                  const((LANES, d_in))],
        out_specs=(pl.BlockSpec((rows, d_in), lambda i: (i, 0)), st_spec),
        compiler_params=_cparams(1), name="ssd_sample")(
            p0, p0, p0, dtr, epad, epad, state, wx, wbc, bx, bbc, dtb, alog, dskip, gn, eexp)


def _perm_matrix(dil):
    i = jnp.arange(PERM_CHUNK)
    blk = BF16_ROWS * dil
    src = (i // blk) * blk + (i % BF16_ROWS) * dil + (i % blk) // BF16_ROWS
    return (src[:, None] == jnp.arange(PERM_CHUNK)[None, :]).astype(BF16)


def _band_kernel(*refs, dil, max_dist, use_sink, want_lse, want_krot, nkv, nj, res_per_iter):
    it = iter(refs)
    q_ref, k_ref, v_ref, cos_ref, sa_ref, sb_ref = (next(it) for _ in range(6))
    p_ref, pt_ref = (next(it), next(it)) if dil > 1 else (None, None)
    sink_ref = next(it) if use_sink else None
    o_ref = next(it)
    lse_ref = next(it) if want_lse else None
    krot_ref = next(it) if want_krot else None
    qlo_s, qhi_s, kd_s, vd_s, kdp_s, vdp_s, op_s = (next(it) for _ in range(7))
    lsep_s = next(it) if want_lse else None
    lse_s = next(it) if (want_lse and nj > 1) else None

    c = pl.program_id(0)
    j = pl.program_id(1)
    rows, wq = q_ref.shape
    wk = k_ref.shape[1]
    n_chunk = rows // PERM_CHUNK if dil > 1 else 0
    n_grp = BAND // BF16_ROWS

    def perm(x):
        if dil == 1:
            return x
        x16 = x.astype(BF16)
        return jnp.concatenate(
            [jnp.dot(p_ref[...], x16[k * PERM_CHUNK:(k + 1) * PERM_CHUNK], preferred_element_type=F32)
             for k in range(n_chunk)], axis=0)

    def unperm(xp16):
        if dil == 1:
            return [xp16]
        return [jnp.dot(pt_ref[...], xp16[k * PERM_CHUNK:(k + 1) * PERM_CHUNK], preferred_element_type=F32)
                for k in range(n_chunk)]

    tabs = (cos_ref[...], sa_ref[...], sb_ref[...])
    q = perm(_rope(q_ref[...], *tabs) * (ATT_SCALE * LOG2E))
    lowq = _low_half((rows, wq))
    qlo_s[...] = jnp.where(lowq, q, 0.0).astype(BF16)
    qhi_s[...] = jnp.where(lowq, 0.0, q).astype(BF16)
    kr = _rope(k_ref[...], *tabs)
    if want_krot:
        krot_ref[...] = kr
    lowk = _low_half((rows, LANES))
    for x, d_s in ((perm(kr), kd_s), (perm(v_ref[...]), vd_s)):
        up = pltpu.roll(x, HEAD_DIM, 1)
        dn = pltpu.roll(x, wk - HEAD_DIM, 1)
        for t in range(wk // LANES):
            sl = slice(t * LANES, (t + 1) * LANES)
            d_s[2 * t] = jnp.where(lowk, x[:, sl], up[:, sl]).astype(BF16)
            d_s[2 * t + 1] = jnp.where(lowk, dn[:, sl], x[:, sl]).astype(BF16)

    @pl.when(c == 0)
    def _():
        kdp_s[j] = jnp.zeros(kdp_s.shape[1:], BF16)
        vdp_s[j] = jnp.zeros(vdp_s.shape[1:], BF16)

    qi = lax.broadcasted_iota(jnp.int32, (BAND, 2 * BAND), 0)
    kj = lax.broadcasted_iota(jnp.int32, (BAND, 2 * BAND), 1)
    dist = qi - kj + BAND
    bias = jnp.where((dist >= 0) & (dist <= max_dist) & ((kj >= BAND) | (c > 0)), 0.0, NEG)
    col0 = kj == 0
    low = _low_half((BAND, LANES))
    lane = lax.broadcasted_iota(jnp.int32, (BAND, LANES), 1)
    step = BF16_ROWS * dil

    def residue(rho):
        starts = [pl.multiple_of(b * step + rho * BF16_ROWS, BF16_ROWS) for b in range(n_grp)]

        def gather(get):
            return jnp.concatenate([get(s) for s in starts], axis=0)

        tiles = []
        lse_acc = jnp.zeros((BAND, LANES), F32)
        for g in range(nkv):
            ca, cb = slice(2 * g * LANES, (2 * g + 1) * LANES), slice((2 * g + 1) * LANES, (2 * g + 2) * LANES)
            lhs = jnp.concatenate([gather(lambda s: qlo_s[pl.ds(s, BF16_ROWS), ca]),
                                   gather(lambda s: qhi_s[pl.ds(s, BF16_ROWS), ca]),
                                   gather(lambda s: qlo_s[pl.ds(s, BF16_ROWS), cb]),
                                   gather(lambda s: qhi_s[pl.ds(s, BF16_ROWS), cb])], axis=0)
            kcat = jnp.concatenate([gather(lambda s: kdp_s[j, g, pl.ds(s, BF16_ROWS), :]),
                                    gather(lambda s: kd_s[g, pl.ds(s, BF16_ROWS), :])], axis=0)
            vcat = jnp.concatenate([gather(lambda s: vdp_s[j, g, pl.ds(s, BF16_ROWS), :]),
                                    gather(lambda s: vd_s[g, pl.ds(s, BF16_ROWS), :])], axis=0)
            head0 = j * REP * nkv + g * REP
            s_full = lax.dot_general(lhs, kcat, _NT, preferred_element_type=F32)
            rsl = [slice(r * BAND, (r + 1) * BAND) for r in range(REP)]
            if use_sink:
                sc = jnp.concatenate([jnp.where(col0, sink_ref[head0 + r] * LOG2E, s_full[rsl[r]] + bias)
                                      for r in range(REP)], axis=0)
            else:
                sc = s_full + jnp.concatenate([bias] * REP, axis=0)
            m = jnp.max(sc, axis=-1, keepdims=True)
            e = jnp.exp2(sc - m)
            den = jnp.sum(e, axis=-1, keepdims=True)
            o = jnp.dot(e.astype(BF16), vcat, preferred_element_type=F32) * (1.0 / den)
            tiles.append(jnp.where(low, o[0:BAND], o[BAND:2 * BAND]))
            tiles.append(jnp.where(low, o[2 * BAND:3 * BAND], o[3 * BAND:4 * BAND]))
            if want_lse:
                lse = (m + jnp.log2(den)) * LN2
                for r in range(REP):
                    lse_acc = lse_acc + jnp.where(lane == head0 + r, jnp.broadcast_to(lse[rsl[r]], (BAND, LANES)), 0.0)
        o_res = jnp.concatenate(tiles, axis=1)
        for b, s0 in enumerate(starts):
            op_s[pl.ds(s0, BF16_ROWS), :] = o_res[b * BF16_ROWS:(b + 1) * BF16_ROWS]
            if want_lse:
                lsep_s[pl.ds(s0, BF16_ROWS), :] = lse_acc[b * BF16_ROWS:(b + 1) * BF16_ROWS]

    if dil == 1:
        residue(0)
    else:
        def body(i, carry):
            for u in range(res_per_iter):
                residue(i * res_per_iter + u)
            return carry

        lax.fori_loop(0, dil // res_per_iter, body, 0)

    for k, blk in enumerate(unperm(op_s[...].astype(BF16))):
        if dil == 1:
            o_ref[...] = blk
        else:
            o_ref[k * PERM_CHUNK:(k + 1) * PERM_CHUNK, :] = blk.astype(o_ref.dtype)
    if want_lse:
        if dil == 1:
            lse_nat = lsep_s[...]
        else:
            parts = [unperm(x) for x in _split3(lsep_s[...])]
            lse_nat = jnp.concatenate([a + b + d for a, b, d in zip(*parts)], axis=0)
        if nj == 1:
            lse_ref[...] = lse_nat
        else:
            @pl.when(j == 0)
            def _():
                lse_s[...] = lse_nat

            @pl.when(j > 0)
            def _():
                lse_s[...] = lse_s[...] + lse_nat

            @pl.when(j == nj - 1)
            def _():
                lse_ref[...] = lse_s[...]
    kdp_s[j] = kd_s[...]
    if use_sink:
        keep = (lax.broadcasted_iota(jnp.int32, (rows, LANES), 0) > 0).astype(F32).astype(BF16)
        vdp_s[j] = vd_s[...] * keep
    else:
        vdp_s[j] = vd_s[...]


def _band_attention(p, seq, q_off, k_off, v_off, tables, sinks, *, dil, max_dist, nkv, want_lse, want_krot, name):
    rows = BAND * dil
    nj = N_KV // nkv
    qw, kw = nkv * REP * HEAD_DIM, nkv * HEAD_DIM
    assert seq % rows == 0 and kw % LANES == 0 and q_off % qw == 0 and k_off % kw == 0 and v_off % kw == 0
    assert dil == 1 or (rows % PERM_CHUNK == 0 and PERM_CHUNK % (BF16_ROWS * dil) == 0)
    use_sink = sinks is not None
    d_att = N_HEADS * HEAD_DIM
    res_per_iter = 1 if dil == 1 else max(1, UNITS_IN_FLIGHT // nkv)
    assert dil % res_per_iter == 0 and (not use_sink or (dil == 1 and max_dist < BAND))
    blk = lambda width, cb: pl.BlockSpec((rows, width), lambda c, j: (c, cb + j))
    t_spec = pl.BlockSpec((rows, LANES), lambda c, j: (c, 0))
    in_specs = [blk(qw, q_off // qw), blk(kw, k_off // kw), blk(kw, v_off // kw), t_spec, t_spec, t_spec]
    args = [p, p, p] + list(tables)
    if dil > 1:
        pm = _perm_matrix(dil)
        in_specs += [pl.BlockSpec((PERM_CHUNK, PERM_CHUNK), lambda c, j: (0, 0))] * 2
        args += [pm, pm.T]
    if use_sink:
        in_specs.append(pl.BlockSpec(memory_space=pltpu.SMEM))
        args.append(sinks)
    out_shape = [jax.ShapeDtypeStruct((seq, d_att), BF16)]
    out_specs = [pl.BlockSpec((rows, qw), lambda c, j: (c, j))]
    scratch = [pltpu.VMEM((rows, qw), BF16), pltpu.VMEM((rows, qw), BF16),
               pltpu.VMEM((nkv, rows, LANES), BF16), pltpu.VMEM((nkv, rows, LANES), BF16),
               pltpu.VMEM((nj, nkv, rows, LANES), BF16), pltpu.VMEM((nj, nkv, rows, LANES), BF16),
               pltpu.VMEM((rows, qw), F32)]
    if want_lse:
        out_shape.append(jax.ShapeDtypeStruct((seq, LANES), F32))
        out_specs.append(pl.BlockSpec((rows, LANES), lambda c, j: (c, 0)))
        scratch.append(pltpu.VMEM((rows, LANES), F32))
        if nj > 1:
            scratch.append(pltpu.VMEM((rows, LANES), F32))
    if want_krot:
        out_shape.append(jax.ShapeDtypeStruct((seq, N_KV * HEAD_DIM), F32))
        out_specs.append(pl.BlockSpec((rows, kw), lambda c, j: (c, j)))
    return pl.pallas_call(
        functools.partial(_band_kernel, dil=dil, max_dist=max_dist, use_sink=use_sink, want_lse=want_lse,
                          want_krot=want_krot, nkv=nkv, nj=nj, res_per_iter=res_per_iter),
        out_shape=tuple(out_shape), grid=(seq // rows, nj), in_specs=in_specs, out_specs=tuple(out_specs),
        scratch_shapes=scratch, compiler_params=_cparams(2), name=name)(*args)


def _merge_kernel(o1_ref, o2_ref, o3_ref, l1_ref, l2_ref, l3_ref, e_ref, out_ref):
    ls = [l1_ref[...], l2_ref[...], l3_ref[...]]
    mx = jnp.maximum(jnp.maximum(ls[0], ls[1]), ls[2])
    es = [jnp.exp(l - mx) for l in ls]
    inv = 1.0 / (es[0] + es[1] + es[2])
    e = e_ref[...]
    acc = None
    for en, o_ref in zip(es, (o1_ref, o2_ref, o3_ref)):
        w = en * inv
        hi = w.astype(BF16)
        lo = (w - hi.astype(F32)).astype(BF16)
        wf = jnp.dot(hi, e, preferred_element_type=F32) + jnp.dot(lo, e, preferred_element_type=F32)
        term = wf * o_ref[...].astype(F32)
        acc = term if acc is None else acc + term
    out_ref[...] = acc.astype(out_ref.dtype)


def _merge(os_, ls_, eexp, *, tm):
    seq, d = os_[0].shape
    o_spec = pl.BlockSpec((tm, d), lambda i: (i, 0))
    l_spec = pl.BlockSpec((tm, LANES), lambda i: (i, 0))
    return pl.pallas_call(
        _merge_kernel, out_shape=jax.ShapeDtypeStruct((seq, d), BF16), grid=(seq // tm,),
        in_specs=[o_spec] * 3 + [l_spec] * 3 + [pl.BlockSpec((LANES, d), lambda i: (0, 0))],
        out_specs=o_spec, compiler_params=_cparams(1), name="dil_merge")(*os_, *ls_, eexp)


def _expand_q(qb, t_len):
    gd = N_KV * HEAD_DIM
    sub = lax.broadcasted_iota(jnp.int32, (N_KV, gd), 0)
    lg = lax.broadcasted_iota(jnp.int32, (N_KV, gd), 1) >> _LOG2_HEAD_DIM
    diag = sub == lg
    tiles = []
    for r in range(REP):
        vr = jnp.concatenate(
            [qb[:, (g * REP + r) * HEAD_DIM:(g * REP + r + 1) * HEAD_DIM] for g in range(N_KV)], axis=1)
        for t in range(t_len):
            tiles.append(jnp.where(diag, jnp.broadcast_to(vr[t:t + 1], (N_KV, gd)), 0.0))
    return jnp.concatenate(tiles, axis=0).astype(BF16)


def _collapse_o(r_full, t_len):
    gd = N_KV * HEAD_DIM
    n = r_full.shape[0]
    sub = lax.broadcasted_iota(jnp.int32, (n, gd), 0) & (N_KV - 1)
    lg = lax.broadcasted_iota(jnp.int32, (n, gd), 1) >> _LOG2_HEAD_DIM
    masked = jnp.where(sub == lg, r_full, 0.0)
    red = jnp.sum(masked.reshape(n // N_KV, N_KV, gd), axis=1)
    pieces = []
    for g in range(N_KV):
        for r in range(REP):
            pieces.append(red[r * t_len:(r + 1) * t_len, g * HEAD_DIM:(g + 1) * HEAD_DIM])
    return jnp.concatenate(pieces, axis=1)


def _row_t(shape, t_len):
    r = lax.broadcasted_iota(jnp.int32, shape, 0)
    return (r >> 3) & (t_len - 1)


def _cached_attend(qx, kt, vt, mask_w, kn, vn, mask_n, sink):
    s_w = jnp.where(mask_w, jnp.dot(qx, kt, preferred_element_type=F32), NEG)
    s_n = jnp.where(mask_n, lax.dot_general(qx, kn, _NT, preferred_element_type=F32), NEG)
    m = jnp.maximum(jnp.max(s_w, axis=-1, keepdims=True), jnp.max(s_n, axis=-1, keepdims=True))
    if sink is not None:
        m = jnp.maximum(m, sink)
    e_w = jnp.exp(s_w - m)
    e_n = jnp.exp(s_n - m)
    den = jnp.sum(e_w, axis=-1, keepdims=True) + jnp.sum(e_n, axis=-1, keepdims=True)
    if sink is not None:
        den = den + jnp.exp(sink - m)
    r = (lax.dot_general(e_w.astype(BF16), vt, _NT, preferred_element_type=F32)
         + jnp.dot(e_n.astype(BF16), vn, preferred_element_type=F32))
    return r / den, m + jnp.log(den)


def _swa_cached_kernel(q_ref, kn_ref, vn_ref, cos_ref, sa_ref, sb_ref, ck_ref, cv_ref, sink_ref,
                       o_ref, krot_ref, *, nb, t_len):
    tabs = (cos_ref[...], sa_ref[...], sb_ref[...])
    q = _rope(q_ref[...], *tabs) * ATT_SCALE
    kn = _rope(kn_ref[...], *tabs)
    krot_ref[...] = kn
    kn16 = kn.astype(BF16)
    vn16 = vn_ref[...].astype(BF16)
    lb = ck_ref.shape[3]
    gd = N_KV * HEAD_DIM
    nrow = REP * t_len * N_KV
    tq = _row_t((nrow, lb), t_len)
    mask_w = lax.broadcasted_iota(jnp.int32, (nrow, lb), 1) >= tq + 1
    mask_n = lax.broadcasted_iota(jnp.int32, (nrow, BF16_ROWS), 1) <= _row_t((nrow, BF16_ROWS), t_len)
    sk = sink_ref[...]
    pad = jnp.zeros((BF16_ROWS - t_len, gd), BF16)
    outs = []
    for b in range(nb):
        rs = slice(b * t_len, (b + 1) * t_len)
        r_full, _ = _cached_attend(
            _expand_q(q[rs], t_len), ck_ref[b].reshape(gd, lb).astype(BF16), cv_ref[b].reshape(gd, lb).astype(BF16),
            mask_w, jnp.concatenate([kn16[rs], pad], axis=0), jnp.concatenate([vn16[rs], pad], axis=0), mask_n, sk)
        outs.append(_collapse_o(r_full, t_len))
    o_ref[...] = jnp.concatenate(outs, axis=0).astype(o_ref.dtype)


def _swa_cached(p0, off, seq, n_b, t_len, tables, cache_kt, cache_vt, sink_col, *, nb):
    rows = nb * t_len
    d_att = N_HEADS * HEAD_DIM
    d_kv = N_KV * HEAD_DIM
    lb = cache_kt.shape[3]
    assert lb == SWA_WINDOW and t_len <= BF16_ROWS
    r0 = seq // rows
    nrow = REP * t_len * N_KV
    c_spec = pl.BlockSpec((nb, N_KV, HEAD_DIM, lb), lambda i: (i, 0, 0, 0))
    t_spec = pl.BlockSpec((rows, LANES), lambda i: (r0 + i, 0))
    return pl.pallas_call(
        functools.partial(_swa_cached_kernel, nb=nb, t_len=t_len),
        out_shape=(jax.ShapeDtypeStruct((n_b * t_len, d_att), BF16),
                   jax.ShapeDtypeStruct((n_b * t_len, d_kv), F32)),
        grid=(n_b // nb,),
        in_specs=[pl.BlockSpec((rows, d_att), lambda i: (i, off["q"] // d_att)),
                  pl.BlockSpec((rows, d_kv), lambda i: (i, off["k"] // d_kv)),
                  pl.BlockSpec((rows, d_kv), lambda i: (i, off["v"] // d_kv)),
                  t_spec, t_spec, t_spec, c_spec, c_spec,
                  pl.BlockSpec((nrow, 1), lambda i: (0, 0))],
        out_specs=(pl.BlockSpec((rows, d_att), lambda i: (i, 0)),
                   pl.BlockSpec((rows, d_kv), lambda i: (i, 0))),
        compiler_params=_cparams(1), name="swa_cached")(p0, p0, p0, *tables, cache_kt, cache_vt, sink_col)


def _dil_cached_kernel(q1_ref, q2_ref, q3_ref, kn_ref, vn_ref, cos_ref, sa_ref, sb_ref, ck_ref, cv_ref,
                       o_ref, krot_ref, *, nb, t_len):
    tabs = (cos_ref[...], sa_ref[...], sb_ref[...])
    qs = [_rope(r[...], *tabs) * ATT_SCALE for r in (q1_ref, q2_ref, q3_ref)]
    kn = _rope(kn_ref[...], *tabs)
    krot_ref[...] = kn
    kn16 = kn.astype(BF16)
    vn16 = vn_ref[...].astype(BF16)
    lbuf = ck_ref.shape[3]
    gd = N_KV * HEAD_DIM
    nrow = REP * t_len * N_KV
    tqn = _row_t((nrow, BF16_ROWS), t_len)
    coln = lax.broadcasted_iota(jnp.int32, (nrow, BF16_ROWS), 1)
    masks = []
    for window, dil in DIL_PATTERNS:
        tq = _row_t((nrow, window), t_len)
        col = lax.broadcasted_iota(jnp.int32, (nrow, window), 1)
        if dil == 1:
            masks.append((col >= tq, coln <= tqn))
        else:
            masks.append(((col & (dil - 1)) == tq, coln == tqn))
    pad = jnp.zeros((BF16_ROWS - t_len, gd), BF16)
    outs = []
    for b in range(nb):
        rs = slice(b * t_len, (b + 1) * t_len)
        kt = ck_ref[b].reshape(gd, lbuf).astype(BF16)
        vt = cv_ref[b].reshape(gd, lbuf).astype(BF16)
        knp = jnp.concatenate([kn16[rs], pad], axis=0)
        vnp = jnp.concatenate([vn16[rs], pad], axis=0)
        res = []
        for gi, (window, dil) in enumerate(DIL_PATTERNS):
            res.append(_cached_attend(_expand_q(qs[gi][rs], t_len), kt[:, lbuf - window:], vt[:, lbuf - window:],
                                      masks[gi][0], knp, vnp, masks[gi][1], None))
        mx = jnp.maximum(jnp.maximum(res[0][1], res[1][1]), res[2][1])
        ws = [jnp.exp(l - mx) for _, l in res]
        inv = 1.0 / (ws[0] + ws[1] + ws[2])
        merged = (ws[0] * inv) * res[0][0] + (ws[1] * inv) * res[1][0] + (ws[2] * inv) * res[2][0]
        outs.append(_collapse_o(merged, t_len))
    o_ref[...] = jnp.concatenate(outs, axis=0).astype(o_ref.dtype)


def _dil_cached(p1, seq, n_b, t_len, tables, cache_kt, cache_vt, *, nb):
    rows = nb * t_len
    d_att = N_HEADS * HEAD_DIM
    d_kv = N_KV * HEAD_DIM
    lbuf = cache_kt.shape[3]
    assert rows % 8 == 0 and seq % rows == 0 and t_len <= BF16_ROWS
    for window, dil in DIL_PATTERNS:
        assert window <= lbuf and (PAST_LEN - lbuf) % dil == 0 and lbuf % dil == 0 and (dil == 1 or dil >= t_len)
    r0 = seq // rows
    c_spec = pl.BlockSpec((nb, N_KV, HEAD_DIM, lbuf), lambda i: (i, 0, 0, 0))
    t_spec = pl.BlockSpec((rows, LANES), lambda i: (r0 + i, 0))
    nq = d_att * len(DIL_PATTERNS)
    return pl.pallas_call(
        functools.partial(_dil_cached_kernel, nb=nb, t_len=t_len),
        out_shape=(jax.ShapeDtypeStruct((n_b * t_len, d_att), F32),
                   jax.ShapeDtypeStruct((n_b * t_len, d_kv), F32)),
        grid=(n_b // nb,),
        in_specs=[pl.BlockSpec((rows, d_att), lambda i: (i, 0)),
                  pl.BlockSpec((rows, d_att), lambda i: (i, 1)),
                  pl.BlockSpec((rows, d_att), lambda i: (i, 2)),
                  pl.BlockSpec((rows, d_kv), lambda i: (i, nq // d_kv)),
                  pl.BlockSpec((rows, d_kv), lambda i: (i, nq // d_kv + 1)),
                  t_spec, t_spec, t_spec, c_spec, c_spec],
        out_specs=(pl.BlockSpec((rows, d_att), lambda i: (i, 0)),
                   pl.BlockSpec((rows, d_kv), lambda i: (i, 0))),
        compiler_params=_cparams(1), name="dil_cached")(p1, p1, p1, p1, p1, *tables, cache_kt, cache_vt)


def _row_tile(m, pref):
    for t in (1024, 512, 256, 128, 64, 32, 16, 8):
        if t <= pref and m % t == 0:
            return t
    raise ValueError(m)


def kernel(x_prompt, x_sample, state_conv, state_ssm, cache_swa_k, cache_swa_v, cache_dil_k, cache_dil_v, norm_mix, norm_mlp, e_w_in, e_conv_w, e_conv_b, e_dt_bias, e_a_log, e_d_skip, e_gate_norm, e_sinks, e_w_out, o_w_in, o_w_out, mlp_w1, mlp_w2, norm_final):
    nbp, seq, d = x_prompt.shape
    n_b, t_len, _ = x_sample.shape
    assert nbp == 1 and d == N_HEADS * HEAD_DIM and norm_mix.shape[0] == 2
    ms = n_b * t_len
    m = seq + ms
    tms = (_row_tile(seq, 1024), _row_tile(ms, 512))
    d_kv = N_KV * HEAD_DIM
    d_bc = 2 * SSD_GROUPS * SSD_STATE
    conv_dim = d + d_bc
    both = lambda fn, *rows, **kw: tuple(
        fn(*(r[s] for r in rows), tm=tms[s], name=kw["name"] + ("_p", "_s")[s], **{k: v for k, v in kw.items() if k != "name"})
        for s in range(2))

    h = (x_prompt.reshape(seq, d), x_sample.reshape(ms, d))
    tables = _rope_tables(m, seq, t_len, _row_tile(math.gcd(seq, ms), 512))
    eexp = (lax.broadcasted_iota(jnp.int32, (LANES, d), 1) // HEAD_DIM
            == lax.broadcasted_iota(jnp.int32, (LANES, d), 0)).astype(BF16)
    keys_minor = lambda c: jnp.transpose(c[0], (0, 2, 3, 1))

    wi = e_w_in[0]
    c_z, c_xbc, c_dt, c_q = 0, d, d + conv_dim, d + conv_dim + N_HEADS
    c_k, c_v = c_q + d, c_q + d + d_kv
    w0s = [wi[:, c_z:c_z + 2 * d].astype(BF16), wi[:, c_q:c_q + d].astype(BF16),
           wi[:, c_xbc + d:c_xbc + conv_dim].astype(BF16), wi[:, c_k:c_v + d_kv].astype(BF16)]
    w0_dt = jnp.pad(wi[:, c_dt:c_dt + N_HEADS], ((0, 0), (0, LANES - N_HEADS))).astype(BF16)
    off0 = {"z": 0, "xs": d, "q": 2 * d, "bc": 3 * d, "k": 3 * d + d_bc, "v": 3 * d + d_bc + d_kv}
    g0 = norm_mix[0]
    (p0_p, dt_p), (p0_s, dt_s) = both(
        lambda x, tm, name: _norm_matmul_multi(x, g0, w0s, w0_dt, tm=tm, tn=512, name=name), h, name="l0_in_proj")

    cw, cb = e_conv_w[0], e_conv_b[0]
    wx, wbc = cw[:, :d], cw[:, d:]
    bx, bbc = cb[:d].reshape(1, d), cb[d:].reshape(1, d_bc)
    pad_h = lambda v: jnp.pad(v.reshape(1, N_HEADS), ((0, 0), (0, LANES - N_HEADS)))
    dtb, alog = pad_h(e_dt_bias[0]), pad_h(e_a_log[0])
    dskip = jnp.repeat(e_d_skip[0], HEAD_DIM).reshape(1, d)
    gn = e_gate_norm[0].reshape(1, d)

    ssd_p, tail_x, tail_bc, p_ssm = _ssd_prompt(p0_p, dt_p, off0, seq, wx, wbc, bx, bbc, dtb, alog, dskip, gn, eexp)
    epad = jnp.pad(state_conv[0], ((0, 0), (0, 1), (0, 0))).reshape(ms, conv_dim)
    ssd_s, s_ssm = _ssd_sample(p0_s, dt_s, off0, seq, n_b, t_len, epad, state_ssm[0], wx, wbc, bx, bbc, dtb, alog,
                               dskip, gn, eexp, nb=4)

    att_p, krot0_p = _band_attention(p0_p, seq, off0["q"], off0["k"], off0["v"], tables, e_sinks[0], dil=1,
                                     max_dist=SWA_WINDOW - 1, nkv=N_KV, want_lse=False, want_krot=True,
                                     name="swa_prompt")
    sink_col = jnp.broadcast_to(e_sinks[0].reshape(N_KV, REP).T[:, None, :], (REP, t_len, N_KV)).reshape(-1, 1)
    att_s, krot0_s = _swa_cached(p0_s, off0, seq, n_b, t_len, tables, keys_minor(cache_swa_k),
                                 keys_minor(cache_swa_v), sink_col, nb=4)

    def mlp(hh, layer):
        g, w1, w2 = norm_mlp[layer], mlp_w1[layer].astype(BF16), mlp_w2[layer].astype(BF16)
        return both(lambda x, tm, name: _mlp(x, g, w1, w2, tm=tm, tk=512, name=name), hh, name=f"l{layer}_mlp")

    wo = e_w_out[0].astype(BF16)
    wo_a, wo_b = wo[:d], wo[d:]
    h = both(lambda a, b, r, tm, name: _matmul_resid([a, b], [wo_a, wo_b], r, tm=tm, tn=1024, name=name),
             (ssd_p, ssd_s), (att_p, att_s), h, name="l0_out_proj")
    h = mlp(h, 0)

    n_pat = len(DIL_PATTERNS)
    g1, w1i = norm_mix[1], o_w_in[0].astype(BF16)
    p1_p, p1_s = both(lambda x, tm, name: _norm_matmul(x, g1, w1i, tm=tm, tn=1024, name=name), h, name="l1_in_proj")
    k_off, v_off = n_pat * d, n_pat * d + d_kv
    os_, ls_ = [], []
    krot1_p = None
    for gi, (window, dil) in enumerate(DIL_PATTERNS):
        res = _band_attention(p1_p, seq, gi * d, k_off, v_off, tables, None, dil=dil, max_dist=window // dil,
                              nkv=(2 if BAND * dil * N_HEADS * HEAD_DIM * 4 > 8 * 2 ** 20 else N_KV),
                              want_lse=True, want_krot=(gi == 0), name=f"dil_prompt_{dil}")
        os_.append(res[0])
        ls_.append(res[1])
        if gi == 0:
            krot1_p = res[2]
    merged_p = _merge(os_, ls_, eexp, tm=_row_tile(seq, 256))
    dil_s, krot1_s = _dil_cached(p1_s, seq, n_b, t_len, tables, keys_minor(cache_dil_k), keys_minor(cache_dil_v),
                                 nb=2)
    wo1 = o_w_out[0].astype(BF16)
    h = both(lambda a, r, tm, name: _matmul_resid([a], [wo1], r, tm=tm, tn=1024, name=name),
             (merged_p, dil_s.astype(BF16)), h, name="l1_out_proj")
    h = mlp(h, 1)
    y_p, y_s = both(lambda x, tm, name: _final_norm(x, norm_final, tm=tm, name=name), h, name="final_norm")

    keep_swa = min(SWA_WINDOW, seq)
    keep_dil = min(max(w for w, _ in DIL_PATTERNS), seq)
    kv4 = lambda a: a.reshape(a.shape[0], N_KV, HEAD_DIM)
    y_prompt = y_p.reshape(1, seq, d)
    y_sample = y_s.reshape(n_b, t_len, d)
    p_conv = jnp.concatenate([tail_x[8 - (SSD_CONV - 1):], tail_bc[8 - (SSD_CONV - 1):]], axis=1)[None, None]
    p_swa_k = kv4(krot0_p[seq - keep_swa:])[None, None]
    p_swa_v = kv4(p0_p[seq - keep_swa:, off0["v"]:off0["v"] + d_kv])[None, None]
    p_dil_k = kv4(krot1_p[seq - keep_dil:])[None, None]
    p_dil_v = kv4(p1_p[seq - keep_dil:, v_off:v_off + d_kv])[None, None]
    xbc_s = jnp.concatenate([p0_s[:, off0["xs"]:off0["xs"] + d], p0_s[:, off0["bc"]:off0["bc"] + d_bc]], axis=1)
    s_conv = xbc_s.reshape(n_b, t_len, conv_dim)[:, t_len - (SSD_CONV - 1):][None]
    bt = lambda a: a.reshape(n_b, t_len, N_KV, HEAD_DIM)[None]
    s_swa_k = bt(krot0_s)
    s_swa_v = bt(p0_s[:, off0["v"]:off0["v"] + d_kv])
    s_dil_k = bt(krot1_s)
    s_dil_v = bt(p1_s[:, v_off:v_off + d_kv])
    return (y_prompt, y_sample, p_conv, p_ssm[None, None], p_swa_k, p_swa_v, p_dil_k, p_dil_v,
            s_conv, s_ssm[None], s_swa_k, s_swa_v, s_dil_k, s_dil_v)
```

```python
import functools
import math

import jax
import jax.numpy as jnp
from jax import lax
from jax.experimental import pallas as pl
from jax.experimental.pallas import tpu as pltpu

F32 = jnp.float32
BF16 = jnp.bfloat16

NORM_EPS = 1e-5
HEAD_DIM = 64
ROT_HALF = 8
ROPE_THETA = 500000.0
PAST_LEN = 8192
N_HEADS = 32
N_KV = 8
REP = N_HEADS // N_KV
SSD_GROUPS = 4
SSD_STATE = 128
SSD_CONV = 4
SSD_CHUNK = 128
SWA_WINDOW = 128
DIL_PATTERNS = ((128, 1), (512, 4), (2048, 16))
BAND = 128
ATT_SCALE = HEAD_DIM ** -0.5
LOG2E = math.log2(math.e)
LN2 = math.log(2.0)
NEG = -1e30
UNITS_IN_FLIGHT = 4
_LOG2_HEAD_DIM = 6

LANES = 128
BF16_ROWS = 16
PERM_CHUNK = 256
VMEM_LIMIT_BYTES = 56 * 1024 * 1024

_NT = (((1,), (1,)), ((), ()))
_TN = (((0,), (0,)), ((), ()))


def _cparams(n_axes):
    return pltpu.CompilerParams(dimension_semantics=("arbitrary",) * n_axes,
                                vmem_limit_bytes=VMEM_LIMIT_BYTES)


def _silu(x):
    return x * jax.nn.sigmoid(x)


def _tile_lanes(t, width):
    k = width // t.shape[1]
    return t if k == 1 else jnp.concatenate([t] * k, axis=1)


def _rope(x, cos_t, sa_t, sb_t):
    w = x.shape[1]
    return (x * _tile_lanes(cos_t, w)
            + pltpu.roll(x, w - ROT_HALF, 1) * _tile_lanes(sa_t, w)
            + pltpu.roll(x, ROT_HALF, 1) * _tile_lanes(sb_t, w))


def _low_half(shape):
    return (lax.broadcasted_iota(jnp.int32, shape, 1) & (LANES - 1)) < HEAD_DIM


def _rope_table_kernel(cos_ref, sa_ref, sb_ref, *, tm, seq, dec_seq):
    i = pl.program_id(0)
    row = i * tm + lax.broadcasted_iota(jnp.int32, (tm, LANES), 0)
    lane = lax.broadcasted_iota(jnp.int32, (tm, LANES), 1)
    pos = jnp.where(row < seq, row, PAST_LEN + ((row - seq) & (dec_seq - 1)))
    c = lane & (HEAD_DIM - 1)
    f = (c & (ROT_HALF - 1)).astype(F32)
    inv_freq = jnp.exp(f * (-math.log(ROPE_THETA) / ROT_HALF))
    ang = pos.astype(F32) * inv_freq
    cs = jnp.cos(ang)
    sn = jnp.sin(ang)
    cos_ref[...] = jnp.where(c < 2 * ROT_HALF, cs, 1.0)
    sa_ref[...] = jnp.where(c < ROT_HALF, -sn, 0.0)
    sb_ref[...] = jnp.where((c >= ROT_HALF) & (c < 2 * ROT_HALF), sn, 0.0)


def _rope_tables(m, seq, dec_seq, tm):
    assert dec_seq & (dec_seq - 1) == 0 and m % tm == 0
    shp = jax.ShapeDtypeStruct((m, LANES), F32)
    spec = pl.BlockSpec((tm, LANES), lambda i: (i, 0))
    return pl.pallas_call(
        functools.partial(_rope_table_kernel, tm=tm, seq=seq, dec_seq=dec_seq),
        out_shape=(shp, shp, shp), grid=(m // tm,), out_specs=(spec, spec, spec),
        compiler_params=_cparams(1), name="rope_tables")()


def _rmsnorm_bf16(x, g):
    ms = jnp.mean(x * x, axis=-1, keepdims=True)
    return (x * lax.rsqrt(ms + NORM_EPS) * g).astype(BF16)


def _norm_mm_kernel(x_ref, g_ref, w_ref, o_ref, xn_ref):
    @pl.when(pl.program_id(1) == 0)
    def _():
        xn_ref[...] = _rmsnorm_bf16(x_ref[...], g_ref[...])

    o_ref[...] = jnp.dot(xn_ref[...], w_ref[...], preferred_element_type=F32).astype(o_ref.dtype)


def _norm_matmul(x, g, w, *, tm, tn, name):
    m, k = x.shape
    n = w.shape[1]
    assert m % tm == 0 and n % tn == 0 and w.shape[0] == k
    return pl.pallas_call(
        _norm_mm_kernel,
        out_shape=jax.ShapeDtypeStruct((m, n), F32),
        grid=(m // tm, n // tn),
        in_specs=[pl.BlockSpec((tm, k), lambda i, j: (i, 0)),
                  pl.BlockSpec((1, k), lambda i, j: (0, 0)),
                  pl.BlockSpec((k, tn), lambda i, j: (0, j))],
        out_specs=pl.BlockSpec((tm, tn), lambda i, j: (i, j)),
        scratch_shapes=[pltpu.VMEM((tm, k), BF16)],
        compiler_params=_cparams(2), name=name)(x, g.reshape(1, k), w)


def _norm_mm_multi_kernel(*refs, owners):
    n_w = len(owners)
    x_ref, g_ref = refs[:2]
    w_refs = refs[2:2 + n_w]
    ws_ref, o_ref, os_ref, xn_ref = refs[2 + n_w:]
    j = pl.program_id(1)

    @pl.when(j == 0)
    def _():
        xn_ref[...] = _rmsnorm_bf16(x_ref[...], g_ref[...])
        os_ref[...] = jnp.dot(xn_ref[...], ws_ref[...], preferred_element_type=F32)

    for w_ref, tiles in zip(w_refs, owners):
        own = functools.reduce(jnp.logical_or, [(j >= lo) & (j < hi) for lo, hi in tiles])

        @pl.when(own)
        def _():
            o_ref[...] = jnp.dot(xn_ref[...], w_ref[...], preferred_element_type=F32)


def _norm_matmul_multi(x, g, segs, w_small, *, tm, tn, name):
    m, k = x.shape
    arrays, owners, maps = [], [], []
    start = 0
    for w, cb0, n in segs:
        if not any(w is a for a in arrays):
            arrays.append(w)
            owners.append([])
            maps.append([])
        idx = [i for i, a in enumerate(arrays) if a is w][0]
        owners[idx].append((start, start + n))
        maps[idx].append((start, n, cb0))
        start += n
    n_tiles = start

    def index_map(pieces):
        def col(i, j):
            c = pieces[-1][2] + pieces[-1][1] - 1
            for a, n, cb0 in reversed(pieces):
                c = jnp.where(j < a + n, cb0 + jnp.maximum(j - a, 0), c)
            return 0, c
        return col

    n_small = w_small.shape[1]
    return pl.pallas_call(
        functools.partial(_norm_mm_multi_kernel, owners=tuple(tuple(o) for o in owners)),
        out_shape=(jax.ShapeDtypeStruct((m, n_tiles * tn), F32), jax.ShapeDtypeStruct((m, n_small), F32)),
        grid=(m // tm, n_tiles),
        in_specs=[pl.BlockSpec((tm, k), lambda i, j: (i, 0)), pl.BlockSpec((1, k), lambda i, j: (0, 0))]
        + [pl.BlockSpec((k, tn), index_map(p)) for p in maps]
        + [pl.BlockSpec((k, n_small), lambda i, j: (0, 0))],
        out_specs=(pl.BlockSpec((tm, tn), lambda i, j: (i, j)), pl.BlockSpec((tm, n_small), lambda i, j: (i, 0))),
        scratch_shapes=[pltpu.VMEM((tm, k), BF16)],
        compiler_params=_cparams(2), name=name)(x, g.reshape(1, k), *arrays, w_small)


def _mm_resid_kernel(*refs, n_in):
    x_refs, w_refs = refs[:n_in], refs[n_in:2 * n_in]
    r_ref, o_ref = refs[2 * n_in], refs[2 * n_in + 1]
    acc = r_ref[...]
    for x_ref, w_ref in zip(x_refs, w_refs):
        acc = acc + jnp.dot(x_ref[...], w_ref[...], preferred_element_type=F32)
    o_ref[...] = acc


def _matmul_resid(xs, ws, resid, *, tm, tn, name):
    m, n = resid.shape
    assert m % tm == 0 and n % tn == 0
    in_specs = [pl.BlockSpec((tm, x.shape[1]), lambda i, j: (i, 0)) for x in xs]
    in_specs += [pl.BlockSpec((w.shape[0], tn), lambda i, j: (0, j)) for w in ws]
    in_specs.append(pl.BlockSpec((tm, tn), lambda i, j: (i, j)))
    return pl.pallas_call(
        functools.partial(_mm_resid_kernel, n_in=len(xs)),
        out_shape=jax.ShapeDtypeStruct((m, n), F32),
        grid=(m // tm, n // tn),
        in_specs=in_specs,
        out_specs=pl.BlockSpec((tm, tn), lambda i, j: (i, j)),
        compiler_params=_cparams(2), name=name)(*xs, *ws, resid)


def _mlp_kernel(x_ref, g_ref, w1_ref, w2_ref, gout_ref, o_ref, xn_ref, *, norm_out):
    k = pl.program_id(1)

    @pl.when(k == 0)
    def _():
        x = x_ref[...]
        xn_ref[...] = _rmsnorm_bf16(x, g_ref[...])
        o_ref[...] = x

    u = jnp.dot(xn_ref[...], w1_ref[...], preferred_element_type=F32)
    u = jnp.square(jnp.maximum(u, 0.0)).astype(BF16)
    o_ref[...] += jnp.dot(u, w2_ref[...], preferred_element_type=F32)

    if norm_out:
        @pl.when(k == pl.num_programs(1) - 1)
        def _():
            o = o_ref[...]
            ms = jnp.mean(o * o, axis=-1, keepdims=True)
            o_ref[...] = o * lax.rsqrt(ms + NORM_EPS) * gout_ref[...]


def _mlp(x, g, w1, w2, layer, g_out, *, tm, tk, name):
    m, d = x.shape
    hid = w1.shape[2]
    assert m % tm == 0 and hid % tk == 0
    return pl.pallas_call(
        functools.partial(_mlp_kernel, norm_out=g_out is not None),
        out_shape=jax.ShapeDtypeStruct((m, d), F32),
        grid=(m // tm, hid // tk),
        in_specs=[pl.BlockSpec((tm, d), lambda i, k: (i, 0)),
                  pl.BlockSpec((1, d), lambda i, k: (0, 0)),
                  pl.BlockSpec((None, d, tk), lambda i, k: (layer, 0, k)),
                  pl.BlockSpec((None, tk, d), lambda i, k: (layer, k, 0)),
                  pl.BlockSpec((1, d), lambda i, k: (0, 0))],
        out_specs=pl.BlockSpec((tm, d), lambda i, k: (i, 0)),
        scratch_shapes=[pltpu.VMEM((tm, d), BF16)],
        compiler_params=_cparams(2), name=name)(x, g.reshape(1, d), w1, w2,
                                                (g if g_out is None else g_out).reshape(1, d))


def _causal_conv(raw, tail, w_ref, b_ref):
    n = raw.shape[0]
    full = jnp.concatenate([tail, raw], axis=0)
    acc = b_ref[...]
    for i in range(SSD_CONV):
        s = SSD_CONV - 1 - i
        acc = acc + full[8 - s:8 - s + n] * w_ref[i:i + 1, :]
    return _silu(acc)


def _gate_groupnorm(y, z, gn):
    y = y * _silu(z)
    gw = y.shape[1] // SSD_GROUPS
    outs = []
    for g in range(SSD_GROUPS):
        yg = y[:, g * gw:(g + 1) * gw]
        ms = jnp.mean(yg * yg, axis=-1, keepdims=True)
        outs.append(yg * lax.rsqrt(ms + NORM_EPS))
    return jnp.concatenate(outs, axis=1) * gn


def _ssd_prompt_kernel(z_ref, xs_ref, bc_ref, dt_ref, wx_ref, wbc_ref, bx_ref, bbc_ref, dtb_ref, alog_ref,
                       dskip_ref, gn_ref, eexp_ref, y_ref, tailx_ref, tailbc_ref, h_ref, y_s):
    c = pl.program_id(0)
    q = SSD_CHUNK
    gs = SSD_STATE
    hp = HEAD_DIM

    @pl.when(c == 0)
    def _():
        tailx_ref[...] = jnp.zeros_like(tailx_ref)
        tailbc_ref[...] = jnp.zeros_like(tailbc_ref)
        h_ref[...] = jnp.zeros_like(h_ref)

    xs_raw = xs_ref[...]
    bc_raw = bc_ref[...]
    xs_c = _causal_conv(xs_raw, tailx_ref[...], wx_ref, bx_ref)
    bc_c = _causal_conv(bc_raw, tailbc_ref[...], wbc_ref, bbc_ref)
    tailx_ref[...] = xs_raw[q - 8:q]
    tailbc_ref[...] = bc_raw[q - 8:q]

    lane = lax.broadcasted_iota(jnp.int32, (q, LANES), 1)
    row = lax.broadcasted_iota(jnp.int32, (q, LANES), 0)
    dt = jax.nn.softplus(dt_ref[...] + dtb_ref[...])
    a = -jnp.exp(alog_ref[...])
    la = jnp.where(lane < N_HEADS, dt * a, 0.0)
    acs = la
    s = 1
    while s < q:
        acs = acs + jnp.where(row >= s, pltpu.roll(acs, s, 0), 0.0)
        s *= 2
    acs_t = acs.T
    acs_last = acs[q - 1:q, :]
    to_end = jnp.exp(acs_last - acs)
    eacs = jnp.exp(acs)
    cdec = jnp.exp(acs_last)
    causal = (lax.broadcasted_iota(jnp.int32, (q, q), 0) >= lax.broadcasted_iota(jnp.int32, (q, q), 1))

    stacked = jnp.concatenate([dt, eacs, to_end], axis=0)
    hi = stacked.astype(BF16)
    lo = (stacked - hi.astype(F32)).astype(BF16)
    e = eexp_ref[...]
    full = jnp.dot(hi, e, preferred_element_type=F32) + jnp.dot(lo, e, preferred_element_type=F32)
    xdt = xs_c * full[0:q]
    eacs_f = full[q:2 * q]
    w_f = xdt * full[2 * q:3 * q]
    low = _low_half((q, LANES))
    top = lax.broadcasted_iota(jnp.int32, (2 * hp, gs), 0) < hp

    rep = N_HEADS // SSD_GROUPS
    for g in range(SSD_GROUPS):
        bg = bc_c[:, g * gs:(g + 1) * gs].astype(BF16)
        cg = bc_c[:, SSD_GROUPS * gs + g * gs:SSD_GROUPS * gs + (g + 1) * gs].astype(BF16)
        cb = lax.dot_general(cg, bg, _NT, preferred_element_type=F32)
        for rp in range(rep // 2):
            h0 = g * rep + 2 * rp
            sl = slice(h0 * hp, (h0 + 2) * hp)
            ms = []
            for h in (h0, h0 + 1):
                lmat = jnp.exp(jnp.where(causal, acs[:, h:h + 1] - acs_t[h:h + 1, :], -jnp.inf))
                ms.append((cb * lmat).astype(BF16))
            xp = xdt[:, sl]
            rhs = jnp.concatenate([jnp.where(low, xp, 0.0), jnp.where(low, 0.0, xp)], axis=0).astype(BF16)
            y = jnp.dot(jnp.concatenate(ms, axis=1), rhs, preferred_element_type=F32)
            hs = h_ref[h0:h0 + 2].reshape(2 * hp, gs)
            y = y + eacs_f[:, sl] * lax.dot_general(cg, hs.astype(BF16), _NT, preferred_element_type=F32)
            st = lax.dot_general(w_f[:, sl].astype(BF16), bg, _TN, preferred_element_type=F32)
            cd = jnp.where(top, cdec[:, h0:h0 + 1], cdec[:, h0 + 1:h0 + 2])
            h_ref[h0:h0 + 2] = (hs * cd + st).reshape(2, hp, gs)
            y_s[:, sl] = y + dskip_ref[:, sl] * xs_c[:, sl]

    y_ref[...] = _gate_groupnorm(y_s[...], z_ref[...], gn_ref[...]).astype(y_ref.dtype)


def _ssd_prompt(p0, dtr, off, seq, wx, wbc, bx, bbc, dtb, alog, dskip, gn, eexp):
    q = SSD_CHUNK
    d_in = wx.shape[1]
    d_bc = wbc.shape[1]
    assert seq % q == 0
    const = lambda shape: pl.BlockSpec(shape, lambda c: (0,) * len(shape))
    return pl.pallas_call(
        _ssd_prompt_kernel,
        out_shape=(jax.ShapeDtypeStruct((seq, d_in), BF16),
                   jax.ShapeDtypeStruct((8, d_in), F32),
                   jax.ShapeDtypeStruct((8, d_bc), F32),
                   jax.ShapeDtypeStruct((N_HEADS, HEAD_DIM, SSD_STATE), F32)),
        grid=(seq // q,),
        in_specs=[pl.BlockSpec((q, d_in), lambda c: (c, off["z"] // d_in)),
                  pl.BlockSpec((q, d_in), lambda c: (c, off["xs"] // d_in)),
                  pl.BlockSpec((q, d_bc), lambda c: (c, off["bc"] // d_bc)),
                  pl.BlockSpec((q, LANES), lambda c: (c, 0)),
                  const((SSD_CONV, d_in)), const((SSD_CONV, d_bc)), const((1, d_in)), const((1, d_bc)),
                  const((1, LANES)), const((1, LANES)), const((1, d_in)), const((1, d_in)), const((LANES, d_in))],
        out_specs=(pl.BlockSpec((q, d_in), lambda c: (c, 0)),
                   const((8, d_in)), const((8, d_bc)), const((N_HEADS, HEAD_DIM, SSD_STATE))),
        scratch_shapes=[pltpu.VMEM((q, d_in), F32)],
        compiler_params=_cparams(1), name="ssd_prompt")(p0, p0, p0, dtr, wx, wbc, bx, bbc, dtb, alog, dskip, gn, eexp)


def _split3(x):
    hi = x.astype(BF16)
    r1 = x - hi.astype(F32)
    mid = r1.astype(BF16)
    lo = (r1 - mid.astype(F32)).astype(BF16)
    return hi, mid, lo


def _ssd_sample_kernel(z_ref, xs_ref, bc_ref, dt_ref, ex_ref, ebc_ref, st_ref, wx_ref, wbc_ref, bx_ref, bbc_ref,
                       dtb_ref, alog_ref, dskip_ref, gn_ref, eexp_ref, y_ref, sto_ref, *, nb, t_len):
    rows = nb * t_len
    gs = SSD_STATE
    gw = xs_ref.shape[1] // SSD_GROUPS
    rep = N_HEADS // SSD_GROUPS

    def tcol(width):
        r = lax.broadcasted_iota(jnp.int32, (rows, width), 0)
        return r & (t_len - 1), r >> int(math.log2(t_len))

    def conv(raw, est, w_ref, b_ref):
        t, _ = tcol(raw.shape[1])
        acc = b_ref[...]
        for i in range(SSD_CONV):
            s = SSD_CONV - 1 - i
            if s == 0:
                sh = raw
            else:
                k = (rows - (SSD_CONV - 1) + s) % rows
                sh = jnp.where(t >= s, pltpu.roll(raw, s, 0), est if k == 0 else pltpu.roll(est, k, 0))
            acc = acc + sh * w_ref[i:i + 1, :]
        return _silu(acc)

    xs_c = conv(xs_ref[...], ex_ref[...], wx_ref, bx_ref)
    bc_c = conv(bc_ref[...], ebc_ref[...], wbc_ref, bbc_ref)
    bm = bc_c[:, :SSD_GROUPS * gs]
    cm = bc_c[:, SSD_GROUPS * gs:]

    t1, _ = tcol(LANES)
    lane = lax.broadcasted_iota(jnp.int32, (rows, LANES), 1)
    dt = jax.nn.softplus(dt_ref[...] + dtb_ref[...])
    a = -jnp.exp(alog_ref[...])
    la = jnp.where(lane < N_HEADS, dt * a, 0.0)
    acs = la
    for s in range(1, t_len):
        acs = acs + jnp.where(t1 >= s, pltpu.roll(la, s, 0), 0.0)
    alast = jnp.where(t1 == t_len - 1, acs, 0.0)
    for u in range(1, t_len):
        alast = alast + jnp.where(t1 == t_len - 1 - u, pltpu.roll(acs, rows - u, 0), 0.0)
    parts = [dt, jnp.exp(acs), jnp.exp(alast - acs), jnp.exp(alast)]
    for k in range(1, t_len):
        parts.append(jnp.exp(acs - pltpu.roll(acs, k, 0)))
    stacked = jnp.concatenate(parts, axis=0)
    hi, mid, lo = _split3(stacked)
    e = eexp_ref[...]
    full = (jnp.dot(hi, e, preferred_element_type=F32) + jnp.dot(mid, e, preferred_element_type=F32)
            + jnp.dot(lo, e, preferred_element_type=F32))
    dt_f, eacs_f, toend_f, cdec_f = (full[i * rows:(i + 1) * rows] for i in range(4))
    dec_f = [None] + [full[(3 + k) * rows:(4 + k) * rows] for k in range(1, t_len)]

    tw, _ = tcol(xs_c.shape[1])
    xdt = xs_c * dt_f
    y = jnp.zeros_like(xs_c)
    for k in range(t_len):
        bmk = bm if k == 0 else pltpu.roll(bm, k, 0)
        prod = cm * bmk
        cbs = []
        for g in range(SSD_GROUPS):
            sg = jnp.sum(prod[:, g * gs:(g + 1) * gs], axis=-1, keepdims=True)
            cbs.append(jnp.broadcast_to(sg, (rows, gw)))
        cb_f = jnp.concatenate(cbs, axis=1)
        if k == 0:
            y = y + cb_f * xdt
        else:
            y = y + jnp.where(tw >= k, cb_f * dec_f[k] * pltpu.roll(xdt, k, 0), 0.0)

    cm16 = cm.astype(BF16)
    bm16 = bm.astype(BF16)
    w_f = (toend_f * xdt)
    _, bg_ = tcol(gw)
    ones_blk = jnp.ones((3, gs), BF16)
    yoffs = []
    for g in range(SSD_GROUPS):
        cg = cm16[:, g * gs:(g + 1) * gs]
        rhs_top = jnp.concatenate([bm16[:, g * gs:(g + 1) * gs], jnp.zeros((rows, gs), BF16)], axis=1)
        rhs_mid = jnp.concatenate([jnp.zeros((3, gs), BF16), ones_blk], axis=1)
        rhs = jnp.concatenate([rhs_top, rhs_mid, jnp.zeros((13, 2 * gs), BF16)], axis=0)
        wg = w_f[:, g * gw:(g + 1) * gw]
        cdg = cdec_f[:, g * gw:(g + 1) * gw]
        yg = jnp.zeros((rows, gw), F32)
        for b in range(nb):
            hb = st_ref[b, g * rep:(g + 1) * rep].reshape(gw, gs)
            yb = lax.dot_general(cg, hb.astype(BF16), _NT, preferred_element_type=F32)
            yg = jnp.where(bg_ == b, yb, yg)
            wb = jnp.where(bg_ == b, wg, 0.0).astype(BF16)
            d_hi, d_mid, d_lo = _split3(cdg[b * t_len:b * t_len + 1])
            lhs = jnp.concatenate([wb, d_hi, d_mid, d_lo, jnp.zeros((13, gw), BF16)], axis=0)
            sd = lax.dot_general(lhs, rhs, _TN, preferred_element_type=F32)
            sto_ref[b, g * rep:(g + 1) * rep] = (hb * sd[:, gs:] + sd[:, :gs]).reshape(rep, HEAD_DIM, gs)
        yoffs.append(yg)
    y = y + eacs_f * jnp.concatenate(yoffs, axis=1) + dskip_ref[...] * xs_c
    y_ref[...] = _gate_groupnorm(y, z_ref[...], gn_ref[...]).astype(y_ref.dtype)


def _ssd_sample(p0, dtr, off, seq, n_b, t_len, epad, state, wx, wbc, bx, bbc, dtb, alog, dskip, gn, eexp, *, nb):
    rows = nb * t_len
    d_in = wx.shape[1]
    d_bc = wbc.shape[1]
    assert n_b % nb == 0 and t_len & (t_len - 1) == 0
    const = lambda shape: pl.BlockSpec(shape, lambda i: (0,) * len(shape))
    st_spec = pl.BlockSpec((nb, N_HEADS, HEAD_DIM, SSD_STATE), lambda i: (i, 0, 0, 0))
    return pl.pallas_call(
        functools.partial(_ssd_sample_kernel, nb=nb, t_len=t_len),
        out_shape=(jax.ShapeDtypeStruct((n_b * t_len, d_in), BF16),
                   jax.ShapeDtypeStruct(state.shape, F32)),
        grid=(n_b // nb,),
        in_specs=[pl.BlockSpec((rows, d_in), lambda i: (i, off["z"] // d_in)),
                  pl.BlockSpec((rows, d_in), lambda i: (i, off["xs"] // d_in)),
                  pl.BlockSpec((rows, d_bc), lambda i: (i, off["bc"] // d_bc)),
                  pl.BlockSpec((rows, LANES), lambda i: (i, 0)),
                  pl.BlockSpec((rows, d_in), lambda i: (i, 0)),
                  pl.BlockSpec((rows, d_bc), lambda i: (i, d_in // d_bc)),
                  st_spec,
                  const((SSD_CONV, d_in)), const((SSD_CONV, d_bc)), const((1, d_in)), const((1, d_bc)),
                  const((1, LANES)), const((1, LANES)), const((1, d_in)), const((1, d_in)),
                  const((LANES, d_in))],
        out_specs=(pl.BlockSpec((rows, d_in), lambda i: (i, 0)), st_spec),
        compiler_params=_cparams(1), name="ssd_sample")(
            p0, p0, p0, dtr, epad, epad, state, wx, wbc, bx, bbc, dtb, alog, dskip, gn, eexp)


def _perm_matrix(dil):
    i = jnp.arange(PERM_CHUNK)
    blk = BF16_ROWS * dil
    src = (i // blk) * blk + (i % BF16_ROWS) * dil + (i % blk) // BF16_ROWS
    return (src[:, None] == jnp.arange(PERM_CHUNK)[None, :]).astype(BF16)


def _band_kernel(*refs, dil, max_dist, use_sink, want_lse, want_krot, nkv, nj, res_per_iter):
    it = iter(refs)
    q_ref, k_ref, v_ref, cos_ref, sa_ref, sb_ref = (next(it) for _ in range(6))
    p_ref, pt_ref = (next(it), next(it)) if dil > 1 else (None, None)
    sink_ref = next(it) if use_sink else None
    o_ref = next(it)
    lse_ref = next(it) if want_lse else None
    krot_ref = next(it) if want_krot else None
    qlo_s, qhi_s, kd_s, vd_s, kdp_s, vdp_s, op_s = (next(it) for _ in range(7))
    lsep_s = next(it) if want_lse else None
    lse_s = next(it) if (want_lse and nj > 1) else None

    c = pl.program_id(0)
    j = pl.program_id(1)
    rows, wq = q_ref.shape
    wk = k_ref.shape[1]
    n_chunk = rows // PERM_CHUNK if dil > 1 else 0
    n_grp = BAND // BF16_ROWS

    def perm(x):
        if dil == 1:
            return x
        x16 = x.astype(BF16)
        return jnp.concatenate(
            [jnp.dot(p_ref[...], x16[k * PERM_CHUNK:(k + 1) * PERM_CHUNK], preferred_element_type=F32)
             for k in range(n_chunk)], axis=0)

    def unperm(xp16):
        if dil == 1:
            return [xp16]
        return [jnp.dot(pt_ref[...], xp16[k * PERM_CHUNK:(k + 1) * PERM_CHUNK], preferred_element_type=F32)
                for k in range(n_chunk)]

    tabs = (cos_ref[...], sa_ref[...], sb_ref[...])
    q = perm(_rope(q_ref[...], *tabs) * (ATT_SCALE * LOG2E))
    lowq = _low_half((rows, wq))
    qlo_s[...] = jnp.where(lowq, q, 0.0).astype(BF16)
    qhi_s[...] = jnp.where(lowq, 0.0, q).astype(BF16)
    kr = _rope(k_ref[...], *tabs)
    if want_krot:
        krot_ref[...] = kr
    lowk = _low_half((rows, LANES))
    for x, d_s in ((perm(kr), kd_s), (perm(v_ref[...]), vd_s)):
        up = pltpu.roll(x, HEAD_DIM, 1)
        dn = pltpu.roll(x, wk - HEAD_DIM, 1)
        for t in range(wk // LANES):
            sl = slice(t * LANES, (t + 1) * LANES)
            d_s[2 * t] = jnp.where(lowk, x[:, sl], up[:, sl]).astype(BF16)
            d_s[2 * t + 1] = jnp.where(lowk, dn[:, sl], x[:, sl]).astype(BF16)

    @pl.when(c == 0)
    def _():
        kdp_s[j] = jnp.zeros(kdp_s.shape[1:], BF16)
        vdp_s[j] = jnp.zeros(vdp_s.shape[1:], BF16)

    qi = lax.broadcasted_iota(jnp.int32, (BAND, 2 * BAND), 0)
    kj = lax.broadcasted_iota(jnp.int32, (BAND, 2 * BAND), 1)
    dist = qi - kj + BAND
    bias = jnp.where((dist >= 0) & (dist <= max_dist) & ((kj >= BAND) | (c > 0)), 0.0, NEG)
    col0 = kj == 0
    ones_blk = jnp.ones((2 * BAND, LANES), BF16)
    low = _low_half((BAND, LANES))
    lane = lax.broadcasted_iota(jnp.int32, (BAND, LANES), 1)
    step = BF16_ROWS * dil

    def residue(rho):
        starts = [pl.multiple_of(b * step + rho * BF16_ROWS, BF16_ROWS) for b in range(n_grp)]

        def gather(get):
            return jnp.concatenate([get(s) for s in starts], axis=0)

        tiles = []
        lse_acc = jnp.zeros((BAND, LANES), F32)
        for g in range(nkv):
            ca, cb = slice(2 * g * LANES, (2 * g + 1) * LANES), slice((2 * g + 1) * LANES, (2 * g + 2) * LANES)
            lhs = jnp.concatenate([gather(lambda s: qlo_s[pl.ds(s, BF16_ROWS), ca]),
                                   gather(lambda s: qhi_s[pl.ds(s, BF16_ROWS), ca]),
                                   gather(lambda s: qlo_s[pl.ds(s, BF16_ROWS), cb]),
                                   gather(lambda s: qhi_s[pl.ds(s, BF16_ROWS), cb])], axis=0)
            kcat = jnp.concatenate([gather(lambda s: kdp_s[j, g, pl.ds(s, BF16_ROWS), :]),
                                    gather(lambda s: kd_s[g, pl.ds(s, BF16_ROWS), :])], axis=0)
            vcat = jnp.concatenate([gather(lambda s: vdp_s[j, g, pl.ds(s, BF16_ROWS), :]),
                                    gather(lambda s: vd_s[g, pl.ds(s, BF16_ROWS), :])], axis=0)
            head0 = j * REP * nkv + g * REP
            s_full = lax.dot_general(lhs, kcat, _NT, preferred_element_type=F32)
            rsl = [slice(r * BAND, (r + 1) * BAND) for r in range(REP)]
            if use_sink:
                sc = jnp.concatenate([jnp.where(col0, sink_ref[head0 + r] * LOG2E, s_full[rsl[r]] + bias)
                                      for r in range(REP)], axis=0)
            else:
                sc = s_full + jnp.concatenate([bias] * REP, axis=0)
            m = jnp.max(sc, axis=-1, keepdims=True)
            e = jnp.exp2(sc - m)
            od = jnp.dot(e.astype(BF16), jnp.concatenate([vcat, ones_blk], axis=1), preferred_element_type=F32)
            den = od[:, LANES:]
            o = od[:, :LANES] * (1.0 / den)
            tiles.append(jnp.where(low, o[0:BAND], o[BAND:2 * BAND]))
            tiles.append(jnp.where(low, o[2 * BAND:3 * BAND], o[3 * BAND:4 * BAND]))
            if want_lse:
                lse = (m + jnp.log2(den)) * LN2
                for r in range(REP):
                    lse_acc = lse_acc + jnp.where(lane == head0 + r, lse[rsl[r]], 0.0)
        o_res = jnp.concatenate(tiles, axis=1)
        for b, s0 in enumerate(starts):
            op_s[pl.ds(s0, BF16_ROWS), :] = o_res[b * BF16_ROWS:(b + 1) * BF16_ROWS]
            if want_lse:
                lsep_s[pl.ds(s0, BF16_ROWS), :] = lse_acc[b * BF16_ROWS:(b + 1) * BF16_ROWS]

    if dil == 1:
        residue(0)
    else:
        def body(i, carry):
            for u in range(res_per_iter):
                residue(i * res_per_iter + u)
            return carry

        lax.fori_loop(0, dil // res_per_iter, body, 0)

    for k, blk in enumerate(unperm(op_s[...].astype(BF16))):
        if dil == 1:
            o_ref[...] = blk
        else:
            o_ref[k * PERM_CHUNK:(k + 1) * PERM_CHUNK, :] = blk.astype(o_ref.dtype)
    if want_lse:
        if dil == 1:
            lse_nat = lsep_s[...]
        else:
            parts = [unperm(x) for x in _split3(lsep_s[...])]
            lse_nat = jnp.concatenate([a + b + d for a, b, d in zip(*parts)], axis=0)
        if nj == 1:
            lse_ref[...] = lse_nat
        else:
            @pl.when(j == 0)
            def _():
                lse_s[...] = lse_nat

            @pl.when(j > 0)
            def _():
                lse_s[...] = lse_s[...] + lse_nat

            @pl.when(j == nj - 1)
            def _():
                lse_ref[...] = lse_s[...]
    kdp_s[j] = kd_s[...]
    if use_sink:
        keep = (lax.broadcasted_iota(jnp.int32, (rows, LANES), 0) > 0).astype(F32).astype(BF16)
        vdp_s[j] = vd_s[...] * keep
    else:
        vdp_s[j] = vd_s[...]


def _band_attention(p, seq, q_off, k_off, v_off, tables, sinks, *, dil, max_dist, nkv, want_lse, want_krot, name):
    rows = BAND * dil
    nj = N_KV // nkv
    qw, kw = nkv * REP * HEAD_DIM, nkv * HEAD_DIM
    assert seq % rows == 0 and kw % LANES == 0 and q_off % qw == 0 and k_off % kw == 0 and v_off % kw == 0
    assert dil == 1 or (rows % PERM_CHUNK == 0 and PERM_CHUNK % (BF16_ROWS * dil) == 0)
    use_sink = sinks is not None
    d_att = N_HEADS * HEAD_DIM
    res_per_iter = 1 if dil == 1 else max(1, UNITS_IN_FLIGHT // nkv)
    assert dil % res_per_iter == 0 and (not use_sink or (dil == 1 and max_dist < BAND))
    blk = lambda width, cb: pl.BlockSpec((rows, width), lambda c, j: (c, cb + j))
    t_spec = pl.BlockSpec((rows, LANES), lambda c, j: (c, 0))
    in_specs = [blk(qw, q_off // qw), blk(kw, k_off // kw), blk(kw, v_off // kw), t_spec, t_spec, t_spec]
    args = [p, p, p] + list(tables)
    if dil > 1:
        pm = _perm_matrix(dil)
        in_specs += [pl.BlockSpec((PERM_CHUNK, PERM_CHUNK), lambda c, j: (0, 0))] * 2
        args += [pm, pm.T]
    if use_sink:
        in_specs.append(pl.BlockSpec(memory_space=pltpu.SMEM))
        args.append(sinks)
    out_shape = [jax.ShapeDtypeStruct((seq, d_att), BF16)]
    out_specs = [pl.BlockSpec((rows, qw), lambda c, j: (c, j))]
    scratch = [pltpu.VMEM((rows, qw), BF16), pltpu.VMEM((rows, qw), BF16),
               pltpu.VMEM((nkv, rows, LANES), BF16), pltpu.VMEM((nkv, rows, LANES), BF16),
               pltpu.VMEM((nj, nkv, rows, LANES), BF16), pltpu.VMEM((nj, nkv, rows, LANES), BF16),
               pltpu.VMEM((rows, qw), F32)]
    if want_lse:
        out_shape.append(jax.ShapeDtypeStruct((seq, LANES), F32))
        out_specs.append(pl.BlockSpec((rows, LANES), lambda c, j: (c, 0)))
        scratch.append(pltpu.VMEM((rows, LANES), F32))
        if nj > 1:
            scratch.append(pltpu.VMEM((rows, LANES), F32))
    if want_krot:
        out_shape.append(jax.ShapeDtypeStruct((seq, N_KV * HEAD_DIM), F32))
        out_specs.append(pl.BlockSpec((rows, kw), lambda c, j: (c, j)))
    return pl.pallas_call(
        functools.partial(_band_kernel, dil=dil, max_dist=max_dist, use_sink=use_sink, want_lse=want_lse,
                          want_krot=want_krot, nkv=nkv, nj=nj, res_per_iter=res_per_iter),
        out_shape=tuple(out_shape), grid=(seq // rows, nj), in_specs=in_specs, out_specs=tuple(out_specs),
        scratch_shapes=scratch, compiler_params=_cparams(2), name=name)(*args)


def _merge_kernel(o1_ref, o2_ref, o3_ref, l1_ref, l2_ref, l3_ref, e_ref, out_ref):
    ls = [l1_ref[...], l2_ref[...], l3_ref[...]]
    mx = jnp.maximum(jnp.maximum(ls[0], ls[1]), ls[2])
    es = [jnp.exp(l - mx) for l in ls]
    inv = 1.0 / (es[0] + es[1] + es[2])
    e = e_ref[...]
    acc = None
    for en, o_ref in zip(es, (o1_ref, o2_ref, o3_ref)):
        w = en * inv
        hi = w.astype(BF16)
        lo = (w - hi.astype(F32)).astype(BF16)
        wf = jnp.dot(hi, e, preferred_element_type=F32) + jnp.dot(lo, e, preferred_element_type=F32)
        term = wf * o_ref[...].astype(F32)
        acc = term if acc is None else acc + term
    out_ref[...] = acc.astype(out_ref.dtype)


def _merge(os_, ls_, eexp, *, tm):
    seq, d = os_[0].shape
    o_spec = pl.BlockSpec((tm, d), lambda i: (i, 0))
    l_spec = pl.BlockSpec((tm, LANES), lambda i: (i, 0))
    return pl.pallas_call(
        _merge_kernel, out_shape=jax.ShapeDtypeStruct((seq, d), BF16), grid=(seq // tm,),
        in_specs=[o_spec] * 3 + [l_spec] * 3 + [pl.BlockSpec((LANES, d), lambda i: (0, 0))],
        out_specs=o_spec, compiler_params=_cparams(1), name="dil_merge")(*os_, *ls_, eexp)


def _expand_q(qb, t_len):
    gd = N_KV * HEAD_DIM
    sub = lax.broadcasted_iota(jnp.int32, (N_KV, gd), 0)
    lg = lax.broadcasted_iota(jnp.int32, (N_KV, gd), 1) >> _LOG2_HEAD_DIM
    diag = sub == lg
    tiles = []
    for r in range(REP):
        vr = jnp.concatenate(
            [qb[:, (g * REP + r) * HEAD_DIM:(g * REP + r + 1) * HEAD_DIM] for g in range(N_KV)], axis=1)
        for t in range(t_len):
            tiles.append(jnp.where(diag, jnp.broadcast_to(vr[t:t + 1], (N_KV, gd)), 0.0))
    return jnp.concatenate(tiles, axis=0).astype(BF16)


def _collapse_o(r_full, t_len):
    gd = N_KV * HEAD_DIM
    n = r_full.shape[0]
    sub = lax.broadcasted_iota(jnp.int32, (n, gd), 0) & (N_KV - 1)
    lg = lax.broadcasted_iota(jnp.int32, (n, gd), 1) >> _LOG2_HEAD_DIM
    masked = jnp.where(sub == lg, r_full, 0.0)
    red = jnp.sum(masked.reshape(n // N_KV, N_KV, gd), axis=1)
    pieces = []
    for g in range(N_KV):
        for r in range(REP):
            pieces.append(red[r * t_len:(r + 1) * t_len, g * HEAD_DIM:(g + 1) * HEAD_DIM])
    return jnp.concatenate(pieces, axis=1)


def _row_t(shape, t_len):
    r = lax.broadcasted_iota(jnp.int32, shape, 0)
    return (r >> 3) & (t_len - 1)


def _cached_attend(qx, kt, vt, mask_w, kn, vn, mask_n, sink):
    s_w = jnp.where(mask_w, jnp.dot(qx, kt, preferred_element_type=F32), NEG)
    s_n = jnp.where(mask_n, lax.dot_general(qx, kn, _NT, preferred_element_type=F32), NEG)
    m = jnp.maximum(jnp.max(s_w, axis=-1, keepdims=True), jnp.max(s_n, axis=-1, keepdims=True))
    if sink is not None:
        m = jnp.maximum(m, sink)
    e_w = jnp.exp(s_w - m)
    e_n = jnp.exp(s_n - m)
    den = jnp.sum(e_w, axis=-1, keepdims=True) + jnp.sum(e_n, axis=-1, keepdims=True)
    if sink is not None:
        den = den + jnp.exp(sink - m)
    r = (lax.dot_general(e_w.astype(BF16), vt, _NT, preferred_element_type=F32)
         + jnp.dot(e_n.astype(BF16), vn, preferred_element_type=F32))
    return r / den, m + jnp.log(den)


def _swa_cached_kernel(q_ref, kn_ref, vn_ref, cos_ref, sa_ref, sb_ref, ck_ref, cv_ref, sink_ref,
                       o_ref, krot_ref, *, nb, t_len):
    tabs = (cos_ref[...], sa_ref[...], sb_ref[...])
    q = _rope(q_ref[...], *tabs) * ATT_SCALE
    kn = _rope(kn_ref[...], *tabs)
    krot_ref[...] = kn
    kn16 = kn.astype(BF16)
    vn16 = vn_ref[...].astype(BF16)
    lb = ck_ref.shape[3]
    gd = N_KV * HEAD_DIM
    nrow = REP * t_len * N_KV
    tq = _row_t((nrow, lb), t_len)
    mask_w = lax.broadcasted_iota(jnp.int32, (nrow, lb), 1) >= tq + 1
    mask_n = lax.broadcasted_iota(jnp.int32, (nrow, BF16_ROWS), 1) <= _row_t((nrow, BF16_ROWS), t_len)
    sk = sink_ref[...]
    pad = jnp.zeros((BF16_ROWS - t_len, gd), BF16)
    outs = []
    for b in range(nb):
        rs = slice(b * t_len, (b + 1) * t_len)
        r_full, _ = _cached_attend(
            _expand_q(q[rs], t_len), ck_ref[b].reshape(gd, lb).astype(BF16), cv_ref[b].reshape(gd, lb).astype(BF16),
            mask_w, jnp.concatenate([kn16[rs], pad], axis=0), jnp.concatenate([vn16[rs], pad], axis=0), mask_n, sk)
        outs.append(_collapse_o(r_full, t_len))
    o_ref[...] = jnp.concatenate(outs, axis=0).astype(o_ref.dtype)


def _swa_cached(p0, off, seq, n_b, t_len, tables, cache_kt, cache_vt, sink_col, *, nb):
    rows = nb * t_len
    d_att = N_HEADS * HEAD_DIM
    d_kv = N_KV * HEAD_DIM
    lb = cache_kt.shape[3]
    assert lb == SWA_WINDOW and t_len <= BF16_ROWS
    r0 = seq // rows
    nrow = REP * t_len * N_KV
    c_spec = pl.BlockSpec((nb, N_KV, HEAD_DIM, lb), lambda i: (i, 0, 0, 0))
    t_spec = pl.BlockSpec((rows, LANES), lambda i: (r0 + i, 0))
    return pl.pallas_call(
        functools.partial(_swa_cached_kernel, nb=nb, t_len=t_len),
        out_shape=(jax.ShapeDtypeStruct((n_b * t_len, d_att), BF16),
                   jax.ShapeDtypeStruct((n_b * t_len, d_kv), F32)),
        grid=(n_b // nb,),
        in_specs=[pl.BlockSpec((rows, d_att), lambda i: (i, off["q"] // d_att)),
                  pl.BlockSpec((rows, d_kv), lambda i: (i, off["k"] // d_kv)),
                  pl.BlockSpec((rows, d_kv), lambda i: (i, off["v"] // d_kv)),
                  t_spec, t_spec, t_spec, c_spec, c_spec,
                  pl.BlockSpec((nrow, 1), lambda i: (0, 0))],
        out_specs=(pl.BlockSpec((rows, d_att), lambda i: (i, 0)),
                   pl.BlockSpec((rows, d_kv), lambda i: (i, 0))),
        compiler_params=_cparams(1), name="swa_cached")(p0, p0, p0, *tables, cache_kt, cache_vt, sink_col)


def _dil_cached_kernel(q1_ref, q2_ref, q3_ref, kn_ref, vn_ref, cos_ref, sa_ref, sb_ref, ck_ref, cv_ref,
                       o_ref, krot_ref, *, nb, t_len):
    tabs = (cos_ref[...], sa_ref[...], sb_ref[...])
    qs = [_rope(r[...], *tabs) * ATT_SCALE for r in (q1_ref, q2_ref, q3_ref)]
    kn = _rope(kn_ref[...], *tabs)
    krot_ref[...] = kn
    kn16 = kn.astype(BF16)
    vn16 = vn_ref[...].astype(BF16)
    lbuf = ck_ref.shape[3]
    gd = N_KV * HEAD_DIM
    nrow = REP * t_len * N_KV
    tqn = _row_t((nrow, BF16_ROWS), t_len)
    coln = lax.broadcasted_iota(jnp.int32, (nrow, BF16_ROWS), 1)
    masks = []
    for window, dil in DIL_PATTERNS:
        tq = _row_t((nrow, window), t_len)
        col = lax.broadcasted_iota(jnp.int32, (nrow, window), 1)
        if dil == 1:
            masks.append((col >= tq, coln <= tqn))
        else:
            masks.append(((col & (dil - 1)) == tq, coln == tqn))
    pad = jnp.zeros((BF16_ROWS - t_len, gd), BF16)
    outs = []
    for b in range(nb):
        rs = slice(b * t_len, (b + 1) * t_len)
        kt = ck_ref[b].reshape(gd, lbuf).astype(BF16)
        vt = cv_ref[b].reshape(gd, lbuf).astype(BF16)
        knp = jnp.concatenate([kn16[rs], pad], axis=0)
        vnp = jnp.concatenate([vn16[rs], pad], axis=0)
        res = []
        for gi, (window, dil) in enumerate(DIL_PATTERNS):
            res.append(_cached_attend(_expand_q(qs[gi][rs], t_len), kt[:, lbuf - window:], vt[:, lbuf - window:],
                                      masks[gi][0], knp, vnp, masks[gi][1], None))
        mx = jnp.maximum(jnp.maximum(res[0][1], res[1][1]), res[2][1])
        ws = [jnp.exp(l - mx) for _, l in res]
        inv = 1.0 / (ws[0] + ws[1] + ws[2])
        merged = (ws[0] * inv) * res[0][0] + (ws[1] * inv) * res[1][0] + (ws[2] * inv) * res[2][0]
        outs.append(_collapse_o(merged, t_len))
    o_ref[...] = jnp.concatenate(outs, axis=0).astype(o_ref.dtype)


def _dil_cached(p1, seq, n_b, t_len, tables, cache_kt, cache_vt, *, nb):
    rows = nb * t_len
    d_att = N_HEADS * HEAD_DIM
    d_kv = N_KV * HEAD_DIM
    lbuf = cache_kt.shape[3]
    assert rows % 8 == 0 and seq % rows == 0 and t_len <= BF16_ROWS
    for window, dil in DIL_PATTERNS:
        assert window <= lbuf and (PAST_LEN - lbuf) % dil == 0 and lbuf % dil == 0 and (dil == 1 or dil >= t_len)
    r0 = seq // rows
    c_spec = pl.BlockSpec((nb, N_KV, HEAD_DIM, lbuf), lambda i: (i, 0, 0, 0))
    t_spec = pl.BlockSpec((rows, LANES), lambda i: (r0 + i, 0))
    nq = d_att * len(DIL_PATTERNS)
    return pl.pallas_call(
        functools.partial(_dil_cached_kernel, nb=nb, t_len=t_len),
        out_shape=(jax.ShapeDtypeStruct((n_b * t_len, d_att), F32),
                   jax.ShapeDtypeStruct((n_b * t_len, d_kv), F32)),
        grid=(n_b // nb,),
        in_specs=[pl.BlockSpec((rows, d_att), lambda i: (i, 0)),
                  pl.BlockSpec((rows, d_att), lambda i: (i, 1)),
                  pl.BlockSpec((rows, d_att), lambda i: (i, 2)),
                  pl.BlockSpec((rows, d_kv), lambda i: (i, nq // d_kv)),
                  pl.BlockSpec((rows, d_kv), lambda i: (i, nq // d_kv + 1)),
                  t_spec, t_spec, t_spec, c_spec, c_spec],
        out_specs=(pl.BlockSpec((rows, d_att), lambda i: (i, 0)),
                   pl.BlockSpec((rows, d_kv), lambda i: (i, 0))),
        compiler_params=_cparams(1), name="dil_cached")(p1, p1, p1, p1, p1, *tables, cache_kt, cache_vt)


def _row_tile(m, pref):
    for t in (1024, 512, 256, 128, 64, 32, 16, 8):
        if t <= pref and m % t == 0:
            return t
    raise ValueError(m)


def kernel(x_prompt, x_sample, state_conv, state_ssm, cache_swa_k, cache_swa_v, cache_dil_k, cache_dil_v, norm_mix, norm_mlp, e_w_in, e_conv_w, e_conv_b, e_dt_bias, e_a_log, e_d_skip, e_gate_norm, e_sinks, e_w_out, o_w_in, o_w_out, mlp_w1, mlp_w2, norm_final):
    nbp, seq, d = x_prompt.shape
    n_b, t_len, _ = x_sample.shape
    assert nbp == 1 and d == N_HEADS * HEAD_DIM and norm_mix.shape[0] == 2
    ms = n_b * t_len
    m = seq + ms
    tms = (_row_tile(seq, 1024), _row_tile(ms, 512))
    d_kv = N_KV * HEAD_DIM
    d_bc = 2 * SSD_GROUPS * SSD_STATE
    conv_dim = d + d_bc
    both = lambda fn, *rows, **kw: tuple(
        fn(*(r[s] for r in rows), tm=tms[s], name=kw["name"] + ("_p", "_s")[s], **{k: v for k, v in kw.items() if k != "name"})
        for s in range(2))

    h = (x_prompt.reshape(seq, d), x_sample.reshape(ms, d))
    tables = _rope_tables(m, seq, t_len, _row_tile(math.gcd(seq, ms), 512))
    eexp = (lax.broadcasted_iota(jnp.int32, (LANES, d), 1) // HEAD_DIM
            == lax.broadcasted_iota(jnp.int32, (LANES, d), 0)).astype(BF16)
    keys_minor = lambda c: jnp.transpose(c[0], (0, 2, 3, 1))

    wi = e_w_in[0]
    c_z, c_xbc, c_dt, c_q = 0, d, d + conv_dim, d + conv_dim + N_HEADS
    c_k, c_v = c_q + d, c_q + d + d_kv
    tn0 = 512
    wi16 = wi.astype(BF16)
    w0_q, w0_kv = wi16[:, c_q:c_q + d], wi16[:, c_k:c_v + d_kv]
    w0_dt = jnp.pad(wi16[:, c_dt:c_dt + N_HEADS], ((0, 0), (0, LANES - N_HEADS)))
    assert c_z == 0 and (c_xbc + d) % tn0 == 0 and d_bc % tn0 == 0 and d % tn0 == 0
    segs0 = [(wi16, 0, 2 * d // tn0), (w0_q, 0, d // tn0), (wi16, (c_xbc + d) // tn0, d_bc // tn0),
             (w0_kv, 0, 2 * d_kv // tn0)]
    off0 = {"z": 0, "xs": d, "q": 2 * d, "bc": 3 * d, "k": 3 * d + d_bc, "v": 3 * d + d_bc + d_kv}
    g0 = norm_mix[0]
    (p0_p, dt_p), (p0_s, dt_s) = both(
        lambda x, tm, name: _norm_matmul_multi(x, g0, segs0, w0_dt, tm=tm, tn=tn0, name=name), h, name="l0_in_proj")

    cw, cb = e_conv_w[0], e_conv_b[0]
    wx, wbc = cw[:, :d], cw[:, d:]
    bx, bbc = cb[:d].reshape(1, d), cb[d:].reshape(1, d_bc)
    pad_h = lambda v: jnp.pad(v.reshape(1, N_HEADS), ((0, 0), (0, LANES - N_HEADS)))
    dtb, alog = pad_h(e_dt_bias[0]), pad_h(e_a_log[0])
    dskip = jnp.repeat(e_d_skip[0], HEAD_DIM).reshape(1, d)
    gn = e_gate_norm[0].reshape(1, d)

    ssd_p, tail_x, tail_bc, p_ssm = _ssd_prompt(p0_p, dt_p, off0, seq, wx, wbc, bx, bbc, dtb, alog, dskip, gn, eexp)
    epad = jnp.pad(state_conv[0], ((0, 0), (0, 1), (0, 0))).reshape(ms, conv_dim)
    ssd_s, s_ssm = _ssd_sample(p0_s, dt_s, off0, seq, n_b, t_len, epad, state_ssm[0], wx, wbc, bx, bbc, dtb, alog,
                               dskip, gn, eexp, nb=4)

    att_p, krot0_p = _band_attention(p0_p, seq, off0["q"], off0["k"], off0["v"], tables, e_sinks[0], dil=1,
                                     max_dist=SWA_WINDOW - 1, nkv=N_KV, want_lse=False, want_krot=True,
                                     name="swa_prompt")
    sink_col = jnp.broadcast_to(e_sinks[0].reshape(N_KV, REP).T[:, None, :], (REP, t_len, N_KV)).reshape(-1, 1)
    att_s, krot0_s = _swa_cached(p0_s, off0, seq, n_b, t_len, tables, keys_minor(cache_swa_k),
                                 keys_minor(cache_swa_v), sink_col, nb=4)

    w1_all, w2_all = mlp_w1.astype(BF16), mlp_w2.astype(BF16)

    def mlp(hh, layer, g_out=None):
        g = norm_mlp[layer]
        return both(lambda x, tm, name: _mlp(x, g, w1_all, w2_all, layer, g_out, tm=tm, tk=512, name=name), hh,
                    name=f"l{layer}_mlp")

    wo = e_w_out[0].astype(BF16)
    wo_a, wo_b = wo[:d], wo[d:]
    h = both(lambda a, b, r, tm, name: _matmul_resid([a, b], [wo_a, wo_b], r, tm=tm, tn=1024, name=name),
             (ssd_p, ssd_s), (att_p, att_s), h, name="l0_out_proj")
    h = mlp(h, 0)

    n_pat = len(DIL_PATTERNS)
    g1, w1i = norm_mix[1], o_w_in[0].astype(BF16)
    p1_p, p1_s = both(lambda x, tm, name: _norm_matmul(x, g1, w1i, tm=tm, tn=1024, name=name), h, name="l1_in_proj")
    k_off, v_off = n_pat * d, n_pat * d + d_kv
    os_, ls_ = [], []
    krot1_p = None
    for gi, (window, dil) in enumerate(DIL_PATTERNS):
        res = _band_attention(p1_p, seq, gi * d, k_off, v_off, tables, None, dil=dil, max_dist=window // dil,
                              nkv=(2 if BAND * dil * N_HEADS * HEAD_DIM * 4 > 8 * 2 ** 20 else N_KV),
                              want_lse=True, want_krot=(gi == 0), name=f"dil_prompt_{dil}")
        os_.append(res[0])
        ls_.append(res[1])
        if gi == 0:
            krot1_p = res[2]
    merged_p = _merge(os_, ls_, eexp, tm=_row_tile(seq, 256))
    dil_s, krot1_s = _dil_cached(p1_s, seq, n_b, t_len, tables, keys_minor(cache_dil_k), keys_minor(cache_dil_v),
                                 nb=2)
    wo1 = o_w_out[0].astype(BF16)
    h = both(lambda a, r, tm, name: _matmul_resid([a], [wo1], r, tm=tm, tn=1024, name=name),
             (merged_p, dil_s.astype(BF16)), h, name="l1_out_proj")
    y_p, y_s = mlp(h, 1, norm_final)

    keep_swa = min(SWA_WINDOW, seq)
    keep_dil = min(max(w for w, _ in DIL_PATTERNS), seq)
    kv4 = lambda a: a.reshape(a.shape[0], N_KV, HEAD_DIM)
    y_prompt = y_p.reshape(1, seq, d)
    y_sample = y_s.reshape(n_b, t_len, d)
    p_conv = jnp.concatenate([tail_x[8 - (SSD_CONV - 1):], tail_bc[8 - (SSD_CONV - 1):]], axis=1)[None, None]
    p_swa_k = kv4(krot0_p[seq - keep_swa:])[None, None]
    p_swa_v = kv4(p0_p[seq - keep_swa:, off0["v"]:off0["v"] + d_kv])[None, None]
    p_dil_k = kv4(krot1_p[seq - keep_dil:])[None, None]
    p_dil_v = kv4(p1_p[seq - keep_dil:, v_off:v_off + d_kv])[None, None]
    xbc_s = jnp.concatenate([p0_s[:, off0["xs"]:off0["xs"] + d], p0_s[:, off0["bc"]:off0["bc"] + d_bc]], axis=1)
    s_conv = xbc_s.reshape(n_b, t_len, conv_dim)[:, t_len - (SSD_CONV - 1):][None]
    bt = lambda a: a.reshape(n_b, t_len, N_KV, HEAD_DIM)[None]
    s_swa_k = bt(krot0_s)
    s_swa_v = bt(p0_s[:, off0["v"]:off0["v"] + d_kv])
    s_dil_k = bt(krot1_s)
    s_dil_v = bt(p1_s[:, v_off:v_off + d_kv])
    return (y_prompt, y_sample, p_conv, p_ssm[None, None], p_swa_k, p_swa_v, p_dil_k, p_dil_v,
            s_conv, s_ssm[None], s_swa_k, s_swa_v, s_dil_k, s_dil_v)
```

```python
import functools
import math

import jax
import jax.numpy as jnp
from jax import lax
from jax.experimental import pallas as pl
from jax.experimental.pallas import tpu as pltpu

F32 = jnp.float32
BF16 = jnp.bfloat16

NORM_EPS = 1e-5
HEAD_DIM = 64
ROT_HALF = 8
ROPE_THETA = 500000.0
PAST_LEN = 8192
N_HEADS = 32
N_KV = 8
REP = N_HEADS // N_KV
SSD_GROUPS = 4
SSD_STATE = 128
SSD_CONV = 4
SSD_CHUNK = 128
SWA_WINDOW = 128
DIL_PATTERNS = ((128, 1), (512, 4), (2048, 16))
BAND = 128
ATT_SCALE = HEAD_DIM ** -0.5
LOG2E = math.log2(math.e)
LN2 = math.log(2.0)
NEG = -1e30
UNITS_IN_FLIGHT = 4
_LOG2_HEAD_DIM = 6

LANES = 128
BF16_ROWS = 16
PERM_CHUNK = 256
VMEM_LIMIT_BYTES = 56 * 1024 * 1024

_NT = (((1,), (1,)), ((), ()))
_TN = (((0,), (0,)), ((), ()))


def _cparams(n_axes):
    return pltpu.CompilerParams(dimension_semantics=("arbitrary",) * n_axes,
                                vmem_limit_bytes=VMEM_LIMIT_BYTES)


def _silu(x):
    return x * jax.nn.sigmoid(x)


def _tile_lanes(t, width):
    k = width // t.shape[1]
    return t if k == 1 else jnp.concatenate([t] * k, axis=1)


def _rope(x, cos_t, sa_t, sb_t):
    w = x.shape[1]
    return (x * _tile_lanes(cos_t, w)
            + pltpu.roll(x, w - ROT_HALF, 1) * _tile_lanes(sa_t, w)
            + pltpu.roll(x, ROT_HALF, 1) * _tile_lanes(sb_t, w))


def _low_half(shape):
    return (lax.broadcasted_iota(jnp.int32, shape, 1) & (LANES - 1)) < HEAD_DIM


def _rope_table_kernel(cos_ref, sa_ref, sb_ref, *, tm, seq, dec_seq):
    i = pl.program_id(0)
    row = i * tm + lax.broadcasted_iota(jnp.int32, (tm, LANES), 0)
    lane = lax.broadcasted_iota(jnp.int32, (tm, LANES), 1)
    pos = jnp.where(row < seq, row, PAST_LEN + ((row - seq) & (dec_seq - 1)))
    c = lane & (HEAD_DIM - 1)
    f = (c & (ROT_HALF - 1)).astype(F32)
    inv_freq = jnp.exp(f * (-math.log(ROPE_THETA) / ROT_HALF))
    ang = pos.astype(F32) * inv_freq
    cs = jnp.cos(ang)
    sn = jnp.sin(ang)
    cos_ref[...] = jnp.where(c < 2 * ROT_HALF, cs, 1.0)
    sa_ref[...] = jnp.where(c < ROT_HALF, -sn, 0.0)
    sb_ref[...] = jnp.where((c >= ROT_HALF) & (c < 2 * ROT_HALF), sn, 0.0)


def _rope_tables(m, seq, dec_seq, tm):
    assert dec_seq & (dec_seq - 1) == 0 and m % tm == 0
    shp = jax.ShapeDtypeStruct((m, LANES), F32)
    spec = pl.BlockSpec((tm, LANES), lambda i: (i, 0))
    return pl.pallas_call(
        functools.partial(_rope_table_kernel, tm=tm, seq=seq, dec_seq=dec_seq),
        out_shape=(shp, shp, shp), grid=(m // tm,), out_specs=(spec, spec, spec),
        compiler_params=_cparams(1), name="rope_tables")()


def _rmsnorm_bf16(x, g):
    ms = jnp.mean(x * x, axis=-1, keepdims=True)
    return (x * lax.rsqrt(ms + NORM_EPS) * g).astype(BF16)


def _norm_mm_kernel(x_ref, g_ref, w_ref, o_ref, *rest, emit):
    xn_ref = rest[-1]

    @pl.when(pl.program_id(1) == 0)
    def _():
        xn_ref[...] = _rmsnorm_bf16(x_ref[...], g_ref[...])

    w = w_ref[...]
    if emit:
        w = w.astype(BF16)
        rest[0][...] = w
    o_ref[...] = jnp.dot(xn_ref[...], w, preferred_element_type=F32)


def _norm_matmul(x, g, w, *, tm, tn, name, emit=False):
    m, k = x.shape
    n = w.shape[1]
    assert m % tm == 0 and n % tn == 0 and w.shape[0] == k and (not emit or m == tm)
    w_spec = pl.BlockSpec((k, tn), lambda i, j: (0, j))
    o_spec = pl.BlockSpec((tm, tn), lambda i, j: (i, j))
    o_shape = jax.ShapeDtypeStruct((m, n), F32)
    return pl.pallas_call(
        functools.partial(_norm_mm_kernel, emit=emit),
        out_shape=(o_shape, jax.ShapeDtypeStruct((k, n), BF16)) if emit else o_shape,
        grid=(m // tm, n // tn),
        in_specs=[pl.BlockSpec((tm, k), lambda i, j: (i, 0)),
                  pl.BlockSpec((1, k), lambda i, j: (0, 0)),
                  w_spec],
        out_specs=(o_spec, w_spec) if emit else o_spec,
        scratch_shapes=[pltpu.VMEM((tm, k), BF16)],
        compiler_params=_cparams(2), name=name)(x, g.reshape(1, k), w)


def _norm_mm_multi_kernel(*refs, owners):
    n_w = len(owners)
    x_ref, g_ref = refs[:2]
    w_refs = refs[2:2 + n_w]
    ws_ref, o_ref, os_ref, xn_ref = refs[2 + n_w:]
    j = pl.program_id(1)

    @pl.when(j == 0)
    def _():
        xn_ref[...] = _rmsnorm_bf16(x_ref[...], g_ref[...])
        os_ref[...] = jnp.dot(xn_ref[...], ws_ref[...], preferred_element_type=F32)

    for w_ref, tiles in zip(w_refs, owners):
        own = functools.reduce(jnp.logical_or, [(j >= lo) & (j < hi) for lo, hi in tiles])

        @pl.when(own)
        def _():
            o_ref[...] = jnp.dot(xn_ref[...], w_ref[...], preferred_element_type=F32)


def _norm_matmul_multi(x, g, segs, w_small, *, tm, tn, name):
    m, k = x.shape
    arrays, owners, maps = [], [], []
    start = 0
    for w, cb0, n in segs:
        if not any(w is a for a in arrays):
            arrays.append(w)
            owners.append([])
            maps.append([])
        idx = [i for i, a in enumerate(arrays) if a is w][0]
        owners[idx].append((start, start + n))
        maps[idx].append((start, n, cb0))
        start += n
    n_tiles = start

    def index_map(pieces):
        def col(i, j):
            c = pieces[-1][2] + pieces[-1][1] - 1
            for a, n, cb0 in reversed(pieces):
                c = jnp.where(j < a + n, cb0 + jnp.maximum(j - a, 0), c)
            return 0, c
        return col

    n_small = w_small.shape[1]
    return pl.pallas_call(
        functools.partial(_norm_mm_multi_kernel, owners=tuple(tuple(o) for o in owners)),
        out_shape=(jax.ShapeDtypeStruct((m, n_tiles * tn), F32), jax.ShapeDtypeStruct((m, n_small), F32)),
        grid=(m // tm, n_tiles),
        in_specs=[pl.BlockSpec((tm, k), lambda i, j: (i, 0), pipeline_mode=pl.Buffered(1)),
                  pl.BlockSpec((1, k), lambda i, j: (0, 0))]
        + [pl.BlockSpec((k, tn), index_map(p)) for p in maps]
        + [pl.BlockSpec((k, n_small), lambda i, j: (0, 0))],
        out_specs=(pl.BlockSpec((tm, tn), lambda i, j: (i, j)), pl.BlockSpec((tm, n_small), lambda i, j: (i, 0))),
        scratch_shapes=[pltpu.VMEM((tm, k), BF16)],
        compiler_params=_cparams(2), name=name)(x, g.reshape(1, k), *arrays, w_small)


def _mm_resid_kernel(*refs, n_in, emit):
    x_refs, w_refs = refs[:n_in], refs[n_in:2 * n_in]
    r_ref, o_ref = refs[2 * n_in], refs[2 * n_in + 1]
    wo_refs = refs[2 * n_in + 2:]
    acc = r_ref[...]
    for s, (x_ref, w_ref) in enumerate(zip(x_refs, w_refs)):
        w = w_ref[...]
        if emit:
            w = w.astype(BF16)
            wo_refs[s][...] = w
        acc = acc + jnp.dot(x_ref[...], w, preferred_element_type=F32)
    o_ref[...] = acc


def _matmul_resid(xs, w, resid, *, tm, tn, name, emit=False):
    m, n = resid.shape
    k = xs[0].shape[1]
    assert m % tm == 0 and n % tn == 0 and (not emit or (m == tm and w.shape == (k * len(xs), n)))
    w_args = [w] * len(xs) if emit else list(w)
    in_specs = [pl.BlockSpec((tm, k), lambda i, j: (i, 0)) for _ in xs]
    in_specs += [pl.BlockSpec((k, tn), functools.partial(lambda i, j, s: (s, j), s=s if emit else 0))
                 for s in range(len(xs))]
    in_specs.append(pl.BlockSpec((tm, tn), lambda i, j: (i, j)))
    o_spec = pl.BlockSpec((tm, tn), lambda i, j: (i, j))
    o_shape = jax.ShapeDtypeStruct((m, n), F32)
    if emit:
        out_shape = (o_shape,) + (jax.ShapeDtypeStruct((k, n), BF16),) * len(xs)
        out_specs = (o_spec,) + (pl.BlockSpec((k, tn), lambda i, j: (0, j)),) * len(xs)
    else:
        out_shape, out_specs = o_shape, o_spec
    return pl.pallas_call(
        functools.partial(_mm_resid_kernel, n_in=len(xs), emit=emit),
        out_shape=out_shape,
        grid=(m // tm, n // tn),
        in_specs=in_specs,
        out_specs=out_specs,
        compiler_params=_cparams(2), name=name)(*xs, *w_args, resid)


def _mlp_kernel(x_ref, g_ref, w1_ref, w2_ref, gout_ref, o_ref, *rest, norm_out, emit):
    xn_ref = rest[-1]
    k = pl.program_id(1)

    @pl.when(k == 0)
    def _():
        x = x_ref[...]
        xn_ref[...] = _rmsnorm_bf16(x, g_ref[...])
        o_ref[...] = x

    w1, w2 = w1_ref[...], w2_ref[...]
    if emit:
        w1, w2 = w1.astype(BF16), w2.astype(BF16)
        rest[0][...] = w1
        rest[1][...] = w2
    u = jnp.dot(xn_ref[...], w1, preferred_element_type=F32)
    u = jnp.square(jnp.maximum(u, 0.0)).astype(BF16)
    o_ref[...] += jnp.dot(u, w2, preferred_element_type=F32)

    if norm_out:
        @pl.when(k == pl.num_programs(1) - 1)
        def _():
            o = o_ref[...]
            ms = jnp.mean(o * o, axis=-1, keepdims=True)
            o_ref[...] = o * lax.rsqrt(ms + NORM_EPS) * gout_ref[...]


def _mlp(x, g, w1, w2, g_out, *, tm, tk, name, layer=None):
    m, d = x.shape
    emit = layer is not None
    hid = w1.shape[-1]
    assert m % tm == 0 and hid % tk == 0 and (not emit or m == tm)
    w1o_spec = pl.BlockSpec((d, tk), lambda i, k: (0, k))
    w2o_spec = pl.BlockSpec((tk, d), lambda i, k: (k, 0))
    if emit:
        w_specs = [pl.BlockSpec((None, d, tk), lambda i, k: (layer, 0, k)),
                   pl.BlockSpec((None, tk, d), lambda i, k: (layer, k, 0))]
        out_shape = (jax.ShapeDtypeStruct((m, d), F32), jax.ShapeDtypeStruct((d, hid), BF16),
                     jax.ShapeDtypeStruct((hid, d), BF16))
        out_specs = (pl.BlockSpec((tm, d), lambda i, k: (i, 0)), w1o_spec, w2o_spec)
    else:
        w_specs = [w1o_spec, w2o_spec]
        out_shape = jax.ShapeDtypeStruct((m, d), F32)
        out_specs = pl.BlockSpec((tm, d), lambda i, k: (i, 0))
    return pl.pallas_call(
        functools.partial(_mlp_kernel, norm_out=g_out is not None, emit=emit),
        out_shape=out_shape,
        grid=(m // tm, hid // tk),
        in_specs=[pl.BlockSpec((tm, d), lambda i, k: (i, 0)),
                  pl.BlockSpec((1, d), lambda i, k: (0, 0))] + w_specs + [pl.BlockSpec((1, d), lambda i, k: (0, 0))],
        out_specs=out_specs,
        scratch_shapes=[pltpu.VMEM((tm, d), BF16)],
        compiler_params=_cparams(2), name=name)(x, g.reshape(1, d), w1, w2,
                                                (g if g_out is None else g_out).reshape(1, d))


def _causal_conv(raw, tail, w_ref, b_ref):
    n = raw.shape[0]
    full = jnp.concatenate([tail, raw], axis=0)
    acc = b_ref[...]
    for i in range(SSD_CONV):
        s = SSD_CONV - 1 - i
        acc = acc + full[8 - s:8 - s + n] * w_ref[i:i + 1, :]
    return _silu(acc)


def _gate_groupnorm(y, z, gn):
    y = y * _silu(z)
    gw = y.shape[1] // SSD_GROUPS
    outs = []
    for g in range(SSD_GROUPS):
        yg = y[:, g * gw:(g + 1) * gw]
        ms = jnp.mean(yg * yg, axis=-1, keepdims=True)
        outs.append(yg * lax.rsqrt(ms + NORM_EPS))
    return jnp.concatenate(outs, axis=1) * gn


def _ssd_prompt_kernel(z_ref, xs_ref, bc_ref, dt_ref, wx_ref, wbc_ref, bx_ref, bbc_ref, dtb_ref, alog_ref,
                       dskip_ref, gn_ref, eexp_ref, y_ref, tailx_ref, tailbc_ref, h_ref, y_s):
    c = pl.program_id(0)
    q = SSD_CHUNK
    gs = SSD_STATE
    hp = HEAD_DIM

    @pl.when(c == 0)
    def _():
        tailx_ref[...] = jnp.zeros_like(tailx_ref)
        tailbc_ref[...] = jnp.zeros_like(tailbc_ref)
        h_ref[...] = jnp.zeros_like(h_ref)

    xs_raw = xs_ref[...]
    bc_raw = bc_ref[...]
    xs_c = _causal_conv(xs_raw, tailx_ref[...], wx_ref, bx_ref)
    bc_c = _causal_conv(bc_raw, tailbc_ref[...], wbc_ref, bbc_ref)
    tailx_ref[...] = xs_raw[q - 8:q]
    tailbc_ref[...] = bc_raw[q - 8:q]

    lane = lax.broadcasted_iota(jnp.int32, (q, LANES), 1)
    row = lax.broadcasted_iota(jnp.int32, (q, LANES), 0)
    dt = jax.nn.softplus(dt_ref[...] + dtb_ref[...])
    a = -jnp.exp(alog_ref[...])
    la = jnp.where(lane < N_HEADS, dt * a, 0.0)
    acs = la
    s = 1
    while s < q:
        acs = acs + jnp.where(row >= s, pltpu.roll(acs, s, 0), 0.0)
        s *= 2
    acs_t = acs.T
    acs_last = acs[q - 1:q, :]
    to_end = jnp.exp(acs_last - acs)
    eacs = jnp.exp(acs)
    cdec = jnp.exp(acs_last)
    causal = (lax.broadcasted_iota(jnp.int32, (q, q), 0) >= lax.broadcasted_iota(jnp.int32, (q, q), 1))

    stacked = jnp.concatenate([dt, eacs, to_end], axis=0)
    hi = stacked.astype(BF16)
    lo = (stacked - hi.astype(F32)).astype(BF16)
    e = eexp_ref[...]
    full = jnp.dot(hi, e, preferred_element_type=F32) + jnp.dot(lo, e, preferred_element_type=F32)
    xdt = xs_c * full[0:q]
    eacs_f = full[q:2 * q]
    w_f = xdt * full[2 * q:3 * q]
    low = _low_half((q, LANES))
    top = lax.broadcasted_iota(jnp.int32, (2 * hp, gs), 0) < hp

    rep = N_HEADS // SSD_GROUPS
    for g in range(SSD_GROUPS):
        bg = bc_c[:, g * gs:(g + 1) * gs].astype(BF16)
        cg = bc_c[:, SSD_GROUPS * gs + g * gs:SSD_GROUPS * gs + (g + 1) * gs].astype(BF16)
        cb = lax.dot_general(cg, bg, _NT, preferred_element_type=F32)
        for rp in range(rep // 2):
            h0 = g * rep + 2 * rp
            sl = slice(h0 * hp, (h0 + 2) * hp)
            ms = []
            for h in (h0, h0 + 1):
                lmat = jnp.exp(jnp.where(causal, acs[:, h:h + 1] - acs_t[h:h + 1, :], -jnp.inf))
                ms.append((cb * lmat).astype(BF16))
            xp = xdt[:, sl]
            rhs = jnp.concatenate([jnp.where(low, xp, 0.0), jnp.where(low, 0.0, xp)], axis=0).astype(BF16)
            y = jnp.dot(jnp.concatenate(ms, axis=1), rhs, preferred_element_type=F32)
            hs = h_ref[h0:h0 + 2].reshape(2 * hp, gs)
            y = y + eacs_f[:, sl] * lax.dot_general(cg, hs.astype(BF16), _NT, preferred_element_type=F32)
            st = lax.dot_general(w_f[:, sl].astype(BF16), bg, _TN, preferred_element_type=F32)
            cd = jnp.where(top, cdec[:, h0:h0 + 1], cdec[:, h0 + 1:h0 + 2])
            h_ref[h0:h0 + 2] = (hs * cd + st).reshape(2, hp, gs)
            y_s[:, sl] = y + dskip_ref[:, sl] * xs_c[:, sl]

    y_ref[...] = _gate_groupnorm(y_s[...], z_ref[...], gn_ref[...]).astype(y_ref.dtype)


def _ssd_prompt(p0, dtr, off, seq, wx, wbc, bx, bbc, dtb, alog, dskip, gn, eexp):
    q = SSD_CHUNK
    d_in = wx.shape[1]
    d_bc = wbc.shape[1]
    assert seq % q == 0
    const = lambda shape: pl.BlockSpec(shape, lambda c: (0,) * len(shape))
    return pl.pallas_call(
        _ssd_prompt_kernel,
        out_shape=(jax.ShapeDtypeStruct((seq, d_in), BF16),
                   jax.ShapeDtypeStruct((8, d_in), F32),
                   jax.ShapeDtypeStruct((8, d_bc), F32),
                   jax.ShapeDtypeStruct((N_HEADS, HEAD_DIM, SSD_STATE), F32)),
        grid=(seq // q,),
        in_specs=[pl.BlockSpec((q, d_in), lambda c: (c, off["z"] // d_in)),
                  pl.BlockSpec((q, d_in), lambda c: (c, off["xs"] // d_in)),
                  pl.BlockSpec((q, d_bc), lambda c: (c, off["bc"] // d_bc)),
                  pl.BlockSpec((q, LANES), lambda c: (c, 0)),
                  const((SSD_CONV, d_in)), const((SSD_CONV, d_bc)), const((1, d_in)), const((1, d_bc)),
                  const((1, LANES)), const((1, LANES)), const((1, d_in)), const((1, d_in)), const((LANES, d_in))],
        out_specs=(pl.BlockSpec((q, d_in), lambda c: (c, 0)),
                   const((8, d_in)), const((8, d_bc)), const((N_HEADS, HEAD_DIM, SSD_STATE))),
        scratch_shapes=[pltpu.VMEM((q, d_in), F32)],
        compiler_params=_cparams(1), name="ssd_prompt")(p0, p0, p0, dtr, wx, wbc, bx, bbc, dtb, alog, dskip, gn, eexp)


def _split3(x):
    hi = x.astype(BF16)
    r1 = x - hi.astype(F32)
    mid = r1.astype(BF16)
    lo = (r1 - mid.astype(F32)).astype(BF16)
    return hi, mid, lo


def _ssd_sample_kernel(z_ref, xs_ref, bc_ref, dt_ref, ex_ref, ebc_ref, st_ref, wx_ref, wbc_ref, bx_ref, bbc_ref,
                       dtb_ref, alog_ref, dskip_ref, gn_ref, eexp_ref, y_ref, sto_ref, *, nb, t_len):
    rows = nb * t_len
    gs = SSD_STATE
    gw = xs_ref.shape[1] // SSD_GROUPS
    rep = N_HEADS // SSD_GROUPS

    def tcol(width):
        r = lax.broadcasted_iota(jnp.int32, (rows, width), 0)
        return r & (t_len - 1), r >> int(math.log2(t_len))

    def conv(raw, est, w_ref, b_ref):
        t, _ = tcol(raw.shape[1])
        acc = b_ref[...]
        for i in range(SSD_CONV):
            s = SSD_CONV - 1 - i
            if s == 0:
                sh = raw
            else:
                k = (rows - (SSD_CONV - 1) + s) % rows
                sh = jnp.where(t >= s, pltpu.roll(raw, s, 0), est if k == 0 else pltpu.roll(est, k, 0))
            acc = acc + sh * w_ref[i:i + 1, :]
        return _silu(acc)

    xs_c = conv(xs_ref[...], ex_ref[...], wx_ref, bx_ref)
    bc_c = conv(bc_ref[...], ebc_ref[...], wbc_ref, bbc_ref)
    bm = bc_c[:, :SSD_GROUPS * gs]
    cm = bc_c[:, SSD_GROUPS * gs:]

    t1, _ = tcol(LANES)
    lane = lax.broadcasted_iota(jnp.int32, (rows, LANES), 1)
    dt = jax.nn.softplus(dt_ref[...] + dtb_ref[...])
    a = -jnp.exp(alog_ref[...])
    la = jnp.where(lane < N_HEADS, dt * a, 0.0)
    acs = la
    for s in range(1, t_len):
        acs = acs + jnp.where(t1 >= s, pltpu.roll(la, s, 0), 0.0)
    alast = jnp.where(t1 == t_len - 1, acs, 0.0)
    for u in range(1, t_len):
        alast = alast + jnp.where(t1 == t_len - 1 - u, pltpu.roll(acs, rows - u, 0), 0.0)
    parts = [dt, jnp.exp(acs), jnp.exp(alast - acs), jnp.exp(alast)]
    for k in range(1, t_len):
        parts.append(jnp.exp(acs - pltpu.roll(acs, k, 0)))
    stacked = jnp.concatenate(parts, axis=0)
    hi, mid, lo = _split3(stacked)
    e = eexp_ref[...]
    full = (jnp.dot(hi, e, preferred_element_type=F32) + jnp.dot(mid, e, preferred_element_type=F32)
            + jnp.dot(lo, e, preferred_element_type=F32))
    dt_f, eacs_f, toend_f, cdec_f = (full[i * rows:(i + 1) * rows] for i in range(4))
    dec_f = [None] + [full[(3 + k) * rows:(4 + k) * rows] for k in range(1, t_len)]

    tw, _ = tcol(xs_c.shape[1])
    xdt = xs_c * dt_f
    y = jnp.zeros_like(xs_c)
    for k in range(t_len):
        bmk = bm if k == 0 else pltpu.roll(bm, k, 0)
        prod = cm * bmk
        cbs = []
        for g in range(SSD_GROUPS):
            sg = jnp.sum(prod[:, g * gs:(g + 1) * gs], axis=-1, keepdims=True)
            cbs.append(jnp.broadcast_to(sg, (rows, gw)))
        cb_f = jnp.concatenate(cbs, axis=1)
        if k == 0:
            y = y + cb_f * xdt
        else:
            y = y + jnp.where(tw >= k, cb_f * dec_f[k] * pltpu.roll(xdt, k, 0), 0.0)

    cm16 = cm.astype(BF16)
    bm16 = bm.astype(BF16)
    w_f = (toend_f * xdt)
    _, bg_ = tcol(gw)
    ones_blk = jnp.ones((3, gs), BF16)
    yoffs = []
    for g in range(SSD_GROUPS):
        cg = cm16[:, g * gs:(g + 1) * gs]
        rhs_top = jnp.concatenate([bm16[:, g * gs:(g + 1) * gs], jnp.zeros((rows, gs), BF16)], axis=1)
        rhs_mid = jnp.concatenate([jnp.zeros((3, gs), BF16), ones_blk], axis=1)
        rhs = jnp.concatenate([rhs_top, rhs_mid, jnp.zeros((13, 2 * gs), BF16)], axis=0)
        wg = w_f[:, g * gw:(g + 1) * gw]
        cdg = cdec_f[:, g * gw:(g + 1) * gw]
        yg = jnp.zeros((rows, gw), F32)
        for b in range(nb):
            hb = st_ref[b, g * rep:(g + 1) * rep].reshape(gw, gs)
            yb = lax.dot_general(cg, hb.astype(BF16), _NT, preferred_element_type=F32)
            yg = jnp.where(bg_ == b, yb, yg)
            wb = jnp.where(bg_ == b, wg, 0.0).astype(BF16)
            d_hi, d_mid, d_lo = _split3(cdg[b * t_len:b * t_len + 1])
            lhs = jnp.concatenate([wb, d_hi, d_mid, d_lo, jnp.zeros((13, gw), BF16)], axis=0)
            sd = lax.dot_general(lhs, rhs, _TN, preferred_element_type=F32)
            sto_ref[b, g * rep:(g + 1) * rep] = (hb * sd[:, gs:] + sd[:, :gs]).reshape(rep, HEAD_DIM, gs)
        yoffs.append(yg)
    y = y + eacs_f * jnp.concatenate(yoffs, axis=1) + dskip_ref[...] * xs_c
    y_ref[...] = _gate_groupnorm(y, z_ref[...], gn_ref[...]).astype(y_ref.dtype)


def _ssd_sample(p0, dtr, off, seq, n_b, t_len, epad, state, wx, wbc, bx, bbc, dtb, alog, dskip, gn, eexp, *, nb):
    rows = nb * t_len
    d_in = wx.shape[1]
    d_bc = wbc.shape[1]
    assert n_b % nb == 0 and t_len & (t_len - 1) == 0
    const = lambda shape: pl.BlockSpec(shape, lambda i: (0,) * len(shape))
    st_spec = pl.BlockSpec((nb, N_HEADS, HEAD_DIM, SSD_STATE), lambda i: (i, 0, 0, 0))
    return pl.pallas_call(
        functools.partial(_ssd_sample_kernel, nb=nb, t_len=t_len),
        out_shape=(jax.ShapeDtypeStruct((n_b * t_len, d_in), BF16),
                   jax.ShapeDtypeStruct(state.shape, F32)),
        grid=(n_b // nb,),
        in_specs=[pl.BlockSpec((rows, d_in), lambda i: (i, off["z"] // d_in)),
                  pl.BlockSpec((rows, d_in), lambda i: (i, off["xs"] // d_in)),
                  pl.BlockSpec((rows, d_bc), lambda i: (i, off["bc"] // d_bc)),
                  pl.BlockSpec((rows, LANES), lambda i: (i, 0)),
                  pl.BlockSpec((rows, d_in), lambda i: (i, 0)),
                  pl.BlockSpec((rows, d_bc), lambda i: (i, d_in // d_bc)),
                  st_spec,
                  const((SSD_CONV, d_in)), const((SSD_CONV, d_bc)), const((1, d_in)), const((1, d_bc)),
                  const((1, LANES)), const((1, LANES)), const((1, d_in)), const((1, d_in)),
                  const((LANES, d_in))],
        out_specs=(pl.BlockSpec((rows, d_in), lambda i: (i, 0)), st_spec),
        compiler_params=_cparams(1), name="ssd_sample")(
            p0, p0, p0, dtr, epad, epad, state, wx, wbc, bx, bbc, dtb, alog, dskip, gn, eexp)


def _perm_matrix(dil):
    i = jnp.arange(PERM_CHUNK)
    blk = BF16_ROWS * dil
    src = (i // blk) * blk + (i % BF16_ROWS) * dil + (i % blk) // BF16_ROWS
    return (src[:, None] == jnp.arange(PERM_CHUNK)[None, :]).astype(BF16)


def _band_kernel(*refs, dil, max_dist, use_sink, want_lse, want_krot, nkv, nj, res_per_iter):
    it = iter(refs)
    q_ref, k_ref, v_ref, cos_ref, sa_ref, sb_ref = (next(it) for _ in range(6))
    p_ref, pt_ref = (next(it), next(it)) if dil > 1 else (None, None)
    sink_ref = next(it) if use_sink else None
    o_ref = next(it)
    lse_ref = next(it) if want_lse else None
    krot_ref = next(it) if want_krot else None
    qlo_s, qhi_s, kd_s, vd_s, kdp_s, vdp_s, op_s = (next(it) for _ in range(7))
    lsep_s = next(it) if want_lse else None
    lse_s = next(it) if (want_lse and nj > 1) else None

    c = pl.program_id(0)
    j = pl.program_id(1)
    rows, wq = q_ref.shape
    wk = k_ref.shape[1]
    n_chunk = rows // PERM_CHUNK if dil > 1 else 0
    n_grp = BAND // BF16_ROWS

    def perm(x):
        if dil == 1:
            return x
        x16 = x.astype(BF16)
        return jnp.concatenate(
            [jnp.dot(p_ref[...], x16[k * PERM_CHUNK:(k + 1) * PERM_CHUNK], preferred_element_type=F32)
             for k in range(n_chunk)], axis=0)

    def unperm(xp16):
        if dil == 1:
            return [xp16]
        return [jnp.dot(pt_ref[...], xp16[k * PERM_CHUNK:(k + 1) * PERM_CHUNK], preferred_element_type=F32)
                for k in range(n_chunk)]

    tabs = (cos_ref[...], sa_ref[...], sb_ref[...])
    q = perm(_rope(q_ref[...], *tabs) * (ATT_SCALE * LOG2E))
    lowq = _low_half((rows, wq))
    qlo_s[...] = jnp.where(lowq, q, 0.0).astype(BF16)
    qhi_s[...] = jnp.where(lowq, 0.0, q).astype(BF16)
    kr = _rope(k_ref[...], *tabs)
    if want_krot:
        krot_ref[...] = kr
    lowk = _low_half((rows, LANES))
    for x, d_s in ((perm(kr), kd_s), (perm(v_ref[...]), vd_s)):
        up = pltpu.roll(x, HEAD_DIM, 1)
        dn = pltpu.roll(x, wk - HEAD_DIM, 1)
        for t in range(wk // LANES):
            sl = slice(t * LANES, (t + 1) * LANES)
            d_s[2 * t] = jnp.where(lowk, x[:, sl], up[:, sl]).astype(BF16)
            d_s[2 * t + 1] = jnp.where(lowk, dn[:, sl], x[:, sl]).astype(BF16)

    @pl.when(c == 0)
    def _():
        kdp_s[j] = jnp.zeros(kdp_s.shape[1:], BF16)
        vdp_s[j] = jnp.zeros(vdp_s.shape[1:], BF16)

    qi = lax.broadcasted_iota(jnp.int32, (BAND, 2 * BAND), 0)
    kj = lax.broadcasted_iota(jnp.int32, (BAND, 2 * BAND), 1)
    dist = qi - kj + BAND
    bias = jnp.where((dist >= 0) & (dist <= max_dist) & ((kj >= BAND) | (c > 0)), 0.0, NEG)
    col0 = kj == 0
    ones_blk = jnp.ones((2 * BAND, LANES), BF16)
    low = _low_half((BAND, LANES))
    lane = lax.broadcasted_iota(jnp.int32, (BAND, LANES), 1)
    step = BF16_ROWS * dil

    def residue(rho):
        starts = [pl.multiple_of(b * step + rho * BF16_ROWS, BF16_ROWS) for b in range(n_grp)]

        def gather(get):
            return jnp.concatenate([get(s) for s in starts], axis=0)

        tiles = []
        lse_acc = jnp.zeros((BAND, LANES), F32)
        for g in range(nkv):
            ca, cb = slice(2 * g * LANES, (2 * g + 1) * LANES), slice((2 * g + 1) * LANES, (2 * g + 2) * LANES)
            lhs = jnp.concatenate([gather(lambda s: qlo_s[pl.ds(s, BF16_ROWS), ca]),
                                   gather(lambda s: qhi_s[pl.ds(s, BF16_ROWS), ca]),
                                   gather(lambda s: qlo_s[pl.ds(s, BF16_ROWS), cb]),
                                   gather(lambda s: qhi_s[pl.ds(s, BF16_ROWS), cb])], axis=0)
            kcat = jnp.concatenate([gather(lambda s: kdp_s[j, g, pl.ds(s, BF16_ROWS), :]),
                                    gather(lambda s: kd_s[g, pl.ds(s, BF16_ROWS), :])], axis=0)
            vcat = jnp.concatenate([gather(lambda s: vdp_s[j, g, pl.ds(s, BF16_ROWS), :]),
                                    gather(lambda s: vd_s[g, pl.ds(s, BF16_ROWS), :])], axis=0)
            head0 = j * REP * nkv + g * REP
            s_full = lax.dot_general(lhs, kcat, _NT, preferred_element_type=F32)
            rsl = [slice(r * BAND, (r + 1) * BAND) for r in range(REP)]
            if use_sink:
                sc = jnp.concatenate([jnp.where(col0, sink_ref[head0 + r] * LOG2E, s_full[rsl[r]] + bias)
                                      for r in range(REP)], axis=0)
            else:
                sc = s_full + jnp.concatenate([bias] * REP, axis=0)
            m = jnp.max(sc, axis=-1, keepdims=True)
            e = jnp.exp2(sc - m)
            od = jnp.dot(e.astype(BF16), jnp.concatenate([vcat, ones_blk], axis=1), preferred_element_type=F32)
            den = od[:, LANES:]
            o = od[:, :LANES] * (1.0 / den)
            tiles.append(jnp.where(low, o[0:BAND], o[BAND:2 * BAND]))
            tiles.append(jnp.where(low, o[2 * BAND:3 * BAND], o[3 * BAND:4 * BAND]))
            if want_lse:
                lse = (m + jnp.log2(den)) * LN2
                for r in range(REP):
                    lse_acc = lse_acc + jnp.where(lane == head0 + r, lse[rsl[r]], 0.0)
        o_res = jnp.concatenate(tiles, axis=1)
        for b, s0 in enumerate(starts):
            op_s[pl.ds(s0, BF16_ROWS), :] = o_res[b * BF16_ROWS:(b + 1) * BF16_ROWS]
            if want_lse:
                lsep_s[pl.ds(s0, BF16_ROWS), :] = lse_acc[b * BF16_ROWS:(b + 1) * BF16_ROWS]

    if dil == 1:
        residue(0)
    else:
        def body(i, carry):
            for u in range(res_per_iter):
                residue(i * res_per_iter + u)
            return carry

        lax.fori_loop(0, dil // res_per_iter, body, 0)

    for k, blk in enumerate(unperm(op_s[...].astype(BF16))):
        if dil == 1:
            o_ref[...] = blk
        else:
            o_ref[k * PERM_CHUNK:(k + 1) * PERM_CHUNK, :] = blk.astype(o_ref.dtype)
    if want_lse:
        if dil == 1:
            lse_nat = lsep_s[...]
        else:
            parts = [unperm(x) for x in _split3(lsep_s[...])]
            lse_nat = jnp.concatenate([a + b + d for a, b, d in zip(*parts)], axis=0)
        if nj == 1:
            lse_ref[...] = lse_nat
        else:
            @pl.when(j == 0)
            def _():
                lse_s[...] = lse_nat

            @pl.when(j > 0)
            def _():
                lse_s[...] = lse_s[...] + lse_nat

            @pl.when(j == nj - 1)
            def _():
                lse_ref[...] = lse_s[...]
    kdp_s[j] = kd_s[...]
    if use_sink:
        keep = (lax.broadcasted_iota(jnp.int32, (rows, LANES), 0) > 0).astype(F32).astype(BF16)
        vdp_s[j] = vd_s[...] * keep
    else:
        vdp_s[j] = vd_s[...]


def _band_attention(p, seq, q_off, k_off, v_off, tables, sinks, *, dil, max_dist, nkv, want_lse, want_krot, name):
    rows = BAND * dil
    nj = N_KV // nkv
    qw, kw = nkv * REP * HEAD_DIM, nkv * HEAD_DIM
    assert seq % rows == 0 and kw % LANES == 0 and q_off % qw == 0 and k_off % kw == 0 and v_off % kw == 0
    assert dil == 1 or (rows % PERM_CHUNK == 0 and PERM_CHUNK % (BF16_ROWS * dil) == 0)
    use_sink = sinks is not None
    d_att = N_HEADS * HEAD_DIM
    res_per_iter = 1 if dil == 1 else max(1, UNITS_IN_FLIGHT // nkv)
    assert dil % res_per_iter == 0 and (not use_sink or (dil == 1 and max_dist < BAND))
    blk = lambda width, cb: pl.BlockSpec((rows, width), lambda c, j: (c, cb + j))
    t_spec = pl.BlockSpec((rows, LANES), lambda c, j: (c, 0))
    in_specs = [blk(qw, q_off // qw), blk(kw, k_off // kw), blk(kw, v_off // kw), t_spec, t_spec, t_spec]
    args = [p, p, p] + list(tables)
    if dil > 1:
        pm = _perm_matrix(dil)
        in_specs += [pl.BlockSpec((PERM_CHUNK, PERM_CHUNK), lambda c, j: (0, 0))] * 2
        args += [pm, pm.T]
    if use_sink:
        in_specs.append(pl.BlockSpec(memory_space=pltpu.SMEM))
        args.append(sinks)
    out_shape = [jax.ShapeDtypeStruct((seq, d_att), BF16)]
    out_specs = [pl.BlockSpec((rows, qw), lambda c, j: (c, j))]
    scratch = [pltpu.VMEM((rows, qw), BF16), pltpu.VMEM((rows, qw), BF16),
               pltpu.VMEM((nkv, rows, LANES), BF16), pltpu.VMEM((nkv, rows, LANES), BF16),
               pltpu.VMEM((nj, nkv, rows, LANES), BF16), pltpu.VMEM((nj, nkv, rows, LANES), BF16),
               pltpu.VMEM((rows, qw), F32)]
    if want_lse:
        out_shape.append(jax.ShapeDtypeStruct((seq, LANES), F32))
        out_specs.append(pl.BlockSpec((rows, LANES), lambda c, j: (c, 0)))
        scratch.append(pltpu.VMEM((rows, LANES), F32))
        if nj > 1:
            scratch.append(pltpu.VMEM((rows, LANES), F32))
    if want_krot:
        out_shape.append(jax.ShapeDtypeStruct((seq, N_KV * HEAD_DIM), F32))
        out_specs.append(pl.BlockSpec((rows, kw), lambda c, j: (c, j)))
    return pl.pallas_call(
        functools.partial(_band_kernel, dil=dil, max_dist=max_dist, use_sink=use_sink, want_lse=want_lse,
                          want_krot=want_krot, nkv=nkv, nj=nj, res_per_iter=res_per_iter),
        out_shape=tuple(out_shape), grid=(seq // rows, nj), in_specs=in_specs, out_specs=tuple(out_specs),
        scratch_shapes=scratch, compiler_params=_cparams(2), name=name)(*args)


def _merge_kernel(o1_ref, o2_ref, o3_ref, l1_ref, l2_ref, l3_ref, e_ref, out_ref):
    ls = [l1_ref[...], l2_ref[...], l3_ref[...]]
    mx = jnp.maximum(jnp.maximum(ls[0], ls[1]), ls[2])
    es = [jnp.exp(l - mx) for l in ls]
    inv = 1.0 / (es[0] + es[1] + es[2])
    e = e_ref[...]
    acc = None
    for en, o_ref in zip(es, (o1_ref, o2_ref, o3_ref)):
        w = en * inv
        hi = w.astype(BF16)
        lo = (w - hi.astype(F32)).astype(BF16)
        wf = jnp.dot(hi, e, preferred_element_type=F32) + jnp.dot(lo, e, preferred_element_type=F32)
        term = wf * o_ref[...].astype(F32)
        acc = term if acc is None else acc + term
    out_ref[...] = acc.astype(out_ref.dtype)


def _merge(os_, ls_, eexp, *, tm):
    seq, d = os_[0].shape
    o_spec = pl.BlockSpec((tm, d), lambda i: (i, 0))
    l_spec = pl.BlockSpec((tm, LANES), lambda i: (i, 0))
    return pl.pallas_call(
        _merge_kernel, out_shape=jax.ShapeDtypeStruct((seq, d), BF16), grid=(seq // tm,),
        in_specs=[o_spec] * 3 + [l_spec] * 3 + [pl.BlockSpec((LANES, d), lambda i: (0, 0))],
        out_specs=o_spec, compiler_params=_cparams(1), name="dil_merge")(*os_, *ls_, eexp)


def _expand_q(qb, t_len):
    gd = N_KV * HEAD_DIM
    sub = lax.broadcasted_iota(jnp.int32, (N_KV, gd), 0)
    lg = lax.broadcasted_iota(jnp.int32, (N_KV, gd), 1) >> _LOG2_HEAD_DIM
    diag = sub == lg
    tiles = []
    for r in range(REP):
        vr = jnp.concatenate(
            [qb[:, (g * REP + r) * HEAD_DIM:(g * REP + r + 1) * HEAD_DIM] for g in range(N_KV)], axis=1)
        for t in range(t_len):
            tiles.append(jnp.where(diag, jnp.broadcast_to(vr[t:t + 1], (N_KV, gd)), 0.0))
    return jnp.concatenate(tiles, axis=0).astype(BF16)


def _collapse_o(r_full, t_len):
    gd = N_KV * HEAD_DIM
    n = r_full.shape[0]
    sub = lax.broadcasted_iota(jnp.int32, (n, gd), 0) & (N_KV - 1)
    lg = lax.broadcasted_iota(jnp.int32, (n, gd), 1) >> _LOG2_HEAD_DIM
    masked = jnp.where(sub == lg, r_full, 0.0)
    red = jnp.sum(masked.reshape(n // N_KV, N_KV, gd), axis=1)
    pieces = []
    for g in range(N_KV):
        for r in range(REP):
            pieces.append(red[r * t_len:(r + 1) * t_len, g * HEAD_DIM:(g + 1) * HEAD_DIM])
    return jnp.concatenate(pieces, axis=1)


def _row_t(shape, t_len):
    r = lax.broadcasted_iota(jnp.int32, shape, 0)
    return (r >> 3) & (t_len - 1)


def _cached_attend(qx, kt, vt, mask_w, kn, vn, mask_n, sink):
    s_w = jnp.where(mask_w, jnp.dot(qx, kt, preferred_element_type=F32), NEG)
    s_n = jnp.where(mask_n, lax.dot_general(qx, kn, _NT, preferred_element_type=F32), NEG)
    m = jnp.maximum(jnp.max(s_w, axis=-1, keepdims=True), jnp.max(s_n, axis=-1, keepdims=True))
    if sink is not None:
        m = jnp.maximum(m, sink)
    e_w = jnp.exp(s_w - m)
    e_n = jnp.exp(s_n - m)
    den = jnp.sum(e_w, axis=-1, keepdims=True) + jnp.sum(e_n, axis=-1, keepdims=True)
    if sink is not None:
        den = den + jnp.exp(sink - m)
    r = (lax.dot_general(e_w.astype(BF16), vt, _NT, preferred_element_type=F32)
         + jnp.dot(e_n.astype(BF16), vn, preferred_element_type=F32))
    return r / den, m + jnp.log(den)


def _swa_cached_kernel(q_ref, kn_ref, vn_ref, cos_ref, sa_ref, sb_ref, ck_ref, cv_ref, sink_ref,
                       o_ref, krot_ref, *, nb, t_len):
    tabs = (cos_ref[...], sa_ref[...], sb_ref[...])
    q = _rope(q_ref[...], *tabs) * ATT_SCALE
    kn = _rope(kn_ref[...], *tabs)
    krot_ref[...] = kn
    kn16 = kn.astype(BF16)
    vn16 = vn_ref[...].astype(BF16)
    lb = ck_ref.shape[3]
    gd = N_KV * HEAD_DIM
    nrow = REP * t_len * N_KV
    tq = _row_t((nrow, lb), t_len)
    mask_w = lax.broadcasted_iota(jnp.int32, (nrow, lb), 1) >= tq + 1
    mask_n = lax.broadcasted_iota(jnp.int32, (nrow, BF16_ROWS), 1) <= _row_t((nrow, BF16_ROWS), t_len)
    sk = sink_ref[...]
    pad = jnp.zeros((BF16_ROWS - t_len, gd), BF16)
    outs = []
    for b in range(nb):
        rs = slice(b * t_len, (b + 1) * t_len)
        r_full, _ = _cached_attend(
            _expand_q(q[rs], t_len), ck_ref[b].reshape(gd, lb).astype(BF16), cv_ref[b].reshape(gd, lb).astype(BF16),
            mask_w, jnp.concatenate([kn16[rs], pad], axis=0), jnp.concatenate([vn16[rs], pad], axis=0), mask_n, sk)
        outs.append(_collapse_o(r_full, t_len))
    o_ref[...] = jnp.concatenate(outs, axis=0).astype(o_ref.dtype)


def _swa_cached(p0, off, seq, n_b, t_len, tables, cache_kt, cache_vt, sink_col, *, nb):
    rows = nb * t_len
    d_att = N_HEADS * HEAD_DIM
    d_kv = N_KV * HEAD_DIM
    lb = cache_kt.shape[3]
    assert lb == SWA_WINDOW and t_len <= BF16_ROWS
    r0 = seq // rows
    nrow = REP * t_len * N_KV
    c_spec = pl.BlockSpec((nb, N_KV, HEAD_DIM, lb), lambda i: (i, 0, 0, 0))
    t_spec = pl.BlockSpec((rows, LANES), lambda i: (r0 + i, 0))
    return pl.pallas_call(
        functools.partial(_swa_cached_kernel, nb=nb, t_len=t_len),
        out_shape=(jax.ShapeDtypeStruct((n_b * t_len, d_att), BF16),
                   jax.ShapeDtypeStruct((n_b * t_len, d_kv), F32)),
        grid=(n_b // nb,),
        in_specs=[pl.BlockSpec((rows, d_att), lambda i: (i, off["q"] // d_att)),
                  pl.BlockSpec((rows, d_kv), lambda i: (i, off["k"] // d_kv)),
                  pl.BlockSpec((rows, d_kv), lambda i: (i, off["v"] // d_kv)),
                  t_spec, t_spec, t_spec, c_spec, c_spec,
                  pl.BlockSpec((nrow, 1), lambda i: (0, 0))],
        out_specs=(pl.BlockSpec((rows, d_att), lambda i: (i, 0)),
                   pl.BlockSpec((rows, d_kv), lambda i: (i, 0))),
        compiler_params=_cparams(1), name="swa_cached")(p0, p0, p0, *tables, cache_kt, cache_vt, sink_col)


def _dil_cached_kernel(q1_ref, q2_ref, q3_ref, kn_ref, vn_ref, cos_ref, sa_ref, sb_ref, ck_ref, cv_ref,
                       o_ref, krot_ref, *, nb, t_len):
    tabs = (cos_ref[...], sa_ref[...], sb_ref[...])
    qs = [_rope(r[...], *tabs) * ATT_SCALE for r in (q1_ref, q2_ref, q3_ref)]
    kn = _rope(kn_ref[...], *tabs)
    krot_ref[...] = kn
    kn16 = kn.astype(BF16)
    vn16 = vn_ref[...].astype(BF16)
    lbuf = ck_ref.shape[3]
    gd = N_KV * HEAD_DIM
    nrow = REP * t_len * N_KV
    tqn = _row_t((nrow, BF16_ROWS), t_len)
    coln = lax.broadcasted_iota(jnp.int32, (nrow, BF16_ROWS), 1)
    masks = []
    for window, dil in DIL_PATTERNS:
        tq = _row_t((nrow, window), t_len)
        col = lax.broadcasted_iota(jnp.int32, (nrow, window), 1)
        if dil == 1:
            masks.append((col >= tq, coln <= tqn))
        else:
            masks.append(((col & (dil - 1)) == tq, coln == tqn))
    pad = jnp.zeros((BF16_ROWS - t_len, gd), BF16)
    outs = []
    for b in range(nb):
        rs = slice(b * t_len, (b + 1) * t_len)
        kt = ck_ref[b].reshape(gd, lbuf).astype(BF16)
        vt = cv_ref[b].reshape(gd, lbuf).astype(BF16)
        knp = jnp.concatenate([kn16[rs], pad], axis=0)
        vnp = jnp.concatenate([vn16[rs], pad], axis=0)
        res = []
        for gi, (window, dil) in enumerate(DIL_PATTERNS):
            res.append(_cached_attend(_expand_q(qs[gi][rs], t_len), kt[:, lbuf - window:], vt[:, lbuf - window:],
                                      masks[gi][0], knp, vnp, masks[gi][1], None))
        mx = jnp.maximum(jnp.maximum(res[0][1], res[1][1]), res[2][1])
        ws = [jnp.exp(l - mx) for _, l in res]
        inv = 1.0 / (ws[0] + ws[1] + ws[2])
        merged = (ws[0] * inv) * res[0][0] + (ws[1] * inv) * res[1][0] + (ws[2] * inv) * res[2][0]
        outs.append(_collapse_o(merged, t_len))
    o_ref[...] = jnp.concatenate(outs, axis=0).astype(o_ref.dtype)


def _dil_cached(p1, seq, n_b, t_len, tables, cache_kt, cache_vt, *, nb):
    rows = nb * t_len
    d_att = N_HEADS * HEAD_DIM
    d_kv = N_KV * HEAD_DIM
    lbuf = cache_kt.shape[3]
    assert rows % 8 == 0 and seq % rows == 0 and t_len <= BF16_ROWS
    for window, dil in DIL_PATTERNS:
        assert window <= lbuf and (PAST_LEN - lbuf) % dil == 0 and lbuf % dil == 0 and (dil == 1 or dil >= t_len)
    r0 = seq // rows
    c_spec = pl.BlockSpec((nb, N_KV, HEAD_DIM, lbuf), lambda i: (i, 0, 0, 0))
    t_spec = pl.BlockSpec((rows, LANES), lambda i: (r0 + i, 0))
    nq = d_att * len(DIL_PATTERNS)
    return pl.pallas_call(
        functools.partial(_dil_cached_kernel, nb=nb, t_len=t_len),
        out_shape=(jax.ShapeDtypeStruct((n_b * t_len, d_att), F32),
                   jax.ShapeDtypeStruct((n_b * t_len, d_kv), F32)),
        grid=(n_b // nb,),
        in_specs=[pl.BlockSpec((rows, d_att), lambda i: (i, 0)),
                  pl.BlockSpec((rows, d_att), lambda i: (i, 1)),
                  pl.BlockSpec((rows, d_att), lambda i: (i, 2)),
                  pl.BlockSpec((rows, d_kv), lambda i: (i, nq // d_kv)),
                  pl.BlockSpec((rows, d_kv), lambda i: (i, nq // d_kv + 1)),
                  t_spec, t_spec, t_spec, c_spec, c_spec],
        out_specs=(pl.BlockSpec((rows, d_att), lambda i: (i, 0)),
                   pl.BlockSpec((rows, d_kv), lambda i: (i, 0))),
        compiler_params=_cparams(1), name="dil_cached")(p1, p1, p1, p1, p1, *tables, cache_kt, cache_vt)


def _row_tile(m, pref):
    for t in (1024, 512, 256, 128, 64, 32, 16, 8):
        if t <= pref and m % t == 0:
            return t
    raise ValueError(m)


def kernel(x_prompt, x_sample, state_conv, state_ssm, cache_swa_k, cache_swa_v, cache_dil_k, cache_dil_v, norm_mix, norm_mlp, e_w_in, e_conv_w, e_conv_b, e_dt_bias, e_a_log, e_d_skip, e_gate_norm, e_sinks, e_w_out, o_w_in, o_w_out, mlp_w1, mlp_w2, norm_final):
    nbp, seq, d = x_prompt.shape
    n_b, t_len, _ = x_sample.shape
    assert nbp == 1 and d == N_HEADS * HEAD_DIM and norm_mix.shape[0] == 2
    ms = n_b * t_len
    m = seq + ms
    tms = (_row_tile(seq, 1024), _row_tile(ms, 512))
    d_kv = N_KV * HEAD_DIM
    d_bc = 2 * SSD_GROUPS * SSD_STATE
    conv_dim = d + d_bc
    both = lambda fn, *rows, **kw: tuple(
        fn(*(r[s] for r in rows), tm=tms[s], name=kw["name"] + ("_p", "_s")[s], **{k: v for k, v in kw.items() if k != "name"})
        for s in range(2))

    h = (x_prompt.reshape(seq, d), x_sample.reshape(ms, d))
    tables = _rope_tables(m, seq, t_len, _row_tile(math.gcd(seq, ms), 512))
    eexp = (lax.broadcasted_iota(jnp.int32, (LANES, d), 1) // HEAD_DIM
            == lax.broadcasted_iota(jnp.int32, (LANES, d), 0)).astype(BF16)
    keys_minor = lambda c: jnp.transpose(c[0], (0, 2, 3, 1))

    wi = e_w_in[0]
    c_z, c_xbc, c_dt, c_q = 0, d, d + conv_dim, d + conv_dim + N_HEADS
    c_k, c_v = c_q + d, c_q + d + d_kv
    tn0 = 1024
    wi16 = wi.astype(BF16)
    w0_q, w0_kv = wi16[:, c_q:c_q + d], wi16[:, c_k:c_v + d_kv]
    w0_dt = jnp.pad(wi16[:, c_dt:c_dt + N_HEADS], ((0, 0), (0, LANES - N_HEADS)))
    assert c_z == 0 and (c_xbc + d) % tn0 == 0 and d_bc % tn0 == 0 and d % tn0 == 0
    segs0 = [(wi16, 0, 2 * d // tn0), (w0_q, 0, d // tn0), (wi16, (c_xbc + d) // tn0, d_bc // tn0),
             (w0_kv, 0, 2 * d_kv // tn0)]
    off0 = {"z": 0, "xs": d, "q": 2 * d, "bc": 3 * d, "k": 3 * d + d_bc, "v": 3 * d + d_bc + d_kv}
    g0 = norm_mix[0]
    (p0_p, dt_p), (p0_s, dt_s) = both(
        lambda x, tm, name: _norm_matmul_multi(x, g0, segs0, w0_dt, tm=tm, tn=tn0, name=name), h, name="l0_in_proj")

    cw, cb = e_conv_w[0], e_conv_b[0]
    wx, wbc = cw[:, :d], cw[:, d:]
    bx, bbc = cb[:d].reshape(1, d), cb[d:].reshape(1, d_bc)
    pad_h = lambda v: jnp.pad(v.reshape(1, N_HEADS), ((0, 0), (0, LANES - N_HEADS)))
    dtb, alog = pad_h(e_dt_bias[0]), pad_h(e_a_log[0])
    dskip = jnp.repeat(e_d_skip[0], HEAD_DIM).reshape(1, d)
    gn = e_gate_norm[0].reshape(1, d)

    ssd_p, tail_x, tail_bc, p_ssm = _ssd_prompt(p0_p, dt_p, off0, seq, wx, wbc, bx, bbc, dtb, alog, dskip, gn, eexp)
    epad = jnp.pad(state_conv[0], ((0, 0), (0, 1), (0, 0))).reshape(ms, conv_dim)
    ssd_s, s_ssm = _ssd_sample(p0_s, dt_s, off0, seq, n_b, t_len, epad, state_ssm[0], wx, wbc, bx, bbc, dtb, alog,
                               dskip, gn, eexp, nb=4)

    att_p, krot0_p = _band_attention(p0_p, seq, off0["q"], off0["k"], off0["v"], tables, e_sinks[0], dil=1,
                                     max_dist=SWA_WINDOW - 1, nkv=N_KV, want_lse=False, want_krot=True,
                                     name="swa_prompt")
    sink_col = jnp.broadcast_to(e_sinks[0].reshape(N_KV, REP).T[:, None, :], (REP, t_len, N_KV)).reshape(-1, 1)
    att_s, krot0_s = _swa_cached(p0_s, off0, seq, n_b, t_len, tables, keys_minor(cache_swa_k),
                                 keys_minor(cache_swa_v), sink_col, nb=4)

    def mlp(hh, layer, g_out=None):
        g = norm_mlp[layer]
        o_s, w1_16, w2_16 = _mlp(hh[1], g, mlp_w1, mlp_w2, g_out, tm=ms, tk=512, name=f"l{layer}_mlp_s", layer=layer)
        o_p = _mlp(hh[0], g, w1_16, w2_16, g_out, tm=tms[0], tk=512, name=f"l{layer}_mlp_p")
        return o_p, o_s

    def out_proj(xs_p, xs_s, w, hh, name):
        h_s, *w16 = _matmul_resid(xs_s, w, hh[1], tm=ms, tn=512, name=name + "_s", emit=True)
        return _matmul_resid(xs_p, w16, hh[0], tm=tms[0], tn=1024, name=name + "_p"), h_s

    h = out_proj([ssd_p, att_p], [ssd_s, att_s], e_w_out[0], h, "l0_out_proj")
    h = mlp(h, 0)

    n_pat = len(DIL_PATTERNS)
    p1_s, w1i = _norm_matmul(h[1], norm_mix[1], o_w_in[0], tm=ms, tn=1024, name="l1_in_proj_s", emit=True)
    p1_p = _norm_matmul(h[0], norm_mix[1], w1i, tm=tms[0], tn=1024, name="l1_in_proj_p")
    k_off, v_off = n_pat * d, n_pat * d + d_kv
    os_, ls_ = [], []
    krot1_p = None
    for gi, (window, dil) in enumerate(DIL_PATTERNS):
        res = _band_attention(p1_p, seq, gi * d, k_off, v_off, tables, None, dil=dil, max_dist=window // dil,
                              nkv=(2 if BAND * dil * N_HEADS * HEAD_DIM * 4 > 8 * 2 ** 20 else N_KV),
                              want_lse=True, want_krot=(gi == 0), name=f"dil_prompt_{dil}")
        os_.append(res[0])
        ls_.append(res[1])
        if gi == 0:
            krot1_p = res[2]
    merged_p = _merge(os_, ls_, eexp, tm=_row_tile(seq, 256))
    dil_s, krot1_s = _dil_cached(p1_s, seq, n_b, t_len, tables, keys_minor(cache_dil_k), keys_minor(cache_dil_v),
                                 nb=2)
    h = out_proj([merged_p], [dil_s.astype(BF16)], o_w_out[0], h, "l1_out_proj")
    y_p, y_s = mlp(h, 1, norm_final)

    keep_swa = min(SWA_WINDOW, seq)
    keep_dil = min(max(w for w, _ in DIL_PATTERNS), seq)
    kv4 = lambda a: a.reshape(a.shape[0], N_KV, HEAD_DIM)
    y_prompt = y_p.reshape(1, seq, d)
    y_sample = y_s.reshape(n_b, t_len, d)
    p_conv = jnp.concatenate([tail_x[8 - (SSD_CONV - 1):], tail_bc[8 - (SSD_CONV - 1):]], axis=1)[None, None]
    p_swa_k = kv4(krot0_p[seq - keep_swa:])[None, None]
    p_swa_v = kv4(p0_p[seq - keep_swa:, off0["v"]:off0["v"] + d_kv])[None, None]
    p_dil_k = kv4(krot1_p[seq - keep_dil:])[None, None]
    p_dil_v = kv4(p1_p[seq - keep_dil:, v_off:v_off + d_kv])[None, None]
    xbc_s = jnp.concatenate([p0_s[:, off0["xs"]:off0["xs"] + d], p0_s[:, off0["bc"]:off0["bc"] + d_bc]], axis=1)
    s_conv = xbc_s.reshape(n_b, t_len, conv_dim)[:, t_len - (SSD_CONV - 1):][None]
    bt = lambda a: a.reshape(n_b, t_len, N_KV, HEAD_DIM)[None]
    s_swa_k = bt(krot0_s)
    s_swa_v = bt(p0_s[:, off0["v"]:off0["v"] + d_kv])
    s_dil_k = bt(krot1_s)
    s_dil_v = bt(p1_s[:, v_off:v_off + d_kv])
    return (y_prompt, y_sample, p_conv, p_ssm[None, None], p_swa_k, p_swa_v, p_dil_k, p_dil_v,
            s_conv, s_ssm[None], s_swa_k, s_swa_v, s_dil_k, s_dil_v)
```

```python
import functools
import math

import jax
import jax.numpy as jnp
from jax import lax
from jax.experimental import pallas as pl
from jax.experimental.pallas import tpu as pltpu

F32 = jnp.float32
BF16 = jnp.bfloat16

NORM_EPS = 1e-5
HEAD_DIM = 64
ROT_HALF = 8
ROPE_THETA = 500000.0
PAST_LEN = 8192
N_HEADS = 32
N_KV = 8
REP = N_HEADS // N_KV
SSD_GROUPS = 4
SSD_STATE = 128
SSD_CONV = 4
SSD_CHUNK = 128
SWA_WINDOW = 128
DIL_PATTERNS = ((128, 1), (512, 4), (2048, 16))
BAND = 128
ATT_SCALE = HEAD_DIM ** -0.5
LOG2E = math.log2(math.e)
LN2 = math.log(2.0)
NEG = -1e30
UNITS_IN_FLIGHT = 4
_LOG2_HEAD_DIM = 6

LANES = 128
BF16_ROWS = 16
PERM_CHUNK = 256
VMEM_LIMIT_BYTES = 56 * 1024 * 1024

_NT = (((1,), (1,)), ((), ()))
_TN = (((0,), (0,)), ((), ()))


def _cparams(n_axes):
    return pltpu.CompilerParams(dimension_semantics=("arbitrary",) * n_axes,
                                vmem_limit_bytes=VMEM_LIMIT_BYTES)


def _silu(x):
    return x * jax.nn.sigmoid(x)


def _tile_lanes(t, width):
    k = width // t.shape[1]
    return t if k == 1 else jnp.concatenate([t] * k, axis=1)


def _rope(x, cos_t, sa_t, sb_t):
    w = x.shape[1]
    return (x * _tile_lanes(cos_t, w)
            + pltpu.roll(x, w - ROT_HALF, 1) * _tile_lanes(sa_t, w)
            + pltpu.roll(x, ROT_HALF, 1) * _tile_lanes(sb_t, w))


def _low_half(shape):
    return (lax.broadcasted_iota(jnp.int32, shape, 1) & (LANES - 1)) < HEAD_DIM


def _rope_table_kernel(cos_ref, sa_ref, sb_ref, *, tm, seq, dec_seq):
    i = pl.program_id(0)
    row = i * tm + lax.broadcasted_iota(jnp.int32, (tm, LANES), 0)
    lane = lax.broadcasted_iota(jnp.int32, (tm, LANES), 1)
    pos = jnp.where(row < seq, row, PAST_LEN + ((row - seq) & (dec_seq - 1)))
    c = lane & (HEAD_DIM - 1)
    f = (c & (ROT_HALF - 1)).astype(F32)
    inv_freq = jnp.exp(f * (-math.log(ROPE_THETA) / ROT_HALF))
    ang = pos.astype(F32) * inv_freq
    cs = jnp.cos(ang)
    sn = jnp.sin(ang)
    cos_ref[...] = jnp.where(c < 2 * ROT_HALF, cs, 1.0)
    sa_ref[...] = jnp.where(c < ROT_HALF, -sn, 0.0)
    sb_ref[...] = jnp.where((c >= ROT_HALF) & (c < 2 * ROT_HALF), sn, 0.0)


def _rope_tables(m, seq, dec_seq, tm):
    assert dec_seq & (dec_seq - 1) == 0 and m % tm == 0
    shp = jax.ShapeDtypeStruct((m, LANES), F32)
    spec = pl.BlockSpec((tm, LANES), lambda i: (i, 0))
    return pl.pallas_call(
        functools.partial(_rope_table_kernel, tm=tm, seq=seq, dec_seq=dec_seq),
        out_shape=(shp, shp, shp), grid=(m // tm,), out_specs=(spec, spec, spec),
        compiler_params=_cparams(1), name="rope_tables")()


def _rmsnorm_bf16(x, g):
    ms = jnp.mean(x * x, axis=-1, keepdims=True)
    return (x * lax.rsqrt(ms + NORM_EPS) * g).astype(BF16)


def _norm_mm_kernel(x_ref, g_ref, w_ref, o_ref, *rest, emit):
    xn_ref = rest[-1]

    @pl.when(pl.program_id(1) == 0)
    def _():
        xn_ref[...] = _rmsnorm_bf16(x_ref[...], g_ref[...])

    w = w_ref[...]
    if emit:
        w = w.astype(BF16)
        rest[0][...] = w
    o_ref[...] = jnp.dot(xn_ref[...], w, preferred_element_type=F32)


def _norm_matmul(x, g, w, *, tm, tn, name, emit=False):
    m, k = x.shape
    n = w.shape[1]
    assert m % tm == 0 and n % tn == 0 and w.shape[0] == k and (not emit or m == tm)
    w_spec = pl.BlockSpec((k, tn), lambda i, j: (0, j))
    o_spec = pl.BlockSpec((tm, tn), lambda i, j: (i, j))
    o_shape = jax.ShapeDtypeStruct((m, n), F32)
    return pl.pallas_call(
        functools.partial(_norm_mm_kernel, emit=emit),
        out_shape=(o_shape, jax.ShapeDtypeStruct((k, n), BF16)) if emit else o_shape,
        grid=(m // tm, n // tn),
        in_specs=[pl.BlockSpec((tm, k), lambda i, j: (i, 0)),
                  pl.BlockSpec((1, k), lambda i, j: (0, 0)),
                  w_spec],
        out_specs=(o_spec, w_spec) if emit else o_spec,
        scratch_shapes=[pltpu.VMEM((tm, k), BF16)],
        compiler_params=_cparams(2), name=name)(x, g.reshape(1, k), w)


def _norm_mm_multi_kernel(*refs, owners):
    n_w = len(owners)
    x_ref, g_ref = refs[:2]
    w_refs = refs[2:2 + n_w]
    ws_ref, o_ref, os_ref, xn_ref = refs[2 + n_w:]
    j = pl.program_id(1)

    @pl.when(j == 0)
    def _():
        xn_ref[...] = _rmsnorm_bf16(x_ref[...], g_ref[...])
        os_ref[...] = jnp.dot(xn_ref[...], ws_ref[...], preferred_element_type=F32)

    for w_ref, tiles in zip(w_refs, owners):
        own = functools.reduce(jnp.logical_or, [(j >= lo) & (j < hi) for lo, hi in tiles])

        @pl.when(own)
        def _():
            o_ref[...] = jnp.dot(xn_ref[...], w_ref[...], preferred_element_type=F32)


def _norm_matmul_multi(x, g, segs, w_small, *, tm, tn, name):
    m, k = x.shape
    arrays, owners, maps = [], [], []
    start = 0
    for w, cb0, n in segs:
        if not any(w is a for a in arrays):
            arrays.append(w)
            owners.append([])
            maps.append([])
        idx = [i for i, a in enumerate(arrays) if a is w][0]
        owners[idx].append((start, start + n))
        maps[idx].append((start, n, cb0))
        start += n
    n_tiles = start

    def index_map(pieces):
        def col(i, j):
            c = pieces[-1][2] + pieces[-1][1] - 1
            for a, n, cb0 in reversed(pieces):
                c = jnp.where(j < a + n, cb0 + jnp.maximum(j - a, 0), c)
            return 0, c
        return col

    n_small = w_small.shape[1]
    return pl.pallas_call(
        functools.partial(_norm_mm_multi_kernel, owners=tuple(tuple(o) for o in owners)),
        out_shape=(jax.ShapeDtypeStruct((m, n_tiles * tn), F32), jax.ShapeDtypeStruct((m, n_small), F32)),
        grid=(m // tm, n_tiles),
        in_specs=[pl.BlockSpec((tm, k), lambda i, j: (i, 0), pipeline_mode=pl.Buffered(1)),
                  pl.BlockSpec((1, k), lambda i, j: (0, 0))]
        + [pl.BlockSpec((k, tn), index_map(p)) for p in maps]
        + [pl.BlockSpec((k, n_small), lambda i, j: (0, 0))],
        out_specs=(pl.BlockSpec((tm, tn), lambda i, j: (i, j)), pl.BlockSpec((tm, n_small), lambda i, j: (i, 0))),
        scratch_shapes=[pltpu.VMEM((tm, k), BF16)],
        compiler_params=_cparams(2), name=name)(x, g.reshape(1, k), *arrays, w_small)


def _mm_resid_kernel(*refs, n_in, emit):
    x_refs, w_refs = refs[:n_in], refs[n_in:2 * n_in]
    r_ref, o_ref = refs[2 * n_in], refs[2 * n_in + 1]
    wo_refs = refs[2 * n_in + 2:]
    acc = r_ref[...]
    for s, (x_ref, w_ref) in enumerate(zip(x_refs, w_refs)):
        w = w_ref[...]
        if emit:
            w = w.astype(BF16)
            wo_refs[s][...] = w
        acc = acc + jnp.dot(x_ref[...], w, preferred_element_type=F32)
    o_ref[...] = acc


def _matmul_resid(xs, w, resid, *, tm, tn, name, emit=False):
    m, n = resid.shape
    k = xs[0].shape[1]
    assert m % tm == 0 and n % tn == 0 and (not emit or (m == tm and w.shape == (k * len(xs), n)))
    w_args = [w] * len(xs) if emit else list(w)
    in_specs = [pl.BlockSpec((tm, k), lambda i, j: (i, 0)) for _ in xs]
    in_specs += [pl.BlockSpec((k, tn), functools.partial(lambda i, j, s: (s, j), s=s if emit else 0))
                 for s in range(len(xs))]
    in_specs.append(pl.BlockSpec((tm, tn), lambda i, j: (i, j)))
    o_spec = pl.BlockSpec((tm, tn), lambda i, j: (i, j))
    o_shape = jax.ShapeDtypeStruct((m, n), F32)
    if emit:
        out_shape = (o_shape,) + (jax.ShapeDtypeStruct((k, n), BF16),) * len(xs)
        out_specs = (o_spec,) + (pl.BlockSpec((k, tn), lambda i, j: (0, j)),) * len(xs)
    else:
        out_shape, out_specs = o_shape, o_spec
    return pl.pallas_call(
        functools.partial(_mm_resid_kernel, n_in=len(xs), emit=emit),
        out_shape=out_shape,
        grid=(m // tm, n // tn),
        in_specs=in_specs,
        out_specs=out_specs,
        compiler_params=_cparams(2), name=name)(*xs, *w_args, resid)


def _mlp_kernel(x_ref, g_ref, w1_ref, w2_ref, gout_ref, o_ref, *rest, norm_out, emit):
    xn_ref = rest[-1]
    k = pl.program_id(1)

    @pl.when(k == 0)
    def _():
        x = x_ref[...]
        xn_ref[...] = _rmsnorm_bf16(x, g_ref[...])
        o_ref[...] = x

    w1, w2 = w1_ref[...], w2_ref[...]
    if emit:
        w1, w2 = w1.astype(BF16), w2.astype(BF16)
        rest[0][...] = w1
        rest[1][...] = w2
    u = jnp.dot(xn_ref[...], w1, preferred_element_type=F32)
    u = jnp.square(jnp.maximum(u, 0.0)).astype(BF16)
    o_ref[...] += jnp.dot(u, w2, preferred_element_type=F32)

    if norm_out:
        @pl.when(k == pl.num_programs(1) - 1)
        def _():
            o = o_ref[...]
            ms = jnp.mean(o * o, axis=-1, keepdims=True)
            o_ref[...] = o * lax.rsqrt(ms + NORM_EPS) * gout_ref[...]


def _mlp(x, g, w1, w2, g_out, *, tm, tk, name, layer=None):
    m, d = x.shape
    emit = layer is not None
    hid = w1.shape[-1]
    assert m % tm == 0 and hid % tk == 0 and (not emit or m == tm)
    w1o_spec = pl.BlockSpec((d, tk), lambda i, k: (0, k))
    w2o_spec = pl.BlockSpec((tk, d), lambda i, k: (k, 0))
    if emit:
        w_specs = [pl.BlockSpec((None, d, tk), lambda i, k: (layer, 0, k)),
                   pl.BlockSpec((None, tk, d), lambda i, k: (layer, k, 0))]
        out_shape = (jax.ShapeDtypeStruct((m, d), F32), jax.ShapeDtypeStruct((d, hid), BF16),
                     jax.ShapeDtypeStruct((hid, d), BF16))
        out_specs = (pl.BlockSpec((tm, d), lambda i, k: (i, 0)), w1o_spec, w2o_spec)
    else:
        w_specs = [w1o_spec, w2o_spec]
        out_shape = jax.ShapeDtypeStruct((m, d), F32)
        out_specs = pl.BlockSpec((tm, d), lambda i, k: (i, 0))
    return pl.pallas_call(
        functools.partial(_mlp_kernel, norm_out=g_out is not None, emit=emit),
        out_shape=out_shape,
        grid=(m // tm, hid // tk),
        in_specs=[pl.BlockSpec((tm, d), lambda i, k: (i, 0)),
                  pl.BlockSpec((1, d), lambda i, k: (0, 0))] + w_specs + [pl.BlockSpec((1, d), lambda i, k: (0, 0))],
        out_specs=out_specs,
        scratch_shapes=[pltpu.VMEM((tm, d), BF16)],
        compiler_params=_cparams(2), name=name)(x, g.reshape(1, d), w1, w2,
                                                (g if g_out is None else g_out).reshape(1, d))


def _causal_conv(raw, tail, w_ref, b_ref):
    n = raw.shape[0]
    full = jnp.concatenate([tail, raw], axis=0)
    acc = b_ref[...]
    for i in range(SSD_CONV):
        s = SSD_CONV - 1 - i
        acc = acc + full[8 - s:8 - s + n] * w_ref[i:i + 1, :]
    return _silu(acc)


def _gate_groupnorm(y, z, gn):
    y = y * _silu(z)
    gw = y.shape[1] // SSD_GROUPS
    outs = []
    for g in range(SSD_GROUPS):
        yg = y[:, g * gw:(g + 1) * gw]
        ms = jnp.mean(yg * yg, axis=-1, keepdims=True)
        outs.append(yg * lax.rsqrt(ms + NORM_EPS))
    return jnp.concatenate(outs, axis=1) * gn


def _ssd_prompt_kernel(z_ref, xs_ref, bc_ref, dt_ref, wx_ref, wbc_ref, bx_ref, bbc_ref, dtb_ref, alog_ref,
                       dskip_ref, gn_ref, eexp_ref, y_ref, tailx_ref, tailbc_ref, h_ref, y_s):
    c = pl.program_id(0)
    q = SSD_CHUNK
    gs = SSD_STATE
    hp = HEAD_DIM

    @pl.when(c == 0)
    def _():
        tailx_ref[...] = jnp.zeros_like(tailx_ref)
        tailbc_ref[...] = jnp.zeros_like(tailbc_ref)
        h_ref[...] = jnp.zeros_like(h_ref)

    xs_raw = xs_ref[...]
    bc_raw = bc_ref[...]
    xs_c = _causal_conv(xs_raw, tailx_ref[...], wx_ref, bx_ref)
    bc_c = _causal_conv(bc_raw, tailbc_ref[...], wbc_ref, bbc_ref)
    tailx_ref[...] = xs_raw[q - 8:q]
    tailbc_ref[...] = bc_raw[q - 8:q]

    lane = lax.broadcasted_iota(jnp.int32, (q, LANES), 1)
    row = lax.broadcasted_iota(jnp.int32, (q, LANES), 0)
    dt = jax.nn.softplus(dt_ref[...] + dtb_ref[...])
    a = -jnp.exp(alog_ref[...])
    la = jnp.where(lane < N_HEADS, dt * a, 0.0)
    acs = la
    s = 1
    while s < q:
        acs = acs + jnp.where(row >= s, pltpu.roll(acs, s, 0), 0.0)
        s *= 2
    acs_t = acs.T
    acs_last = acs[q - 1:q, :]
    to_end = jnp.exp(acs_last - acs)
    eacs = jnp.exp(acs)
    cdec = jnp.exp(acs_last)
    causal = (lax.broadcasted_iota(jnp.int32, (q, q), 0) >= lax.broadcasted_iota(jnp.int32, (q, q), 1))

    stacked = jnp.concatenate([dt, eacs, to_end], axis=0)
    hi = stacked.astype(BF16)
    lo = (stacked - hi.astype(F32)).astype(BF16)
    e = eexp_ref[...]
    full = jnp.dot(hi, e, preferred_element_type=F32) + jnp.dot(lo, e, preferred_element_type=F32)
    xdt = xs_c * full[0:q]
    eacs_f = full[q:2 * q]
    w_f = xdt * full[2 * q:3 * q]
    low = _low_half((q, LANES))
    top = lax.broadcasted_iota(jnp.int32, (2 * hp, gs), 0) < hp

    rep = N_HEADS // SSD_GROUPS
    for g in range(SSD_GROUPS):
        bg = bc_c[:, g * gs:(g + 1) * gs].astype(BF16)
        cg = bc_c[:, SSD_GROUPS * gs + g * gs:SSD_GROUPS * gs + (g + 1) * gs].astype(BF16)
        cb = lax.dot_general(cg, bg, _NT, preferred_element_type=F32)
        for rp in range(rep // 2):
            h0 = g * rep + 2 * rp
            sl = slice(h0 * hp, (h0 + 2) * hp)
            ms = []
            for h in (h0, h0 + 1):
                lmat = jnp.exp(jnp.where(causal, acs[:, h:h + 1] - acs_t[h:h + 1, :], -jnp.inf))
                ms.append((cb * lmat).astype(BF16))
            xp = xdt[:, sl]
            rhs = jnp.concatenate([jnp.where(low, xp, 0.0), jnp.where(low, 0.0, xp)], axis=0).astype(BF16)
            y = jnp.dot(jnp.concatenate(ms, axis=1), rhs, preferred_element_type=F32)
            hs = h_ref[h0:h0 + 2].reshape(2 * hp, gs)
            y = y + eacs_f[:, sl] * lax.dot_general(cg, hs.astype(BF16), _NT, preferred_element_type=F32)
            st = lax.dot_general(w_f[:, sl].astype(BF16), bg, _TN, preferred_element_type=F32)
            cd = jnp.where(top, cdec[:, h0:h0 + 1], cdec[:, h0 + 1:h0 + 2])
            h_ref[h0:h0 + 2] = (hs * cd + st).reshape(2, hp, gs)
            y_s[:, sl] = y + dskip_ref[:, sl] * xs_c[:, sl]

    y_ref[...] = _gate_groupnorm(y_s[...], z_ref[...], gn_ref[...]).astype(y_ref.dtype)


def _ssd_prompt(p0, dtr, off, seq, wx, wbc, bx, bbc, dtb, alog, dskip, gn, eexp):
    q = SSD_CHUNK
    d_in = wx.shape[1]
    d_bc = wbc.shape[1]
    assert seq % q == 0
    const = lambda shape: pl.BlockSpec(shape, lambda c: (0,) * len(shape))
    return pl.pallas_call(
        _ssd_prompt_kernel,
        out_shape=(jax.ShapeDtypeStruct((seq, d_in), BF16),
                   jax.ShapeDtypeStruct((8, d_in), F32),
                   jax.ShapeDtypeStruct((8, d_bc), F32),
                   jax.ShapeDtypeStruct((N_HEADS, HEAD_DIM, SSD_STATE), F32)),
        grid=(seq // q,),
        in_specs=[pl.BlockSpec((q, d_in), lambda c: (c, off["z"] // d_in)),
                  pl.BlockSpec((q, d_in), lambda c: (c, off["xs"] // d_in)),
                  pl.BlockSpec((q, d_bc), lambda c: (c, off["bc"] // d_bc)),
                  pl.BlockSpec((q, LANES), lambda c: (c, 0)),
                  const((SSD_CONV, d_in)), const((SSD_CONV, d_bc)), const((1, d_in)), const((1, d_bc)),
                  const((1, LANES)), const((1, LANES)), const((1, d_in)), const((1, d_in)), const((LANES, d_in))],
        out_specs=(pl.BlockSpec((q, d_in), lambda c: (c, 0)),
                   const((8, d_in)), const((8, d_bc)), const((N_HEADS, HEAD_DIM, SSD_STATE))),
        scratch_shapes=[pltpu.VMEM((q, d_in), F32)],
        compiler_params=_cparams(1), name="ssd_prompt")(p0, p0, p0, dtr, wx, wbc, bx, bbc, dtb, alog, dskip, gn, eexp)


def _split3(x):
    hi = x.astype(BF16)
    r1 = x - hi.astype(F32)
    mid = r1.astype(BF16)
    lo = (r1 - mid.astype(F32)).astype(BF16)
    return hi, mid, lo


def _ssd_sample_kernel(z_ref, xs_ref, bc_ref, dt_ref, ex_ref, ebc_ref, st_ref, wx_ref, wbc_ref, bx_ref, bbc_ref,
                       dtb_ref, alog_ref, dskip_ref, gn_ref, eexp_ref, y_ref, sto_ref, *, nb, t_len):
    rows = nb * t_len
    gs = SSD_STATE
    gw = xs_ref.shape[1] // SSD_GROUPS
    rep = N_HEADS // SSD_GROUPS

    def tcol(width):
        r = lax.broadcasted_iota(jnp.int32, (rows, width), 0)
        return r & (t_len - 1), r >> int(math.log2(t_len))

    def conv(raw, est, w_ref, b_ref):
        t, _ = tcol(raw.shape[1])
        acc = b_ref[...]
        for i in range(SSD_CONV):
            s = SSD_CONV - 1 - i
            if s == 0:
                sh = raw
            else:
                k = (rows - (SSD_CONV - 1) + s) % rows
                sh = jnp.where(t >= s, pltpu.roll(raw, s, 0), est if k == 0 else pltpu.roll(est, k, 0))
            acc = acc + sh * w_ref[i:i + 1, :]
        return _silu(acc)

    xs_c = conv(xs_ref[...], ex_ref[...], wx_ref, bx_ref)
    bc_c = conv(bc_ref[...], ebc_ref[...], wbc_ref, bbc_ref)
    bm = bc_c[:, :SSD_GROUPS * gs]
    cm = bc_c[:, SSD_GROUPS * gs:]

    t1, _ = tcol(LANES)
    lane = lax.broadcasted_iota(jnp.int32, (rows, LANES), 1)
    dt = jax.nn.softplus(dt_ref[...] + dtb_ref[...])
    a = -jnp.exp(alog_ref[...])
    la = jnp.where(lane < N_HEADS, dt * a, 0.0)
    acs = la
    for s in range(1, t_len):
        acs = acs + jnp.where(t1 >= s, pltpu.roll(la, s, 0), 0.0)
    alast = jnp.where(t1 == t_len - 1, acs, 0.0)
    for u in range(1, t_len):
        alast = alast + jnp.where(t1 == t_len - 1 - u, pltpu.roll(acs, rows - u, 0), 0.0)
    parts = [dt, jnp.exp(acs), jnp.exp(alast - acs), jnp.exp(alast)]
    for k in range(1, t_len):
        parts.append(jnp.exp(acs - pltpu.roll(acs, k, 0)))
    stacked = jnp.concatenate(parts, axis=0)
    hi, mid, lo = _split3(stacked)
    e = eexp_ref[...]
    full = (jnp.dot(hi, e, preferred_element_type=F32) + jnp.dot(mid, e, preferred_element_type=F32)
            + jnp.dot(lo, e, preferred_element_type=F32))
    dt_f, eacs_f, toend_f, cdec_f = (full[i * rows:(i + 1) * rows] for i in range(4))
    dec_f = [None] + [full[(3 + k) * rows:(4 + k) * rows] for k in range(1, t_len)]

    tw, _ = tcol(xs_c.shape[1])
    xdt = xs_c * dt_f
    y = jnp.zeros_like(xs_c)
    for k in range(t_len):
        bmk = bm if k == 0 else pltpu.roll(bm, k, 0)
        prod = cm * bmk
        cbs = []
        for g in range(SSD_GROUPS):
            sg = jnp.sum(prod[:, g * gs:(g + 1) * gs], axis=-1, keepdims=True)
            cbs.append(jnp.broadcast_to(sg, (rows, gw)))
        cb_f = jnp.concatenate(cbs, axis=1)
        if k == 0:
            y = y + cb_f * xdt
        else:
            y = y + jnp.where(tw >= k, cb_f * dec_f[k] * pltpu.roll(xdt, k, 0), 0.0)

    cm16 = cm.astype(BF16)
    bm16 = bm.astype(BF16)
    w_f = (toend_f * xdt)
    _, bg_ = tcol(gw)
    ones_blk = jnp.ones((3, gs), BF16)
    yoffs = []
    for g in range(SSD_GROUPS):
        cg = cm16[:, g * gs:(g + 1) * gs]
        rhs_top = jnp.concatenate([bm16[:, g * gs:(g + 1) * gs], jnp.zeros((rows, gs), BF16)], axis=1)
        rhs_mid = jnp.concatenate([jnp.zeros((3, gs), BF16), ones_blk], axis=1)
        rhs = jnp.concatenate([rhs_top, rhs_mid, jnp.zeros((13, 2 * gs), BF16)], axis=0)
        wg = w_f[:, g * gw:(g + 1) * gw]
        cdg = cdec_f[:, g * gw:(g + 1) * gw]
        yg = jnp.zeros((rows, gw), F32)
        for b in range(nb):
            hb = st_ref[b, g * rep:(g + 1) * rep].reshape(gw, gs)
            yb = lax.dot_general(cg, hb.astype(BF16), _NT, preferred_element_type=F32)
            yg = jnp.where(bg_ == b, yb, yg)
            wb = jnp.where(bg_ == b, wg, 0.0).astype(BF16)
            d_hi, d_mid, d_lo = _split3(cdg[b * t_len:b * t_len + 1])
            lhs = jnp.concatenate([wb, d_hi, d_mid, d_lo, jnp.zeros((13, gw), BF16)], axis=0)
            sd = lax.dot_general(lhs, rhs, _TN, preferred_element_type=F32)
            sto_ref[b, g * rep:(g + 1) * rep] = (hb * sd[:, gs:] + sd[:, :gs]).reshape(rep, HEAD_DIM, gs)
        yoffs.append(yg)
    y = y + eacs_f * jnp.concatenate(yoffs, axis=1) + dskip_ref[...] * xs_c
    y_ref[...] = _gate_groupnorm(y, z_ref[...], gn_ref[...]).astype(y_ref.dtype)


def _ssd_sample(p0, dtr, off, seq, n_b, t_len, epad, state, wx, wbc, bx, bbc, dtb, alog, dskip, gn, eexp, *, nb):
    rows = nb * t_len
    d_in = wx.shape[1]
    d_bc = wbc.shape[1]
    assert n_b % nb == 0 and t_len & (t_len - 1) == 0
    const = lambda shape: pl.BlockSpec(shape, lambda i: (0,) * len(shape))
    st_spec = pl.BlockSpec((nb, N_HEADS, HEAD_DIM, SSD_STATE), lambda i: (i, 0, 0, 0))
    return pl.pallas_call(
        functools.partial(_ssd_sample_kernel, nb=nb, t_len=t_len),
        out_shape=(jax.ShapeDtypeStruct((n_b * t_len, d_in), BF16),
                   jax.ShapeDtypeStruct(state.shape, F32)),
        grid=(n_b // nb,),
        in_specs=[pl.BlockSpec((rows, d_in), lambda i: (i, off["z"] // d_in)),
                  pl.BlockSpec((rows, d_in), lambda i: (i, off["xs"] // d_in)),
                  pl.BlockSpec((rows, d_bc), lambda i: (i, off["bc"] // d_bc)),
                  pl.BlockSpec((rows, LANES), lambda i: (i, 0)),
                  pl.BlockSpec((rows, d_in), lambda i: (i, 0)),
                  pl.BlockSpec((rows, d_bc), lambda i: (i, d_in // d_bc)),
                  st_spec,
                  const((SSD_CONV, d_in)), const((SSD_CONV, d_bc)), const((1, d_in)), const((1, d_bc)),
                  const((1, LANES)), const((1, LANES)), const((1, d_in)), const((1, d_in)),
                  const((LANES, d_in))],
        out_specs=(pl.BlockSpec((rows, d_in), lambda i: (i, 0)), st_spec),
        compiler_params=_cparams(1), name="ssd_sample")(
            p0, p0, p0, dtr, epad, epad, state, wx, wbc, bx, bbc, dtb, alog, dskip, gn, eexp)


def _perm_matrix(dil):
    i = jnp.arange(PERM_CHUNK)
    blk = BF16_ROWS * dil
    src = (i // blk) * blk + (i % BF16_ROWS) * dil + (i % blk) // BF16_ROWS
    return (src[:, None] == jnp.arange(PERM_CHUNK)[None, :]).astype(BF16)


def _band_kernel(*refs, dil, max_dist, use_sink, want_lse, want_krot, nkv, nj, res_per_iter):
    it = iter(refs)
    q_ref, k_ref, v_ref, cos_ref, sa_ref, sb_ref = (next(it) for _ in range(6))
    p_ref, pt_ref = (next(it), next(it)) if dil > 1 else (None, None)
    sink_ref = next(it) if use_sink else None
    o_ref = next(it)
    lse_ref = next(it) if want_lse else None
    krot_ref = next(it) if want_krot else None
    qlo_s, qhi_s, kd_s, vd_s, kdp_s, vdp_s, op_s = (next(it) for _ in range(7))
    lsep_s = next(it) if want_lse else None
    lse_s = next(it) if (want_lse and nj > 1) else None

    c = pl.program_id(0)
    j = pl.program_id(1)
    rows, wq = q_ref.shape
    wk = k_ref.shape[1]
    n_chunk = rows // PERM_CHUNK if dil > 1 else 0
    n_grp = BAND // BF16_ROWS

    def perm(x):
        if dil == 1:
            return x
        x16 = x.astype(BF16)
        return jnp.concatenate(
            [jnp.dot(p_ref[...], x16[k * PERM_CHUNK:(k + 1) * PERM_CHUNK], preferred_element_type=F32)
             for k in range(n_chunk)], axis=0)

    def unperm(xp16):
        if dil == 1:
            return [xp16]
        return [jnp.dot(pt_ref[...], xp16[k * PERM_CHUNK:(k + 1) * PERM_CHUNK], preferred_element_type=F32)
                for k in range(n_chunk)]

    tabs = (cos_ref[...], sa_ref[...], sb_ref[...])
    q = perm(_rope(q_ref[...], *tabs) * (ATT_SCALE * LOG2E))
    lowq = _low_half((rows, wq))
    qlo_s[...] = jnp.where(lowq, q, 0.0).astype(BF16)
    qhi_s[...] = jnp.where(lowq, 0.0, q).astype(BF16)
    kr = _rope(k_ref[...], *tabs)
    if want_krot:
        krot_ref[...] = kr
    lowk = _low_half((rows, LANES))
    for x, d_s in ((perm(kr), kd_s), (perm(v_ref[...]), vd_s)):
        up = pltpu.roll(x, HEAD_DIM, 1)
        dn = pltpu.roll(x, wk - HEAD_DIM, 1)
        for t in range(wk // LANES):
            sl = slice(t * LANES, (t + 1) * LANES)
            d_s[2 * t] = jnp.where(lowk, x[:, sl], up[:, sl]).astype(BF16)
            d_s[2 * t + 1] = jnp.where(lowk, dn[:, sl], x[:, sl]).astype(BF16)

    @pl.when(c == 0)
    def _():
        kdp_s[j] = jnp.zeros(kdp_s.shape[1:], BF16)
        vdp_s[j] = jnp.zeros(vdp_s.shape[1:], BF16)

    qi = lax.broadcasted_iota(jnp.int32, (BAND, 2 * BAND), 0)
    kj = lax.broadcasted_iota(jnp.int32, (BAND, 2 * BAND), 1)
    dist = qi - kj + BAND
    bias = jnp.where((dist >= 0) & (dist <= max_dist) & ((kj >= BAND) | (c > 0)), 0.0, NEG)
    col0 = kj == 0
    ones_blk = jnp.ones((2 * BAND, LANES), BF16)
    low = _low_half((BAND, LANES))
    lane = lax.broadcasted_iota(jnp.int32, (BAND, LANES), 1)
    step = BF16_ROWS * dil

    def residue(rho):
        starts = [pl.multiple_of(b * step + rho * BF16_ROWS, BF16_ROWS) for b in range(n_grp)]

        def gather(get):
            return jnp.concatenate([get(s) for s in starts], axis=0)

        tiles = []
        lse_acc = jnp.zeros((BAND, LANES), F32)
        for g in range(nkv):
            ca, cb = slice(2 * g * LANES, (2 * g + 1) * LANES), slice((2 * g + 1) * LANES, (2 * g + 2) * LANES)
            lhs = jnp.concatenate([gather(lambda s: qlo_s[pl.ds(s, BF16_ROWS), ca]),
                                   gather(lambda s: qhi_s[pl.ds(s, BF16_ROWS), ca]),
                                   gather(lambda s: qlo_s[pl.ds(s, BF16_ROWS), cb]),
                                   gather(lambda s: qhi_s[pl.ds(s, BF16_ROWS), cb])], axis=0)
            kcat = jnp.concatenate([gather(lambda s: kdp_s[j, g, pl.ds(s, BF16_ROWS), :]),
                                    gather(lambda s: kd_s[g, pl.ds(s, BF16_ROWS), :])], axis=0)
            vcat = jnp.concatenate([gather(lambda s: vdp_s[j, g, pl.ds(s, BF16_ROWS), :]),
                                    gather(lambda s: vd_s[g, pl.ds(s, BF16_ROWS), :])], axis=0)
            head0 = j * REP * nkv + g * REP
            s_full = lax.dot_general(lhs, kcat, _NT, preferred_element_type=F32)
            rsl = [slice(r * BAND, (r + 1) * BAND) for r in range(REP)]
            if use_sink:
                sc = jnp.concatenate([jnp.where(col0, sink_ref[head0 + r] * LOG2E, s_full[rsl[r]] + bias)
                                      for r in range(REP)], axis=0)
            else:
                sc = s_full + jnp.concatenate([bias] * REP, axis=0)
            m = jnp.max(sc, axis=-1, keepdims=True)
            e = jnp.exp2(sc - m)
            od = jnp.dot(e.astype(BF16), jnp.concatenate([vcat, ones_blk], axis=1), preferred_element_type=F32)
            den = od[:, LANES:]
            o = od[:, :LANES] * (1.0 / den)
            tiles.append(jnp.where(low, o[0:BAND], o[BAND:2 * BAND]))
            tiles.append(jnp.where(low, o[2 * BAND:3 * BAND], o[3 * BAND:4 * BAND]))
            if want_lse:
                lse = (m + jnp.log2(den)) * LN2
                for r in range(REP):
                    lse_acc = lse_acc + jnp.where(lane == head0 + r, lse[rsl[r]], 0.0)
        o_res = jnp.concatenate(tiles, axis=1)
        for b, s0 in enumerate(starts):
            op_s[pl.ds(s0, BF16_ROWS), :] = o_res[b * BF16_ROWS:(b + 1) * BF16_ROWS]
            if want_lse:
                lsep_s[pl.ds(s0, BF16_ROWS), :] = lse_acc[b * BF16_ROWS:(b + 1) * BF16_ROWS]

    if dil == 1:
        residue(0)
    else:
        def body(i, carry):
            for u in range(res_per_iter):
                residue(i * res_per_iter + u)
            return carry

        lax.fori_loop(0, dil // res_per_iter, body, 0)

    for k, blk in enumerate(unperm(op_s[...].astype(BF16))):
        if dil == 1:
            o_ref[...] = blk
        else:
            o_ref[k * PERM_CHUNK:(k + 1) * PERM_CHUNK, :] = blk.astype(o_ref.dtype)
    if want_lse:
        if dil == 1:
            lse_nat = lsep_s[...]
        else:
            parts = [unperm(x) for x in _split3(lsep_s[...])]
            lse_nat = jnp.concatenate([a + b + d for a, b, d in zip(*parts)], axis=0)
        if nj == 1:
            lse_ref[...] = lse_nat
        else:
            @pl.when(j == 0)
            def _():
                lse_s[...] = lse_nat

            @pl.when(j > 0)
            def _():
                lse_s[...] = lse_s[...] + lse_nat

            @pl.when(j == nj - 1)
            def _():
                lse_ref[...] = lse_s[...]
    kdp_s[j] = kd_s[...]
    if use_sink:
        keep = (lax.broadcasted_iota(jnp.int32, (rows, LANES), 0) > 0).astype(F32).astype(BF16)
        vdp_s[j] = vd_s[...] * keep
    else:
        vdp_s[j] = vd_s[...]


def _band_attention(p, seq, q_off, k_off, v_off, tables, sinks, *, dil, max_dist, nkv, want_lse, want_krot, name):
    rows = BAND * dil
    nj = N_KV // nkv
    qw, kw = nkv * REP * HEAD_DIM, nkv * HEAD_DIM
    assert seq % rows == 0 and kw % LANES == 0 and q_off % qw == 0 and k_off % kw == 0 and v_off % kw == 0
    assert dil == 1 or (rows % PERM_CHUNK == 0 and PERM_CHUNK % (BF16_ROWS * dil) == 0)
    use_sink = sinks is not None
    d_att = N_HEADS * HEAD_DIM
    res_per_iter = 1 if dil == 1 else max(1, UNITS_IN_FLIGHT // nkv)
    assert dil % res_per_iter == 0 and (not use_sink or (dil == 1 and max_dist < BAND))
    blk = lambda width, cb: pl.BlockSpec((rows, width), lambda c, j: (c, cb + j))
    t_spec = pl.BlockSpec((rows, LANES), lambda c, j: (c, 0))
    in_specs = [blk(qw, q_off // qw), blk(kw, k_off // kw), blk(kw, v_off // kw), t_spec, t_spec, t_spec]
    args = [p, p, p] + list(tables)
    if dil > 1:
        pm = _perm_matrix(dil)
        in_specs += [pl.BlockSpec((PERM_CHUNK, PERM_CHUNK), lambda c, j: (0, 0))] * 2
        args += [pm, pm.T]
    if use_sink:
        in_specs.append(pl.BlockSpec(memory_space=pltpu.SMEM))
        args.append(sinks)
    out_shape = [jax.ShapeDtypeStruct((seq, d_att), BF16)]
    out_specs = [pl.BlockSpec((rows, qw), lambda c, j: (c, j))]
    scratch = [pltpu.VMEM((rows, qw), BF16), pltpu.VMEM((rows, qw), BF16),
               pltpu.VMEM((nkv, rows, LANES), BF16), pltpu.VMEM((nkv, rows, LANES), BF16),
               pltpu.VMEM((nj, nkv, rows, LANES), BF16), pltpu.VMEM((nj, nkv, rows, LANES), BF16),
               pltpu.VMEM((rows, qw), F32)]
    if want_lse:
        out_shape.append(jax.ShapeDtypeStruct((seq, LANES), F32))
        out_specs.append(pl.BlockSpec((rows, LANES), lambda c, j: (c, 0)))
        scratch.append(pltpu.VMEM((rows, LANES), F32))
        if nj > 1:
            scratch.append(pltpu.VMEM((rows, LANES), F32))
    if want_krot:
        out_shape.append(jax.ShapeDtypeStruct((seq, N_KV * HEAD_DIM), F32))
        out_specs.append(pl.BlockSpec((rows, kw), lambda c, j: (c, j)))
    return pl.pallas_call(
        functools.partial(_band_kernel, dil=dil, max_dist=max_dist, use_sink=use_sink, want_lse=want_lse,
                          want_krot=want_krot, nkv=nkv, nj=nj, res_per_iter=res_per_iter),
        out_shape=tuple(out_shape), grid=(seq // rows, nj), in_specs=in_specs, out_specs=tuple(out_specs),
        scratch_shapes=scratch, compiler_params=_cparams(2), name=name)(*args)


def _merge_kernel(o1_ref, o2_ref, o3_ref, l1_ref, l2_ref, l3_ref, e_ref, out_ref):
    ls = [l1_ref[...], l2_ref[...], l3_ref[...]]
    mx = jnp.maximum(jnp.maximum(ls[0], ls[1]), ls[2])
    es = [jnp.exp(l - mx) for l in ls]
    inv = 1.0 / (es[0] + es[1] + es[2])
    e = e_ref[...]
    acc = None
    for en, o_ref in zip(es, (o1_ref, o2_ref, o3_ref)):
        wf = jnp.dot((en * inv).astype(BF16), e, preferred_element_type=F32)
        term = wf * o_ref[...].astype(F32)
        acc = term if acc is None else acc + term
    out_ref[...] = acc.astype(out_ref.dtype)


def _merge(os_, ls_, eexp, *, tm):
    seq, d = os_[0].shape
    o_spec = pl.BlockSpec((tm, d), lambda i: (i, 0))
    l_spec = pl.BlockSpec((tm, LANES), lambda i: (i, 0))
    return pl.pallas_call(
        _merge_kernel, out_shape=jax.ShapeDtypeStruct((seq, d), BF16), grid=(seq // tm,),
        in_specs=[o_spec] * 3 + [l_spec] * 3 + [pl.BlockSpec((LANES, d), lambda i: (0, 0))],
        out_specs=o_spec, compiler_params=_cparams(1), name="dil_merge")(*os_, *ls_, eexp)


def _expand_q(qb, t_len):
    gd = N_KV * HEAD_DIM
    sub = lax.broadcasted_iota(jnp.int32, (N_KV, gd), 0)
    lg = lax.broadcasted_iota(jnp.int32, (N_KV, gd), 1) >> _LOG2_HEAD_DIM
    diag = sub == lg
    tiles = []
    for r in range(REP):
        vr = jnp.concatenate(
            [qb[:, (g * REP + r) * HEAD_DIM:(g * REP + r + 1) * HEAD_DIM] for g in range(N_KV)], axis=1)
        for t in range(t_len):
            tiles.append(jnp.where(diag, jnp.broadcast_to(vr[t:t + 1], (N_KV, gd)), 0.0))
    return jnp.concatenate(tiles, axis=0).astype(BF16)


def _collapse_o(r_full, t_len):
    gd = N_KV * HEAD_DIM
    n = r_full.shape[0]
    sub = lax.broadcasted_iota(jnp.int32, (n, gd), 0) & (N_KV - 1)
    lg = lax.broadcasted_iota(jnp.int32, (n, gd), 1) >> _LOG2_HEAD_DIM
    masked = jnp.where(sub == lg, r_full, 0.0)
    red = jnp.sum(masked.reshape(n // N_KV, N_KV, gd), axis=1)
    pieces = []
    for g in range(N_KV):
        for r in range(REP):
            pieces.append(red[r * t_len:(r + 1) * t_len, g * HEAD_DIM:(g + 1) * HEAD_DIM])
    return jnp.concatenate(pieces, axis=1)


def _row_t(shape, t_len):
    r = lax.broadcasted_iota(jnp.int32, shape, 0)
    return (r >> 3) & (t_len - 1)


def _cached_attend(qx, kt, vt, mask_w, kn, vn, mask_n, sink):
    s_w = jnp.where(mask_w, jnp.dot(qx.astype(kt.dtype), kt, preferred_element_type=F32), NEG)
    s_n = jnp.where(mask_n, lax.dot_general(qx, kn, _NT, preferred_element_type=F32), NEG)
    m = jnp.maximum(jnp.max(s_w, axis=-1, keepdims=True), jnp.max(s_n, axis=-1, keepdims=True))
    if sink is not None:
        m = jnp.maximum(m, sink)
    e_w = jnp.exp(s_w - m)
    e_n = jnp.exp(s_n - m)
    den = jnp.sum(e_w, axis=-1, keepdims=True) + jnp.sum(e_n, axis=-1, keepdims=True)
    if sink is not None:
        den = den + jnp.exp(sink - m)
    r = (lax.dot_general(e_w.astype(vt.dtype), vt, _NT, preferred_element_type=F32)
         + jnp.dot(e_n.astype(BF16), vn, preferred_element_type=F32))
    return r / den, m + jnp.log(den)


def _swa_cached_kernel(q_ref, kn_ref, vn_ref, cos_ref, sa_ref, sb_ref, ck_ref, cv_ref, sink_ref,
                       o_ref, krot_ref, *, nb, t_len):
    tabs = (cos_ref[...], sa_ref[...], sb_ref[...])
    q = _rope(q_ref[...], *tabs) * ATT_SCALE
    kn = _rope(kn_ref[...], *tabs)
    krot_ref[...] = kn
    kn16 = kn.astype(BF16)
    vn16 = vn_ref[...].astype(BF16)
    lb = ck_ref.shape[3]
    gd = N_KV * HEAD_DIM
    nrow = REP * t_len * N_KV
    tq = _row_t((nrow, lb), t_len)
    mask_w = lax.broadcasted_iota(jnp.int32, (nrow, lb), 1) >= tq + 1
    mask_n = lax.broadcasted_iota(jnp.int32, (nrow, BF16_ROWS), 1) <= _row_t((nrow, BF16_ROWS), t_len)
    sk = sink_ref[...]
    pad = jnp.zeros((BF16_ROWS - t_len, gd), BF16)
    outs = []
    for b in range(nb):
        rs = slice(b * t_len, (b + 1) * t_len)
        r_full, _ = _cached_attend(
            _expand_q(q[rs], t_len), ck_ref[b].reshape(gd, lb).astype(BF16), cv_ref[b].reshape(gd, lb).astype(BF16),
            mask_w, jnp.concatenate([kn16[rs], pad], axis=0), jnp.concatenate([vn16[rs], pad], axis=0), mask_n, sk)
        outs.append(_collapse_o(r_full, t_len))
    o_ref[...] = jnp.concatenate(outs, axis=0).astype(o_ref.dtype)


def _swa_cached(p0, off, seq, n_b, t_len, tables, cache_kt, cache_vt, sink_col, *, nb):
    rows = nb * t_len
    d_att = N_HEADS * HEAD_DIM
    d_kv = N_KV * HEAD_DIM
    lb = cache_kt.shape[3]
    assert lb == SWA_WINDOW and t_len <= BF16_ROWS
    r0 = seq // rows
    nrow = REP * t_len * N_KV
    c_spec = pl.BlockSpec((nb, N_KV, HEAD_DIM, lb), lambda i: (i, 0, 0, 0))
    t_spec = pl.BlockSpec((rows, LANES), lambda i: (r0 + i, 0))
    return pl.pallas_call(
        functools.partial(_swa_cached_kernel, nb=nb, t_len=t_len),
        out_shape=(jax.ShapeDtypeStruct((n_b * t_len, d_att), BF16),
                   jax.ShapeDtypeStruct((n_b * t_len, d_kv), F32)),
        grid=(n_b // nb,),
        in_specs=[pl.BlockSpec((rows, d_att), lambda i: (i, off["q"] // d_att)),
                  pl.BlockSpec((rows, d_kv), lambda i: (i, off["k"] // d_kv)),
                  pl.BlockSpec((rows, d_kv), lambda i: (i, off["v"] // d_kv)),
                  t_spec, t_spec, t_spec, c_spec, c_spec,
                  pl.BlockSpec((nrow, 1), lambda i: (0, 0))],
        out_specs=(pl.BlockSpec((rows, d_att), lambda i: (i, 0)),
                   pl.BlockSpec((rows, d_kv), lambda i: (i, 0))),
        compiler_params=_cparams(1), name="swa_cached")(p0, p0, p0, *tables, cache_kt, cache_vt, sink_col)


def _dil_cached_kernel(q1_ref, q2_ref, q3_ref, kn_ref, vn_ref, cos_ref, sa_ref, sb_ref, ck_ref, cv_ref,
                       o_ref, krot_ref, *, nb, t_len):
    tabs = (cos_ref[...], sa_ref[...], sb_ref[...])
    qs = [_rope(r[...], *tabs) * ATT_SCALE for r in (q1_ref, q2_ref, q3_ref)]
    kn = _rope(kn_ref[...], *tabs)
    krot_ref[...] = kn
    kn16 = kn.astype(BF16)
    vn16 = vn_ref[...].astype(BF16)
    lbuf = ck_ref.shape[3]
    gd = N_KV * HEAD_DIM
    nrow = REP * t_len * N_KV
    tqn = _row_t((nrow, BF16_ROWS), t_len)
    coln = lax.broadcasted_iota(jnp.int32, (nrow, BF16_ROWS), 1)
    masks = []
    for window, dil in DIL_PATTERNS:
        tq = _row_t((nrow, window), t_len)
        col = lax.broadcasted_iota(jnp.int32, (nrow, window), 1)
        if dil == 1:
            masks.append((col >= tq, coln <= tqn))
        else:
            masks.append(((col & (dil - 1)) == tq, coln == tqn))
    pad = jnp.zeros((BF16_ROWS - t_len, gd), BF16)
    outs = []
    for b in range(nb):
        rs = slice(b * t_len, (b + 1) * t_len)
        kt = ck_ref[b].reshape(gd, lbuf)
        vt = cv_ref[b].reshape(gd, lbuf)
        knp = jnp.concatenate([kn16[rs], pad], axis=0)
        vnp = jnp.concatenate([vn16[rs], pad], axis=0)
        res = []
        for gi, (window, dil) in enumerate(DIL_PATTERNS):
            res.append(_cached_attend(_expand_q(qs[gi][rs], t_len), kt[:, lbuf - window:], vt[:, lbuf - window:],
                                      masks[gi][0], knp, vnp, masks[gi][1], None))
        mx = jnp.maximum(jnp.maximum(res[0][1], res[1][1]), res[2][1])
        ws = [jnp.exp(l - mx) for _, l in res]
        inv = 1.0 / (ws[0] + ws[1] + ws[2])
        merged = (ws[0] * inv) * res[0][0] + (ws[1] * inv) * res[1][0] + (ws[2] * inv) * res[2][0]
        outs.append(_collapse_o(merged, t_len))
    o_ref[...] = jnp.concatenate(outs, axis=0).astype(o_ref.dtype)


def _dil_cached(p1, seq, n_b, t_len, tables, cache_kt, cache_vt, *, nb):
    rows = nb * t_len
    d_att = N_HEADS * HEAD_DIM
    d_kv = N_KV * HEAD_DIM
    lbuf = cache_kt.shape[3]
    assert rows % 8 == 0 and seq % rows == 0 and t_len <= BF16_ROWS
    for window, dil in DIL_PATTERNS:
        assert window <= lbuf and (PAST_LEN - lbuf) % dil == 0 and lbuf % dil == 0 and (dil == 1 or dil >= t_len)
    r0 = seq // rows
    c_spec = pl.BlockSpec((nb, N_KV, HEAD_DIM, lbuf), lambda i: (i, 0, 0, 0))
    t_spec = pl.BlockSpec((rows, LANES), lambda i: (r0 + i, 0))
    nq = d_att * len(DIL_PATTERNS)
    return pl.pallas_call(
        functools.partial(_dil_cached_kernel, nb=nb, t_len=t_len),
        out_shape=(jax.ShapeDtypeStruct((n_b * t_len, d_att), F32),
                   jax.ShapeDtypeStruct((n_b * t_len, d_kv), F32)),
        grid=(n_b // nb,),
        in_specs=[pl.BlockSpec((rows, d_att), lambda i: (i, 0)),
                  pl.BlockSpec((rows, d_att), lambda i: (i, 1)),
                  pl.BlockSpec((rows, d_att), lambda i: (i, 2)),
                  pl.BlockSpec((rows, d_kv), lambda i: (i, nq // d_kv)),
                  pl.BlockSpec((rows, d_kv), lambda i: (i, nq // d_kv + 1)),
                  t_spec, t_spec, t_spec, c_spec, c_spec],
        out_specs=(pl.BlockSpec((rows, d_att), lambda i: (i, 0)),
                   pl.BlockSpec((rows, d_kv), lambda i: (i, 0))),
        compiler_params=_cparams(1), name="dil_cached")(p1, p1, p1, p1, p1, *tables, cache_kt, cache_vt)


def _row_tile(m, pref):
    for t in (1024, 512, 256, 128, 64, 32, 16, 8):
        if t <= pref and m % t == 0:
            return t
    raise ValueError(m)


def kernel(x_prompt, x_sample, state_conv, state_ssm, cache_swa_k, cache_swa_v, cache_dil_k, cache_dil_v, norm_mix, norm_mlp, e_w_in, e_conv_w, e_conv_b, e_dt_bias, e_a_log, e_d_skip, e_gate_norm, e_sinks, e_w_out, o_w_in, o_w_out, mlp_w1, mlp_w2, norm_final):
    nbp, seq, d = x_prompt.shape
    n_b, t_len, _ = x_sample.shape
    assert nbp == 1 and d == N_HEADS * HEAD_DIM and norm_mix.shape[0] == 2
    ms = n_b * t_len
    m = seq + ms
    tms = (_row_tile(seq, 1024), _row_tile(ms, 512))
    d_kv = N_KV * HEAD_DIM
    d_bc = 2 * SSD_GROUPS * SSD_STATE
    conv_dim = d + d_bc
    both = lambda fn, *rows, **kw: tuple(
        fn(*(r[s] for r in rows), tm=tms[s], name=kw["name"] + ("_p", "_s")[s], **{k: v for k, v in kw.items() if k != "name"})
        for s in range(2))

    h = (x_prompt.reshape(seq, d), x_sample.reshape(ms, d))
    tables = _rope_tables(m, seq, t_len, _row_tile(math.gcd(seq, ms), 512))
    eexp = (lax.broadcasted_iota(jnp.int32, (LANES, d), 1) // HEAD_DIM
            == lax.broadcasted_iota(jnp.int32, (LANES, d), 0)).astype(BF16)
    keys_minor = lambda c: jnp.transpose(c[0], (0, 2, 3, 1))

    wi = e_w_in[0]
    c_z, c_xbc, c_dt, c_q = 0, d, d + conv_dim, d + conv_dim + N_HEADS
    c_k, c_v = c_q + d, c_q + d + d_kv
    tn0 = 1024
    wi16 = wi.astype(BF16)
    w0_q, w0_kv = wi16[:, c_q:c_q + d], wi16[:, c_k:c_v + d_kv]
    w0_dt = jnp.pad(wi16[:, c_dt:c_dt + N_HEADS], ((0, 0), (0, LANES - N_HEADS)))
    assert c_z == 0 and (c_xbc + d) % tn0 == 0 and d_bc % tn0 == 0 and d % tn0 == 0
    segs0 = [(wi16, 0, 2 * d // tn0), (w0_q, 0, d // tn0), (wi16, (c_xbc + d) // tn0, d_bc // tn0),
             (w0_kv, 0, 2 * d_kv // tn0)]
    off0 = {"z": 0, "xs": d, "q": 2 * d, "bc": 3 * d, "k": 3 * d + d_bc, "v": 3 * d + d_bc + d_kv}
    g0 = norm_mix[0]
    (p0_p, dt_p), (p0_s, dt_s) = both(
        lambda x, tm, name: _norm_matmul_multi(x, g0, segs0, w0_dt, tm=tm, tn=tn0, name=name), h, name="l0_in_proj")

    cw, cb = e_conv_w[0], e_conv_b[0]
    wx, wbc = cw[:, :d], cw[:, d:]
    bx, bbc = cb[:d].reshape(1, d), cb[d:].reshape(1, d_bc)
    pad_h = lambda v: jnp.pad(v.reshape(1, N_HEADS), ((0, 0), (0, LANES - N_HEADS)))
    dtb, alog = pad_h(e_dt_bias[0]), pad_h(e_a_log[0])
    dskip = jnp.repeat(e_d_skip[0], HEAD_DIM).reshape(1, d)
    gn = e_gate_norm[0].reshape(1, d)

    ssd_p, tail_x, tail_bc, p_ssm = _ssd_prompt(p0_p, dt_p, off0, seq, wx, wbc, bx, bbc, dtb, alog, dskip, gn, eexp)
    epad = jnp.pad(state_conv[0], ((0, 0), (0, 1), (0, 0))).reshape(ms, conv_dim)
    ssd_s, s_ssm = _ssd_sample(p0_s, dt_s, off0, seq, n_b, t_len, epad, state_ssm[0], wx, wbc, bx, bbc, dtb, alog,
                               dskip, gn, eexp, nb=4)

    att_p, krot0_p = _band_attention(p0_p, seq, off0["q"], off0["k"], off0["v"], tables, e_sinks[0], dil=1,
                                     max_dist=SWA_WINDOW - 1, nkv=N_KV, want_lse=False, want_krot=True,
                                     name="swa_prompt")
    sink_col = jnp.broadcast_to(e_sinks[0].reshape(N_KV, REP).T[:, None, :], (REP, t_len, N_KV)).reshape(-1, 1)
    att_s, krot0_s = _swa_cached(p0_s, off0, seq, n_b, t_len, tables, keys_minor(cache_swa_k),
                                 keys_minor(cache_swa_v), sink_col, nb=4)

    def mlp(hh, layer, g_out=None):
        g = norm_mlp[layer]
        o_s, w1_16, w2_16 = _mlp(hh[1], g, mlp_w1, mlp_w2, g_out, tm=ms, tk=512, name=f"l{layer}_mlp_s", layer=layer)
        o_p = _mlp(hh[0], g, w1_16, w2_16, g_out, tm=tms[0], tk=512, name=f"l{layer}_mlp_p")
        return o_p, o_s

    def out_proj(xs_p, xs_s, w, hh, name):
        h_s, *w16 = _matmul_resid(xs_s, w, hh[1], tm=ms, tn=512, name=name + "_s", emit=True)
        return _matmul_resid(xs_p, w16, hh[0], tm=tms[0], tn=1024, name=name + "_p"), h_s

    h = out_proj([ssd_p, att_p], [ssd_s, att_s], e_w_out[0], h, "l0_out_proj")
    h = mlp(h, 0)

    n_pat = len(DIL_PATTERNS)
    p1_s, w1i = _norm_matmul(h[1], norm_mix[1], o_w_in[0], tm=ms, tn=1024, name="l1_in_proj_s", emit=True)
    p1_p = _norm_matmul(h[0], norm_mix[1], w1i, tm=tms[0], tn=1024, name="l1_in_proj_p")
    k_off, v_off = n_pat * d, n_pat * d + d_kv
    os_, ls_ = [], []
    krot1_p = None
    for gi, (window, dil) in enumerate(DIL_PATTERNS):
        res = _band_attention(p1_p, seq, gi * d, k_off, v_off, tables, None, dil=dil, max_dist=window // dil,
                              nkv=(2 if BAND * dil * N_HEADS * HEAD_DIM * 4 > 8 * 2 ** 20 else N_KV),
                              want_lse=True, want_krot=(gi == 0), name=f"dil_prompt_{dil}")
        os_.append(res[0])
        ls_.append(res[1])
        if gi == 0:
            krot1_p = res[2]
    merged_p = _merge(os_, ls_, eexp, tm=_row_tile(seq, 256))
    dil_s, krot1_s = _dil_cached(p1_s, seq, n_b, t_len, tables, keys_minor(cache_dil_k), keys_minor(cache_dil_v),
                                 nb=2)
    h = out_proj([merged_p], [dil_s.astype(BF16)], o_w_out[0], h, "l1_out_proj")
    y_p, y_s = mlp(h, 1, norm_final)

    keep_swa = min(SWA_WINDOW, seq)
    keep_dil = min(max(w for w, _ in DIL_PATTERNS), seq)
    kv4 = lambda a: a.reshape(a.shape[0], N_KV, HEAD_DIM)
    y_prompt = y_p.reshape(1, seq, d)
    y_sample = y_s.reshape(n_b, t_len, d)
    p_conv = jnp.concatenate([tail_x[8 - (SSD_CONV - 1):], tail_bc[8 - (SSD_CONV - 1):]], axis=1)[None, None]
    p_swa_k = kv4(krot0_p[seq - keep_swa:])[None, None]
    p_swa_v = kv4(p0_p[seq - keep_swa:, off0["v"]:off0["v"] + d_kv])[None, None]
    p_dil_k = kv4(krot1_p[seq - keep_dil:])[None, None]
    p_dil_v = kv4(p1_p[seq - keep_dil:, v_off:v_off + d_kv])[None, None]
    xbc_s = jnp.concatenate([p0_s[:, off0["xs"]:off0["xs"] + d], p0_s[:, off0["bc"]:off0["bc"] + d_bc]], axis=1)
    s_conv = xbc_s.reshape(n_b, t_len, conv_dim)[:, t_len - (SSD_CONV - 1):][None]
    bt = lambda a: a.reshape(n_b, t_len, N_KV, HEAD_DIM)[None]
    s_swa_k = bt(krot0_s)
    s_swa_v = bt(p0_s[:, off0["v"]:off0["v"] + d_kv])
    s_dil_k = bt(krot1_s)
    s_dil_v = bt(p1_s[:, v_off:v_off + d_kv])
    return (y_prompt, y_sample, p_conv, p_ssm[None, None], p_swa_k, p_swa_v, p_dil_k, p_dil_v,
            s_conv, s_ssm[None], s_swa_k, s_swa_v, s_dil_k, s_dil_v)
```

```python
import functools
import math

import jax
import jax.numpy as jnp
from jax import lax
from jax.experimental import pallas as pl
from jax.experimental.pallas import tpu as pltpu

F32 = jnp.float32
BF16 = jnp.bfloat16

NORM_EPS = 1e-5
HEAD_DIM = 64
ROT_HALF = 8
ROPE_THETA = 500000.0
PAST_LEN = 8192
N_HEADS = 32
N_KV = 8
REP = N_HEADS // N_KV
SSD_GROUPS = 4
SSD_STATE = 128
SSD_CONV = 4
SSD_CHUNK = 128
SWA_WINDOW = 128
DIL_PATTERNS = ((128, 1), (512, 4), (2048, 16))
BAND = 128
ATT_SCALE = HEAD_DIM ** -0.5
LOG2E = math.log2(math.e)
LN2 = math.log(2.0)
NEG = -1e30
_LOG2_HEAD_DIM = 6

LANES = 128
BF16_ROWS = 16
PERM_CHUNK = 256
VMEM_LIMIT_BYTES = 60 * 1024 * 1024

_NT = (((1,), (1,)), ((), ()))
_TN = (((0,), (0,)), ((), ()))


def _cparams(n_axes):
    return pltpu.CompilerParams(dimension_semantics=("arbitrary",) * n_axes,
                                vmem_limit_bytes=VMEM_LIMIT_BYTES)


def _silu(x):
    return x * jax.nn.sigmoid(x)


def _tile_lanes(t, width):
    k = width // t.shape[1]
    return t if k == 1 else jnp.concatenate([t] * k, axis=1)


def _rope(x, cos_t, sa_t, sb_t):
    w = x.shape[1]
    return (x * _tile_lanes(cos_t, w)
            + pltpu.roll(x, w - ROT_HALF, 1) * _tile_lanes(sa_t, w)
            + pltpu.roll(x, ROT_HALF, 1) * _tile_lanes(sb_t, w))


def _low_half(shape):
    return (lax.broadcasted_iota(jnp.int32, shape, 1) & (LANES - 1)) < HEAD_DIM


def _rope_table_kernel(cos_ref, sa_ref, sb_ref, *, tm, seq, dec_seq):
    i = pl.program_id(0)
    row = i * tm + lax.broadcasted_iota(jnp.int32, (tm, LANES), 0)
    lane = lax.broadcasted_iota(jnp.int32, (tm, LANES), 1)
    pos = jnp.where(row < seq, row, PAST_LEN + ((row - seq) & (dec_seq - 1)))
    c = lane & (HEAD_DIM - 1)
    f = (c & (ROT_HALF - 1)).astype(F32)
    inv_freq = jnp.exp(f * (-math.log(ROPE_THETA) / ROT_HALF))
    ang = pos.astype(F32) * inv_freq
    cs = jnp.cos(ang)
    sn = jnp.sin(ang)
    cos_ref[...] = jnp.where(c < 2 * ROT_HALF, cs, 1.0)
    sa_ref[...] = jnp.where(c < ROT_HALF, -sn, 0.0)
    sb_ref[...] = jnp.where((c >= ROT_HALF) & (c < 2 * ROT_HALF), sn, 0.0)


def _rope_tables(m, seq, dec_seq, tm):
    assert dec_seq & (dec_seq - 1) == 0 and m % tm == 0
    shp = jax.ShapeDtypeStruct((m, LANES), F32)
    spec = pl.BlockSpec((tm, LANES), lambda i: (i, 0))
    return pl.pallas_call(
        functools.partial(_rope_table_kernel, tm=tm, seq=seq, dec_seq=dec_seq),
        out_shape=(shp, shp, shp), grid=(m // tm,), out_specs=(spec, spec, spec),
        compiler_params=_cparams(1), name="rope_tables")()


def _rmsnorm_bf16(x, g):
    ms = jnp.mean(x * x, axis=-1, keepdims=True)
    return (x * lax.rsqrt(ms + NORM_EPS) * g).astype(BF16)


def _norm_mm_kernel(x_ref, g_ref, w_ref, o_ref, *rest, emit):
    xn_ref = rest[-1]

    @pl.when(pl.program_id(1) == 0)
    def _():
        xn_ref[...] = _rmsnorm_bf16(x_ref[...], g_ref[...])

    w = w_ref[...]
    if emit:
        w = w.astype(BF16)
        rest[0][...] = w
    o_ref[...] = jnp.dot(xn_ref[...], w, preferred_element_type=F32)


def _norm_matmul(x, g, w, *, tm, tn, name, emit=False):
    m, k = x.shape
    n = w.shape[1]
    assert m % tm == 0 and n % tn == 0 and w.shape[0] == k and (not emit or m == tm)
    w_spec = pl.BlockSpec((k, tn), lambda i, j: (0, j))
    o_spec = pl.BlockSpec((tm, tn), lambda i, j: (i, j))
    o_shape = jax.ShapeDtypeStruct((m, n), F32)
    return pl.pallas_call(
        functools.partial(_norm_mm_kernel, emit=emit),
        out_shape=(o_shape, jax.ShapeDtypeStruct((k, n), BF16)) if emit else o_shape,
        grid=(m // tm, n // tn),
        in_specs=[pl.BlockSpec((tm, k), lambda i, j: (i, 0)),
                  pl.BlockSpec((1, k), lambda i, j: (0, 0)),
                  w_spec],
        out_specs=(o_spec, w_spec) if emit else o_spec,
        scratch_shapes=[pltpu.VMEM((tm, k), BF16)],
        compiler_params=_cparams(2), name=name)(x, g.reshape(1, k), w)


def _norm_mm_multi_kernel(*refs, owners):
    n_w = len(owners)
    x_ref, g_ref = refs[:2]
    w_refs = refs[2:2 + n_w]
    ws_ref, o_ref, os_ref, xn_ref = refs[2 + n_w:]
    j = pl.program_id(1)

    @pl.when(j == 0)
    def _():
        xn_ref[...] = _rmsnorm_bf16(x_ref[...], g_ref[...])
        os_ref[...] = jnp.dot(xn_ref[...], ws_ref[...], preferred_element_type=F32)

    for w_ref, tiles in zip(w_refs, owners):
        own = functools.reduce(jnp.logical_or, [(j >= lo) & (j < hi) for lo, hi in tiles])

        @pl.when(own)
        def _():
            o_ref[...] = jnp.dot(xn_ref[...], w_ref[...], preferred_element_type=F32)


def _norm_matmul_multi(x, g, segs, w_small, *, tm, tn, name):
    m, k = x.shape
    arrays, owners, maps = [], [], []
    start = 0
    for w, cb0, n in segs:
        if not any(w is a for a in arrays):
            arrays.append(w)
            owners.append([])
            maps.append([])
        idx = [i for i, a in enumerate(arrays) if a is w][0]
        owners[idx].append((start, start + n))
        maps[idx].append((start, n, cb0))
        start += n
    n_tiles = start

    def index_map(pieces):
        def col(i, j):
            c = pieces[-1][2] + pieces[-1][1] - 1
            for a, n, cb0 in reversed(pieces):
                c = jnp.where(j < a + n, cb0 + jnp.maximum(j - a, 0), c)
            return 0, c
        return col

    n_small = w_small.shape[1]
    return pl.pallas_call(
        functools.partial(_norm_mm_multi_kernel, owners=tuple(tuple(o) for o in owners)),
        out_shape=(jax.ShapeDtypeStruct((m, n_tiles * tn), F32), jax.ShapeDtypeStruct((m, n_small), F32)),
        grid=(m // tm, n_tiles),
        in_specs=[pl.BlockSpec((tm, k), lambda i, j: (i, 0)),
                  pl.BlockSpec((1, k), lambda i, j: (0, 0))]
        + [pl.BlockSpec((k, tn), index_map(p)) for p in maps]
        + [pl.BlockSpec((k, n_small), lambda i, j: (0, 0))],
        out_specs=(pl.BlockSpec((tm, tn), lambda i, j: (i, j)), pl.BlockSpec((tm, n_small), lambda i, j: (i, 0))),
        scratch_shapes=[pltpu.VMEM((tm, k), BF16)],
        compiler_params=_cparams(2), name=name)(x, g.reshape(1, k), *arrays, w_small)


def _mm_resid_kernel(*refs, n_in, emit):
    x_refs, w_refs = refs[:n_in], refs[n_in:2 * n_in]
    r_ref, o_ref = refs[2 * n_in], refs[2 * n_in + 1]
    wo_refs = refs[2 * n_in + 2:]
    acc = r_ref[...]
    for s, (x_ref, w_ref) in enumerate(zip(x_refs, w_refs)):
        w = w_ref[...]
        if emit:
            w = w.astype(BF16)
            wo_refs[s][...] = w
        acc = acc + jnp.dot(x_ref[...], w, preferred_element_type=F32)
    o_ref[...] = acc


def _matmul_resid(xs, w, resid, *, tm, tn, name, emit=False):
    m, n = resid.shape
    k = xs[0].shape[1]
    assert m % tm == 0 and n % tn == 0 and (not emit or (m == tm and w.shape == (k * len(xs), n)))
    w_args = [w] * len(xs) if emit else list(w)
    in_specs = [pl.BlockSpec((tm, k), lambda i, j: (i, 0)) for _ in xs]
    in_specs += [pl.BlockSpec((k, tn), functools.partial(lambda i, j, s: (s, j), s=s if emit else 0))
                 for s in range(len(xs))]
    in_specs.append(pl.BlockSpec((tm, tn), lambda i, j: (i, j)))
    o_spec = pl.BlockSpec((tm, tn), lambda i, j: (i, j))
    o_shape = jax.ShapeDtypeStruct((m, n), F32)
    if emit:
        out_shape = (o_shape,) + (jax.ShapeDtypeStruct((k, n), BF16),) * len(xs)
        out_specs = (o_spec,) + (pl.BlockSpec((k, tn), lambda i, j: (0, j)),) * len(xs)
    else:
        out_shape, out_specs = o_shape, o_spec
    return pl.pallas_call(
        functools.partial(_mm_resid_kernel, n_in=len(xs), emit=emit),
        out_shape=out_shape,
        grid=(m // tm, n // tn),
        in_specs=in_specs,
        out_specs=out_specs,
        compiler_params=_cparams(2), name=name)(*xs, *w_args, resid)


def _mlp_kernel(x_ref, g_ref, w1_ref, w2_ref, gout_ref, o_ref, *rest, norm_out, emit):
    xn_ref = rest[-1]
    k = pl.program_id(1)

    @pl.when(k == 0)
    def _():
        x = x_ref[...]
        xn_ref[...] = _rmsnorm_bf16(x, g_ref[...])
        o_ref[...] = x

    w1, w2 = w1_ref[...], w2_ref[...]
    if emit:
        w1, w2 = w1.astype(BF16), w2.astype(BF16)
        rest[0][...] = w1
        rest[1][...] = w2
    u = jnp.dot(xn_ref[...], w1, preferred_element_type=F32)
    u = jnp.square(jnp.maximum(u, 0.0)).astype(BF16)
    o_ref[...] += jnp.dot(u, w2, preferred_element_type=F32)

    if norm_out:
        @pl.when(k == pl.num_programs(1) - 1)
        def _():
            o = o_ref[...]
            ms = jnp.mean(o * o, axis=-1, keepdims=True)
            o_ref[...] = o * lax.rsqrt(ms + NORM_EPS) * gout_ref[...]


def _mlp(x, g, w1, w2, g_out, *, tm, tk, name, layer=None):
    m, d = x.shape
    emit = layer is not None
    hid = w1.shape[-1]
    assert m % tm == 0 and hid % tk == 0 and (not emit or m == tm)
    w1o_spec = pl.BlockSpec((d, tk), lambda i, k: (0, k))
    w2o_spec = pl.BlockSpec((tk, d), lambda i, k: (k, 0))
    if emit:
        w_specs = [pl.BlockSpec((None, d, tk), lambda i, k: (layer, 0, k)),
                   pl.BlockSpec((None, tk, d), lambda i, k: (layer, k, 0))]
        out_shape = (jax.ShapeDtypeStruct((m, d), F32), jax.ShapeDtypeStruct((d, hid), BF16),
                     jax.ShapeDtypeStruct((hid, d), BF16))
        out_specs = (pl.BlockSpec((tm, d), lambda i, k: (i, 0)), w1o_spec, w2o_spec)
    else:
        w_specs = [w1o_spec, w2o_spec]
        out_shape = jax.ShapeDtypeStruct((m, d), F32)
        out_specs = pl.BlockSpec((tm, d), lambda i, k: (i, 0))
    return pl.pallas_call(
        functools.partial(_mlp_kernel, norm_out=g_out is not None, emit=emit),
        out_shape=out_shape,
        grid=(m // tm, hid // tk),
        in_specs=[pl.BlockSpec((tm, d), lambda i, k: (i, 0)),
                  pl.BlockSpec((1, d), lambda i, k: (0, 0))] + w_specs + [pl.BlockSpec((1, d), lambda i, k: (0, 0))],
        out_specs=out_specs,
        scratch_shapes=[pltpu.VMEM((tm, d), BF16)],
        compiler_params=_cparams(2), name=name)(x, g.reshape(1, d), w1, w2,
                                                (g if g_out is None else g_out).reshape(1, d))


def _causal_conv(raw, tail, w_ref, b_ref):
    n = raw.shape[0]
    full = jnp.concatenate([tail, raw], axis=0)
    acc = b_ref[...]
    for i in range(SSD_CONV):
        s = SSD_CONV - 1 - i
        acc = acc + full[8 - s:8 - s + n] * w_ref[i:i + 1, :]
    return _silu(acc)


def _gate_groupnorm(y, z, gn):
    y = y * _silu(z)
    gw = y.shape[1] // SSD_GROUPS
    outs = []
    for g in range(SSD_GROUPS):
        yg = y[:, g * gw:(g + 1) * gw]
        ms = jnp.mean(yg * yg, axis=-1, keepdims=True)
        outs.append(yg * lax.rsqrt(ms + NORM_EPS))
    return jnp.concatenate(outs, axis=1) * gn


def _ssd_prompt_kernel(z_ref, xs_ref, bc_ref, dt_ref, wx_ref, wbc_ref, bx_ref, bbc_ref, dtb_ref, alog_ref,
                       dskip_ref, gn_ref, eexp_ref, y_ref, tailx_ref, tailbc_ref, h_ref, y_s):
    c = pl.program_id(0)
    q = SSD_CHUNK
    gs = SSD_STATE
    hp = HEAD_DIM

    @pl.when(c == 0)
    def _():
        tailx_ref[...] = jnp.zeros_like(tailx_ref)
        tailbc_ref[...] = jnp.zeros_like(tailbc_ref)
        h_ref[...] = jnp.zeros_like(h_ref)

    xs_raw = xs_ref[...]
    bc_raw = bc_ref[...]
    xs_c = _causal_conv(xs_raw, tailx_ref[...], wx_ref, bx_ref)
    bc_c = _causal_conv(bc_raw, tailbc_ref[...], wbc_ref, bbc_ref)
    tailx_ref[...] = xs_raw[q - 8:q]
    tailbc_ref[...] = bc_raw[q - 8:q]

    lane = lax.broadcasted_iota(jnp.int32, (q, LANES), 1)
    row = lax.broadcasted_iota(jnp.int32, (q, LANES), 0)
    dt = jax.nn.softplus(dt_ref[...] + dtb_ref[...])
    a = -jnp.exp(alog_ref[...])
    la = jnp.where(lane < N_HEADS, dt * a, 0.0)
    acs = la
    s = 1
    while s < q:
        acs = acs + jnp.where(row >= s, pltpu.roll(acs, s, 0), 0.0)
        s *= 2
    acs_t = acs.T
    acs_last = acs[q - 1:q, :]
    to_end = jnp.exp(acs_last - acs)
    eacs = jnp.exp(acs)
    cdec = jnp.exp(acs_last)
    causal = (lax.broadcasted_iota(jnp.int32, (q, q), 0) >= lax.broadcasted_iota(jnp.int32, (q, q), 1))

    stacked = jnp.concatenate([dt, eacs, to_end], axis=0)
    hi = stacked.astype(BF16)
    lo = (stacked - hi.astype(F32)).astype(BF16)
    e = eexp_ref[...]
    full = jnp.dot(hi, e, preferred_element_type=F32) + jnp.dot(lo, e, preferred_element_type=F32)
    xdt = xs_c * full[0:q]
    eacs_f = full[q:2 * q]
    w_f = xdt * full[2 * q:3 * q]
    low = _low_half((q, LANES))
    top = lax.broadcasted_iota(jnp.int32, (2 * hp, gs), 0) < hp

    rep = N_HEADS // SSD_GROUPS
    for g in range(SSD_GROUPS):
        bg = bc_c[:, g * gs:(g + 1) * gs].astype(BF16)
        cg = bc_c[:, SSD_GROUPS * gs + g * gs:SSD_GROUPS * gs + (g + 1) * gs].astype(BF16)
        cb = lax.dot_general(cg, bg, _NT, preferred_element_type=F32)
        for rp in range(rep // 2):
            h0 = g * rep + 2 * rp
            sl = slice(h0 * hp, (h0 + 2) * hp)
            ms = []
            for h in (h0, h0 + 1):
                lmat = jnp.exp(jnp.where(causal, acs[:, h:h + 1] - acs_t[h:h + 1, :], -jnp.inf))
                ms.append((cb * lmat).astype(BF16))
            xp = xdt[:, sl]
            rhs = jnp.concatenate([jnp.where(low, xp, 0.0), jnp.where(low, 0.0, xp)], axis=0).astype(BF16)
            y = jnp.dot(jnp.concatenate(ms, axis=1), rhs, preferred_element_type=F32)
            hs = h_ref[h0:h0 + 2].reshape(2 * hp, gs)
            y = y + eacs_f[:, sl] * lax.dot_general(cg, hs.astype(BF16), _NT, preferred_element_type=F32)
            st = lax.dot_general(w_f[:, sl].astype(BF16), bg, _TN, preferred_element_type=F32)
            cd = jnp.where(top, cdec[:, h0:h0 + 1], cdec[:, h0 + 1:h0 + 2])
            h_ref[h0:h0 + 2] = (hs * cd + st).reshape(2, hp, gs)
            y_s[:, sl] = y + dskip_ref[:, sl] * xs_c[:, sl]

    y_ref[...] = _gate_groupnorm(y_s[...], z_ref[...], gn_ref[...]).astype(y_ref.dtype)


def _ssd_prompt(p0, dtr, off, seq, wx, wbc, bx, bbc, dtb, alog, dskip, gn, eexp):
    q = SSD_CHUNK
    d_in = wx.shape[1]
    d_bc = wbc.shape[1]
    assert seq % q == 0
    const = lambda shape: pl.BlockSpec(shape, lambda c: (0,) * len(shape))
    return pl.pallas_call(
        _ssd_prompt_kernel,
        out_shape=(jax.ShapeDtypeStruct((seq, d_in), BF16),
                   jax.ShapeDtypeStruct((8, d_in), F32),
                   jax.ShapeDtypeStruct((8, d_bc), F32),
                   jax.ShapeDtypeStruct((N_HEADS, HEAD_DIM, SSD_STATE), F32)),
        grid=(seq // q,),
        in_specs=[pl.BlockSpec((q, d_in), lambda c: (c, off["z"] // d_in)),
                  pl.BlockSpec((q, d_in), lambda c: (c, off["xs"] // d_in)),
                  pl.BlockSpec((q, d_bc), lambda c: (c, off["bc"] // d_bc)),
                  pl.BlockSpec((q, LANES), lambda c: (c, 0)),
                  const((SSD_CONV, d_in)), const((SSD_CONV, d_bc)), const((1, d_in)), const((1, d_bc)),
                  const((1, LANES)), const((1, LANES)), const((1, d_in)), const((1, d_in)), const((LANES, d_in))],
        out_specs=(pl.BlockSpec((q, d_in), lambda c: (c, 0)),
                   const((8, d_in)), const((8, d_bc)), const((N_HEADS, HEAD_DIM, SSD_STATE))),
        scratch_shapes=[pltpu.VMEM((q, d_in), F32)],
        compiler_params=_cparams(1), name="ssd_prompt")(p0, p0, p0, dtr, wx, wbc, bx, bbc, dtb, alog, dskip, gn, eexp)


def _split3(x):
    hi = x.astype(BF16)
    r1 = x - hi.astype(F32)
    mid = r1.astype(BF16)
    lo = (r1 - mid.astype(F32)).astype(BF16)
    return hi, mid, lo


def _ssd_sample_kernel(z_ref, xs_ref, bc_ref, dt_ref, ex_ref, ebc_ref, st_ref, wx_ref, wbc_ref, bx_ref, bbc_ref,
                       dtb_ref, alog_ref, dskip_ref, gn_ref, eexp_ref, y_ref, sto_ref, *, nb, t_len):
    rows = nb * t_len
    gs = SSD_STATE
    gw = xs_ref.shape[1] // SSD_GROUPS
    rep = N_HEADS // SSD_GROUPS

    def tcol(width):
        r = lax.broadcasted_iota(jnp.int32, (rows, width), 0)
        return r & (t_len - 1), r >> int(math.log2(t_len))

    def conv(raw, est, w_ref, b_ref):
        t, _ = tcol(raw.shape[1])
        acc = b_ref[...]
        for i in range(SSD_CONV):
            s = SSD_CONV - 1 - i
            if s == 0:
                sh = raw
            else:
                k = (rows - (SSD_CONV - 1) + s) % rows
                sh = jnp.where(t >= s, pltpu.roll(raw, s, 0), est if k == 0 else pltpu.roll(est, k, 0))
            acc = acc + sh * w_ref[i:i + 1, :]
        return _silu(acc)

    xs_c = conv(xs_ref[...], ex_ref[...], wx_ref, bx_ref)
    bc_c = conv(bc_ref[...], ebc_ref[...], wbc_ref, bbc_ref)
    bm = bc_c[:, :SSD_GROUPS * gs]
    cm = bc_c[:, SSD_GROUPS * gs:]

    t1, _ = tcol(LANES)
    lane = lax.broadcasted_iota(jnp.int32, (rows, LANES), 1)
    dt = jax.nn.softplus(dt_ref[...] + dtb_ref[...])
    a = -jnp.exp(alog_ref[...])
    la = jnp.where(lane < N_HEADS, dt * a, 0.0)
    acs = la
    for s in range(1, t_len):
        acs = acs + jnp.where(t1 >= s, pltpu.roll(la, s, 0), 0.0)
    alast = jnp.where(t1 == t_len - 1, acs, 0.0)
    for u in range(1, t_len):
        alast = alast + jnp.where(t1 == t_len - 1 - u, pltpu.roll(acs, rows - u, 0), 0.0)
    parts = [dt, jnp.exp(acs), jnp.exp(alast - acs), jnp.exp(alast)]
    for k in range(1, t_len):
        parts.append(jnp.exp(acs - pltpu.roll(acs, k, 0)))
    stacked = jnp.concatenate(parts, axis=0)
    hi, mid, lo = _split3(stacked)
    e = eexp_ref[...]
    full = (jnp.dot(hi, e, preferred_element_type=F32) + jnp.dot(mid, e, preferred_element_type=F32)
            + jnp.dot(lo, e, preferred_element_type=F32))
    dt_f, eacs_f, toend_f, cdec_f = (full[i * rows:(i + 1) * rows] for i in range(4))
    dec_f = [None] + [full[(3 + k) * rows:(4 + k) * rows] for k in range(1, t_len)]

    tw, _ = tcol(xs_c.shape[1])
    xdt = xs_c * dt_f
    y = jnp.zeros_like(xs_c)
    for k in range(t_len):
        bmk = bm if k == 0 else pltpu.roll(bm, k, 0)
        prod = cm * bmk
        cbs = []
        for g in range(SSD_GROUPS):
            sg = jnp.sum(prod[:, g * gs:(g + 1) * gs], axis=-1, keepdims=True)
            cbs.append(jnp.broadcast_to(sg, (rows, gw)))
        cb_f = jnp.concatenate(cbs, axis=1)
        if k == 0:
            y = y + cb_f * xdt
        else:
            y = y + jnp.where(tw >= k, cb_f * dec_f[k] * pltpu.roll(xdt, k, 0), 0.0)

    cm16 = cm.astype(BF16)
    bm16 = bm.astype(BF16)
    w_f = (toend_f * xdt)
    _, bg_ = tcol(gw)
    ones_blk = jnp.ones((3, gs), BF16)
    yoffs = []
    for g in range(SSD_GROUPS):
        cg = cm16[:, g * gs:(g + 1) * gs]
        rhs_top = jnp.concatenate([bm16[:, g * gs:(g + 1) * gs], jnp.zeros((rows, gs), BF16)], axis=1)
        rhs_mid = jnp.concatenate([jnp.zeros((3, gs), BF16), ones_blk], axis=1)
        rhs = jnp.concatenate([rhs_top, rhs_mid, jnp.zeros((13, 2 * gs), BF16)], axis=0)
        wg = w_f[:, g * gw:(g + 1) * gw]
        cdg = cdec_f[:, g * gw:(g + 1) * gw]
        yg = jnp.zeros((rows, gw), F32)
        for b in range(nb):
            hb = st_ref[b, g * rep:(g + 1) * rep].reshape(gw, gs)
            yb = lax.dot_general(cg, hb.astype(BF16), _NT, preferred_element_type=F32)
            yg = jnp.where(bg_ == b, yb, yg)
            wb = jnp.where(bg_ == b, wg, 0.0).astype(BF16)
            d_hi, d_mid, d_lo = _split3(cdg[b * t_len:b * t_len + 1])
            lhs = jnp.concatenate([wb, d_hi, d_mid, d_lo, jnp.zeros((13, gw), BF16)], axis=0)
            sd = lax.dot_general(lhs, rhs, _TN, preferred_element_type=F32)
            sto_ref[b, g * rep:(g + 1) * rep] = (hb * sd[:, gs:] + sd[:, :gs]).reshape(rep, HEAD_DIM, gs)
        yoffs.append(yg)
    y = y + eacs_f * jnp.concatenate(yoffs, axis=1) + dskip_ref[...] * xs_c
    y_ref[...] = _gate_groupnorm(y, z_ref[...], gn_ref[...]).astype(y_ref.dtype)


def _ssd_sample(p0, dtr, off, seq, n_b, t_len, epad, state, wx, wbc, bx, bbc, dtb, alog, dskip, gn, eexp, *, nb):
    rows = nb * t_len
    d_in = wx.shape[1]
    d_bc = wbc.shape[1]
    assert n_b % nb == 0 and t_len & (t_len - 1) == 0
    const = lambda shape: pl.BlockSpec(shape, lambda i: (0,) * len(shape))
    st_spec = pl.BlockSpec((nb, N_HEADS, HEAD_DIM, SSD_STATE), lambda i: (i, 0, 0, 0))
    return pl.pallas_call(
        functools.partial(_ssd_sample_kernel, nb=nb, t_len=t_len),
        out_shape=(jax.ShapeDtypeStruct((n_b * t_len, d_in), BF16),
                   jax.ShapeDtypeStruct(state.shape, F32)),
        grid=(n_b // nb,),
        in_specs=[pl.BlockSpec((rows, d_in), lambda i: (i, off["z"] // d_in)),
                  pl.BlockSpec((rows, d_in), lambda i: (i, off["xs"] // d_in)),
                  pl.BlockSpec((rows, d_bc), lambda i: (i, off["bc"] // d_bc)),
                  pl.BlockSpec((rows, LANES), lambda i: (i, 0)),
                  pl.BlockSpec((rows, d_in), lambda i: (i, 0)),
                  pl.BlockSpec((rows, d_bc), lambda i: (i, d_in // d_bc)),
                  st_spec,
                  const((SSD_CONV, d_in)), const((SSD_CONV, d_bc)), const((1, d_in)), const((1, d_bc)),
                  const((1, LANES)), const((1, LANES)), const((1, d_in)), const((1, d_in)),
                  const((LANES, d_in))],
        out_specs=(pl.BlockSpec((rows, d_in), lambda i: (i, 0)), st_spec),
        compiler_params=_cparams(1), name="ssd_sample")(
            p0, p0, p0, dtr, epad, epad, state, wx, wbc, bx, bbc, dtb, alog, dskip, gn, eexp)


def _perm_matrix(dil):
    i = jnp.arange(PERM_CHUNK)
    blk = BF16_ROWS * dil
    src = (i // blk) * blk + (i % BF16_ROWS) * dil + (i % blk) // BF16_ROWS
    return (src[:, None] == jnp.arange(PERM_CHUNK)[None, :]).astype(BF16)


def _band_kernel(*refs, dil, max_dist, use_sink, want_lse, want_krot, nkv, nj):
    it = iter(refs)
    q_ref, k_ref, v_ref, cos_ref, sa_ref, sb_ref = (next(it) for _ in range(6))
    p_ref, pt_ref = (next(it), next(it)) if dil > 1 else (None, None)
    sink_ref = next(it) if use_sink else None
    o_ref = next(it)
    lse_ref = next(it) if want_lse else None
    krot_ref = next(it) if want_krot else None
    qlo_s, qhi_s, kd_s, vd_s, kdp_s, vdp_s, op_s = (next(it) for _ in range(7))
    lsep_s = next(it) if want_lse else None
    lse_s = next(it) if (want_lse and nj > 1) else None

    c = pl.program_id(0)
    j = pl.program_id(1)
    rows, wq = q_ref.shape
    wk = k_ref.shape[1]
    n_chunk = rows // PERM_CHUNK if dil > 1 else 0
    n_grp = BAND // BF16_ROWS

    def perm(x):
        if dil == 1:
            return x
        x16 = x.astype(BF16)
        return jnp.concatenate(
            [jnp.dot(p_ref[...], x16[k * PERM_CHUNK:(k + 1) * PERM_CHUNK], preferred_element_type=F32)
             for k in range(n_chunk)], axis=0)

    def unperm(xp16):
        if dil == 1:
            return [xp16]
        return [jnp.dot(pt_ref[...], xp16[k * PERM_CHUNK:(k + 1) * PERM_CHUNK], preferred_element_type=F32)
                for k in range(n_chunk)]

    tabs = (cos_ref[...], sa_ref[...], sb_ref[...])
    q = perm(_rope(q_ref[...], *tabs) * (ATT_SCALE * LOG2E))
    lowq = _low_half((rows, wq))
    qlo_s[...] = jnp.where(lowq, q, 0.0).astype(BF16)
    qhi_s[...] = jnp.where(lowq, 0.0, q).astype(BF16)
    kr = _rope(k_ref[...], *tabs)
    if want_krot:
        krot_ref[...] = kr
    lowk = _low_half((rows, LANES))
    for x, d_s in ((perm(kr), kd_s), (perm(v_ref[...]), vd_s)):
        up = pltpu.roll(x, HEAD_DIM, 1)
        dn = pltpu.roll(x, wk - HEAD_DIM, 1)
        for t in range(wk // LANES):
            sl = slice(t * LANES, (t + 1) * LANES)
            d_s[2 * t] = jnp.where(lowk, x[:, sl], up[:, sl]).astype(BF16)
            d_s[2 * t + 1] = jnp.where(lowk, dn[:, sl], x[:, sl]).astype(BF16)

    @pl.when(c == 0)
    def _():
        kdp_s[j] = jnp.zeros(kdp_s.shape[1:], BF16)
        vdp_s[j] = jnp.zeros(vdp_s.shape[1:], BF16)

    qi = lax.broadcasted_iota(jnp.int32, (BAND, 2 * BAND), 0)
    kj = lax.broadcasted_iota(jnp.int32, (BAND, 2 * BAND), 1)
    dist = qi - kj + BAND
    bias = jnp.where((dist >= 0) & (dist <= max_dist) & ((kj >= BAND) | (c > 0)), 0.0, NEG)
    col0 = kj == 0
    ones_blk = jnp.ones((2 * BAND, LANES), BF16)
    low = _low_half((BAND, LANES))
    lane = lax.broadcasted_iota(jnp.int32, (BAND, LANES), 1)
    step = BF16_ROWS * dil

    def residue(rho):
        starts = [b * step + rho * BF16_ROWS for b in range(n_grp)]

        def gather(get):
            return jnp.concatenate([get(s) for s in starts], axis=0)

        tiles = []
        lse_acc = jnp.zeros((BAND, LANES), F32)
        for g in range(nkv):
            ca, cb = slice(2 * g * LANES, (2 * g + 1) * LANES), slice((2 * g + 1) * LANES, (2 * g + 2) * LANES)
            lhs = jnp.concatenate([gather(lambda s: qlo_s[pl.ds(s, BF16_ROWS), ca]),
                                   gather(lambda s: qhi_s[pl.ds(s, BF16_ROWS), ca]),
                                   gather(lambda s: qlo_s[pl.ds(s, BF16_ROWS), cb]),
                                   gather(lambda s: qhi_s[pl.ds(s, BF16_ROWS), cb])], axis=0)
            kcat = jnp.concatenate([gather(lambda s: kdp_s[j, g, pl.ds(s, BF16_ROWS), :]),
                                    gather(lambda s: kd_s[g, pl.ds(s, BF16_ROWS), :])], axis=0)
            vcat = jnp.concatenate([gather(lambda s: vdp_s[j, g, pl.ds(s, BF16_ROWS), :]),
                                    gather(lambda s: vd_s[g, pl.ds(s, BF16_ROWS), :])], axis=0)
            head0 = j * REP * nkv + g * REP
            s_full = lax.dot_general(lhs, kcat, _NT, preferred_element_type=F32)
            rsl = [slice(r * BAND, (r + 1) * BAND) for r in range(REP)]
            if use_sink:
                sc = jnp.concatenate([jnp.where(col0, sink_ref[head0 + r] * LOG2E, s_full[rsl[r]] + bias)
                                      for r in range(REP)], axis=0)
            else:
                sc = s_full + jnp.concatenate([bias] * REP, axis=0)
            m = jnp.max(sc, axis=-1, keepdims=True)
            e = jnp.exp2(sc - m)
            od = jnp.dot(e.astype(BF16), jnp.concatenate([vcat, ones_blk], axis=1), preferred_element_type=F32)
            den = od[:, LANES:]
            o = od[:, :LANES] * (1.0 / den)
            tiles.append(jnp.where(low, o[0:BAND], o[BAND:2 * BAND]))
            tiles.append(jnp.where(low, o[2 * BAND:3 * BAND], o[3 * BAND:4 * BAND]))
            if want_lse:
                lse = (m + jnp.log2(den)) * LN2
                for r in range(REP):
                    lse_acc = lse_acc + jnp.where(lane == head0 + r, lse[rsl[r]], 0.0)
        o_res = jnp.concatenate(tiles, axis=1)
        for b, s0 in enumerate(starts):
            op_s[pl.ds(s0, BF16_ROWS), :] = o_res[b * BF16_ROWS:(b + 1) * BF16_ROWS]
            if want_lse:
                lsep_s[pl.ds(s0, BF16_ROWS), :] = lse_acc[b * BF16_ROWS:(b + 1) * BF16_ROWS]

    for rho in range(dil):
        residue(rho)

    for k, blk in enumerate(unperm(op_s[...].astype(BF16))):
        if dil == 1:
            o_ref[...] = blk
        else:
            o_ref[k * PERM_CHUNK:(k + 1) * PERM_CHUNK, :] = blk.astype(o_ref.dtype)
    if want_lse:
        if dil == 1:
            lse_nat = lsep_s[...]
        else:
            parts = [unperm(x) for x in _split3(lsep_s[...])]
            lse_nat = jnp.concatenate([a + b + d for a, b, d in zip(*parts)], axis=0)
        if nj == 1:
            lse_ref[...] = lse_nat
        else:
            @pl.when(j == 0)
            def _():
                lse_s[...] = lse_nat

            @pl.when(j > 0)
            def _():
                lse_s[...] = lse_s[...] + lse_nat

            @pl.when(j == nj - 1)
            def _():
                lse_ref[...] = lse_s[...]
    kdp_s[j] = kd_s[...]
    if use_sink:
        keep = (lax.broadcasted_iota(jnp.int32, (rows, LANES), 0) > 0).astype(F32).astype(BF16)
        vdp_s[j] = vd_s[...] * keep
    else:
        vdp_s[j] = vd_s[...]


def _band_attention(p, seq, q_off, k_off, v_off, tables, sinks, *, dil, max_dist, nkv, want_lse, want_krot, name):
    rows = BAND * dil
    nj = N_KV // nkv
    qw, kw = nkv * REP * HEAD_DIM, nkv * HEAD_DIM
    assert seq % rows == 0 and kw % LANES == 0 and q_off % qw == 0 and k_off % kw == 0 and v_off % kw == 0
    assert dil == 1 or (rows % PERM_CHUNK == 0 and PERM_CHUNK % (BF16_ROWS * dil) == 0)
    use_sink = sinks is not None
    d_att = N_HEADS * HEAD_DIM
    assert not use_sink or (dil == 1 and max_dist < BAND)
    blk = lambda width, cb: pl.BlockSpec((rows, width), lambda c, j: (c, cb + j))
    t_spec = pl.BlockSpec((rows, LANES), lambda c, j: (c, 0))
    in_specs = [blk(qw, q_off // qw), blk(kw, k_off // kw), blk(kw, v_off // kw), t_spec, t_spec, t_spec]
    args = [p, p, p] + list(tables)
    if dil > 1:
        pm = _perm_matrix(dil)
        in_specs += [pl.BlockSpec((PERM_CHUNK, PERM_CHUNK), lambda c, j: (0, 0))] * 2
        args += [pm, pm.T]
    if use_sink:
        in_specs.append(pl.BlockSpec(memory_space=pltpu.SMEM))
        args.append(sinks)
    out_shape = [jax.ShapeDtypeStruct((seq, d_att), BF16)]
    out_specs = [pl.BlockSpec((rows, qw), lambda c, j: (c, j))]
    scratch = [pltpu.VMEM((rows, qw), BF16), pltpu.VMEM((rows, qw), BF16),
               pltpu.VMEM((nkv, rows, LANES), BF16), pltpu.VMEM((nkv, rows, LANES), BF16),
               pltpu.VMEM((nj, nkv, rows, LANES), BF16), pltpu.VMEM((nj, nkv, rows, LANES), BF16),
               pltpu.VMEM((rows, qw), F32)]
    if want_lse:
        out_shape.append(jax.ShapeDtypeStruct((seq, LANES), F32))
        out_specs.append(pl.BlockSpec((rows, LANES), lambda c, j: (c, 0)))
        scratch.append(pltpu.VMEM((rows, LANES), F32))
        if nj > 1:
            scratch.append(pltpu.VMEM((rows, LANES), F32))
    if want_krot:
        out_shape.append(jax.ShapeDtypeStruct((seq, N_KV * HEAD_DIM), F32))
        out_specs.append(pl.BlockSpec((rows, kw), lambda c, j: (c, j)))
    return pl.pallas_call(
        functools.partial(_band_kernel, dil=dil, max_dist=max_dist, use_sink=use_sink, want_lse=want_lse,
                          want_krot=want_krot, nkv=nkv, nj=nj),
        out_shape=tuple(out_shape), grid=(seq // rows, nj), in_specs=in_specs, out_specs=tuple(out_specs),
        scratch_shapes=scratch, compiler_params=_cparams(2), name=name)(*args)


def _merge_kernel(o1_ref, o2_ref, o3_ref, l1_ref, l2_ref, l3_ref, e_ref, out_ref):
    ls = [l1_ref[...], l2_ref[...], l3_ref[...]]
    mx = jnp.maximum(jnp.maximum(ls[0], ls[1]), ls[2])
    es = [jnp.exp(l - mx) for l in ls]
    inv = 1.0 / (es[0] + es[1] + es[2])
    e = e_ref[...]
    acc = None
    for en, o_ref in zip(es, (o1_ref, o2_ref, o3_ref)):
        wf = jnp.dot((en * inv).astype(BF16), e, preferred_element_type=F32)
        term = wf * o_ref[...].astype(F32)
        acc = term if acc is None else acc + term
    out_ref[...] = acc.astype(out_ref.dtype)


def _merge(os_, ls_, eexp, *, tm):
    seq, d = os_[0].shape
    o_spec = pl.BlockSpec((tm, d), lambda i: (i, 0))
    l_spec = pl.BlockSpec((tm, LANES), lambda i: (i, 0))
    return pl.pallas_call(
        _merge_kernel, out_shape=jax.ShapeDtypeStruct((seq, d), BF16), grid=(seq // tm,),
        in_specs=[o_spec] * 3 + [l_spec] * 3 + [pl.BlockSpec((LANES, d), lambda i: (0, 0))],
        out_specs=o_spec, compiler_params=_cparams(1), name="dil_merge")(*os_, *ls_, eexp)


def _expand_q(qb, t_len):
    gd = N_KV * HEAD_DIM
    sub = lax.broadcasted_iota(jnp.int32, (N_KV, gd), 0)
    lg = lax.broadcasted_iota(jnp.int32, (N_KV, gd), 1) >> _LOG2_HEAD_DIM
    diag = sub == lg
    tiles = []
    for r in range(REP):
        vr = jnp.concatenate(
            [qb[:, (g * REP + r) * HEAD_DIM:(g * REP + r + 1) * HEAD_DIM] for g in range(N_KV)], axis=1)
        for t in range(t_len):
            tiles.append(jnp.where(diag, jnp.broadcast_to(vr[t:t + 1], (N_KV, gd)), 0.0))
    return jnp.concatenate(tiles, axis=0).astype(BF16)


def _collapse_o(r_full, t_len):
    gd = N_KV * HEAD_DIM
    n = r_full.shape[0]
    sub = lax.broadcasted_iota(jnp.int32, (n, gd), 0) & (N_KV - 1)
    lg = lax.broadcasted_iota(jnp.int32, (n, gd), 1) >> _LOG2_HEAD_DIM
    masked = jnp.where(sub == lg, r_full, 0.0)
    red = jnp.sum(masked.reshape(n // N_KV, N_KV, gd), axis=1)
    pieces = []
    for g in range(N_KV):
        for r in range(REP):
            pieces.append(red[r * t_len:(r + 1) * t_len, g * HEAD_DIM:(g + 1) * HEAD_DIM])
    return jnp.concatenate(pieces, axis=1)


def _row_t(shape, t_len):
    r = lax.broadcasted_iota(jnp.int32, shape, 0)
    return (r >> 3) & (t_len - 1)


def _cached_attend(qx, kt, vt, mask_w, kn, vn, mask_n, sink):
    s_w = jnp.where(mask_w, jnp.dot(qx.astype(kt.dtype), kt, preferred_element_type=F32), NEG)
    s_n = jnp.where(mask_n, lax.dot_general(qx, kn, _NT, preferred_element_type=F32), NEG)
    m = jnp.maximum(jnp.max(s_w, axis=-1, keepdims=True), jnp.max(s_n, axis=-1, keepdims=True))
    if sink is not None:
        m = jnp.maximum(m, sink)
    e_w = jnp.exp(s_w - m)
    e_n = jnp.exp(s_n - m)
    den = jnp.sum(e_w, axis=-1, keepdims=True) + jnp.sum(e_n, axis=-1, keepdims=True)
    if sink is not None:
        den = den + jnp.exp(sink - m)
    r = (lax.dot_general(e_w.astype(vt.dtype), vt, _NT, preferred_element_type=F32)
         + jnp.dot(e_n.astype(BF16), vn, preferred_element_type=F32))
    return r / den, m + jnp.log(den)


def _swa_cached_kernel(q_ref, kn_ref, vn_ref, cos_ref, sa_ref, sb_ref, ck_ref, cv_ref, sink_ref,
                       o_ref, krot_ref, *, nb, t_len):
    tabs = (cos_ref[...], sa_ref[...], sb_ref[...])
    q = _rope(q_ref[...], *tabs) * ATT_SCALE
    kn = _rope(kn_ref[...], *tabs)
    krot_ref[...] = kn
    kn16 = kn.astype(BF16)
    vn16 = vn_ref[...].astype(BF16)
    lb = ck_ref.shape[3]
    gd = N_KV * HEAD_DIM
    nrow = REP * t_len * N_KV
    tq = _row_t((nrow, lb), t_len)
    mask_w = lax.broadcasted_iota(jnp.int32, (nrow, lb), 1) >= tq + 1
    mask_n = lax.broadcasted_iota(jnp.int32, (nrow, BF16_ROWS), 1) <= _row_t((nrow, BF16_ROWS), t_len)
    sk = sink_ref[...]
    pad = jnp.zeros((BF16_ROWS - t_len, gd), BF16)
    outs = []
    for b in range(nb):
        rs = slice(b * t_len, (b + 1) * t_len)
        r_full, _ = _cached_attend(
            _expand_q(q[rs], t_len), ck_ref[b].reshape(gd, lb).astype(BF16), cv_ref[b].reshape(gd, lb).astype(BF16),
            mask_w, jnp.concatenate([kn16[rs], pad], axis=0), jnp.concatenate([vn16[rs], pad], axis=0), mask_n, sk)
        outs.append(_collapse_o(r_full, t_len))
    o_ref[...] = jnp.concatenate(outs, axis=0).astype(o_ref.dtype)


def _swa_cached(p0, off, seq, n_b, t_len, tables, cache_kt, cache_vt, sink_col, *, nb):
    rows = nb * t_len
    d_att = N_HEADS * HEAD_DIM
    d_kv = N_KV * HEAD_DIM
    lb = cache_kt.shape[3]
    assert lb == SWA_WINDOW and t_len <= BF16_ROWS
    r0 = seq // rows
    nrow = REP * t_len * N_KV
    c_spec = pl.BlockSpec((nb, N_KV, HEAD_DIM, lb), lambda i: (i, 0, 0, 0))
    t_spec = pl.BlockSpec((rows, LANES), lambda i: (r0 + i, 0))
    return pl.pallas_call(
        functools.partial(_swa_cached_kernel, nb=nb, t_len=t_len),
        out_shape=(jax.ShapeDtypeStruct((n_b * t_len, d_att), BF16),
                   jax.ShapeDtypeStruct((n_b * t_len, d_kv), F32)),
        grid=(n_b // nb,),
        in_specs=[pl.BlockSpec((rows, d_att), lambda i: (i, off["q"] // d_att)),
                  pl.BlockSpec((rows, d_kv), lambda i: (i, off["k"] // d_kv)),
                  pl.BlockSpec((rows, d_kv), lambda i: (i, off["v"] // d_kv)),
                  t_spec, t_spec, t_spec, c_spec, c_spec,
                  pl.BlockSpec((nrow, 1), lambda i: (0, 0))],
        out_specs=(pl.BlockSpec((rows, d_att), lambda i: (i, 0)),
                   pl.BlockSpec((rows, d_kv), lambda i: (i, 0))),
        compiler_params=_cparams(1), name="swa_cached")(p0, p0, p0, *tables, cache_kt, cache_vt, sink_col)


def _dil_cached_kernel(q1_ref, q2_ref, q3_ref, kn_ref, vn_ref, cos_ref, sa_ref, sb_ref, ck_ref, cv_ref,
                       o_ref, krot_ref, *, nb, t_len):
    tabs = (cos_ref[...], sa_ref[...], sb_ref[...])
    qs = [_rope(r[...], *tabs) * ATT_SCALE for r in (q1_ref, q2_ref, q3_ref)]
    kn = _rope(kn_ref[...], *tabs)
    krot_ref[...] = kn
    kn16 = kn.astype(BF16)
    vn16 = vn_ref[...].astype(BF16)
    lbuf = ck_ref.shape[3]
    gd = N_KV * HEAD_DIM
    nrow = REP * t_len * N_KV
    tqn = _row_t((nrow, BF16_ROWS), t_len)
    coln = lax.broadcasted_iota(jnp.int32, (nrow, BF16_ROWS), 1)
    masks = []
    for window, dil in DIL_PATTERNS:
        tq = _row_t((nrow, window), t_len)
        col = lax.broadcasted_iota(jnp.int32, (nrow, window), 1)
        if dil == 1:
            masks.append((col >= tq, coln <= tqn))
        else:
            masks.append(((col & (dil - 1)) == tq, coln == tqn))
    pad = jnp.zeros((BF16_ROWS - t_len, gd), BF16)
    outs = []
    for b in range(nb):
        rs = slice(b * t_len, (b + 1) * t_len)
        kt = ck_ref[b].reshape(gd, lbuf)
        vt = cv_ref[b].reshape(gd, lbuf)
        knp = jnp.concatenate([kn16[rs], pad], axis=0)
        vnp = jnp.concatenate([vn16[rs], pad], axis=0)
        res = []
        for gi, (window, dil) in enumerate(DIL_PATTERNS):
            res.append(_cached_attend(_expand_q(qs[gi][rs], t_len), kt[:, lbuf - window:], vt[:, lbuf - window:],
                                      masks[gi][0], knp, vnp, masks[gi][1], None))
        mx = jnp.maximum(jnp.maximum(res[0][1], res[1][1]), res[2][1])
        ws = [jnp.exp(l - mx) for _, l in res]
        inv = 1.0 / (ws[0] + ws[1] + ws[2])
        merged = (ws[0] * inv) * res[0][0] + (ws[1] * inv) * res[1][0] + (ws[2] * inv) * res[2][0]
        outs.append(_collapse_o(merged, t_len))
    o_ref[...] = jnp.concatenate(outs, axis=0).astype(o_ref.dtype)


def _dil_cached(p1, seq, n_b, t_len, tables, cache_kt, cache_vt, *, nb):
    rows = nb * t_len
    d_att = N_HEADS * HEAD_DIM
    d_kv = N_KV * HEAD_DIM
    lbuf = cache_kt.shape[3]
    assert rows % 8 == 0 and seq % rows == 0 and t_len <= BF16_ROWS
    for window, dil in DIL_PATTERNS:
        assert window <= lbuf and (PAST_LEN - lbuf) % dil == 0 and lbuf % dil == 0 and (dil == 1 or dil >= t_len)
    r0 = seq // rows
    c_spec = pl.BlockSpec((nb, N_KV, HEAD_DIM, lbuf), lambda i: (i, 0, 0, 0))
    t_spec = pl.BlockSpec((rows, LANES), lambda i: (r0 + i, 0))
    nq = d_att * len(DIL_PATTERNS)
    return pl.pallas_call(
        functools.partial(_dil_cached_kernel, nb=nb, t_len=t_len),
        out_shape=(jax.ShapeDtypeStruct((n_b * t_len, d_att), F32),
                   jax.ShapeDtypeStruct((n_b * t_len, d_kv), F32)),
        grid=(n_b // nb,),
        in_specs=[pl.BlockSpec((rows, d_att), lambda i: (i, 0)),
                  pl.BlockSpec((rows, d_att), lambda i: (i, 1)),
                  pl.BlockSpec((rows, d_att), lambda i: (i, 2)),
                  pl.BlockSpec((rows, d_kv), lambda i: (i, nq // d_kv)),
                  pl.BlockSpec((rows, d_kv), lambda i: (i, nq // d_kv + 1)),
                  t_spec, t_spec, t_spec, c_spec, c_spec],
        out_specs=(pl.BlockSpec((rows, d_att), lambda i: (i, 0)),
                   pl.BlockSpec((rows, d_kv), lambda i: (i, 0))),
        compiler_params=_cparams(1), name="dil_cached")(p1, p1, p1, p1, p1, *tables, cache_kt, cache_vt)


def _row_tile(m, pref):
    for t in (1024, 512, 256, 128, 64, 32, 16, 8):
        if t <= pref and m % t == 0:
            return t
    raise ValueError(m)


def kernel(x_prompt, x_sample, state_conv, state_ssm, cache_swa_k, cache_swa_v, cache_dil_k, cache_dil_v, norm_mix, norm_mlp, e_w_in, e_conv_w, e_conv_b, e_dt_bias, e_a_log, e_d_skip, e_gate_norm, e_sinks, e_w_out, o_w_in, o_w_out, mlp_w1, mlp_w2, norm_final):
    nbp, seq, d = x_prompt.shape
    n_b, t_len, _ = x_sample.shape
    assert nbp == 1 and d == N_HEADS * HEAD_DIM and norm_mix.shape[0] == 2
    ms = n_b * t_len
    m = seq + ms
    tms = (_row_tile(seq, 1024), _row_tile(ms, 512))
    d_kv = N_KV * HEAD_DIM
    d_bc = 2 * SSD_GROUPS * SSD_STATE
    conv_dim = d + d_bc
    both = lambda fn, *rows, **kw: tuple(
        fn(*(r[s] for r in rows), tm=tms[s], name=kw["name"] + ("_p", "_s")[s], **{k: v for k, v in kw.items() if k != "name"})
        for s in range(2))

    h = (x_prompt.reshape(seq, d), x_sample.reshape(ms, d))
    tables = _rope_tables(m, seq, t_len, _row_tile(math.gcd(seq, ms), 512))
    eexp = (lax.broadcasted_iota(jnp.int32, (LANES, d), 1) // HEAD_DIM
            == lax.broadcasted_iota(jnp.int32, (LANES, d), 0)).astype(BF16)
    keys_minor = lambda c: jnp.transpose(c[0], (0, 2, 3, 1))

    wi = e_w_in[0]
    c_z, c_xbc, c_dt, c_q = 0, d, d + conv_dim, d + conv_dim + N_HEADS
    c_k, c_v = c_q + d, c_q + d + d_kv
    tn0 = 1024
    wi16 = wi.astype(BF16)
    w0_q, w0_kv = wi16[:, c_q:c_q + d], wi16[:, c_k:c_v + d_kv]
    w0_dt = jnp.pad(wi16[:, c_dt:c_dt + N_HEADS], ((0, 0), (0, LANES - N_HEADS)))
    assert c_z == 0 and (c_xbc + d) % tn0 == 0 and d_bc % tn0 == 0 and d % tn0 == 0
    segs0 = [(wi16, 0, 2 * d // tn0), (w0_q, 0, d // tn0), (wi16, (c_xbc + d) // tn0, d_bc // tn0),
             (w0_kv, 0, 2 * d_kv // tn0)]
    off0 = {"z": 0, "xs": d, "q": 2 * d, "bc": 3 * d, "k": 3 * d + d_bc, "v": 3 * d + d_bc + d_kv}
    g0 = norm_mix[0]
    (p0_p, dt_p), (p0_s, dt_s) = both(
        lambda x, tm, name: _norm_matmul_multi(x, g0, segs0, w0_dt, tm=tm, tn=tn0, name=name), h, name="l0_in_proj")

    cw, cb = e_conv_w[0], e_conv_b[0]
    wx, wbc = cw[:, :d], cw[:, d:]
    bx, bbc = cb[:d].reshape(1, d), cb[d:].reshape(1, d_bc)
    pad_h = lambda v: jnp.pad(v.reshape(1, N_HEADS), ((0, 0), (0, LANES - N_HEADS)))
    dtb, alog = pad_h(e_dt_bias[0]), pad_h(e_a_log[0])
    dskip = jnp.repeat(e_d_skip[0], HEAD_DIM).reshape(1, d)
    gn = e_gate_norm[0].reshape(1, d)

    ssd_p, tail_x, tail_bc, p_ssm = _ssd_prompt(p0_p, dt_p, off0, seq, wx, wbc, bx, bbc, dtb, alog, dskip, gn, eexp)
    epad = jnp.pad(state_conv[0], ((0, 0), (0, 1), (0, 0))).reshape(ms, conv_dim)
    ssd_s, s_ssm = _ssd_sample(p0_s, dt_s, off0, seq, n_b, t_len, epad, state_ssm[0], wx, wbc, bx, bbc, dtb, alog,
                               dskip, gn, eexp, nb=4)

    att_p, krot0_p = _band_attention(p0_p, seq, off0["q"], off0["k"], off0["v"], tables, e_sinks[0], dil=1,
                                     max_dist=SWA_WINDOW - 1, nkv=N_KV, want_lse=False, want_krot=True,
                                     name="swa_prompt")
    sink_col = jnp.broadcast_to(e_sinks[0].reshape(N_KV, REP).T[:, None, :], (REP, t_len, N_KV)).reshape(-1, 1)
    att_s, krot0_s = _swa_cached(p0_s, off0, seq, n_b, t_len, tables, keys_minor(cache_swa_k),
                                 keys_minor(cache_swa_v), sink_col, nb=4)

    def mlp(hh, layer, g_out=None):
        g = norm_mlp[layer]
        o_s, w1_16, w2_16 = _mlp(hh[1], g, mlp_w1, mlp_w2, g_out, tm=ms, tk=512, name=f"l{layer}_mlp_s", layer=layer)
        o_p = _mlp(hh[0], g, w1_16, w2_16, g_out, tm=tms[0], tk=1024, name=f"l{layer}_mlp_p")
        return o_p, o_s

    def out_proj(xs_p, xs_s, w, hh, name):
        h_s, *w16 = _matmul_resid(xs_s, w, hh[1], tm=ms, tn=512, name=name + "_s", emit=True)
        return _matmul_resid(xs_p, w16, hh[0], tm=tms[0], tn=1024, name=name + "_p"), h_s

    h = out_proj([ssd_p, att_p], [ssd_s, att_s], e_w_out[0], h, "l0_out_proj")
    h = mlp(h, 0)

    n_pat = len(DIL_PATTERNS)
    p1_s, w1i = _norm_matmul(h[1], norm_mix[1], o_w_in[0], tm=ms, tn=1024, name="l1_in_proj_s", emit=True)
    p1_p = _norm_matmul(h[0], norm_mix[1], w1i, tm=tms[0], tn=1792, name="l1_in_proj_p")
    k_off, v_off = n_pat * d, n_pat * d + d_kv
    os_, ls_ = [], []
    krot1_p = None
    for gi, (window, dil) in enumerate(DIL_PATTERNS):
        res = _band_attention(p1_p, seq, gi * d, k_off, v_off, tables, None, dil=dil, max_dist=window // dil,
                              nkv=(2 if BAND * dil * N_HEADS * HEAD_DIM * 4 > 8 * 2 ** 20 else N_KV),
                              want_lse=True, want_krot=(gi == 0), name=f"dil_prompt_{dil}")
        os_.append(res[0])
        ls_.append(res[1])
        if gi == 0:
            krot1_p = res[2]
    merged_p = _merge(os_, ls_, eexp, tm=_row_tile(seq, 256))
    dil_s, krot1_s = _dil_cached(p1_s, seq, n_b, t_len, tables, keys_minor(cache_dil_k), keys_minor(cache_dil_v),
                                 nb=2)
    h = out_proj([merged_p], [dil_s.astype(BF16)], o_w_out[0], h, "l1_out_proj")
    y_p, y_s = mlp(h, 1, norm_final)

    keep_swa = min(SWA_WINDOW, seq)
    keep_dil = min(max(w for w, _ in DIL_PATTERNS), seq)
    kv4 = lambda a: a.reshape(a.shape[0], N_KV, HEAD_DIM)
    y_prompt = y_p.reshape(1, seq, d)
    y_sample = y_s.reshape(n_b, t_len, d)
    p_conv = jnp.concatenate([tail_x[8 - (SSD_CONV - 1):], tail_bc[8 - (SSD_CONV - 1):]], axis=1)[None, None]
    p_swa_k = kv4(krot0_p[seq - keep_swa:])[None, None]
    p_swa_v = kv4(p0_p[seq - keep_swa:, off0["v"]:off0["v"] + d_kv])[None, None]
    p_dil_k = kv4(krot1_p[seq - keep_dil:])[None, None]
    p_dil_v = kv4(p1_p[seq - keep_dil:, v_off:v_off + d_kv])[None, None]
    xbc_s = jnp.concatenate([p0_s[:, off0["xs"]:off0["xs"] + d], p0_s[:, off0["bc"]:off0["bc"] + d_bc]], axis=1)
    s_conv = xbc_s.reshape(n_b, t_len, conv_dim)[:, t_len - (SSD_CONV - 1):][None]
    bt = lambda a: a.reshape(n_b, t_len, N_KV, HEAD_DIM)[None]
    s_swa_k = bt(krot0_s)
    s_swa_v = bt(p0_s[:, off0["v"]:off0["v"] + d_kv])
    s_dil_k = bt(krot1_s)
    s_dil_v = bt(p1_s[:, v_off:v_off + d_kv])
    return (y_prompt, y_sample, p_conv, p_ssm[None, None], p_swa_k, p_swa_v, p_dil_k, p_dil_v,
            s_conv, s_ssm[None], s_swa_k, s_swa_v, s_dil_k, s_dil_v)
```

```python
import functools
import math

import jax
import jax.numpy as jnp
from jax import lax
from jax.experimental import pallas as pl
from jax.experimental.pallas import tpu as pltpu

F32 = jnp.float32
BF16 = jnp.bfloat16

NORM_EPS = 1e-5
HEAD_DIM = 64
ROT_HALF = 8
ROPE_THETA = 500000.0
PAST_LEN = 8192
N_HEADS = 32
N_KV = 8
REP = N_HEADS // N_KV
SSD_GROUPS = 4
SSD_STATE = 128
SSD_CONV = 4
SSD_CHUNK = 128
SWA_WINDOW = 128
DIL_PATTERNS = ((128, 1), (512, 4), (2048, 16))
BAND = 128
ATT_SCALE = HEAD_DIM ** -0.5
LOG2E = math.log2(math.e)
LN2 = math.log(2.0)
NEG = -1e30
_LOG2_HEAD_DIM = 6

LANES = 128
BF16_ROWS = 16
PERM_CHUNK = 256
VMEM_LIMIT_BYTES = 60 * 1024 * 1024

_NT = (((1,), (1,)), ((), ()))
_TN = (((0,), (0,)), ((), ()))


def _cparams(n_axes):
    return pltpu.CompilerParams(dimension_semantics=("arbitrary",) * n_axes,
                                vmem_limit_bytes=VMEM_LIMIT_BYTES)


def _silu(x):
    return x * jax.nn.sigmoid(x)


def _tile_lanes(t, width):
    k = width // t.shape[1]
    return t if k == 1 else jnp.concatenate([t] * k, axis=1)


def _rope(x, cos_t, sa_t, sb_t):
    w = x.shape[1]
    return (x * _tile_lanes(cos_t, w)
            + pltpu.roll(x, w - ROT_HALF, 1) * _tile_lanes(sa_t, w)
            + pltpu.roll(x, ROT_HALF, 1) * _tile_lanes(sb_t, w))


def _low_half(shape):
    return (lax.broadcasted_iota(jnp.int32, shape, 1) & (LANES - 1)) < HEAD_DIM


def _rope_table_kernel(cos_ref, sa_ref, sb_ref, *, tm, seq, dec_seq):
    i = pl.program_id(0)
    row = i * tm + lax.broadcasted_iota(jnp.int32, (tm, LANES), 0)
    lane = lax.broadcasted_iota(jnp.int32, (tm, LANES), 1)
    pos = jnp.where(row < seq, row, PAST_LEN + ((row - seq) & (dec_seq - 1)))
    c = lane & (HEAD_DIM - 1)
    f = (c & (ROT_HALF - 1)).astype(F32)
    inv_freq = jnp.exp(f * (-math.log(ROPE_THETA) / ROT_HALF))
    ang = pos.astype(F32) * inv_freq
    cs = jnp.cos(ang)
    sn = jnp.sin(ang)
    cos_ref[...] = jnp.where(c < 2 * ROT_HALF, cs, 1.0)
    sa_ref[...] = jnp.where(c < ROT_HALF, -sn, 0.0)
    sb_ref[...] = jnp.where((c >= ROT_HALF) & (c < 2 * ROT_HALF), sn, 0.0)


def _rope_tables(m, seq, dec_seq, tm):
    assert dec_seq & (dec_seq - 1) == 0 and m % tm == 0
    shp = jax.ShapeDtypeStruct((m, LANES), F32)
    spec = pl.BlockSpec((tm, LANES), lambda i: (i, 0))
    return pl.pallas_call(
        functools.partial(_rope_table_kernel, tm=tm, seq=seq, dec_seq=dec_seq),
        out_shape=(shp, shp, shp), grid=(m // tm,), out_specs=(spec, spec, spec),
        compiler_params=_cparams(1), name="rope_tables")()


def _rmsnorm_bf16(x, g):
    ms = jnp.mean(x * x, axis=-1, keepdims=True)
    return (x * lax.rsqrt(ms + NORM_EPS) * g).astype(BF16)


def _norm_mm_kernel(x_ref, g_ref, w_ref, o_ref, *rest, emit):
    xn_ref = rest[-1]

    @pl.when(pl.program_id(1) == 0)
    def _():
        xn_ref[...] = _rmsnorm_bf16(x_ref[...], g_ref[...])

    w = w_ref[...]
    if emit:
        w = w.astype(BF16)
        rest[0][...] = w
    o_ref[...] = jnp.dot(xn_ref[...], w, preferred_element_type=F32)


def _norm_matmul(x, g, w, *, tm, tn, name, emit=False):
    m, k = x.shape
    n = w.shape[1]
    assert m % tm == 0 and n % tn == 0 and w.shape[0] == k and (not emit or m == tm)
    w_spec = pl.BlockSpec((k, tn), lambda i, j: (0, j))
    o_spec = pl.BlockSpec((tm, tn), lambda i, j: (i, j))
    o_shape = jax.ShapeDtypeStruct((m, n), F32)
    return pl.pallas_call(
        functools.partial(_norm_mm_kernel, emit=emit),
        out_shape=(o_shape, jax.ShapeDtypeStruct((k, n), BF16)) if emit else o_shape,
        grid=(m // tm, n // tn),
        in_specs=[pl.BlockSpec((tm, k), lambda i, j: (i, 0)),
                  pl.BlockSpec((1, k), lambda i, j: (0, 0)),
                  w_spec],
        out_specs=(o_spec, w_spec) if emit else o_spec,
        scratch_shapes=[pltpu.VMEM((tm, k), BF16)],
        compiler_params=_cparams(2), name=name)(x, g.reshape(1, k), w)


def _norm_mm_multi_kernel(*refs, owners):
    n_w = len(owners)
    x_ref, g_ref = refs[:2]
    w_refs = refs[2:2 + n_w]
    ws_ref, o_ref, os_ref, xn_ref = refs[2 + n_w:]
    j = pl.program_id(1)

    @pl.when(j == 0)
    def _():
        xn_ref[...] = _rmsnorm_bf16(x_ref[...], g_ref[...])
        os_ref[...] = jnp.dot(xn_ref[...], ws_ref[...], preferred_element_type=F32)

    for w_ref, tiles in zip(w_refs, owners):
        own = functools.reduce(jnp.logical_or, [(j >= lo) & (j < hi) for lo, hi in tiles])

        @pl.when(own)
        def _():
            o_ref[...] = jnp.dot(xn_ref[...], w_ref[...], preferred_element_type=F32)


def _norm_matmul_multi(x, g, segs, w_small, *, tm, tn, name):
    m, k = x.shape
    arrays, owners, maps = [], [], []
    start = 0
    for w, cb0, n in segs:
        if not any(w is a for a in arrays):
            arrays.append(w)
            owners.append([])
            maps.append([])
        idx = [i for i, a in enumerate(arrays) if a is w][0]
        owners[idx].append((start, start + n))
        maps[idx].append((start, n, cb0))
        start += n
    n_tiles = start

    def index_map(pieces):
        def col(i, j):
            c = pieces[-1][2] + pieces[-1][1] - 1
            for a, n, cb0 in reversed(pieces):
                c = jnp.where(j < a + n, cb0 + jnp.maximum(j - a, 0), c)
            return 0, c
        return col

    n_small = w_small.shape[1]
    return pl.pallas_call(
        functools.partial(_norm_mm_multi_kernel, owners=tuple(tuple(o) for o in owners)),
        out_shape=(jax.ShapeDtypeStruct((m, n_tiles * tn), F32), jax.ShapeDtypeStruct((m, n_small), F32)),
        grid=(m // tm, n_tiles),
        in_specs=[pl.BlockSpec((tm, k), lambda i, j: (i, 0)),
                  pl.BlockSpec((1, k), lambda i, j: (0, 0))]
        + [pl.BlockSpec((k, tn), index_map(p)) for p in maps]
        + [pl.BlockSpec((k, n_small), lambda i, j: (0, 0))],
        out_specs=(pl.BlockSpec((tm, tn), lambda i, j: (i, j)), pl.BlockSpec((tm, n_small), lambda i, j: (i, 0))),
        scratch_shapes=[pltpu.VMEM((tm, k), BF16)],
        compiler_params=_cparams(2), name=name)(x, g.reshape(1, k), *arrays, w_small)


def _mm_resid_kernel(*refs, n_in, emit):
    x_refs, w_refs = refs[:n_in], refs[n_in:2 * n_in]
    r_ref, o_ref = refs[2 * n_in], refs[2 * n_in + 1]
    wo_refs = refs[2 * n_in + 2:]
    acc = r_ref[...]
    for s, (x_ref, w_ref) in enumerate(zip(x_refs, w_refs)):
        w = w_ref[...]
        if emit:
            w = w.astype(BF16)
            wo_refs[s][...] = w
        acc = acc + jnp.dot(x_ref[...], w, preferred_element_type=F32)
    o_ref[...] = acc


def _matmul_resid(xs, w, resid, *, tm, tn, name, emit=False):
    m, n = resid.shape
    k = xs[0].shape[1]
    assert m % tm == 0 and n % tn == 0 and (not emit or (m == tm and w.shape == (k * len(xs), n)))
    w_args = [w] * len(xs) if emit else list(w)
    in_specs = [pl.BlockSpec((tm, k), lambda i, j: (i, 0)) for _ in xs]
    in_specs += [pl.BlockSpec((k, tn), functools.partial(lambda i, j, s: (s, j), s=s if emit else 0))
                 for s in range(len(xs))]
    in_specs.append(pl.BlockSpec((tm, tn), lambda i, j: (i, j)))
    o_spec = pl.BlockSpec((tm, tn), lambda i, j: (i, j))
    o_shape = jax.ShapeDtypeStruct((m, n), F32)
    if emit:
        out_shape = (o_shape,) + (jax.ShapeDtypeStruct((k, n), BF16),) * len(xs)
        out_specs = (o_spec,) + (pl.BlockSpec((k, tn), lambda i, j: (0, j)),) * len(xs)
    else:
        out_shape, out_specs = o_shape, o_spec
    return pl.pallas_call(
        functools.partial(_mm_resid_kernel, n_in=len(xs), emit=emit),
        out_shape=out_shape,
        grid=(m // tm, n // tn),
        in_specs=in_specs,
        out_specs=out_specs,
        compiler_params=_cparams(2), name=name)(*xs, *w_args, resid)


def _mlp_kernel(x_ref, g_ref, w1_ref, w2_ref, gout_ref, o_ref, *rest, norm_out, emit):
    xn_ref = rest[-1]
    k = pl.program_id(1)

    @pl.when(k == 0)
    def _():
        x = x_ref[...]
        xn_ref[...] = _rmsnorm_bf16(x, g_ref[...])
        o_ref[...] = x

    w1, w2 = w1_ref[...], w2_ref[...]
    if emit:
        w1, w2 = w1.astype(BF16), w2.astype(BF16)
        rest[0][...] = w1
        rest[1][...] = w2
    u = jnp.dot(xn_ref[...], w1, preferred_element_type=F32)
    u = jnp.square(jnp.maximum(u, 0.0)).astype(BF16)
    o_ref[...] += jnp.dot(u, w2, preferred_element_type=F32)

    if norm_out:
        @pl.when(k == pl.num_programs(1) - 1)
        def _():
            o = o_ref[...]
            ms = jnp.mean(o * o, axis=-1, keepdims=True)
            o_ref[...] = o * lax.rsqrt(ms + NORM_EPS) * gout_ref[...]


def _mlp(x, g, w1, w2, g_out, *, tm, tk, name, layer=None):
    m, d = x.shape
    emit = layer is not None
    hid = w1.shape[-1]
    assert m % tm == 0 and hid % tk == 0 and (not emit or m == tm)
    w1o_spec = pl.BlockSpec((d, tk), lambda i, k: (0, k))
    w2o_spec = pl.BlockSpec((tk, d), lambda i, k: (k, 0))
    if emit:
        w_specs = [pl.BlockSpec((None, d, tk), lambda i, k: (layer, 0, k)),
                   pl.BlockSpec((None, tk, d), lambda i, k: (layer, k, 0))]
        out_shape = (jax.ShapeDtypeStruct((m, d), F32), jax.ShapeDtypeStruct((d, hid), BF16),
                     jax.ShapeDtypeStruct((hid, d), BF16))
        out_specs = (pl.BlockSpec((tm, d), lambda i, k: (i, 0)), w1o_spec, w2o_spec)
    else:
        w_specs = [w1o_spec, w2o_spec]
        out_shape = jax.ShapeDtypeStruct((m, d), F32)
        out_specs = pl.BlockSpec((tm, d), lambda i, k: (i, 0))
    return pl.pallas_call(
        functools.partial(_mlp_kernel, norm_out=g_out is not None, emit=emit),
        out_shape=out_shape,
        grid=(m // tm, hid // tk),
        in_specs=[pl.BlockSpec((tm, d), lambda i, k: (i, 0)),
                  pl.BlockSpec((1, d), lambda i, k: (0, 0))] + w_specs + [pl.BlockSpec((1, d), lambda i, k: (0, 0))],
        out_specs=out_specs,
        scratch_shapes=[pltpu.VMEM((tm, d), BF16)],
        compiler_params=_cparams(2), name=name)(x, g.reshape(1, d), w1, w2,
                                                (g if g_out is None else g_out).reshape(1, d))


def _causal_conv(raw, tail, w_ref, b_ref):
    n = raw.shape[0]
    full = jnp.concatenate([tail, raw], axis=0)
    acc = b_ref[...]
    for i in range(SSD_CONV):
        s = SSD_CONV - 1 - i
        acc = acc + full[8 - s:8 - s + n] * w_ref[i:i + 1, :]
    return _silu(acc)


def _gate_groupnorm(y, z, gn):
    y = y * _silu(z)
    gw = y.shape[1] // SSD_GROUPS
    outs = []
    for g in range(SSD_GROUPS):
        yg = y[:, g * gw:(g + 1) * gw]
        ms = jnp.mean(yg * yg, axis=-1, keepdims=True)
        outs.append(yg * lax.rsqrt(ms + NORM_EPS))
    return jnp.concatenate(outs, axis=1) * gn


def _ssd_prompt_kernel(z_ref, xs_ref, bc_ref, dt_ref, wx_ref, wbc_ref, bx_ref, bbc_ref, dtb_ref, alog_ref,
                       dskip_ref, gn_ref, eexp_ref, y_ref, tailx_ref, tailbc_ref, h_ref, y_s):
    c = pl.program_id(0)
    q = SSD_CHUNK
    gs = SSD_STATE
    hp = HEAD_DIM

    @pl.when(c == 0)
    def _():
        tailx_ref[...] = jnp.zeros_like(tailx_ref)
        tailbc_ref[...] = jnp.zeros_like(tailbc_ref)
        h_ref[...] = jnp.zeros_like(h_ref)

    xs_raw = xs_ref[...]
    bc_raw = bc_ref[...]
    xs_c = _causal_conv(xs_raw, tailx_ref[...], wx_ref, bx_ref)
    bc_c = _causal_conv(bc_raw, tailbc_ref[...], wbc_ref, bbc_ref)
    tailx_ref[...] = xs_raw[q - 8:q]
    tailbc_ref[...] = bc_raw[q - 8:q]

    lane = lax.broadcasted_iota(jnp.int32, (q, LANES), 1)
    row = lax.broadcasted_iota(jnp.int32, (q, LANES), 0)
    dt = jax.nn.softplus(dt_ref[...] + dtb_ref[...])
    a = -jnp.exp(alog_ref[...])
    la = jnp.where(lane < N_HEADS, dt * a, 0.0)
    acs = la
    s = 1
    while s < q:
        acs = acs + jnp.where(row >= s, pltpu.roll(acs, s, 0), 0.0)
        s *= 2
    acs_t = acs.T
    acs_last = acs[q - 1:q, :]
    to_end = jnp.exp(acs_last - acs)
    eacs = jnp.exp(acs)
    cdec = jnp.exp(acs_last)
    causal = (lax.broadcasted_iota(jnp.int32, (q, q), 0) >= lax.broadcasted_iota(jnp.int32, (q, q), 1))

    stacked = jnp.concatenate([dt, eacs, to_end], axis=0)
    hi = stacked.astype(BF16)
    lo = (stacked - hi.astype(F32)).astype(BF16)
    e = eexp_ref[...]
    full = jnp.dot(hi, e, preferred_element_type=F32) + jnp.dot(lo, e, preferred_element_type=F32)
    xdt = xs_c * full[0:q]
    eacs_f = full[q:2 * q]
    w_f = xdt * full[2 * q:3 * q]
    low = _low_half((q, LANES))
    top = lax.broadcasted_iota(jnp.int32, (2 * hp, gs), 0) < hp

    rep = N_HEADS // SSD_GROUPS
    for g in range(SSD_GROUPS):
        bg = bc_c[:, g * gs:(g + 1) * gs].astype(BF16)
        cg = bc_c[:, SSD_GROUPS * gs + g * gs:SSD_GROUPS * gs + (g + 1) * gs].astype(BF16)
        cb = lax.dot_general(cg, bg, _NT, preferred_element_type=F32)
        for rp in range(rep // 2):
            h0 = g * rep + 2 * rp
            sl = slice(h0 * hp, (h0 + 2) * hp)
            ms = []
            for h in (h0, h0 + 1):
                lmat = jnp.exp(jnp.where(causal, acs[:, h:h + 1] - acs_t[h:h + 1, :], -jnp.inf))
                ms.append((cb * lmat).astype(BF16))
            xp = xdt[:, sl]
            rhs = jnp.concatenate([jnp.where(low, xp, 0.0), jnp.where(low, 0.0, xp)], axis=0).astype(BF16)
            y = jnp.dot(jnp.concatenate(ms, axis=1), rhs, preferred_element_type=F32)
            hs = h_ref[h0:h0 + 2].reshape(2 * hp, gs)
            y = y + eacs_f[:, sl] * lax.dot_general(cg, hs.astype(BF16), _NT, preferred_element_type=F32)
            st = lax.dot_general(w_f[:, sl].astype(BF16), bg, _TN, preferred_element_type=F32)
            cd = jnp.where(top, cdec[:, h0:h0 + 1], cdec[:, h0 + 1:h0 + 2])
            h_ref[h0:h0 + 2] = (hs * cd + st).reshape(2, hp, gs)
            y_s[:, sl] = y + dskip_ref[:, sl] * xs_c[:, sl]

    y_ref[...] = _gate_groupnorm(y_s[...], z_ref[...], gn_ref[...]).astype(y_ref.dtype)


def _ssd_prompt(p0, dtr, off, seq, wx, wbc, bx, bbc, dtb, alog, dskip, gn, eexp):
    q = SSD_CHUNK
    d_in = wx.shape[1]
    d_bc = wbc.shape[1]
    assert seq % q == 0
    const = lambda shape: pl.BlockSpec(shape, lambda c: (0,) * len(shape))
    return pl.pallas_call(
        _ssd_prompt_kernel,
        out_shape=(jax.ShapeDtypeStruct((seq, d_in), BF16),
                   jax.ShapeDtypeStruct((8, d_in), F32),
                   jax.ShapeDtypeStruct((8, d_bc), F32),
                   jax.ShapeDtypeStruct((N_HEADS, HEAD_DIM, SSD_STATE), F32)),
        grid=(seq // q,),
        in_specs=[pl.BlockSpec((q, d_in), lambda c: (c, off["z"] // d_in)),
                  pl.BlockSpec((q, d_in), lambda c: (c, off["xs"] // d_in)),
                  pl.BlockSpec((q, d_bc), lambda c: (c, off["bc"] // d_bc)),
                  pl.BlockSpec((q, LANES), lambda c: (c, 0)),
                  const((SSD_CONV, d_in)), const((SSD_CONV, d_bc)), const((1, d_in)), const((1, d_bc)),
                  const((1, LANES)), const((1, LANES)), const((1, d_in)), const((1, d_in)), const((LANES, d_in))],
        out_specs=(pl.BlockSpec((q, d_in), lambda c: (c, 0)),
                   const((8, d_in)), const((8, d_bc)), const((N_HEADS, HEAD_DIM, SSD_STATE))),
        scratch_shapes=[pltpu.VMEM((q, d_in), F32)],
        compiler_params=_cparams(1), name="ssd_prompt")(p0, p0, p0, dtr, wx, wbc, bx, bbc, dtb, alog, dskip, gn, eexp)


def _split3(x):
    hi = x.astype(BF16)
    r1 = x - hi.astype(F32)
    mid = r1.astype(BF16)
    lo = (r1 - mid.astype(F32)).astype(BF16)
    return hi, mid, lo


def _ssd_sample_kernel(z_ref, xs_ref, bc_ref, dt_ref, ex_ref, ebc_ref, st_ref, wx_ref, wbc_ref, bx_ref, bbc_ref,
                       dtb_ref, alog_ref, dskip_ref, gn_ref, eexp_ref, y_ref, sto_ref, *, nb, t_len):
    rows = nb * t_len
    gs = SSD_STATE
    gw = xs_ref.shape[1] // SSD_GROUPS
    rep = N_HEADS // SSD_GROUPS

    def tcol(width):
        r = lax.broadcasted_iota(jnp.int32, (rows, width), 0)
        return r & (t_len - 1), r >> int(math.log2(t_len))

    def conv(raw, est, w_ref, b_ref):
        t, _ = tcol(raw.shape[1])
        acc = b_ref[...]
        for i in range(SSD_CONV):
            s = SSD_CONV - 1 - i
            if s == 0:
                sh = raw
            else:
                k = (rows - (SSD_CONV - 1) + s) % rows
                sh = jnp.where(t >= s, pltpu.roll(raw, s, 0), est if k == 0 else pltpu.roll(est, k, 0))
            acc = acc + sh * w_ref[i:i + 1, :]
        return _silu(acc)

    xs_c = conv(xs_ref[...], ex_ref[...], wx_ref, bx_ref)
    bc_c = conv(bc_ref[...], ebc_ref[...], wbc_ref, bbc_ref)
    bm = bc_c[:, :SSD_GROUPS * gs]
    cm = bc_c[:, SSD_GROUPS * gs:]

    t1, _ = tcol(LANES)
    lane = lax.broadcasted_iota(jnp.int32, (rows, LANES), 1)
    dt = jax.nn.softplus(dt_ref[...] + dtb_ref[...])
    a = -jnp.exp(alog_ref[...])
    la = jnp.where(lane < N_HEADS, dt * a, 0.0)
    acs = la
    for s in range(1, t_len):
        acs = acs + jnp.where(t1 >= s, pltpu.roll(la, s, 0), 0.0)
    alast = jnp.where(t1 == t_len - 1, acs, 0.0)
    for u in range(1, t_len):
        alast = alast + jnp.where(t1 == t_len - 1 - u, pltpu.roll(acs, rows - u, 0), 0.0)
    parts = [dt, jnp.exp(acs), jnp.exp(alast - acs), jnp.exp(alast)]
    for k in range(1, t_len):
        parts.append(jnp.exp(acs - pltpu.roll(acs, k, 0)))
    stacked = jnp.concatenate(parts, axis=0)
    hi, mid, lo = _split3(stacked)
    e = eexp_ref[...]
    full = (jnp.dot(hi, e, preferred_element_type=F32) + jnp.dot(mid, e, preferred_element_type=F32)
            + jnp.dot(lo, e, preferred_element_type=F32))
    dt_f, eacs_f, toend_f, cdec_f = (full[i * rows:(i + 1) * rows] for i in range(4))
    dec_f = [None] + [full[(3 + k) * rows:(4 + k) * rows] for k in range(1, t_len)]

    tw, _ = tcol(xs_c.shape[1])
    xdt = xs_c * dt_f
    y = jnp.zeros_like(xs_c)
    for k in range(t_len):
        bmk = bm if k == 0 else pltpu.roll(bm, k, 0)
        prod = cm * bmk
        cbs = []
        for g in range(SSD_GROUPS):
            sg = jnp.sum(prod[:, g * gs:(g + 1) * gs], axis=-1, keepdims=True)
            cbs.append(jnp.broadcast_to(sg, (rows, gw)))
        cb_f = jnp.concatenate(cbs, axis=1)
        if k == 0:
            y = y + cb_f * xdt
        else:
            y = y + jnp.where(tw >= k, cb_f * dec_f[k] * pltpu.roll(xdt, k, 0), 0.0)

    cm16 = cm.astype(BF16)
    bm16 = bm.astype(BF16)
    w_f = (toend_f * xdt)
    _, bg_ = tcol(gw)
    ones_blk = jnp.ones((3, gs), BF16)
    yoffs = []
    for g in range(SSD_GROUPS):
        cg = cm16[:, g * gs:(g + 1) * gs]
        rhs_top = jnp.concatenate([bm16[:, g * gs:(g + 1) * gs], jnp.zeros((rows, gs), BF16)], axis=1)
        rhs_mid = jnp.concatenate([jnp.zeros((3, gs), BF16), ones_blk], axis=1)
        rhs = jnp.concatenate([rhs_top, rhs_mid, jnp.zeros((13, 2 * gs), BF16)], axis=0)
        wg = w_f[:, g * gw:(g + 1) * gw]
        cdg = cdec_f[:, g * gw:(g + 1) * gw]
        yg = jnp.zeros((rows, gw), F32)
        for b in range(nb):
            hb = st_ref[b, g * rep:(g + 1) * rep].reshape(gw, gs)
            yb = lax.dot_general(cg, hb.astype(BF16), _NT, preferred_element_type=F32)
            yg = jnp.where(bg_ == b, yb, yg)
            wb = jnp.where(bg_ == b, wg, 0.0).astype(BF16)
            d_hi, d_mid, d_lo = _split3(cdg[b * t_len:b * t_len + 1])
            lhs = jnp.concatenate([wb, d_hi, d_mid, d_lo, jnp.zeros((13, gw), BF16)], axis=0)
            sd = lax.dot_general(lhs, rhs, _TN, preferred_element_type=F32)
            sto_ref[b, g * rep:(g + 1) * rep] = (hb * sd[:, gs:] + sd[:, :gs]).reshape(rep, HEAD_DIM, gs)
        yoffs.append(yg)
    y = y + eacs_f * jnp.concatenate(yoffs, axis=1) + dskip_ref[...] * xs_c
    y_ref[...] = _gate_groupnorm(y, z_ref[...], gn_ref[...]).astype(y_ref.dtype)


def _ssd_sample(p0, dtr, off, seq, n_b, t_len, epad, state, wx, wbc, bx, bbc, dtb, alog, dskip, gn, eexp, *, nb):
    rows = nb * t_len
    d_in = wx.shape[1]
    d_bc = wbc.shape[1]
    assert n_b % nb == 0 and t_len & (t_len - 1) == 0
    const = lambda shape: pl.BlockSpec(shape, lambda i: (0,) * len(shape))
    st_spec = pl.BlockSpec((nb, N_HEADS, HEAD_DIM, SSD_STATE), lambda i: (i, 0, 0, 0))
    return pl.pallas_call(
        functools.partial(_ssd_sample_kernel, nb=nb, t_len=t_len),
        out_shape=(jax.ShapeDtypeStruct((n_b * t_len, d_in), BF16),
                   jax.ShapeDtypeStruct(state.shape, F32)),
        grid=(n_b // nb,),
        in_specs=[pl.BlockSpec((rows, d_in), lambda i: (i, off["z"] // d_in)),
                  pl.BlockSpec((rows, d_in), lambda i: (i, off["xs"] // d_in)),
                  pl.BlockSpec((rows, d_bc), lambda i: (i, off["bc"] // d_bc)),
                  pl.BlockSpec((rows, LANES), lambda i: (i, 0)),
                  pl.BlockSpec((rows, d_in), lambda i: (i, 0)),
                  pl.BlockSpec((rows, d_bc), lambda i: (i, d_in // d_bc)),
                  st_spec,
                  const((SSD_CONV, d_in)), const((SSD_CONV, d_bc)), const((1, d_in)), const((1, d_bc)),
                  const((1, LANES)), const((1, LANES)), const((1, d_in)), const((1, d_in)),
                  const((LANES, d_in))],
        out_specs=(pl.BlockSpec((rows, d_in), lambda i: (i, 0)), st_spec),
        compiler_params=_cparams(1), name="ssd_sample")(
            p0, p0, p0, dtr, epad, epad, state, wx, wbc, bx, bbc, dtb, alog, dskip, gn, eexp)


def _perm_matrix(dil):
    i = jnp.arange(PERM_CHUNK)
    blk = BF16_ROWS * dil
    src = (i // blk) * blk + (i % BF16_ROWS) * dil + (i % blk) // BF16_ROWS
    return (src[:, None] == jnp.arange(PERM_CHUNK)[None, :]).astype(BF16)


def _band_kernel(*refs, dil, max_dist, use_sink, want_lse, want_krot, nkv, nj):
    it = iter(refs)
    q_ref, k_ref, v_ref, cos_ref, sa_ref, sb_ref = (next(it) for _ in range(6))
    p_ref, pt_ref = (next(it), next(it)) if dil > 1 else (None, None)
    sink_ref = next(it) if use_sink else None
    o_ref = next(it)
    lse_ref = next(it) if want_lse else None
    krot_ref = next(it) if want_krot else None
    qlo_s, qhi_s, kd_s, vd_s, kdp_s, vdp_s, op_s = (next(it) for _ in range(7))
    lsep_s = next(it) if want_lse else None
    lse_s = next(it) if (want_lse and nj > 1) else None

    c = pl.program_id(0)
    j = pl.program_id(1)
    rows, wq = q_ref.shape
    wk = k_ref.shape[1]
    n_chunk = rows // PERM_CHUNK if dil > 1 else 0
    n_grp = BAND // BF16_ROWS

    def perm(x):
        if dil == 1:
            return x
        x16 = x.astype(BF16)
        return jnp.concatenate(
            [jnp.dot(p_ref[...], x16[k * PERM_CHUNK:(k + 1) * PERM_CHUNK], preferred_element_type=F32)
             for k in range(n_chunk)], axis=0)

    def unperm(xp16):
        if dil == 1:
            return [xp16]
        return [jnp.dot(pt_ref[...], xp16[k * PERM_CHUNK:(k + 1) * PERM_CHUNK], preferred_element_type=F32)
                for k in range(n_chunk)]

    tabs = (cos_ref[...], sa_ref[...], sb_ref[...])
    q = perm(_rope(q_ref[...], *tabs) * (ATT_SCALE * LOG2E))
    lowq = _low_half((rows, wq))
    qlo_s[...] = jnp.where(lowq, q, 0.0).astype(BF16)
    qhi_s[...] = jnp.where(lowq, 0.0, q).astype(BF16)
    kr = _rope(k_ref[...], *tabs)
    if want_krot:
        krot_ref[...] = kr
    lowk = _low_half((rows, LANES))
    for x, d_s in ((perm(kr), kd_s), (perm(v_ref[...]), vd_s)):
        up = pltpu.roll(x, HEAD_DIM, 1)
        dn = pltpu.roll(x, wk - HEAD_DIM, 1)
        for t in range(wk // LANES):
            sl = slice(t * LANES, (t + 1) * LANES)
            d_s[2 * t] = jnp.where(lowk, x[:, sl], up[:, sl]).astype(BF16)
            d_s[2 * t + 1] = jnp.where(lowk, dn[:, sl], x[:, sl]).astype(BF16)

    @pl.when(c == 0)
    def _():
        kdp_s[j] = jnp.zeros(kdp_s.shape[1:], BF16)
        vdp_s[j] = jnp.zeros(vdp_s.shape[1:], BF16)

    qi = lax.broadcasted_iota(jnp.int32, (BAND, 2 * BAND), 0)
    kj = lax.broadcasted_iota(jnp.int32, (BAND, 2 * BAND), 1)
    dist = qi - kj + BAND
    bias = jnp.where((dist >= 0) & (dist <= max_dist) & ((kj >= BAND) | (c > 0)), 0.0, NEG)
    col0 = kj == 0
    ones_blk = jnp.ones((2 * BAND, LANES), BF16)
    low = _low_half((BAND, LANES))
    lane = lax.broadcasted_iota(jnp.int32, (BAND, LANES), 1)
    step = BF16_ROWS * dil

    def residue(rho):
        starts = [b * step + rho * BF16_ROWS for b in range(n_grp)]

        def gather(get):
            return jnp.concatenate([get(s) for s in starts], axis=0)

        tiles = []
        lse_acc = jnp.zeros((BAND, LANES), F32)
        for g in range(nkv):
            ca, cb = slice(2 * g * LANES, (2 * g + 1) * LANES), slice((2 * g + 1) * LANES, (2 * g + 2) * LANES)
            lhs = jnp.concatenate([gather(lambda s: qlo_s[pl.ds(s, BF16_ROWS), ca]),
                                   gather(lambda s: qhi_s[pl.ds(s, BF16_ROWS), ca]),
                                   gather(lambda s: qlo_s[pl.ds(s, BF16_ROWS), cb]),
                                   gather(lambda s: qhi_s[pl.ds(s, BF16_ROWS), cb])], axis=0)
            kcat = jnp.concatenate([gather(lambda s: kdp_s[j, g, pl.ds(s, BF16_ROWS), :]),
                                    gather(lambda s: kd_s[g, pl.ds(s, BF16_ROWS), :])], axis=0)
            vcat = jnp.concatenate([gather(lambda s: vdp_s[j, g, pl.ds(s, BF16_ROWS), :]),
                                    gather(lambda s: vd_s[g, pl.ds(s, BF16_ROWS), :])], axis=0)
            head0 = j * REP * nkv + g * REP
            s_full = lax.dot_general(lhs, kcat, _NT, preferred_element_type=F32)
            rsl = [slice(r * BAND, (r + 1) * BAND) for r in range(REP)]
            if use_sink:
                sc = jnp.concatenate([jnp.where(col0, sink_ref[head0 + r] * LOG2E, s_full[rsl[r]] + bias)
                                      for r in range(REP)], axis=0)
            else:
                sc = s_full + jnp.concatenate([bias] * REP, axis=0)
            m = jnp.max(sc, axis=-1, keepdims=True)
            e = jnp.exp2(sc - m)
            od = jnp.dot(e.astype(BF16), jnp.concatenate([vcat, ones_blk], axis=1), preferred_element_type=F32)
            den = od[:, LANES:]
            o = od[:, :LANES] * (1.0 / den)
            tiles.append(jnp.where(low, o[0:BAND], o[BAND:2 * BAND]))
            tiles.append(jnp.where(low, o[2 * BAND:3 * BAND], o[3 * BAND:4 * BAND]))
            if want_lse:
                lse = (m + jnp.log2(den)) * LN2
                for r in range(REP):
                    lse_acc = lse_acc + jnp.where(lane == head0 + r, lse[rsl[r]], 0.0)
        o_res = jnp.concatenate(tiles, axis=1)
        for b, s0 in enumerate(starts):
            op_s[pl.ds(s0, BF16_ROWS), :] = o_res[b * BF16_ROWS:(b + 1) * BF16_ROWS]
            if want_lse:
                lsep_s[pl.ds(s0, BF16_ROWS), :] = lse_acc[b * BF16_ROWS:(b + 1) * BF16_ROWS]

    for rho in range(dil):
        residue(rho)

    for k, blk in enumerate(unperm(op_s[...].astype(BF16))):
        if dil == 1:
            o_ref[...] = blk
        else:
            o_ref[k * PERM_CHUNK:(k + 1) * PERM_CHUNK, :] = blk.astype(o_ref.dtype)
    if want_lse:
        if dil == 1:
            lse_nat = lsep_s[...]
        else:
            parts = [unperm(x) for x in _split3(lsep_s[...])]
            lse_nat = jnp.concatenate([a + b + d for a, b, d in zip(*parts)], axis=0)
        if nj == 1:
            lse_ref[...] = lse_nat
        else:
            @pl.when(j == 0)
            def _():
                lse_s[...] = lse_nat

            @pl.when(j > 0)
            def _():
                lse_s[...] = lse_s[...] + lse_nat

            @pl.when(j == nj - 1)
            def _():
                lse_ref[...] = lse_s[...]
    kdp_s[j] = kd_s[...]
    if use_sink:
        keep = (lax.broadcasted_iota(jnp.int32, (rows, LANES), 0) > 0).astype(F32).astype(BF16)
        vdp_s[j] = vd_s[...] * keep
    else:
        vdp_s[j] = vd_s[...]


def _band_attention(p, seq, q_off, k_off, v_off, tables, sinks, *, dil, max_dist, nkv, want_lse, want_krot, name):
    rows = BAND * dil
    nj = N_KV // nkv
    qw, kw = nkv * REP * HEAD_DIM, nkv * HEAD_DIM
    assert seq % rows == 0 and kw % LANES == 0 and q_off % qw == 0 and k_off % kw == 0 and v_off % kw == 0
    assert dil == 1 or (rows % PERM_CHUNK == 0 and PERM_CHUNK % (BF16_ROWS * dil) == 0)
    use_sink = sinks is not None
    d_att = N_HEADS * HEAD_DIM
    assert not use_sink or (dil == 1 and max_dist < BAND)
    blk = lambda width, cb: pl.BlockSpec((rows, width), lambda c, j: (c, cb + j))
    t_spec = pl.BlockSpec((rows, LANES), lambda c, j: (c, 0))
    in_specs = [blk(qw, q_off // qw), blk(kw, k_off // kw), blk(kw, v_off // kw), t_spec, t_spec, t_spec]
    args = [p, p, p] + list(tables)
    if dil > 1:
        pm = _perm_matrix(dil)
        in_specs += [pl.BlockSpec((PERM_CHUNK, PERM_CHUNK), lambda c, j: (0, 0))] * 2
        args += [pm, pm.T]
    if use_sink:
        in_specs.append(pl.BlockSpec(memory_space=pltpu.SMEM))
        args.append(sinks)
    out_shape = [jax.ShapeDtypeStruct((seq, d_att), BF16)]
    out_specs = [pl.BlockSpec((rows, qw), lambda c, j: (c, j))]
    scratch = [pltpu.VMEM((rows, qw), BF16), pltpu.VMEM((rows, qw), BF16),
               pltpu.VMEM((nkv, rows, LANES), BF16), pltpu.VMEM((nkv, rows, LANES), BF16),
               pltpu.VMEM((nj, nkv, rows, LANES), BF16), pltpu.VMEM((nj, nkv, rows, LANES), BF16),
               pltpu.VMEM((rows, qw), F32)]
    if want_lse:
        out_shape.append(jax.ShapeDtypeStruct((seq, LANES), F32))
        out_specs.append(pl.BlockSpec((rows, LANES), lambda c, j: (c, 0)))
        scratch.append(pltpu.VMEM((rows, LANES), F32))
        if nj > 1:
            scratch.append(pltpu.VMEM((rows, LANES), F32))
    if want_krot:
        out_shape.append(jax.ShapeDtypeStruct((seq, N_KV * HEAD_DIM), F32))
        out_specs.append(pl.BlockSpec((rows, kw), lambda c, j: (c, j)))
    return pl.pallas_call(
        functools.partial(_band_kernel, dil=dil, max_dist=max_dist, use_sink=use_sink, want_lse=want_lse,
                          want_krot=want_krot, nkv=nkv, nj=nj),
        out_shape=tuple(out_shape), grid=(seq // rows, nj), in_specs=in_specs, out_specs=tuple(out_specs),
        scratch_shapes=scratch, compiler_params=_cparams(2), name=name)(*args)


def _merge_kernel(o1_ref, o2_ref, o3_ref, l1_ref, l2_ref, l3_ref, e_ref, out_ref):
    ls = [l1_ref[...], l2_ref[...], l3_ref[...]]
    mx = jnp.maximum(jnp.maximum(ls[0], ls[1]), ls[2])
    es = [jnp.exp(l - mx) for l in ls]
    inv = 1.0 / (es[0] + es[1] + es[2])
    e = e_ref[...]
    acc = None
    for en, o_ref in zip(es, (o1_ref, o2_ref, o3_ref)):
        wf = jnp.dot((en * inv).astype(BF16), e, preferred_element_type=F32)
        term = wf * o_ref[...].astype(F32)
        acc = term if acc is None else acc + term
    out_ref[...] = acc.astype(out_ref.dtype)


def _merge(os_, ls_, eexp, *, tm):
    seq, d = os_[0].shape
    o_spec = pl.BlockSpec((tm, d), lambda i: (i, 0))
    l_spec = pl.BlockSpec((tm, LANES), lambda i: (i, 0))
    return pl.pallas_call(
        _merge_kernel, out_shape=jax.ShapeDtypeStruct((seq, d), BF16), grid=(seq // tm,),
        in_specs=[o_spec] * 3 + [l_spec] * 3 + [pl.BlockSpec((LANES, d), lambda i: (0, 0))],
        out_specs=o_spec, compiler_params=_cparams(1), name="dil_merge")(*os_, *ls_, eexp)


def _expand_q(qb, t_len):
    gd = N_KV * HEAD_DIM
    sub = lax.broadcasted_iota(jnp.int32, (N_KV, gd), 0)
    lg = lax.broadcasted_iota(jnp.int32, (N_KV, gd), 1) >> _LOG2_HEAD_DIM
    diag = sub == lg
    tiles = []
    for r in range(REP):
        vr = jnp.concatenate(
            [qb[:, (g * REP + r) * HEAD_DIM:(g * REP + r + 1) * HEAD_DIM] for g in range(N_KV)], axis=1)
        for t in range(t_len):
            tiles.append(jnp.where(diag, jnp.broadcast_to(vr[t:t + 1], (N_KV, gd)), 0.0))
    return jnp.concatenate(tiles, axis=0).astype(BF16)


def _collapse_o(r_full, t_len):
    gd = N_KV * HEAD_DIM
    n = r_full.shape[0]
    sub = lax.broadcasted_iota(jnp.int32, (n, gd), 0) & (N_KV - 1)
    lg = lax.broadcasted_iota(jnp.int32, (n, gd), 1) >> _LOG2_HEAD_DIM
    masked = jnp.where(sub == lg, r_full, 0.0)
    red = jnp.sum(masked.reshape(n // N_KV, N_KV, gd), axis=1)
    pieces = []
    for g in range(N_KV):
        for r in range(REP):
            pieces.append(red[r * t_len:(r + 1) * t_len, g * HEAD_DIM:(g + 1) * HEAD_DIM])
    return jnp.concatenate(pieces, axis=1)


def _row_t(shape, t_len):
    r = lax.broadcasted_iota(jnp.int32, shape, 0)
    return (r >> 3) & (t_len - 1)


def _cached_attend(qx, kt, vt, mask_w, kn, vn, mask_n, sink):
    s_w = jnp.where(mask_w, jnp.dot(qx.astype(kt.dtype), kt, preferred_element_type=F32), NEG)
    s_n = jnp.where(mask_n, lax.dot_general(qx, kn, _NT, preferred_element_type=F32), NEG)
    m = jnp.maximum(jnp.max(s_w, axis=-1, keepdims=True), jnp.max(s_n, axis=-1, keepdims=True))
    if sink is not None:
        m = jnp.maximum(m, sink)
    e_w = jnp.exp(s_w - m)
    e_n = jnp.exp(s_n - m)
    den = jnp.sum(e_w, axis=-1, keepdims=True) + jnp.sum(e_n, axis=-1, keepdims=True)
    if sink is not None:
        den = den + jnp.exp(sink - m)
    r = (lax.dot_general(e_w.astype(vt.dtype), vt, _NT, preferred_element_type=F32)
         + jnp.dot(e_n.astype(BF16), vn, preferred_element_type=F32))
    return r / den, m + jnp.log(den)


def _swa_cached_kernel(q_ref, kn_ref, vn_ref, cos_ref, sa_ref, sb_ref, ck_ref, cv_ref, sink_ref,
                       o_ref, krot_ref, *, nb, t_len):
    tabs = (cos_ref[...], sa_ref[...], sb_ref[...])
    q = _rope(q_ref[...], *tabs) * ATT_SCALE
    kn = _rope(kn_ref[...], *tabs)
    krot_ref[...] = kn
    kn16 = kn.astype(BF16)
    vn16 = vn_ref[...].astype(BF16)
    lb = ck_ref.shape[3]
    gd = N_KV * HEAD_DIM
    nrow = REP * t_len * N_KV
    tq = _row_t((nrow, lb), t_len)
    mask_w = lax.broadcasted_iota(jnp.int32, (nrow, lb), 1) >= tq + 1
    mask_n = lax.broadcasted_iota(jnp.int32, (nrow, BF16_ROWS), 1) <= _row_t((nrow, BF16_ROWS), t_len)
    sk = sink_ref[...]
    pad = jnp.zeros((BF16_ROWS - t_len, gd), BF16)
    outs = []
    for b in range(nb):
        rs = slice(b * t_len, (b + 1) * t_len)
        r_full, _ = _cached_attend(
            _expand_q(q[rs], t_len), ck_ref[b].reshape(gd, lb).astype(BF16), cv_ref[b].reshape(gd, lb).astype(BF16),
            mask_w, jnp.concatenate([kn16[rs], pad], axis=0), jnp.concatenate([vn16[rs], pad], axis=0), mask_n, sk)
        outs.append(_collapse_o(r_full, t_len))
    o_ref[...] = jnp.concatenate(outs, axis=0).astype(o_ref.dtype)


def _swa_cached(p0, off, seq, n_b, t_len, tables, cache_kt, cache_vt, sink_col, *, nb):
    rows = nb * t_len
    d_att = N_HEADS * HEAD_DIM
    d_kv = N_KV * HEAD_DIM
    lb = cache_kt.shape[3]
    assert lb == SWA_WINDOW and t_len <= BF16_ROWS
    r0 = seq // rows
    nrow = REP * t_len * N_KV
    c_spec = pl.BlockSpec((nb, N_KV, HEAD_DIM, lb), lambda i: (i, 0, 0, 0))
    t_spec = pl.BlockSpec((rows, LANES), lambda i: (r0 + i, 0))
    return pl.pallas_call(
        functools.partial(_swa_cached_kernel, nb=nb, t_len=t_len),
        out_shape=(jax.ShapeDtypeStruct((n_b * t_len, d_att), BF16),
                   jax.ShapeDtypeStruct((n_b * t_len, d_kv), F32)),
        grid=(n_b // nb,),
        in_specs=[pl.BlockSpec((rows, d_att), lambda i: (i, off["q"] // d_att)),
                  pl.BlockSpec((rows, d_kv), lambda i: (i, off["k"] // d_kv)),
                  pl.BlockSpec((rows, d_kv), lambda i: (i, off["v"] // d_kv)),
                  t_spec, t_spec, t_spec, c_spec, c_spec,
                  pl.BlockSpec((nrow, 1), lambda i: (0, 0))],
        out_specs=(pl.BlockSpec((rows, d_att), lambda i: (i, 0)),
                   pl.BlockSpec((rows, d_kv), lambda i: (i, 0))),
        compiler_params=_cparams(1), name="swa_cached")(p0, p0, p0, *tables, cache_kt, cache_vt, sink_col)


def _dil_cached_kernel(q1_ref, q2_ref, q3_ref, kn_ref, vn_ref, cos_ref, sa_ref, sb_ref, ck_ref, cv_ref,
                       o_ref, krot_ref, *, nb, t_len):
    tabs = (cos_ref[...], sa_ref[...], sb_ref[...])
    qs = [_rope(r[...], *tabs) * ATT_SCALE for r in (q1_ref, q2_ref, q3_ref)]
    kn = _rope(kn_ref[...], *tabs)
    krot_ref[...] = kn
    kn16 = kn.astype(BF16)
    vn16 = vn_ref[...].astype(BF16)
    lbuf = ck_ref.shape[3]
    gd = N_KV * HEAD_DIM
    nrow = REP * t_len * N_KV
    tqn = _row_t((nrow, BF16_ROWS), t_len)
    coln = lax.broadcasted_iota(jnp.int32, (nrow, BF16_ROWS), 1)
    masks = []
    for window, dil in DIL_PATTERNS:
        tq = _row_t((nrow, window), t_len)
        col = lax.broadcasted_iota(jnp.int32, (nrow, window), 1)
        if dil == 1:
            masks.append((col >= tq, coln <= tqn))
        else:
            masks.append(((col & (dil - 1)) == tq, coln == tqn))
    pad = jnp.zeros((BF16_ROWS - t_len, gd), BF16)
    outs = []
    for b in range(nb):
        rs = slice(b * t_len, (b + 1) * t_len)
        kt = ck_ref[b].reshape(gd, lbuf)
        vt = cv_ref[b].reshape(gd, lbuf)
        knp = jnp.concatenate([kn16[rs], pad], axis=0)
        vnp = jnp.concatenate([vn16[rs], pad], axis=0)
        res = []
        for gi, (window, dil) in enumerate(DIL_PATTERNS):
            res.append(_cached_attend(_expand_q(qs[gi][rs], t_len), kt[:, lbuf - window:], vt[:, lbuf - window:],
                                      masks[gi][0], knp, vnp, masks[gi][1], None))
        mx = jnp.maximum(jnp.maximum(res[0][1], res[1][1]), res[2][1])
        ws = [jnp.exp(l - mx) for _, l in res]
        inv = 1.0 / (ws[0] + ws[1] + ws[2])
        merged = (ws[0] * inv) * res[0][0] + (ws[1] * inv) * res[1][0] + (ws[2] * inv) * res[2][0]
        outs.append(_collapse_o(merged, t_len))
    o_ref[...] = jnp.concatenate(outs, axis=0).astype(o_ref.dtype)


def _dil_cached(p1, seq, n_b, t_len, tables, cache_kt, cache_vt, *, nb):
    rows = nb * t_len
    d_att = N_HEADS * HEAD_DIM
    d_kv = N_KV * HEAD_DIM
    lbuf = cache_kt.shape[3]
    assert rows % 8 == 0 and seq % rows == 0 and t_len <= BF16_ROWS
    for window, dil in DIL_PATTERNS:
        assert window <= lbuf and (PAST_LEN - lbuf) % dil == 0 and lbuf % dil == 0 and (dil == 1 or dil >= t_len)
    r0 = seq // rows
    c_spec = pl.BlockSpec((nb, N_KV, HEAD_DIM, lbuf), lambda i: (i, 0, 0, 0))
    t_spec = pl.BlockSpec((rows, LANES), lambda i: (r0 + i, 0))
    nq = d_att * len(DIL_PATTERNS)
    return pl.pallas_call(
        functools.partial(_dil_cached_kernel, nb=nb, t_len=t_len),
        out_shape=(jax.ShapeDtypeStruct((n_b * t_len, d_att), F32),
                   jax.ShapeDtypeStruct((n_b * t_len, d_kv), F32)),
        grid=(n_b // nb,),
        in_specs=[pl.BlockSpec((rows, d_att), lambda i: (i, 0)),
                  pl.BlockSpec((rows, d_att), lambda i: (i, 1)),
                  pl.BlockSpec((rows, d_att), lambda i: (i, 2)),
                  pl.BlockSpec((rows, d_kv), lambda i: (i, nq // d_kv)),
                  pl.BlockSpec((rows, d_kv), lambda i: (i, nq // d_kv + 1)),
                  t_spec, t_spec, t_spec, c_spec, c_spec],
        out_specs=(pl.BlockSpec((rows, d_att), lambda i: (i, 0)),
                   pl.BlockSpec((rows, d_kv), lambda i: (i, 0))),
        compiler_params=_cparams(1), name="dil_cached")(p1, p1, p1, p1, p1, *tables, cache_kt, cache_vt)


def _row_tile(m, pref):
    for t in (1024, 512, 256, 128, 64, 32, 16, 8):
        if t <= pref and m % t == 0:
            return t
    raise ValueError(m)


def kernel(x_prompt, x_sample, state_conv, state_ssm, cache_swa_k, cache_swa_v, cache_dil_k, cache_dil_v, norm_mix, norm_mlp, e_w_in, e_conv_w, e_conv_b, e_dt_bias, e_a_log, e_d_skip, e_gate_norm, e_sinks, e_w_out, o_w_in, o_w_out, mlp_w1, mlp_w2, norm_final):
    nbp, seq, d = x_prompt.shape
    n_b, t_len, _ = x_sample.shape
    assert nbp == 1 and d == N_HEADS * HEAD_DIM and norm_mix.shape[0] == 2
    ms = n_b * t_len
    m = seq + ms
    tms = (_row_tile(seq, 1024), _row_tile(ms, 512))
    d_kv = N_KV * HEAD_DIM
    d_bc = 2 * SSD_GROUPS * SSD_STATE
    conv_dim = d + d_bc
    both = lambda fn, *rows, **kw: tuple(
        fn(*(r[s] for r in rows), tm=tms[s], name=kw["name"] + ("_p", "_s")[s], **{k: v for k, v in kw.items() if k != "name"})
        for s in range(2))

    h = (x_prompt.reshape(seq, d), x_sample.reshape(ms, d))
    tables = _rope_tables(m, seq, t_len, _row_tile(math.gcd(seq, ms), 512))
    eexp = (lax.broadcasted_iota(jnp.int32, (LANES, d), 1) // HEAD_DIM
            == lax.broadcasted_iota(jnp.int32, (LANES, d), 0)).astype(BF16)
    keys_minor = lambda c: jnp.transpose(c[0], (0, 2, 3, 1))

    wi = e_w_in[0]
    c_z, c_xbc, c_dt, c_q = 0, d, d + conv_dim, d + conv_dim + N_HEADS
    c_k, c_v = c_q + d, c_q + d + d_kv
    tn0 = 1024
    wi16 = wi.astype(BF16)
    w0_q, w0_kv = wi16[:, c_q:c_q + d], wi16[:, c_k:c_v + d_kv]
    w0_dt = jnp.pad(wi16[:, c_dt:c_dt + N_HEADS], ((0, 0), (0, LANES - N_HEADS)))
    assert c_z == 0 and (c_xbc + d) % tn0 == 0 and d_bc % tn0 == 0 and d % tn0 == 0
    segs0 = [(wi16, 0, 2 * d // tn0), (w0_q, 0, d // tn0), (wi16, (c_xbc + d) // tn0, d_bc // tn0),
             (w0_kv, 0, 2 * d_kv // tn0)]
    off0 = {"z": 0, "xs": d, "q": 2 * d, "bc": 3 * d, "k": 3 * d + d_bc, "v": 3 * d + d_bc + d_kv}
    g0 = norm_mix[0]
    (p0_p, dt_p), (p0_s, dt_s) = both(
        lambda x, tm, name: _norm_matmul_multi(x, g0, segs0, w0_dt, tm=tm, tn=tn0, name=name), h, name="l0_in_proj")

    cw, cb = e_conv_w[0], e_conv_b[0]
    wx, wbc = cw[:, :d], cw[:, d:]
    bx, bbc = cb[:d].reshape(1, d), cb[d:].reshape(1, d_bc)
    pad_h = lambda v: jnp.pad(v.reshape(1, N_HEADS), ((0, 0), (0, LANES - N_HEADS)))
    dtb, alog = pad_h(e_dt_bias[0]), pad_h(e_a_log[0])
    dskip = jnp.repeat(e_d_skip[0], HEAD_DIM).reshape(1, d)
    gn = e_gate_norm[0].reshape(1, d)

    ssd_p, tail_x, tail_bc, p_ssm = _ssd_prompt(p0_p, dt_p, off0, seq, wx, wbc, bx, bbc, dtb, alog, dskip, gn, eexp)
    epad = jnp.pad(state_conv[0], ((0, 0), (0, 1), (0, 0))).reshape(ms, conv_dim)
    ssd_s, s_ssm = _ssd_sample(p0_s, dt_s, off0, seq, n_b, t_len, epad, state_ssm[0], wx, wbc, bx, bbc, dtb, alog,
                               dskip, gn, eexp, nb=8)

    att_p, krot0_p = _band_attention(p0_p, seq, off0["q"], off0["k"], off0["v"], tables, e_sinks[0], dil=1,
                                     max_dist=SWA_WINDOW - 1, nkv=N_KV, want_lse=False, want_krot=True,
                                     name="swa_prompt")
    sink_col = jnp.broadcast_to(e_sinks[0].reshape(N_KV, REP).T[:, None, :], (REP, t_len, N_KV)).reshape(-1, 1)
    att_s, krot0_s = _swa_cached(p0_s, off0, seq, n_b, t_len, tables, keys_minor(cache_swa_k),
                                 keys_minor(cache_swa_v), sink_col, nb=4)

    def mlp(hh, layer, g_out=None):
        g = norm_mlp[layer]
        o_s, w1_16, w2_16 = _mlp(hh[1], g, mlp_w1, mlp_w2, g_out, tm=ms, tk=512, name=f"l{layer}_mlp_s", layer=layer)
        o_p = _mlp(hh[0], g, w1_16, w2_16, g_out, tm=tms[0], tk=1024, name=f"l{layer}_mlp_p")
        return o_p, o_s

    def out_proj(xs_p, xs_s, w, hh, name):
        h_s, *w16 = _matmul_resid(xs_s, w, hh[1], tm=ms, tn=512, name=name + "_s", emit=True)
        return _matmul_resid(xs_p, w16, hh[0], tm=tms[0], tn=1024, name=name + "_p"), h_s

    h = out_proj([ssd_p, att_p], [ssd_s, att_s], e_w_out[0], h, "l0_out_proj")
    h = mlp(h, 0)

    n_pat = len(DIL_PATTERNS)
    p1_s, w1i = _norm_matmul(h[1], norm_mix[1], o_w_in[0], tm=ms, tn=1024, name="l1_in_proj_s", emit=True)
    p1_p = _norm_matmul(h[0], norm_mix[1], w1i, tm=tms[0], tn=1792, name="l1_in_proj_p")
    k_off, v_off = n_pat * d, n_pat * d + d_kv
    os_, ls_ = [], []
    krot1_p = None
    for gi, (window, dil) in enumerate(DIL_PATTERNS):
        res = _band_attention(p1_p, seq, gi * d, k_off, v_off, tables, None, dil=dil, max_dist=window // dil,
                              nkv=(2 if BAND * dil * N_HEADS * HEAD_DIM * 4 > 8 * 2 ** 20 else N_KV),
                              want_lse=True, want_krot=(gi == 0), name=f"dil_prompt_{dil}")
        os_.append(res[0])
        ls_.append(res[1])
        if gi == 0:
            krot1_p = res[2]
    merged_p = _merge(os_, ls_, eexp, tm=_row_tile(seq, 512))
    dil_s, krot1_s = _dil_cached(p1_s, seq, n_b, t_len, tables, keys_minor(cache_dil_k), keys_minor(cache_dil_v),
                                 nb=2)
    h = out_proj([merged_p], [dil_s.astype(BF16)], o_w_out[0], h, "l1_out_proj")
    y_p, y_s = mlp(h, 1, norm_final)

    keep_swa = min(SWA_WINDOW, seq)
    keep_dil = min(max(w for w, _ in DIL_PATTERNS), seq)
    kv4 = lambda a: a.reshape(a.shape[0], N_KV, HEAD_DIM)
    y_prompt = y_p.reshape(1, seq, d)
    y_sample = y_s.reshape(n_b, t_len, d)
    p_conv = jnp.concatenate([tail_x[8 - (SSD_CONV - 1):], tail_bc[8 - (SSD_CONV - 1):]], axis=1)[None, None]
    p_swa_k = kv4(krot0_p[seq - keep_swa:])[None, None]
    p_swa_v = kv4(p0_p[seq - keep_swa:, off0["v"]:off0["v"] + d_kv])[None, None]
    p_dil_k = kv4(krot1_p[seq - keep_dil:])[None, None]
    p_dil_v = kv4(p1_p[seq - keep_dil:, v_off:v_off + d_kv])[None, None]
    xbc_s = jnp.concatenate([p0_s[:, off0["xs"]:off0["xs"] + d], p0_s[:, off0["bc"]:off0["bc"] + d_bc]], axis=1)
    s_conv = xbc_s.reshape(n_b, t_len, conv_dim)[:, t_len - (SSD_CONV - 1):][None]
    bt = lambda a: a.reshape(n_b, t_len, N_KV, HEAD_DIM)[None]
    s_swa_k = bt(krot0_s)
    s_swa_v = bt(p0_s[:, off0["v"]:off0["v"] + d_kv])
    s_dil_k = bt(krot1_s)
    s_dil_v = bt(p1_s[:, v_off:v_off + d_kv])
    return (y_prompt, y_sample, p_conv, p_ssm[None, None], p_swa_k, p_swa_v, p_dil_k, p_dil_v,
            s_conv, s_ssm[None], s_swa_k, s_swa_v, s_dil_k, s_dil_v)
```

```python
import functools
import math

import jax
import jax.numpy as jnp
from jax import lax
from jax.experimental import pallas as pl
from jax.experimental.pallas import tpu as pltpu

F32 = jnp.float32
BF16 = jnp.bfloat16

NORM_EPS = 1e-5
HEAD_DIM = 64
ROT_HALF = 8
ROPE_THETA = 500000.0
PAST_LEN = 8192
N_HEADS = 32
N_KV = 8
REP = N_HEADS // N_KV
SSD_GROUPS = 4
SSD_STATE = 128
SSD_CONV = 4
SSD_CHUNK = 128
SWA_WINDOW = 128
DIL_PATTERNS = ((128, 1), (512, 4), (2048, 16))
BAND = 128
ATT_SCALE = HEAD_DIM ** -0.5
LOG2E = math.log2(math.e)
LN2 = math.log(2.0)
NEG = -1e30
_LOG2_HEAD_DIM = 6

LANES = 128
BF16_ROWS = 16
PERM_CHUNK = 256
VMEM_LIMIT_BYTES = 60 * 1024 * 1024

_NT = (((1,), (1,)), ((), ()))
_TN = (((0,), (0,)), ((), ()))


def _cparams(n_axes):
    return pltpu.CompilerParams(dimension_semantics=("arbitrary",) * n_axes,
                                vmem_limit_bytes=VMEM_LIMIT_BYTES)


def _silu(x):
    return x * jax.nn.sigmoid(x)


def _tile_lanes(t, width):
    k = width // t.shape[1]
    return t if k == 1 else jnp.concatenate([t] * k, axis=1)


def _rope(x, cos_t, sa_t, sb_t):
    w = x.shape[1]
    return (x * _tile_lanes(cos_t, w)
            + pltpu.roll(x, w - ROT_HALF, 1) * _tile_lanes(sa_t, w)
            + pltpu.roll(x, ROT_HALF, 1) * _tile_lanes(sb_t, w))


def _low_half(shape):
    return (lax.broadcasted_iota(jnp.int32, shape, 1) & (LANES - 1)) < HEAD_DIM


def _rope_table_kernel(cos_ref, sa_ref, sb_ref, *, tm, seq, dec_seq):
    i = pl.program_id(0)
    row = i * tm + lax.broadcasted_iota(jnp.int32, (tm, LANES), 0)
    lane = lax.broadcasted_iota(jnp.int32, (tm, LANES), 1)
    pos = jnp.where(row < seq, row, PAST_LEN + ((row - seq) & (dec_seq - 1)))
    c = lane & (HEAD_DIM - 1)
    f = (c & (ROT_HALF - 1)).astype(F32)
    inv_freq = jnp.exp(f * (-math.log(ROPE_THETA) / ROT_HALF))
    ang = pos.astype(F32) * inv_freq
    cs = jnp.cos(ang)
    sn = jnp.sin(ang)
    cos_ref[...] = jnp.where(c < 2 * ROT_HALF, cs, 1.0)
    sa_ref[...] = jnp.where(c < ROT_HALF, -sn, 0.0)
    sb_ref[...] = jnp.where((c >= ROT_HALF) & (c < 2 * ROT_HALF), sn, 0.0)


def _rope_tables(m, seq, dec_seq, tm):
    assert dec_seq & (dec_seq - 1) == 0 and m % tm == 0
    shp = jax.ShapeDtypeStruct((m, LANES), F32)
    spec = pl.BlockSpec((tm, LANES), lambda i: (i, 0))
    return pl.pallas_call(
        functools.partial(_rope_table_kernel, tm=tm, seq=seq, dec_seq=dec_seq),
        out_shape=(shp, shp, shp), grid=(m // tm,), out_specs=(spec, spec, spec),
        compiler_params=_cparams(1), name="rope_tables")()


def _rmsnorm_bf16(x, g):
    ms = jnp.mean(x * x, axis=-1, keepdims=True)
    return (x * lax.rsqrt(ms + NORM_EPS) * g).astype(BF16)


def _norm_mm_kernel(x_ref, g_ref, w_ref, o_ref, *rest, emit):
    xn_ref = rest[-1]

    @pl.when(pl.program_id(1) == 0)
    def _():
        xn_ref[...] = _rmsnorm_bf16(x_ref[...], g_ref[...])

    w = w_ref[...]
    if emit:
        w = w.astype(BF16)
        rest[0][...] = w
    o_ref[...] = jnp.dot(xn_ref[...], w, preferred_element_type=F32)


def _norm_matmul(x, g, w, *, tm, tn, name, emit=False):
    m, k = x.shape
    n = w.shape[1]
    assert m % tm == 0 and n % tn == 0 and w.shape[0] == k and (not emit or m == tm)
    w_spec = pl.BlockSpec((k, tn), lambda i, j: (0, j))
    o_spec = pl.BlockSpec((tm, tn), lambda i, j: (i, j))
    o_shape = jax.ShapeDtypeStruct((m, n), F32)
    return pl.pallas_call(
        functools.partial(_norm_mm_kernel, emit=emit),
        out_shape=(o_shape, jax.ShapeDtypeStruct((k, n), BF16)) if emit else o_shape,
        grid=(m // tm, n // tn),
        in_specs=[pl.BlockSpec((tm, k), lambda i, j: (i, 0)),
                  pl.BlockSpec((1, k), lambda i, j: (0, 0)),
                  w_spec],
        out_specs=(o_spec, w_spec) if emit else o_spec,
        scratch_shapes=[pltpu.VMEM((tm, k), BF16)],
        compiler_params=_cparams(2), name=name)(x, g.reshape(1, k), w)


def _norm_mm_multi_kernel(*refs, owners):
    n_w = len(owners)
    x_ref, g_ref = refs[:2]
    w_refs = refs[2:2 + n_w]
    ws_ref, o_ref, os_ref, xn_ref = refs[2 + n_w:]
    j = pl.program_id(1)

    @pl.when(j == 0)
    def _():
        xn_ref[...] = _rmsnorm_bf16(x_ref[...], g_ref[...])
        os_ref[...] = jnp.dot(xn_ref[...], ws_ref[...], preferred_element_type=F32)

    for w_ref, tiles in zip(w_refs, owners):
        own = functools.reduce(jnp.logical_or, [(j >= lo) & (j < hi) for lo, hi in tiles])

        @pl.when(own)
        def _():
            o_ref[...] = jnp.dot(xn_ref[...], w_ref[...], preferred_element_type=F32)


def _norm_matmul_multi(x, g, segs, w_small, *, tm, tn, name):
    m, k = x.shape
    arrays, owners, maps = [], [], []
    start = 0
    for w, cb0, n in segs:
        if not any(w is a for a in arrays):
            arrays.append(w)
            owners.append([])
            maps.append([])
        idx = [i for i, a in enumerate(arrays) if a is w][0]
        owners[idx].append((start, start + n))
        maps[idx].append((start, n, cb0))
        start += n
    n_tiles = start

    def index_map(pieces):
        def col(i, j):
            c = pieces[-1][2] + pieces[-1][1] - 1
            for a, n, cb0 in reversed(pieces):
                c = jnp.where(j < a + n, cb0 + jnp.maximum(j - a, 0), c)
            return 0, c
        return col

    n_small = w_small.shape[1]
    return pl.pallas_call(
        functools.partial(_norm_mm_multi_kernel, owners=tuple(tuple(o) for o in owners)),
        out_shape=(jax.ShapeDtypeStruct((m, n_tiles * tn), F32), jax.ShapeDtypeStruct((m, n_small), F32)),
        grid=(m // tm, n_tiles),
        in_specs=[pl.BlockSpec((tm, k), lambda i, j: (i, 0)),
                  pl.BlockSpec((1, k), lambda i, j: (0, 0))]
        + [pl.BlockSpec((k, tn), index_map(p)) for p in maps]
        + [pl.BlockSpec((k, n_small), lambda i, j: (0, 0))],
        out_specs=(pl.BlockSpec((tm, tn), lambda i, j: (i, j)), pl.BlockSpec((tm, n_small), lambda i, j: (i, 0))),
        scratch_shapes=[pltpu.VMEM((tm, k), BF16)],
        compiler_params=_cparams(2), name=name)(x, g.reshape(1, k), *arrays, w_small)


def _mm_resid_kernel(*refs, n_in, emit):
    x_refs, w_refs = refs[:n_in], refs[n_in:2 * n_in]
    r_ref, o_ref = refs[2 * n_in], refs[2 * n_in + 1]
    wo_refs = refs[2 * n_in + 2:]
    acc = r_ref[...]
    for s, (x_ref, w_ref) in enumerate(zip(x_refs, w_refs)):
        w = w_ref[...]
        if emit:
            w = w.astype(BF16)
            wo_refs[s][...] = w
        acc = acc + jnp.dot(x_ref[...], w, preferred_element_type=F32)
    o_ref[...] = acc


def _matmul_resid(xs, w, resid, *, tm, tn, name, emit=False):
    m, n = resid.shape
    k = xs[0].shape[1]
    assert m % tm == 0 and n % tn == 0 and (not emit or (m == tm and w.shape == (k * len(xs), n)))
    w_args = [w] * len(xs) if emit else list(w)
    in_specs = [pl.BlockSpec((tm, k), lambda i, j: (i, 0)) for _ in xs]
    in_specs += [pl.BlockSpec((k, tn), functools.partial(lambda i, j, s: (s, j), s=s if emit else 0))
                 for s in range(len(xs))]
    in_specs.append(pl.BlockSpec((tm, tn), lambda i, j: (i, j)))
    o_spec = pl.BlockSpec((tm, tn), lambda i, j: (i, j))
    o_shape = jax.ShapeDtypeStruct((m, n), F32)
    if emit:
        out_shape = (o_shape,) + (jax.ShapeDtypeStruct((k, n), BF16),) * len(xs)
        out_specs = (o_spec,) + (pl.BlockSpec((k, tn), lambda i, j: (0, j)),) * len(xs)
    else:
        out_shape, out_specs = o_shape, o_spec
    return pl.pallas_call(
        functools.partial(_mm_resid_kernel, n_in=len(xs), emit=emit),
        out_shape=out_shape,
        grid=(m // tm, n // tn),
        in_specs=in_specs,
        out_specs=out_specs,
        compiler_params=_cparams(2), name=name)(*xs, *w_args, resid)


def _mlp_kernel(x_ref, g_ref, w1_ref, w2_ref, gout_ref, o_ref, *rest, norm_out, emit):
    xn_ref = rest[-1]
    k = pl.program_id(1)

    @pl.when(k == 0)
    def _():
        x = x_ref[...]
        xn_ref[...] = _rmsnorm_bf16(x, g_ref[...])
        o_ref[...] = x

    w1, w2 = w1_ref[...], w2_ref[...]
    if emit:
        w1, w2 = w1.astype(BF16), w2.astype(BF16)
        rest[0][...] = w1
        rest[1][...] = w2
    u = jnp.dot(xn_ref[...], w1, preferred_element_type=F32)
    u = jnp.square(jnp.maximum(u, 0.0)).astype(BF16)
    o_ref[...] += jnp.dot(u, w2, preferred_element_type=F32)

    if norm_out:
        @pl.when(k == pl.num_programs(1) - 1)
        def _():
            o = o_ref[...]
            ms = jnp.mean(o * o, axis=-1, keepdims=True)
            o_ref[...] = o * lax.rsqrt(ms + NORM_EPS) * gout_ref[...]


def _mlp(x, g, w1, w2, g_out, *, tm, tk, name, layer=None):
    m, d = x.shape
    emit = layer is not None
    hid = w1.shape[-1]
    assert m % tm == 0 and hid % tk == 0 and (not emit or m == tm)
    w1o_spec = pl.BlockSpec((d, tk), lambda i, k: (0, k))
    w2o_spec = pl.BlockSpec((tk, d), lambda i, k: (k, 0))
    if emit:
        w_specs = [pl.BlockSpec((None, d, tk), lambda i, k: (layer, 0, k)),
                   pl.BlockSpec((None, tk, d), lambda i, k: (layer, k, 0))]
        out_shape = (jax.ShapeDtypeStruct((m, d), F32), jax.ShapeDtypeStruct((d, hid), BF16),
                     jax.ShapeDtypeStruct((hid, d), BF16))
        out_specs = (pl.BlockSpec((tm, d), lambda i, k: (i, 0)), w1o_spec, w2o_spec)
    else:
        w_specs = [w1o_spec, w2o_spec]
        out_shape = jax.ShapeDtypeStruct((m, d), F32)
        out_specs = pl.BlockSpec((tm, d), lambda i, k: (i, 0))
    return pl.pallas_call(
        functools.partial(_mlp_kernel, norm_out=g_out is not None, emit=emit),
        out_shape=out_shape,
        grid=(m // tm, hid // tk),
        in_specs=[pl.BlockSpec((tm, d), lambda i, k: (i, 0)),
                  pl.BlockSpec((1, d), lambda i, k: (0, 0))] + w_specs + [pl.BlockSpec((1, d), lambda i, k: (0, 0))],
        out_specs=out_specs,
        scratch_shapes=[pltpu.VMEM((tm, d), BF16)],
        compiler_params=_cparams(2), name=name)(x, g.reshape(1, d), w1, w2,
                                                (g if g_out is None else g_out).reshape(1, d))


def _causal_conv(raw, tail, w_ref, b_ref):
    n = raw.shape[0]
    full = jnp.concatenate([tail, raw], axis=0)
    acc = b_ref[...]
    for i in range(SSD_CONV):
        s = SSD_CONV - 1 - i
        acc = acc + full[8 - s:8 - s + n] * w_ref[i:i + 1, :]
    return _silu(acc)


def _gate_groupnorm(y, z, gn):
    y = y * _silu(z)
    gw = y.shape[1] // SSD_GROUPS
    outs = []
    for g in range(SSD_GROUPS):
        yg = y[:, g * gw:(g + 1) * gw]
        ms = jnp.mean(yg * yg, axis=-1, keepdims=True)
        outs.append(yg * lax.rsqrt(ms + NORM_EPS))
    return jnp.concatenate(outs, axis=1) * gn


def _ssd_prompt_kernel(z_ref, xs_ref, bc_ref, dt_ref, wx_ref, wbc_ref, bx_ref, bbc_ref, dtb_ref, alog_ref,
                       dskip_ref, gn_ref, eexp_ref, y_ref, tailx_ref, tailbc_ref, h_ref, y_s):
    c = pl.program_id(0)
    q = SSD_CHUNK
    gs = SSD_STATE
    hp = HEAD_DIM

    @pl.when(c == 0)
    def _():
        tailx_ref[...] = jnp.zeros_like(tailx_ref)
        tailbc_ref[...] = jnp.zeros_like(tailbc_ref)
        h_ref[...] = jnp.zeros_like(h_ref)

    xs_raw = xs_ref[...]
    bc_raw = bc_ref[...]
    xs_c = _causal_conv(xs_raw, tailx_ref[...], wx_ref, bx_ref)
    bc_c = _causal_conv(bc_raw, tailbc_ref[...], wbc_ref, bbc_ref)
    tailx_ref[...] = xs_raw[q - 8:q]
    tailbc_ref[...] = bc_raw[q - 8:q]

    lane = lax.broadcasted_iota(jnp.int32, (q, LANES), 1)
    row = lax.broadcasted_iota(jnp.int32, (q, LANES), 0)
    dt = jax.nn.softplus(dt_ref[...] + dtb_ref[...])
    a = -jnp.exp(alog_ref[...])
    la = jnp.where(lane < N_HEADS, dt * a, 0.0)
    acs = la
    s = 1
    while s < q:
        acs = acs + jnp.where(row >= s, pltpu.roll(acs, s, 0), 0.0)
        s *= 2
    acs_t = acs.T
    acs_last = acs[q - 1:q, :]
    to_end = jnp.exp(acs_last - acs)
    eacs = jnp.exp(acs)
    cdec = jnp.exp(acs_last)
    causal = (lax.broadcasted_iota(jnp.int32, (q, q), 0) >= lax.broadcasted_iota(jnp.int32, (q, q), 1))

    stacked = jnp.concatenate([dt, eacs, to_end], axis=0)
    hi = stacked.astype(BF16)
    lo = (stacked - hi.astype(F32)).astype(BF16)
    e = eexp_ref[...]
    full = jnp.dot(hi, e, preferred_element_type=F32) + jnp.dot(lo, e, preferred_element_type=F32)
    xdt = xs_c * full[0:q]
    eacs_f = full[q:2 * q]
    w_f = xdt * full[2 * q:3 * q]
    low = _low_half((q, LANES))
    top = lax.broadcasted_iota(jnp.int32, (2 * hp, gs), 0) < hp

    rep = N_HEADS // SSD_GROUPS
    for g in range(SSD_GROUPS):
        bg = bc_c[:, g * gs:(g + 1) * gs].astype(BF16)
        cg = bc_c[:, SSD_GROUPS * gs + g * gs:SSD_GROUPS * gs + (g + 1) * gs].astype(BF16)
        cb = lax.dot_general(cg, bg, _NT, preferred_element_type=F32)
        for rp in range(rep // 2):
            h0 = g * rep + 2 * rp
            sl = slice(h0 * hp, (h0 + 2) * hp)
            ms = []
            for h in (h0, h0 + 1):
                lmat = jnp.exp(jnp.where(causal, acs[:, h:h + 1] - acs_t[h:h + 1, :], -jnp.inf))
                ms.append((cb * lmat).astype(BF16))
            xp = xdt[:, sl]
            rhs = jnp.concatenate([jnp.where(low, xp, 0.0), jnp.where(low, 0.0, xp)], axis=0).astype(BF16)
            y = jnp.dot(jnp.concatenate(ms, axis=1), rhs, preferred_element_type=F32)
            hs = h_ref[h0:h0 + 2].reshape(2 * hp, gs)
            y = y + eacs_f[:, sl] * lax.dot_general(cg, hs.astype(BF16), _NT, preferred_element_type=F32)
            st = lax.dot_general(w_f[:, sl].astype(BF16), bg, _TN, preferred_element_type=F32)
            cd = jnp.where(top, cdec[:, h0:h0 + 1], cdec[:, h0 + 1:h0 + 2])
            h_ref[h0:h0 + 2] = (hs * cd + st).reshape(2, hp, gs)
            y_s[:, sl] = y + dskip_ref[:, sl] * xs_c[:, sl]

    y_ref[...] = _gate_groupnorm(y_s[...], z_ref[...], gn_ref[...]).astype(y_ref.dtype)


def _ssd_prompt(p0, dtr, off, seq, wx, wbc, bx, bbc, dtb, alog, dskip, gn, eexp):
    q = SSD_CHUNK
    d_in = wx.shape[1]
    d_bc = wbc.shape[1]
    assert seq % q == 0
    const = lambda shape: pl.BlockSpec(shape, lambda c: (0,) * len(shape))
    return pl.pallas_call(
        _ssd_prompt_kernel,
        out_shape=(jax.ShapeDtypeStruct((seq, d_in), BF16),
                   jax.ShapeDtypeStruct((8, d_in), F32),
                   jax.ShapeDtypeStruct((8, d_bc), F32),
                   jax.ShapeDtypeStruct((N_HEADS, HEAD_DIM, SSD_STATE), F32)),
        grid=(seq // q,),
        in_specs=[pl.BlockSpec((q, d_in), lambda c: (c, off["z"] // d_in)),
                  pl.BlockSpec((q, d_in), lambda c: (c, off["xs"] // d_in)),
                  pl.BlockSpec((q, d_bc), lambda c: (c, off["bc"] // d_bc)),
                  pl.BlockSpec((q, LANES), lambda c: (c, 0)),
                  const((SSD_CONV, d_in)), const((SSD_CONV, d_bc)), const((1, d_in)), const((1, d_bc)),
                  const((1, LANES)), const((1, LANES)), const((1, d_in)), const((1, d_in)), const((LANES, d_in))],
        out_specs=(pl.BlockSpec((q, d_in), lambda c: (c, 0)),
                   const((8, d_in)), const((8, d_bc)), const((N_HEADS, HEAD_DIM, SSD_STATE))),
        scratch_shapes=[pltpu.VMEM((q, d_in), F32)],
        compiler_params=_cparams(1), name="ssd_prompt")(p0, p0, p0, dtr, wx, wbc, bx, bbc, dtb, alog, dskip, gn, eexp)


def _split3(x):
    hi = x.astype(BF16)
    r1 = x - hi.astype(F32)
    mid = r1.astype(BF16)
    lo = (r1 - mid.astype(F32)).astype(BF16)
    return hi, mid, lo


def _ssd_sample_kernel(z_ref, xs_ref, bc_ref, dt_ref, ex_ref, ebc_ref, st_ref, wx_ref, wbc_ref, bx_ref, bbc_ref,
                       dtb_ref, alog_ref, dskip_ref, gn_ref, eexp_ref, y_ref, sto_ref, *, nb, t_len):
    rows = nb * t_len
    gs = SSD_STATE
    gw = xs_ref.shape[1] // SSD_GROUPS
    rep = N_HEADS // SSD_GROUPS

    def tcol(width):
        r = lax.broadcasted_iota(jnp.int32, (rows, width), 0)
        return r & (t_len - 1), r >> int(math.log2(t_len))

    def conv(raw, est, w_ref, b_ref):
        t, _ = tcol(raw.shape[1])
        acc = b_ref[...]
        for i in range(SSD_CONV):
            s = SSD_CONV - 1 - i
            if s == 0:
                sh = raw
            else:
                k = (rows - (SSD_CONV - 1) + s) % rows
                sh = jnp.where(t >= s, pltpu.roll(raw, s, 0), est if k == 0 else pltpu.roll(est, k, 0))
            acc = acc + sh * w_ref[i:i + 1, :]
        return _silu(acc)

    xs_c = conv(xs_ref[...], ex_ref[...], wx_ref, bx_ref)
    bc_c = conv(bc_ref[...], ebc_ref[...], wbc_ref, bbc_ref)
    bm = bc_c[:, :SSD_GROUPS * gs]
    cm = bc_c[:, SSD_GROUPS * gs:]

    t1, _ = tcol(LANES)
    lane = lax.broadcasted_iota(jnp.int32, (rows, LANES), 1)
    dt = jax.nn.softplus(dt_ref[...] + dtb_ref[...])
    a = -jnp.exp(alog_ref[...])
    la = jnp.where(lane < N_HEADS, dt * a, 0.0)
    acs = la
    for s in range(1, t_len):
        acs = acs + jnp.where(t1 >= s, pltpu.roll(la, s, 0), 0.0)
    alast = jnp.where(t1 == t_len - 1, acs, 0.0)
    for u in range(1, t_len):
        alast = alast + jnp.where(t1 == t_len - 1 - u, pltpu.roll(acs, rows - u, 0), 0.0)
    parts = [dt, jnp.exp(acs), jnp.exp(alast - acs), jnp.exp(alast)]
    for k in range(1, t_len):
        parts.append(jnp.exp(acs - pltpu.roll(acs, k, 0)))
    stacked = jnp.concatenate(parts, axis=0)
    hi, mid, lo = _split3(stacked)
    e = eexp_ref[...]
    full = (jnp.dot(hi, e, preferred_element_type=F32) + jnp.dot(mid, e, preferred_element_type=F32)
            + jnp.dot(lo, e, preferred_element_type=F32))
    dt_f, eacs_f, toend_f, cdec_f = (full[i * rows:(i + 1) * rows] for i in range(4))
    dec_f = [None] + [full[(3 + k) * rows:(4 + k) * rows] for k in range(1, t_len)]

    tw, _ = tcol(xs_c.shape[1])
    xdt = xs_c * dt_f
    y = jnp.zeros_like(xs_c)
    for k in range(t_len):
        bmk = bm if k == 0 else pltpu.roll(bm, k, 0)
        prod = cm * bmk
        cbs = []
        for g in range(SSD_GROUPS):
            sg = jnp.sum(prod[:, g * gs:(g + 1) * gs], axis=-1, keepdims=True)
            cbs.append(jnp.broadcast_to(sg, (rows, gw)))
        cb_f = jnp.concatenate(cbs, axis=1)
        if k == 0:
            y = y + cb_f * xdt
        else:
            y = y + jnp.where(tw >= k, cb_f * dec_f[k] * pltpu.roll(xdt, k, 0), 0.0)

    cm16 = cm.astype(BF16)
    bm16 = bm.astype(BF16)
    w_f = (toend_f * xdt)
    _, bg_ = tcol(gw)
    ones_blk = jnp.ones((3, gs), BF16)
    yoffs = []
    for g in range(SSD_GROUPS):
        cg = cm16[:, g * gs:(g + 1) * gs]
        rhs_top = jnp.concatenate([bm16[:, g * gs:(g + 1) * gs], jnp.zeros((rows, gs), BF16)], axis=1)
        rhs_mid = jnp.concatenate([jnp.zeros((3, gs), BF16), ones_blk], axis=1)
        rhs = jnp.concatenate([rhs_top, rhs_mid, jnp.zeros((13, 2 * gs), BF16)], axis=0)
        wg = w_f[:, g * gw:(g + 1) * gw]
        cdg = cdec_f[:, g * gw:(g + 1) * gw]
        yg = jnp.zeros((rows, gw), F32)
        for b in range(nb):
            hb = st_ref[b, g * rep:(g + 1) * rep].reshape(gw, gs)
            yb = lax.dot_general(cg, hb.astype(BF16), _NT, preferred_element_type=F32)
            yg = jnp.where(bg_ == b, yb, yg)
            wb = jnp.where(bg_ == b, wg, 0.0).astype(BF16)
            d_hi, d_mid, d_lo = _split3(cdg[b * t_len:b * t_len + 1])
            lhs = jnp.concatenate([wb, d_hi, d_mid, d_lo, jnp.zeros((13, gw), BF16)], axis=0)
            sd = lax.dot_general(lhs, rhs, _TN, preferred_element_type=F32)
            sto_ref[b, g * rep:(g + 1) * rep] = (hb * sd[:, gs:] + sd[:, :gs]).reshape(rep, HEAD_DIM, gs)
        yoffs.append(yg)
    y = y + eacs_f * jnp.concatenate(yoffs, axis=1) + dskip_ref[...] * xs_c
    y_ref[...] = _gate_groupnorm(y, z_ref[...], gn_ref[...]).astype(y_ref.dtype)


def _ssd_sample(p0, dtr, off, seq, n_b, t_len, epad, state, wx, wbc, bx, bbc, dtb, alog, dskip, gn, eexp, *, nb):
    rows = nb * t_len
    d_in = wx.shape[1]
    d_bc = wbc.shape[1]
    assert n_b % nb == 0 and t_len & (t_len - 1) == 0
    const = lambda shape: pl.BlockSpec(shape, lambda i: (0,) * len(shape))
    st_spec = pl.BlockSpec((nb, N_HEADS, HEAD_DIM, SSD_STATE), lambda i: (i, 0, 0, 0))
    return pl.pallas_call(
        functools.partial(_ssd_sample_kernel, nb=nb, t_len=t_len),
        out_shape=(jax.ShapeDtypeStruct((n_b * t_len, d_in), BF16),
                   jax.ShapeDtypeStruct(state.shape, F32)),
        grid=(n_b // nb,),
        in_specs=[pl.BlockSpec((rows, d_in), lambda i: (i, off["z"] // d_in)),
                  pl.BlockSpec((rows, d_in), lambda i: (i, off["xs"] // d_in)),
                  pl.BlockSpec((rows, d_bc), lambda i: (i, off["bc"] // d_bc)),
                  pl.BlockSpec((rows, LANES), lambda i: (i, 0)),
                  pl.BlockSpec((rows, d_in), lambda i: (i, 0)),
                  pl.BlockSpec((rows, d_bc), lambda i: (i, d_in // d_bc)),
                  st_spec,
                  const((SSD_CONV, d_in)), const((SSD_CONV, d_bc)), const((1, d_in)), const((1, d_bc)),
                  const((1, LANES)), const((1, LANES)), const((1, d_in)), const((1, d_in)),
                  const((LANES, d_in))],
        out_specs=(pl.BlockSpec((rows, d_in), lambda i: (i, 0)), st_spec),
        compiler_params=_cparams(1), name="ssd_sample")(
            p0, p0, p0, dtr, epad, epad, state, wx, wbc, bx, bbc, dtb, alog, dskip, gn, eexp)


def _perm_matrix(dil):
    i = jnp.arange(PERM_CHUNK)
    blk = BF16_ROWS * dil
    src = (i // blk) * blk + (i % BF16_ROWS) * dil + (i % blk) // BF16_ROWS
    return (src[:, None] == jnp.arange(PERM_CHUNK)[None, :]).astype(BF16)


def _band_kernel(*refs, dil, max_dist, use_sink, want_lse, want_krot, nkv, nj):
    it = iter(refs)
    q_ref, k_ref, v_ref, cos_ref, sa_ref, sb_ref = (next(it) for _ in range(6))
    p_ref, pt_ref = (next(it), next(it)) if dil > 1 else (None, None)
    sink_ref = next(it) if use_sink else None
    o_ref = next(it)
    lse_ref = next(it) if want_lse else None
    krot_ref = next(it) if want_krot else None
    qlo_s, qhi_s, kd_s, vd_s, kdp_s, vdp_s, op_s = (next(it) for _ in range(7))
    lsep_s = next(it) if want_lse else None
    lse_s = next(it) if (want_lse and nj > 1) else None

    c = pl.program_id(0)
    j = pl.program_id(1)
    rows, wq = q_ref.shape
    wk = k_ref.shape[1]
    n_chunk = rows // PERM_CHUNK if dil > 1 else 0
    n_grp = BAND // BF16_ROWS

    def perm(x):
        if dil == 1:
            return x
        x16 = x.astype(BF16)
        return jnp.concatenate(
            [jnp.dot(p_ref[...], x16[k * PERM_CHUNK:(k + 1) * PERM_CHUNK], preferred_element_type=F32)
             for k in range(n_chunk)], axis=0)

    def unperm(xp16):
        if dil == 1:
            return [xp16]
        return [jnp.dot(pt_ref[...], xp16[k * PERM_CHUNK:(k + 1) * PERM_CHUNK], preferred_element_type=F32)
                for k in range(n_chunk)]

    tabs = (cos_ref[...], sa_ref[...], sb_ref[...])
    q = perm(_rope(q_ref[...], *tabs) * (ATT_SCALE * LOG2E))
    lowq = _low_half((rows, wq))
    qlo_s[...] = jnp.where(lowq, q, 0.0).astype(BF16)
    qhi_s[...] = jnp.where(lowq, 0.0, q).astype(BF16)
    kr = _rope(k_ref[...], *tabs)
    if want_krot:
        krot_ref[...] = kr
    lowk = _low_half((rows, LANES))
    for x, d_s in ((perm(kr), kd_s), (perm(v_ref[...]), vd_s)):
        up = pltpu.roll(x, HEAD_DIM, 1)
        dn = pltpu.roll(x, wk - HEAD_DIM, 1)
        for t in range(wk // LANES):
            sl = slice(t * LANES, (t + 1) * LANES)
            d_s[2 * t] = jnp.where(lowk, x[:, sl], up[:, sl]).astype(BF16)
            d_s[2 * t + 1] = jnp.where(lowk, dn[:, sl], x[:, sl]).astype(BF16)

    @pl.when(c == 0)
    def _():
        kdp_s[j] = jnp.zeros(kdp_s.shape[1:], BF16)
        vdp_s[j] = jnp.zeros(vdp_s.shape[1:], BF16)

    qi = lax.broadcasted_iota(jnp.int32, (BAND, 2 * BAND), 0)
    kj = lax.broadcasted_iota(jnp.int32, (BAND, 2 * BAND), 1)
    dist = qi - kj + BAND
    bias = jnp.where((dist >= 0) & (dist <= max_dist) & ((kj >= BAND) | (c > 0)), 0.0, NEG)
    col0 = kj == 0
    ones_blk = jnp.ones((2 * BAND, LANES), BF16)
    low = _low_half((BAND, LANES))
    lane = lax.broadcasted_iota(jnp.int32, (BAND, LANES), 1)
    step = BF16_ROWS * dil

    def residue(rho):
        starts = [b * step + rho * BF16_ROWS for b in range(n_grp)]

        def gather(get):
            return jnp.concatenate([get(s) for s in starts], axis=0)

        tiles = []
        lse_acc = jnp.zeros((BAND, LANES), F32)
        for g in range(nkv):
            ca, cb = slice(2 * g * LANES, (2 * g + 1) * LANES), slice((2 * g + 1) * LANES, (2 * g + 2) * LANES)
            lhs = jnp.concatenate([gather(lambda s: qlo_s[pl.ds(s, BF16_ROWS), ca]),
                                   gather(lambda s: qhi_s[pl.ds(s, BF16_ROWS), ca]),
                                   gather(lambda s: qlo_s[pl.ds(s, BF16_ROWS), cb]),
                                   gather(lambda s: qhi_s[pl.ds(s, BF16_ROWS), cb])], axis=0)
            kcat = jnp.concatenate([gather(lambda s: kdp_s[j, g, pl.ds(s, BF16_ROWS), :]),
                                    gather(lambda s: kd_s[g, pl.ds(s, BF16_ROWS), :])], axis=0)
            vcat = jnp.concatenate([gather(lambda s: vdp_s[j, g, pl.ds(s, BF16_ROWS), :]),
                                    gather(lambda s: vd_s[g, pl.ds(s, BF16_ROWS), :])], axis=0)
            head0 = j * REP * nkv + g * REP
            s_full = lax.dot_general(lhs, kcat, _NT, preferred_element_type=F32)
            rsl = [slice(r * BAND, (r + 1) * BAND) for r in range(REP)]
            if use_sink:
                sc = jnp.concatenate([jnp.where(col0, sink_ref[head0 + r] * LOG2E, s_full[rsl[r]] + bias)
                                      for r in range(REP)], axis=0)
            else:
                sc = s_full + jnp.concatenate([bias] * REP, axis=0)
            m = jnp.max(sc, axis=-1, keepdims=True)
            e = jnp.exp2(sc - m)
            od = jnp.dot(e.astype(BF16), jnp.concatenate([vcat, ones_blk], axis=1), preferred_element_type=F32)
            den = od[:, LANES:]
            o = od[:, :LANES] * (1.0 / den)
            tiles.append(jnp.where(low, o[0:BAND], o[BAND:2 * BAND]))
            tiles.append(jnp.where(low, o[2 * BAND:3 * BAND], o[3 * BAND:4 * BAND]))
            if want_lse:
                lse = (m + jnp.log2(den)) * LN2
                for r in range(REP):
                    lse_acc = lse_acc + jnp.where(lane == head0 + r, lse[rsl[r]], 0.0)
        o_res = jnp.concatenate(tiles, axis=1)
        for b, s0 in enumerate(starts):
            op_s[pl.ds(s0, BF16_ROWS), :] = o_res[b * BF16_ROWS:(b + 1) * BF16_ROWS]
            if want_lse:
                lsep_s[pl.ds(s0, BF16_ROWS), :] = lse_acc[b * BF16_ROWS:(b + 1) * BF16_ROWS]

    for rho in range(dil):
        residue(rho)

    for k, blk in enumerate(unperm(op_s[...].astype(BF16))):
        if dil == 1:
            o_ref[...] = blk
        else:
            o_ref[k * PERM_CHUNK:(k + 1) * PERM_CHUNK, :] = blk.astype(o_ref.dtype)
    if want_lse:
        if dil == 1:
            lse_nat = lsep_s[...]
        else:
            parts = [unperm(x) for x in _split3(lsep_s[...])]
            lse_nat = jnp.concatenate([a + b + d for a, b, d in zip(*parts)], axis=0)
        if nj == 1:
            lse_ref[...] = lse_nat
        else:
            @pl.when(j == 0)
            def _():
                lse_s[...] = lse_nat

            @pl.when(j > 0)
            def _():
                lse_s[...] = lse_s[...] + lse_nat

            @pl.when(j == nj - 1)
            def _():
                lse_ref[...] = lse_s[...]
    kdp_s[j] = kd_s[...]
    if use_sink:
        keep = (lax.broadcasted_iota(jnp.int32, (rows, LANES), 0) > 0).astype(F32).astype(BF16)
        vdp_s[j] = vd_s[...] * keep
    else:
        vdp_s[j] = vd_s[...]


def _band_attention(p, seq, q_off, k_off, v_off, tables, sinks, *, dil, max_dist, nkv, want_lse, want_krot, name):
    rows = BAND * dil
    nj = N_KV // nkv
    qw, kw = nkv * REP * HEAD_DIM, nkv * HEAD_DIM
    assert seq % rows == 0 and kw % LANES == 0 and q_off % qw == 0 and k_off % kw == 0 and v_off % kw == 0
    assert dil == 1 or (rows % PERM_CHUNK == 0 and PERM_CHUNK % (BF16_ROWS * dil) == 0)
    use_sink = sinks is not None
    d_att = N_HEADS * HEAD_DIM
    assert not use_sink or (dil == 1 and max_dist < BAND)
    blk = lambda width, cb: pl.BlockSpec((rows, width), lambda c, j: (c, cb + j))
    t_spec = pl.BlockSpec((rows, LANES), lambda c, j: (c, 0))
    in_specs = [blk(qw, q_off // qw), blk(kw, k_off // kw), blk(kw, v_off // kw), t_spec, t_spec, t_spec]
    args = [p, p, p] + list(tables)
    if dil > 1:
        pm = _perm_matrix(dil)
        in_specs += [pl.BlockSpec((PERM_CHUNK, PERM_CHUNK), lambda c, j: (0, 0))] * 2
        args += [pm, pm.T]
    if use_sink:
        in_specs.append(pl.BlockSpec(memory_space=pltpu.SMEM))
        args.append(sinks)
    out_shape = [jax.ShapeDtypeStruct((seq, d_att), BF16)]
    out_specs = [pl.BlockSpec((rows, qw), lambda c, j: (c, j))]
    scratch = [pltpu.VMEM((rows, qw), BF16), pltpu.VMEM((rows, qw), BF16),
               pltpu.VMEM((nkv, rows, LANES), BF16), pltpu.VMEM((nkv, rows, LANES), BF16),
               pltpu.VMEM((nj, nkv, rows, LANES), BF16), pltpu.VMEM((nj, nkv, rows, LANES), BF16),
               pltpu.VMEM((rows, qw), F32)]
    if want_lse:
        out_shape.append(jax.ShapeDtypeStruct((seq, LANES), F32))
        out_specs.append(pl.BlockSpec((rows, LANES), lambda c, j: (c, 0)))
        scratch.append(pltpu.VMEM((rows, LANES), F32))
        if nj > 1:
            scratch.append(pltpu.VMEM((rows, LANES), F32))
    if want_krot:
        out_shape.append(jax.ShapeDtypeStruct((seq, N_KV * HEAD_DIM), F32))
        out_specs.append(pl.BlockSpec((rows, kw), lambda c, j: (c, j)))
    return pl.pallas_call(
        functools.partial(_band_kernel, dil=dil, max_dist=max_dist, use_sink=use_sink, want_lse=want_lse,
                          want_krot=want_krot, nkv=nkv, nj=nj),
        out_shape=tuple(out_shape), grid=(seq // rows, nj), in_specs=in_specs, out_specs=tuple(out_specs),
        scratch_shapes=scratch, compiler_params=_cparams(2), name=name)(*args)


def _merge_kernel(o1_ref, o2_ref, o3_ref, l1_ref, l2_ref, l3_ref, e_ref, out_ref):
    ls = [l1_ref[...], l2_ref[...], l3_ref[...]]
    mx = jnp.maximum(jnp.maximum(ls[0], ls[1]), ls[2])
    es = [jnp.exp(l - mx) for l in ls]
    inv = 1.0 / (es[0] + es[1] + es[2])
    e = e_ref[...]
    acc = None
    for en, o_ref in zip(es, (o1_ref, o2_ref, o3_ref)):
        wf = jnp.dot((en * inv).astype(BF16), e, preferred_element_type=F32)
        term = wf * o_ref[...].astype(F32)
        acc = term if acc is None else acc + term
    out_ref[...] = acc.astype(out_ref.dtype)


def _merge(os_, ls_, eexp, *, tm):
    seq, d = os_[0].shape
    o_spec = pl.BlockSpec((tm, d), lambda i: (i, 0))
    l_spec = pl.BlockSpec((tm, LANES), lambda i: (i, 0))
    return pl.pallas_call(
        _merge_kernel, out_shape=jax.ShapeDtypeStruct((seq, d), BF16), grid=(seq // tm,),
        in_specs=[o_spec] * 3 + [l_spec] * 3 + [pl.BlockSpec((LANES, d), lambda i: (0, 0))],
        out_specs=o_spec, compiler_params=_cparams(1), name="dil_merge")(*os_, *ls_, eexp)


def _expand_q(qb, t_len):
    gd = N_KV * HEAD_DIM
    sub = lax.broadcasted_iota(jnp.int32, (N_KV, gd), 0)
    lg = lax.broadcasted_iota(jnp.int32, (N_KV, gd), 1) >> _LOG2_HEAD_DIM
    diag = sub == lg
    tiles = []
    for r in range(REP):
        vr = jnp.concatenate(
            [qb[:, (g * REP + r) * HEAD_DIM:(g * REP + r + 1) * HEAD_DIM] for g in range(N_KV)], axis=1)
        for t in range(t_len):
            tiles.append(jnp.where(diag, jnp.broadcast_to(vr[t:t + 1], (N_KV, gd)), 0.0))
    return jnp.concatenate(tiles, axis=0).astype(BF16)


def _collapse_o(r_full, t_len):
    gd = N_KV * HEAD_DIM
    n = r_full.shape[0]
    sub = lax.broadcasted_iota(jnp.int32, (n, gd), 0) & (N_KV - 1)
    lg = lax.broadcasted_iota(jnp.int32, (n, gd), 1) >> _LOG2_HEAD_DIM
    masked = jnp.where(sub == lg, r_full, 0.0)
    red = jnp.sum(masked.reshape(n // N_KV, N_KV, gd), axis=1)
    pieces = []
    for g in range(N_KV):
        for r in range(REP):
            pieces.append(red[r * t_len:(r + 1) * t_len, g * HEAD_DIM:(g + 1) * HEAD_DIM])
    return jnp.concatenate(pieces, axis=1)


def _row_t(shape, t_len):
    r = lax.broadcasted_iota(jnp.int32, shape, 0)
    return (r >> 3) & (t_len - 1)


def _cached_attend(qx, kt, vt, mask_w, kn, vn, mask_n, sink):
    s_w = jnp.where(mask_w, jnp.dot(qx.astype(kt.dtype), kt, preferred_element_type=F32), NEG)
    s_n = jnp.where(mask_n, lax.dot_general(qx, kn, _NT, preferred_element_type=F32), NEG)
    m = jnp.maximum(jnp.max(s_w, axis=-1, keepdims=True), jnp.max(s_n, axis=-1, keepdims=True))
    if sink is not None:
        m = jnp.maximum(m, sink)
    e_w = jnp.exp(s_w - m)
    e_n = jnp.exp(s_n - m)
    den = jnp.sum(e_w, axis=-1, keepdims=True) + jnp.sum(e_n, axis=-1, keepdims=True)
    if sink is not None:
        den = den + jnp.exp(sink - m)
    r = (lax.dot_general(e_w.astype(vt.dtype), vt, _NT, preferred_element_type=F32)
         + jnp.dot(e_n.astype(BF16), vn, preferred_element_type=F32))
    return r / den, m + jnp.log(den)


def _swa_cached_kernel(q_ref, kn_ref, vn_ref, cos_ref, sa_ref, sb_ref, ck_ref, cv_ref, sink_ref,
                       o_ref, krot_ref, *, nb, t_len):
    tabs = (cos_ref[...], sa_ref[...], sb_ref[...])
    q = _rope(q_ref[...], *tabs) * ATT_SCALE
    kn = _rope(kn_ref[...], *tabs)
    krot_ref[...] = kn
    kn16 = kn.astype(BF16)
    vn16 = vn_ref[...].astype(BF16)
    lb = ck_ref.shape[3]
    gd = N_KV * HEAD_DIM
    nrow = REP * t_len * N_KV
    tq = _row_t((nrow, lb), t_len)
    mask_w = lax.broadcasted_iota(jnp.int32, (nrow, lb), 1) >= tq + 1
    mask_n = lax.broadcasted_iota(jnp.int32, (nrow, BF16_ROWS), 1) <= _row_t((nrow, BF16_ROWS), t_len)
    sk = sink_ref[...]
    pad = jnp.zeros((BF16_ROWS - t_len, gd), BF16)
    outs = []
    for b in range(nb):
        rs = slice(b * t_len, (b + 1) * t_len)
        r_full, _ = _cached_attend(
            _expand_q(q[rs], t_len), ck_ref[b].reshape(gd, lb).astype(BF16), cv_ref[b].reshape(gd, lb).astype(BF16),
            mask_w, jnp.concatenate([kn16[rs], pad], axis=0), jnp.concatenate([vn16[rs], pad], axis=0), mask_n, sk)
        outs.append(_collapse_o(r_full, t_len))
    o_ref[...] = jnp.concatenate(outs, axis=0).astype(o_ref.dtype)


def _swa_cached(p0, off, seq, n_b, t_len, tables, cache_kt, cache_vt, sink_col, *, nb):
    rows = nb * t_len
    d_att = N_HEADS * HEAD_DIM
    d_kv = N_KV * HEAD_DIM
    lb = cache_kt.shape[3]
    assert lb == SWA_WINDOW and t_len <= BF16_ROWS
    r0 = seq // rows
    nrow = REP * t_len * N_KV
    c_spec = pl.BlockSpec((nb, N_KV, HEAD_DIM, lb), lambda i: (i, 0, 0, 0))
    t_spec = pl.BlockSpec((rows, LANES), lambda i: (r0 + i, 0))
    return pl.pallas_call(
        functools.partial(_swa_cached_kernel, nb=nb, t_len=t_len),
        out_shape=(jax.ShapeDtypeStruct((n_b * t_len, d_att), BF16),
                   jax.ShapeDtypeStruct((n_b * t_len, d_kv), F32)),
        grid=(n_b // nb,),
        in_specs=[pl.BlockSpec((rows, d_att), lambda i: (i, off["q"] // d_att)),
                  pl.BlockSpec((rows, d_kv), lambda i: (i, off["k"] // d_kv)),
                  pl.BlockSpec((rows, d_kv), lambda i: (i, off["v"] // d_kv)),
                  t_spec, t_spec, t_spec, c_spec, c_spec,
                  pl.BlockSpec((nrow, 1), lambda i: (0, 0))],
        out_specs=(pl.BlockSpec((rows, d_att), lambda i: (i, 0)),
                   pl.BlockSpec((rows, d_kv), lambda i: (i, 0))),
        compiler_params=_cparams(1), name="swa_cached")(p0, p0, p0, *tables, cache_kt, cache_vt, sink_col)


def _dil_cached_kernel(q1_ref, q2_ref, q3_ref, kn_ref, vn_ref, cos_ref, sa_ref, sb_ref, ck_ref, cv_ref,
                       o_ref, krot_ref, *, nb, t_len):
    tabs = (cos_ref[...], sa_ref[...], sb_ref[...])
    qs = [_rope(r[...], *tabs) * ATT_SCALE for r in (q1_ref, q2_ref, q3_ref)]
    kn = _rope(kn_ref[...], *tabs)
    krot_ref[...] = kn
    kn16 = kn.astype(BF16)
    vn16 = vn_ref[...].astype(BF16)
    lbuf = ck_ref.shape[3]
    gd = N_KV * HEAD_DIM
    nrow = REP * t_len * N_KV
    tqn = _row_t((nrow, BF16_ROWS), t_len)
    coln = lax.broadcasted_iota(jnp.int32, (nrow, BF16_ROWS), 1)
    masks = []
    for window, dil in DIL_PATTERNS:
        tq = _row_t((nrow, window), t_len)
        col = lax.broadcasted_iota(jnp.int32, (nrow, window), 1)
        if dil == 1:
            masks.append((col >= tq, coln <= tqn))
        else:
            masks.append(((col & (dil - 1)) == tq, coln == tqn))
    pad = jnp.zeros((BF16_ROWS - t_len, gd), BF16)
    outs = []
    for b in range(nb):
        rs = slice(b * t_len, (b + 1) * t_len)
        kt = ck_ref[b].reshape(gd, lbuf)
        vt = cv_ref[b].reshape(gd, lbuf)
        knp = jnp.concatenate([kn16[rs], pad], axis=0)
        vnp = jnp.concatenate([vn16[rs], pad], axis=0)
        res = []
        for gi, (window, dil) in enumerate(DIL_PATTERNS):
            res.append(_cached_attend(_expand_q(qs[gi][rs], t_len), kt[:, lbuf - window:], vt[:, lbuf - window:],
                                      masks[gi][0], knp, vnp, masks[gi][1], None))
        mx = jnp.maximum(jnp.maximum(res[0][1], res[1][1]), res[2][1])
        ws = [jnp.exp(l - mx) for _, l in res]
        inv = 1.0 / (ws[0] + ws[1] + ws[2])
        merged = (ws[0] * inv) * res[0][0] + (ws[1] * inv) * res[1][0] + (ws[2] * inv) * res[2][0]
        outs.append(_collapse_o(merged, t_len))
    o_ref[...] = jnp.concatenate(outs, axis=0).astype(o_ref.dtype)


def _dil_cached(p1, seq, n_b, t_len, tables, cache_kt, cache_vt, *, nb):
    rows = nb * t_len
    d_att = N_HEADS * HEAD_DIM
    d_kv = N_KV * HEAD_DIM
    lbuf = cache_kt.shape[3]
    assert rows % 8 == 0 and seq % rows == 0 and t_len <= BF16_ROWS
    for window, dil in DIL_PATTERNS:
        assert window <= lbuf and (PAST_LEN - lbuf) % dil == 0 and lbuf % dil == 0 and (dil == 1 or dil >= t_len)
    r0 = seq // rows
    c_spec = pl.BlockSpec((nb, N_KV, HEAD_DIM, lbuf), lambda i: (i, 0, 0, 0))
    t_spec = pl.BlockSpec((rows, LANES), lambda i: (r0 + i, 0))
    nq = d_att * len(DIL_PATTERNS)
    return pl.pallas_call(
        functools.partial(_dil_cached_kernel, nb=nb, t_len=t_len),
        out_shape=(jax.ShapeDtypeStruct((n_b * t_len, d_att), F32),
                   jax.ShapeDtypeStruct((n_b * t_len, d_kv), F32)),
        grid=(n_b // nb,),
        in_specs=[pl.BlockSpec((rows, d_att), lambda i: (i, 0)),
                  pl.BlockSpec((rows, d_att), lambda i: (i, 1)),
                  pl.BlockSpec((rows, d_att), lambda i: (i, 2)),
                  pl.BlockSpec((rows, d_kv), lambda i: (i, nq // d_kv)),
                  pl.BlockSpec((rows, d_kv), lambda i: (i, nq // d_kv + 1)),
                  t_spec, t_spec, t_spec, c_spec, c_spec],
        out_specs=(pl.BlockSpec((rows, d_att), lambda i: (i, 0)),
                   pl.BlockSpec((rows, d_kv), lambda i: (i, 0))),
        compiler_params=_cparams(1), name="dil_cached")(p1, p1, p1, p1, p1, *tables, cache_kt, cache_vt)


def _row_tile(m, pref):
    for t in (1024, 512, 256, 128, 64, 32, 16, 8):
        if t <= pref and m % t == 0:
            return t
    raise ValueError(m)


def kernel(x_prompt, x_sample, state_conv, state_ssm, cache_swa_k, cache_swa_v, cache_dil_k, cache_dil_v, norm_mix, norm_mlp, e_w_in, e_conv_w, e_conv_b, e_dt_bias, e_a_log, e_d_skip, e_gate_norm, e_sinks, e_w_out, o_w_in, o_w_out, mlp_w1, mlp_w2, norm_final):
    nbp, seq, d = x_prompt.shape
    n_b, t_len, _ = x_sample.shape
    assert nbp == 1 and d == N_HEADS * HEAD_DIM and norm_mix.shape[0] == 2
    ms = n_b * t_len
    m = seq + ms
    tms = (_row_tile(seq, 1024), _row_tile(ms, 512))
    d_kv = N_KV * HEAD_DIM
    d_bc = 2 * SSD_GROUPS * SSD_STATE
    conv_dim = d + d_bc
    both = lambda fn, *rows, **kw: tuple(
        fn(*(r[s] for r in rows), tm=tms[s], name=kw["name"] + ("_p", "_s")[s], **{k: v for k, v in kw.items() if k != "name"})
        for s in range(2))

    h = (x_prompt.reshape(seq, d), x_sample.reshape(ms, d))
    tables = _rope_tables(m, seq, t_len, _row_tile(math.gcd(seq, ms), 512))
    eexp = (lax.broadcasted_iota(jnp.int32, (LANES, d), 1) // HEAD_DIM
            == lax.broadcasted_iota(jnp.int32, (LANES, d), 0)).astype(BF16)
    keys_minor = lambda c: jnp.transpose(c[0], (0, 2, 3, 1))

    wi = e_w_in[0]
    c_z, c_xbc, c_dt, c_q = 0, d, d + conv_dim, d + conv_dim + N_HEADS
    c_k, c_v = c_q + d, c_q + d + d_kv
    tn0 = 1024
    wi16 = wi.astype(BF16)
    w0_q, w0_kv = wi16[:, c_q:c_q + d], wi16[:, c_k:c_v + d_kv]
    w0_dt = jnp.pad(wi16[:, c_dt:c_dt + N_HEADS], ((0, 0), (0, LANES - N_HEADS)))
    assert c_z == 0 and (c_xbc + d) % tn0 == 0 and d_bc % tn0 == 0 and d % tn0 == 0
    segs0 = [(wi16, 0, 2 * d // tn0), (w0_q, 0, d // tn0), (wi16, (c_xbc + d) // tn0, d_bc // tn0),
             (w0_kv, 0, 2 * d_kv // tn0)]
    off0 = {"z": 0, "xs": d, "q": 2 * d, "bc": 3 * d, "k": 3 * d + d_bc, "v": 3 * d + d_bc + d_kv}
    g0 = norm_mix[0]
    (p0_p, dt_p), (p0_s, dt_s) = both(
        lambda x, tm, name: _norm_matmul_multi(x, g0, segs0, w0_dt, tm=tm, tn=tn0, name=name), h, name="l0_in_proj")

    cw, cb = e_conv_w[0], e_conv_b[0]
    wx, wbc = cw[:, :d], cw[:, d:]
    bx, bbc = cb[:d].reshape(1, d), cb[d:].reshape(1, d_bc)
    pad_h = lambda v: jnp.pad(v.reshape(1, N_HEADS), ((0, 0), (0, LANES - N_HEADS)))
    dtb, alog = pad_h(e_dt_bias[0]), pad_h(e_a_log[0])
    dskip = jnp.repeat(e_d_skip[0], HEAD_DIM).reshape(1, d)
    gn = e_gate_norm[0].reshape(1, d)

    ssd_p, tail_x, tail_bc, p_ssm = _ssd_prompt(p0_p, dt_p, off0, seq, wx, wbc, bx, bbc, dtb, alog, dskip, gn, eexp)
    epad = jnp.pad(state_conv[0], ((0, 0), (0, 1), (0, 0))).reshape(ms, conv_dim)
    ssd_s, s_ssm = _ssd_sample(p0_s, dt_s, off0, seq, n_b, t_len, epad, state_ssm[0], wx, wbc, bx, bbc, dtb, alog,
                               dskip, gn, eexp, nb=8)

    att_p, krot0_p = _band_attention(p0_p, seq, off0["q"], off0["k"], off0["v"], tables, e_sinks[0], dil=1,
                                     max_dist=SWA_WINDOW - 1, nkv=N_KV, want_lse=False, want_krot=True,
                                     name="swa_prompt")
    sink_col = jnp.broadcast_to(e_sinks[0].reshape(N_KV, REP).T[:, None, :], (REP, t_len, N_KV)).reshape(-1, 1)
    att_s, krot0_s = _swa_cached(p0_s, off0, seq, n_b, t_len, tables, keys_minor(cache_swa_k),
                                 keys_minor(cache_swa_v), sink_col, nb=8)

    def mlp(hh, layer, g_out=None):
        g = norm_mlp[layer]
        o_s, w1_16, w2_16 = _mlp(hh[1], g, mlp_w1, mlp_w2, g_out, tm=ms, tk=512, name=f"l{layer}_mlp_s", layer=layer)
        o_p = _mlp(hh[0], g, w1_16, w2_16, g_out, tm=tms[0], tk=1024, name=f"l{layer}_mlp_p")
        return o_p, o_s

    def out_proj(xs_p, xs_s, w, hh, name):
        h_s, *w16 = _matmul_resid(xs_s, w, hh[1], tm=ms, tn=512, name=name + "_s", emit=True)
        return _matmul_resid(xs_p, w16, hh[0], tm=tms[0], tn=1024 if len(xs_p) > 1 else 2048, name=name + "_p"), h_s

    h = out_proj([ssd_p, att_p], [ssd_s, att_s], e_w_out[0], h, "l0_out_proj")
    h = mlp(h, 0)

    n_pat = len(DIL_PATTERNS)
    p1_s, w1i = _norm_matmul(h[1], norm_mix[1], o_w_in[0], tm=ms, tn=1024, name="l1_in_proj_s", emit=True)
    p1_p = _norm_matmul(h[0], norm_mix[1], w1i, tm=tms[0], tn=1792, name="l1_in_proj_p")
    k_off, v_off = n_pat * d, n_pat * d + d_kv
    os_, ls_ = [], []
    krot1_p = None
    for gi, (window, dil) in enumerate(DIL_PATTERNS):
        res = _band_attention(p1_p, seq, gi * d, k_off, v_off, tables, None, dil=dil, max_dist=window // dil,
                              nkv=(2 if BAND * dil * N_HEADS * HEAD_DIM * 4 > 8 * 2 ** 20 else N_KV),
                              want_lse=True, want_krot=(gi == 0), name=f"dil_prompt_{dil}")
        os_.append(res[0])
        ls_.append(res[1])
        if gi == 0:
            krot1_p = res[2]
    merged_p = _merge(os_, ls_, eexp, tm=_row_tile(seq, 512))
    dil_s, krot1_s = _dil_cached(p1_s, seq, n_b, t_len, tables, keys_minor(cache_dil_k), keys_minor(cache_dil_v),
                                 nb=2)
    h = out_proj([merged_p], [dil_s.astype(BF16)], o_w_out[0], h, "l1_out_proj")
    y_p, y_s = mlp(h, 1, norm_final)

    keep_swa = min(SWA_WINDOW, seq)
    keep_dil = min(max(w for w, _ in DIL_PATTERNS), seq)
    kv4 = lambda a: a.reshape(a.shape[0], N_KV, HEAD_DIM)
    y_prompt = y_p.reshape(1, seq, d)
    y_sample = y_s.reshape(n_b, t_len, d)
    p_conv = jnp.concatenate([tail_x[8 - (SSD_CONV - 1):], tail_bc[8 - (SSD_CONV - 1):]], axis=1)[None, None]
    p_swa_k = kv4(krot0_p[seq - keep_swa:])[None, None]
    p_swa_v = kv4(p0_p[seq - keep_swa:, off0["v"]:off0["v"] + d_kv])[None, None]
    p_dil_k = kv4(krot1_p[seq - keep_dil:])[None, None]
    p_dil_v = kv4(p1_p[seq - keep_dil:, v_off:v_off + d_kv])[None, None]
    xbc_s = jnp.concatenate([p0_s[:, off0["xs"]:off0["xs"] + d], p0_s[:, off0["bc"]:off0["bc"] + d_bc]], axis=1)
    s_conv = xbc_s.reshape(n_b, t_len, conv_dim)[:, t_len - (SSD_CONV - 1):][None]
    bt = lambda a: a.reshape(n_b, t_len, N_KV, HEAD_DIM)[None]
    s_swa_k = bt(krot0_s)
    s_swa_v = bt(p0_s[:, off0["v"]:off0["v"] + d_kv])
    s_dil_k = bt(krot1_s)
    s_dil_v = bt(p1_s[:, v_off:v_off + d_kv])
    return (y_prompt, y_sample, p_conv, p_ssm[None, None], p_swa_k, p_swa_v, p_dil_k, p_dil_v,
            s_conv, s_ssm[None], s_swa_k, s_swa_v, s_dil_k, s_dil_v)
```

```python
import functools
import math

import jax
import jax.numpy as jnp
from jax import lax
from jax.experimental import pallas as pl
from jax.experimental.pallas import tpu as pltpu

F32 = jnp.float32
BF16 = jnp.bfloat16

NORM_EPS = 1e-5
HEAD_DIM = 64
ROT_HALF = 8
ROPE_THETA = 500000.0
PAST_LEN = 8192
N_HEADS = 32
N_KV = 8
REP = N_HEADS // N_KV
SSD_GROUPS = 4
SSD_STATE = 128
SSD_CONV = 4
SSD_CHUNK = 128
SWA_WINDOW = 128
DIL_PATTERNS = ((128, 1), (512, 4), (2048, 16))
BAND = 128
ATT_SCALE = HEAD_DIM ** -0.5
LOG2E = math.log2(math.e)
LN2 = math.log(2.0)
NEG = -1e30
_LOG2_HEAD_DIM = 6

LANES = 128
BF16_ROWS = 16
PERM_CHUNK = 256
VMEM_LIMIT_BYTES = 60 * 1024 * 1024

_NT = (((1,), (1,)), ((), ()))
_TN = (((0,), (0,)), ((), ()))


def _cparams(n_axes):
    return pltpu.CompilerParams(dimension_semantics=("arbitrary",) * n_axes,
                                vmem_limit_bytes=VMEM_LIMIT_BYTES)


def _silu(x):
    return x * jax.nn.sigmoid(x)


def _tile_lanes(t, width):
    k = width // t.shape[1]
    return t if k == 1 else jnp.concatenate([t] * k, axis=1)


def _rope(x, cos_t, sa_t, sb_t):
    w = x.shape[1]
    return (x * _tile_lanes(cos_t, w)
            + pltpu.roll(x, w - ROT_HALF, 1) * _tile_lanes(sa_t, w)
            + pltpu.roll(x, ROT_HALF, 1) * _tile_lanes(sb_t, w))


def _low_half(shape):
    return (lax.broadcasted_iota(jnp.int32, shape, 1) & (LANES - 1)) < HEAD_DIM


def _rope_table_kernel(cos_ref, sa_ref, sb_ref, *, tm, seq, dec_seq):
    i = pl.program_id(0)
    row = i * tm + lax.broadcasted_iota(jnp.int32, (tm, LANES), 0)
    lane = lax.broadcasted_iota(jnp.int32, (tm, LANES), 1)
    pos = jnp.where(row < seq, row, PAST_LEN + ((row - seq) & (dec_seq - 1)))
    c = lane & (HEAD_DIM - 1)
    f = (c & (ROT_HALF - 1)).astype(F32)
    inv_freq = jnp.exp(f * (-math.log(ROPE_THETA) / ROT_HALF))
    ang = pos.astype(F32) * inv_freq
    cs = jnp.cos(ang)
    sn = jnp.sin(ang)
    cos_ref[...] = jnp.where(c < 2 * ROT_HALF, cs, 1.0)
    sa_ref[...] = jnp.where(c < ROT_HALF, -sn, 0.0)
    sb_ref[...] = jnp.where((c >= ROT_HALF) & (c < 2 * ROT_HALF), sn, 0.0)


def _rope_tables(m, seq, dec_seq, tm):
    assert dec_seq & (dec_seq - 1) == 0 and m % tm == 0
    shp = jax.ShapeDtypeStruct((m, LANES), F32)
    spec = pl.BlockSpec((tm, LANES), lambda i: (i, 0))
    return pl.pallas_call(
        functools.partial(_rope_table_kernel, tm=tm, seq=seq, dec_seq=dec_seq),
        out_shape=(shp, shp, shp), grid=(m // tm,), out_specs=(spec, spec, spec),
        compiler_params=_cparams(1), name="rope_tables")()


def _rmsnorm_bf16(x, g):
    ms = jnp.mean(x * x, axis=-1, keepdims=True)
    return (x * lax.rsqrt(ms + NORM_EPS) * g).astype(BF16)


def _norm_mm_kernel(x_ref, g_ref, w_ref, o_ref, *rest, emit):
    xn_ref = rest[-1]

    @pl.when(pl.program_id(1) == 0)
    def _():
        xn_ref[...] = _rmsnorm_bf16(x_ref[...], g_ref[...])

    w = w_ref[...]
    if emit:
        w = w.astype(BF16)
        rest[0][...] = w
    o_ref[...] = jnp.dot(xn_ref[...], w, preferred_element_type=F32)


def _norm_matmul(x, g, w, *, tm, tn, name, emit=False):
    m, k = x.shape
    n = w.shape[1]
    assert m % tm == 0 and n % tn == 0 and w.shape[0] == k and (not emit or m == tm)
    w_spec = pl.BlockSpec((k, tn), lambda i, j: (0, j))
    o_spec = pl.BlockSpec((tm, tn), lambda i, j: (i, j))
    o_shape = jax.ShapeDtypeStruct((m, n), F32)
    return pl.pallas_call(
        functools.partial(_norm_mm_kernel, emit=emit),
        out_shape=(o_shape, jax.ShapeDtypeStruct((k, n), BF16)) if emit else o_shape,
        grid=(m // tm, n // tn),
        in_specs=[pl.BlockSpec((tm, k), lambda i, j: (i, 0)),
                  pl.BlockSpec((1, k), lambda i, j: (0, 0)),
                  w_spec],
        out_specs=(o_spec, w_spec) if emit else o_spec,
        scratch_shapes=[pltpu.VMEM((tm, k), BF16)],
        compiler_params=_cparams(2), name=name)(x, g.reshape(1, k), w)


def _norm_mm_multi_kernel(*refs, owners):
    n_w = len(owners)
    x_ref, g_ref = refs[:2]
    w_refs = refs[2:2 + n_w]
    ws_ref, o_ref, os_ref, xn_ref = refs[2 + n_w:]
    j = pl.program_id(1)

    @pl.when(j == 0)
    def _():
        xn_ref[...] = _rmsnorm_bf16(x_ref[...], g_ref[...])
        os_ref[...] = jnp.dot(xn_ref[...], ws_ref[...], preferred_element_type=F32)

    for w_ref, tiles in zip(w_refs, owners):
        own = functools.reduce(jnp.logical_or, [(j >= lo) & (j < hi) for lo, hi in tiles])

        @pl.when(own)
        def _():
            o_ref[...] = jnp.dot(xn_ref[...], w_ref[...], preferred_element_type=F32)


def _norm_matmul_multi(x, g, segs, w_small, *, tm, tn, name):
    m, k = x.shape
    arrays, owners, maps = [], [], []
    start = 0
    for w, cb0, n in segs:
        if not any(w is a for a in arrays):
            arrays.append(w)
            owners.append([])
            maps.append([])
        idx = [i for i, a in enumerate(arrays) if a is w][0]
        owners[idx].append((start, start + n))
        maps[idx].append((start, n, cb0))
        start += n
    n_tiles = start

    def index_map(pieces):
        def col(i, j):
            c = pieces[-1][2] + pieces[-1][1] - 1
            for a, n, cb0 in reversed(pieces):
                c = jnp.where(j < a + n, cb0 + jnp.maximum(j - a, 0), c)
            return 0, c
        return col

    n_small = w_small.shape[1]
    return pl.pallas_call(
        functools.partial(_norm_mm_multi_kernel, owners=tuple(tuple(o) for o in owners)),
        out_shape=(jax.ShapeDtypeStruct((m, n_tiles * tn), F32), jax.ShapeDtypeStruct((m, n_small), F32)),
        grid=(m // tm, n_tiles),
        in_specs=[pl.BlockSpec((tm, k), lambda i, j: (i, 0)),
                  pl.BlockSpec((1, k), lambda i, j: (0, 0))]
        + [pl.BlockSpec((k, tn), index_map(p)) for p in maps]
        + [pl.BlockSpec((k, n_small), lambda i, j: (0, 0))],
        out_specs=(pl.BlockSpec((tm, tn), lambda i, j: (i, j)), pl.BlockSpec((tm, n_small), lambda i, j: (i, 0))),
        scratch_shapes=[pltpu.VMEM((tm, k), BF16)],
        compiler_params=_cparams(2), name=name)(x, g.reshape(1, k), *arrays, w_small)


def _mm_resid_kernel(*refs, n_in, emit):
    x_refs, w_refs = refs[:n_in], refs[n_in:2 * n_in]
    r_ref, o_ref = refs[2 * n_in], refs[2 * n_in + 1]
    wo_refs = refs[2 * n_in + 2:]
    acc = r_ref[...]
    for s, (x_ref, w_ref) in enumerate(zip(x_refs, w_refs)):
        w = w_ref[...]
        if emit:
            w = w.astype(BF16)
            wo_refs[s][...] = w
        acc = acc + jnp.dot(x_ref[...], w, preferred_element_type=F32)
    o_ref[...] = acc


def _matmul_resid(xs, w, resid, *, tm, tn, name, emit=False):
    m, n = resid.shape
    k = xs[0].shape[1]
    assert m % tm == 0 and n % tn == 0 and (not emit or (m == tm and w.shape == (k * len(xs), n)))
    w_args = [w] * len(xs) if emit else list(w)
    in_specs = [pl.BlockSpec((tm, k), lambda i, j: (i, 0)) for _ in xs]
    in_specs += [pl.BlockSpec((k, tn), functools.partial(lambda i, j, s: (s, j), s=s if emit else 0))
                 for s in range(len(xs))]
    in_specs.append(pl.BlockSpec((tm, tn), lambda i, j: (i, j)))
    o_spec = pl.BlockSpec((tm, tn), lambda i, j: (i, j))
    o_shape = jax.ShapeDtypeStruct((m, n), F32)
    if emit:
        out_shape = (o_shape,) + (jax.ShapeDtypeStruct((k, n), BF16),) * len(xs)
        out_specs = (o_spec,) + (pl.BlockSpec((k, tn), lambda i, j: (0, j)),) * len(xs)
    else:
        out_shape, out_specs = o_shape, o_spec
    return pl.pallas_call(
        functools.partial(_mm_resid_kernel, n_in=len(xs), emit=emit),
        out_shape=out_shape,
        grid=(m // tm, n // tn),
        in_specs=in_specs,
        out_specs=out_specs,
        compiler_params=_cparams(2), name=name)(*xs, *w_args, resid)


def _mlp_kernel(x_ref, g_ref, w1_ref, w2_ref, gout_ref, o_ref, *rest, norm_out, emit):
    xn_ref = rest[-1]
    k = pl.program_id(1)

    @pl.when(k == 0)
    def _():
        x = x_ref[...]
        xn_ref[...] = _rmsnorm_bf16(x, g_ref[...])
        o_ref[...] = x

    w1, w2 = w1_ref[...], w2_ref[...]
    if emit:
        w1, w2 = w1.astype(BF16), w2.astype(BF16)
        rest[0][...] = w1
        rest[1][...] = w2
    u = jnp.dot(xn_ref[...], w1, preferred_element_type=F32)
    u = jnp.square(jnp.maximum(u, 0.0)).astype(BF16)
    o_ref[...] += jnp.dot(u, w2, preferred_element_type=F32)

    if norm_out:
        @pl.when(k == pl.num_programs(1) - 1)
        def _():
            o = o_ref[...]
            ms = jnp.mean(o * o, axis=-1, keepdims=True)
            o_ref[...] = o * lax.rsqrt(ms + NORM_EPS) * gout_ref[...]


def _mlp(x, g, w1, w2, g_out, *, tm, tk, name, layer=None):
    m, d = x.shape
    emit = layer is not None
    hid = w1.shape[-1]
    assert m % tm == 0 and hid % tk == 0 and (not emit or m == tm)
    w1o_spec = pl.BlockSpec((d, tk), lambda i, k: (0, k))
    w2o_spec = pl.BlockSpec((tk, d), lambda i, k: (k, 0))
    if emit:
        w_specs = [pl.BlockSpec((None, d, tk), lambda i, k: (layer, 0, k)),
                   pl.BlockSpec((None, tk, d), lambda i, k: (layer, k, 0))]
        out_shape = (jax.ShapeDtypeStruct((m, d), F32), jax.ShapeDtypeStruct((d, hid), BF16),
                     jax.ShapeDtypeStruct((hid, d), BF16))
        out_specs = (pl.BlockSpec((tm, d), lambda i, k: (i, 0)), w1o_spec, w2o_spec)
    else:
        w_specs = [w1o_spec, w2o_spec]
        out_shape = jax.ShapeDtypeStruct((m, d), F32)
        out_specs = pl.BlockSpec((tm, d), lambda i, k: (i, 0))
    return pl.pallas_call(
        functools.partial(_mlp_kernel, norm_out=g_out is not None, emit=emit),
        out_shape=out_shape,
        grid=(m // tm, hid // tk),
        in_specs=[pl.BlockSpec((tm, d), lambda i, k: (i, 0)),
                  pl.BlockSpec((1, d), lambda i, k: (0, 0))] + w_specs + [pl.BlockSpec((1, d), lambda i, k: (0, 0))],
        out_specs=out_specs,
        scratch_shapes=[pltpu.VMEM((tm, d), BF16)],
        compiler_params=_cparams(2), name=name)(x, g.reshape(1, d), w1, w2,
                                                (g if g_out is None else g_out).reshape(1, d))


def _causal_conv(raw, tail, w_ref, b_ref):
    n = raw.shape[0]
    full = jnp.concatenate([tail, raw], axis=0)
    acc = b_ref[...]
    for i in range(SSD_CONV):
        s = SSD_CONV - 1 - i
        acc = acc + full[8 - s:8 - s + n] * w_ref[i:i + 1, :]
    return _silu(acc)


def _gate_groupnorm(y, z, gn):
    y = y * _silu(z)
    gw = y.shape[1] // SSD_GROUPS
    outs = []
    for g in range(SSD_GROUPS):
        yg = y[:, g * gw:(g + 1) * gw]
        ms = jnp.mean(yg * yg, axis=-1, keepdims=True)
        outs.append(yg * lax.rsqrt(ms + NORM_EPS))
    return jnp.concatenate(outs, axis=1) * gn


def _ssd_prompt_kernel(z_ref, xs_ref, bc_ref, dt_ref, wx_ref, wbc_ref, bx_ref, bbc_ref, dtb_ref, alog_ref,
                       dskip_ref, gn_ref, eexp_ref, y_ref, tailx_ref, tailbc_ref, h_ref, y_s):
    c = pl.program_id(0)
    q = SSD_CHUNK
    gs = SSD_STATE
    hp = HEAD_DIM

    @pl.when(c == 0)
    def _():
        tailx_ref[...] = jnp.zeros_like(tailx_ref)
        tailbc_ref[...] = jnp.zeros_like(tailbc_ref)
        h_ref[...] = jnp.zeros_like(h_ref)

    xs_raw = xs_ref[...]
    bc_raw = bc_ref[...]
    xs_c = _causal_conv(xs_raw, tailx_ref[...], wx_ref, bx_ref)
    bc_c = _causal_conv(bc_raw, tailbc_ref[...], wbc_ref, bbc_ref)
    tailx_ref[...] = xs_raw[q - 8:q]
    tailbc_ref[...] = bc_raw[q - 8:q]

    lane = lax.broadcasted_iota(jnp.int32, (q, LANES), 1)
    row = lax.broadcasted_iota(jnp.int32, (q, LANES), 0)
    dt = jax.nn.softplus(dt_ref[...] + dtb_ref[...])
    a = -jnp.exp(alog_ref[...])
    la = jnp.where(lane < N_HEADS, dt * a, 0.0)
    acs = la
    s = 1
    while s < q:
        acs = acs + jnp.where(row >= s, pltpu.roll(acs, s, 0), 0.0)
        s *= 2
    acs_t = acs.T
    acs_last = acs[q - 1:q, :]
    to_end = jnp.exp(acs_last - acs)
    eacs = jnp.exp(acs)
    cdec = jnp.exp(acs_last)
    causal = (lax.broadcasted_iota(jnp.int32, (q, q), 0) >= lax.broadcasted_iota(jnp.int32, (q, q), 1))

    stacked = jnp.concatenate([dt, eacs, to_end], axis=0)
    hi = stacked.astype(BF16)
    lo = (stacked - hi.astype(F32)).astype(BF16)
    e = eexp_ref[...]
    full = jnp.dot(hi, e, preferred_element_type=F32) + jnp.dot(lo, e, preferred_element_type=F32)
    xdt = xs_c * full[0:q]
    eacs_f = full[q:2 * q]
    w_f = xdt * full[2 * q:3 * q]
    low = _low_half((q, LANES))
    top = lax.broadcasted_iota(jnp.int32, (2 * hp, gs), 0) < hp

    rep = N_HEADS // SSD_GROUPS
    for g in range(SSD_GROUPS):
        bg = bc_c[:, g * gs:(g + 1) * gs].astype(BF16)
        cg = bc_c[:, SSD_GROUPS * gs + g * gs:SSD_GROUPS * gs + (g + 1) * gs].astype(BF16)
        cb = lax.dot_general(cg, bg, _NT, preferred_element_type=F32)
        for rp in range(rep // 2):
            h0 = g * rep + 2 * rp
            sl = slice(h0 * hp, (h0 + 2) * hp)
            ms = []
            for h in (h0, h0 + 1):
                lmat = jnp.exp(jnp.where(causal, acs[:, h:h + 1] - acs_t[h:h + 1, :], -jnp.inf))
                ms.append((cb * lmat).astype(BF16))
            xp = xdt[:, sl]
            rhs = jnp.concatenate([jnp.where(low, xp, 0.0), jnp.where(low, 0.0, xp)], axis=0).astype(BF16)
            y = jnp.dot(jnp.concatenate(ms, axis=1), rhs, preferred_element_type=F32)
            hs = h_ref[h0:h0 + 2].reshape(2 * hp, gs)
            y = y + eacs_f[:, sl] * lax.dot_general(cg, hs.astype(BF16), _NT, preferred_element_type=F32)
            st = lax.dot_general(w_f[:, sl].astype(BF16), bg, _TN, preferred_element_type=F32)
            cd = jnp.where(top, cdec[:, h0:h0 + 1], cdec[:, h0 + 1:h0 + 2])
            h_ref[h0:h0 + 2] = (hs * cd + st).reshape(2, hp, gs)
            y_s[:, sl] = y + dskip_ref[:, sl] * xs_c[:, sl]

    y_ref[...] = _gate_groupnorm(y_s[...], z_ref[...], gn_ref[...]).astype(y_ref.dtype)


def _ssd_prompt(p0, dtr, off, seq, wx, wbc, bx, bbc, dtb, alog, dskip, gn, eexp):
    q = SSD_CHUNK
    d_in = wx.shape[1]
    d_bc = wbc.shape[1]
    assert seq % q == 0
    const = lambda shape: pl.BlockSpec(shape, lambda c: (0,) * len(shape))
    return pl.pallas_call(
        _ssd_prompt_kernel,
        out_shape=(jax.ShapeDtypeStruct((seq, d_in), BF16),
                   jax.ShapeDtypeStruct((8, d_in), F32),
                   jax.ShapeDtypeStruct((8, d_bc), F32),
                   jax.ShapeDtypeStruct((N_HEADS, HEAD_DIM, SSD_STATE), F32)),
        grid=(seq // q,),
        in_specs=[pl.BlockSpec((q, d_in), lambda c: (c, off["z"] // d_in)),
                  pl.BlockSpec((q, d_in), lambda c: (c, off["xs"] // d_in)),
                  pl.BlockSpec((q, d_bc), lambda c: (c, off["bc"] // d_bc)),
                  pl.BlockSpec((q, LANES), lambda c: (c, 0)),
                  const((SSD_CONV, d_in)), const((SSD_CONV, d_bc)), const((1, d_in)), const((1, d_bc)),
                  const((1, LANES)), const((1, LANES)), const((1, d_in)), const((1, d_in)), const((LANES, d_in))],
        out_specs=(pl.BlockSpec((q, d_in), lambda c: (c, 0)),
                   const((8, d_in)), const((8, d_bc)), const((N_HEADS, HEAD_DIM, SSD_STATE))),
        scratch_shapes=[pltpu.VMEM((q, d_in), F32)],
        compiler_params=_cparams(1), name="ssd_prompt")(p0, p0, p0, dtr, wx, wbc, bx, bbc, dtb, alog, dskip, gn, eexp)


def _split3(x):
    hi = x.astype(BF16)
    r1 = x - hi.astype(F32)
    mid = r1.astype(BF16)
    lo = (r1 - mid.astype(F32)).astype(BF16)
    return hi, mid, lo


def _ssd_sample_kernel(z_ref, xs_ref, bc_ref, dt_ref, ex_ref, ebc_ref, st_ref, wx_ref, wbc_ref, bx_ref, bbc_ref,
                       dtb_ref, alog_ref, dskip_ref, gn_ref, eexp_ref, y_ref, sto_ref, *, nb, t_len):
    rows = nb * t_len
    gs = SSD_STATE
    gw = xs_ref.shape[1] // SSD_GROUPS
    rep = N_HEADS // SSD_GROUPS

    def tcol(width):
        r = lax.broadcasted_iota(jnp.int32, (rows, width), 0)
        return r & (t_len - 1), r >> int(math.log2(t_len))

    def conv(raw, est, w_ref, b_ref):
        t, _ = tcol(raw.shape[1])
        acc = b_ref[...]
        for i in range(SSD_CONV):
            s = SSD_CONV - 1 - i
            if s == 0:
                sh = raw
            else:
                k = (rows - (SSD_CONV - 1) + s) % rows
                sh = jnp.where(t >= s, pltpu.roll(raw, s, 0), est if k == 0 else pltpu.roll(est, k, 0))
            acc = acc + sh * w_ref[i:i + 1, :]
        return _silu(acc)

    xs_c = conv(xs_ref[...], ex_ref[...], wx_ref, bx_ref)
    bc_c = conv(bc_ref[...], ebc_ref[...], wbc_ref, bbc_ref)
    bm = bc_c[:, :SSD_GROUPS * gs]
    cm = bc_c[:, SSD_GROUPS * gs:]

    t1, _ = tcol(LANES)
    lane = lax.broadcasted_iota(jnp.int32, (rows, LANES), 1)
    dt = jax.nn.softplus(dt_ref[...] + dtb_ref[...])
    a = -jnp.exp(alog_ref[...])
    la = jnp.where(lane < N_HEADS, dt * a, 0.0)
    acs = la
    for s in range(1, t_len):
        acs = acs + jnp.where(t1 >= s, pltpu.roll(la, s, 0), 0.0)
    alast = jnp.where(t1 == t_len - 1, acs, 0.0)
    for u in range(1, t_len):
        alast = alast + jnp.where(t1 == t_len - 1 - u, pltpu.roll(acs, rows - u, 0), 0.0)
    parts = [dt, jnp.exp(acs), jnp.exp(alast - acs), jnp.exp(alast)]
    for k in range(1, t_len):
        parts.append(jnp.exp(acs - pltpu.roll(acs, k, 0)))
    stacked = jnp.concatenate(parts, axis=0)
    hi, mid, lo = _split3(stacked)
    e = eexp_ref[...]
    full = (jnp.dot(hi, e, preferred_element_type=F32) + jnp.dot(mid, e, preferred_element_type=F32)
            + jnp.dot(lo, e, preferred_element_type=F32))
    dt_f, eacs_f, toend_f, cdec_f = (full[i * rows:(i + 1) * rows] for i in range(4))
    dec_f = [None] + [full[(3 + k) * rows:(4 + k) * rows] for k in range(1, t_len)]

    tw, _ = tcol(xs_c.shape[1])
    xdt = xs_c * dt_f
    y = jnp.zeros_like(xs_c)
    for k in range(t_len):
        bmk = bm if k == 0 else pltpu.roll(bm, k, 0)
        prod = cm * bmk
        cbs = []
        for g in range(SSD_GROUPS):
            sg = jnp.sum(prod[:, g * gs:(g + 1) * gs], axis=-1, keepdims=True)
            cbs.append(jnp.broadcast_to(sg, (rows, gw)))
        cb_f = jnp.concatenate(cbs, axis=1)
        if k == 0:
            y = y + cb_f * xdt
        else:
            y = y + jnp.where(tw >= k, cb_f * dec_f[k] * pltpu.roll(xdt, k, 0), 0.0)

    cm16 = cm.astype(BF16)
    bm16 = bm.astype(BF16)
    w_f = (toend_f * xdt)
    _, bg_ = tcol(gw)
    ones_blk = jnp.ones((3, gs), BF16)
    yoffs = []
    for g in range(SSD_GROUPS):
        cg = cm16[:, g * gs:(g + 1) * gs]
        rhs_top = jnp.concatenate([bm16[:, g * gs:(g + 1) * gs], jnp.zeros((rows, gs), BF16)], axis=1)
        rhs_mid = jnp.concatenate([jnp.zeros((3, gs), BF16), ones_blk], axis=1)
        rhs = jnp.concatenate([rhs_top, rhs_mid, jnp.zeros((13, 2 * gs), BF16)], axis=0)
        wg = w_f[:, g * gw:(g + 1) * gw]
        cdg = cdec_f[:, g * gw:(g + 1) * gw]
        yg = jnp.zeros((rows, gw), F32)
        for b in range(nb):
            hb = st_ref[b, g * rep:(g + 1) * rep].reshape(gw, gs)
            yb = lax.dot_general(cg, hb.astype(BF16), _NT, preferred_element_type=F32)
            yg = jnp.where(bg_ == b, yb, yg)
            wb = jnp.where(bg_ == b, wg, 0.0).astype(BF16)
            d_hi, d_mid, d_lo = _split3(cdg[b * t_len:b * t_len + 1])
            lhs = jnp.concatenate([wb, d_hi, d_mid, d_lo, jnp.zeros((13, gw), BF16)], axis=0)
            sd = lax.dot_general(lhs, rhs, _TN, preferred_element_type=F32)
            sto_ref[b, g * rep:(g + 1) * rep] = (hb * sd[:, gs:] + sd[:, :gs]).reshape(rep, HEAD_DIM, gs)
        yoffs.append(yg)
    y = y + eacs_f * jnp.concatenate(yoffs, axis=1) + dskip_ref[...] * xs_c
    y_ref[...] = _gate_groupnorm(y, z_ref[...], gn_ref[...]).astype(y_ref.dtype)


def _ssd_sample(p0, dtr, off, seq, n_b, t_len, epad, state, wx, wbc, bx, bbc, dtb, alog, dskip, gn, eexp, *, nb):
    rows = nb * t_len
    d_in = wx.shape[1]
    d_bc = wbc.shape[1]
    assert n_b % nb == 0 and t_len & (t_len - 1) == 0
    const = lambda shape: pl.BlockSpec(shape, lambda i: (0,) * len(shape))
    st_spec = pl.BlockSpec((nb, N_HEADS, HEAD_DIM, SSD_STATE), lambda i: (i, 0, 0, 0))
    return pl.pallas_call(
        functools.partial(_ssd_sample_kernel, nb=nb, t_len=t_len),
        out_shape=(jax.ShapeDtypeStruct((n_b * t_len, d_in), BF16),
                   jax.ShapeDtypeStruct(state.shape, F32)),
        grid=(n_b // nb,),
        in_specs=[pl.BlockSpec((rows, d_in), lambda i: (i, off["z"] // d_in)),
                  pl.BlockSpec((rows, d_in), lambda i: (i, off["xs"] // d_in)),
                  pl.BlockSpec((rows, d_bc), lambda i: (i, off["bc"] // d_bc)),
                  pl.BlockSpec((rows, LANES), lambda i: (i, 0)),
                  pl.BlockSpec((rows, d_in), lambda i: (i, 0)),
                  pl.BlockSpec((rows, d_bc), lambda i: (i, d_in // d_bc)),
                  st_spec,
                  const((SSD_CONV, d_in)), const((SSD_CONV, d_bc)), const((1, d_in)), const((1, d_bc)),
                  const((1, LANES)), const((1, LANES)), const((1, d_in)), const((1, d_in)),
                  const((LANES, d_in))],
        out_specs=(pl.BlockSpec((rows, d_in), lambda i: (i, 0)), st_spec),
        compiler_params=_cparams(1), name="ssd_sample")(
            p0, p0, p0, dtr, epad, epad, state, wx, wbc, bx, bbc, dtb, alog, dskip, gn, eexp)


def _perm_matrix(dil):
    i = jnp.arange(PERM_CHUNK)
    blk = BF16_ROWS * dil
    src = (i // blk) * blk + (i % BF16_ROWS) * dil + (i % blk) // BF16_ROWS
    return (src[:, None] == jnp.arange(PERM_CHUNK)[None, :]).astype(BF16)


def _band_kernel(*refs, dil, max_dist, use_sink, want_lse, want_krot, nkv, nj):
    it = iter(refs)
    q_ref, k_ref, v_ref, cos_ref, sa_ref, sb_ref = (next(it) for _ in range(6))
    rsw_ref = next(it)
    p_ref, pt_ref = (next(it), next(it)) if dil > 1 else (None, None)
    sink_ref = next(it) if use_sink else None
    o_ref = next(it)
    lse_ref = next(it) if want_lse else None
    krot_ref = next(it) if want_krot else None
    qlo_s, qhi_s, kd_s, vd_s, kdp_s, vdp_s, op_s = (next(it) for _ in range(7))
    lsep_s = next(it) if want_lse else None
    lse_s = next(it) if (want_lse and nj > 1) else None

    c = pl.program_id(0)
    j = pl.program_id(1)
    rows, wq = q_ref.shape
    wk = k_ref.shape[1]
    n_chunk = rows // PERM_CHUNK if dil > 1 else 0
    n_grp = BAND // BF16_ROWS

    def perm(x):
        if dil == 1:
            return x
        x16 = x.astype(BF16)
        return jnp.concatenate(
            [jnp.dot(p_ref[...], x16[k * PERM_CHUNK:(k + 1) * PERM_CHUNK], preferred_element_type=F32)
             for k in range(n_chunk)], axis=0)

    def unperm(xp16):
        if dil == 1:
            return [xp16]
        return [jnp.dot(pt_ref[...], xp16[k * PERM_CHUNK:(k + 1) * PERM_CHUNK], preferred_element_type=F32)
                for k in range(n_chunk)]

    tabs = (cos_ref[...], sa_ref[...], sb_ref[...])
    qn = q_ref[...]
    qh = qn.astype(BF16)
    ql = (qn - qh.astype(F32)).astype(BF16)
    rsw = rsw_ref[...]
    wsw = rsw.shape[0]
    partner = jnp.concatenate(
        [jnp.dot(qh[:, t * wsw:(t + 1) * wsw], rsw, preferred_element_type=F32)
         + jnp.dot(ql[:, t * wsw:(t + 1) * wsw], rsw, preferred_element_type=F32) for t in range(wq // wsw)], axis=1)
    q = perm((qn * _tile_lanes(tabs[0], wq) + partner * _tile_lanes(tabs[1] + tabs[2], wq)) * (ATT_SCALE * LOG2E))
    lowq = _low_half((rows, wq))
    qlo_s[...] = jnp.where(lowq, q, 0.0).astype(BF16)
    qhi_s[...] = jnp.where(lowq, 0.0, q).astype(BF16)
    kr = _rope(k_ref[...], *tabs)
    if want_krot:
        krot_ref[...] = kr
    lowk = _low_half((rows, LANES))
    for x, d_s in ((perm(kr), kd_s), (perm(v_ref[...]), vd_s)):
        up = pltpu.roll(x, HEAD_DIM, 1)
        dn = pltpu.roll(x, wk - HEAD_DIM, 1)
        for t in range(wk // LANES):
            sl = slice(t * LANES, (t + 1) * LANES)
            d_s[2 * t] = jnp.where(lowk, x[:, sl], up[:, sl]).astype(BF16)
            d_s[2 * t + 1] = jnp.where(lowk, dn[:, sl], x[:, sl]).astype(BF16)

    @pl.when(c == 0)
    def _():
        kdp_s[j] = jnp.zeros(kdp_s.shape[1:], BF16)
        vdp_s[j] = jnp.zeros(vdp_s.shape[1:], BF16)

    qi = lax.broadcasted_iota(jnp.int32, (BAND, 2 * BAND), 0)
    kj = lax.broadcasted_iota(jnp.int32, (BAND, 2 * BAND), 1)
    dist = qi - kj + BAND
    bias = jnp.where((dist >= 0) & (dist <= max_dist) & ((kj >= BAND) | (c > 0)), 0.0, NEG)
    col0 = kj == 0
    ones_blk = jnp.ones((2 * BAND, LANES), BF16)
    low = _low_half((BAND, LANES))
    lane = lax.broadcasted_iota(jnp.int32, (BAND, LANES), 1)
    step = BF16_ROWS * dil

    def residue(rho):
        starts = [b * step + rho * BF16_ROWS for b in range(n_grp)]

        def gather(get):
            return jnp.concatenate([get(s) for s in starts], axis=0)

        tiles = []
        lse_acc = jnp.zeros((BAND, LANES), F32)
        for g in range(nkv):
            ca, cb = slice(2 * g * LANES, (2 * g + 1) * LANES), slice((2 * g + 1) * LANES, (2 * g + 2) * LANES)
            lhs = jnp.concatenate([gather(lambda s: qlo_s[pl.ds(s, BF16_ROWS), ca]),
                                   gather(lambda s: qhi_s[pl.ds(s, BF16_ROWS), ca]),
                                   gather(lambda s: qlo_s[pl.ds(s, BF16_ROWS), cb]),
                                   gather(lambda s: qhi_s[pl.ds(s, BF16_ROWS), cb])], axis=0)
            kcat = jnp.concatenate([gather(lambda s: kdp_s[j, g, pl.ds(s, BF16_ROWS), :]),
                                    gather(lambda s: kd_s[g, pl.ds(s, BF16_ROWS), :])], axis=0)
            vcat = jnp.concatenate([gather(lambda s: vdp_s[j, g, pl.ds(s, BF16_ROWS), :]),
                                    gather(lambda s: vd_s[g, pl.ds(s, BF16_ROWS), :])], axis=0)
            head0 = j * REP * nkv + g * REP
            s_full = lax.dot_general(lhs, kcat, _NT, preferred_element_type=F32)
            rsl = [slice(r * BAND, (r + 1) * BAND) for r in range(REP)]
            if use_sink:
                sc = jnp.concatenate([jnp.where(col0, sink_ref[head0 + r] * LOG2E, s_full[rsl[r]] + bias)
                                      for r in range(REP)], axis=0)
            else:
                sc = s_full + jnp.concatenate([bias] * REP, axis=0)
            m = jnp.max(sc, axis=-1, keepdims=True)
            e = jnp.exp2(sc - m)
            od = jnp.dot(e.astype(BF16), jnp.concatenate([vcat, ones_blk], axis=1), preferred_element_type=F32)
            den = od[:, LANES:]
            o = od[:, :LANES] * (1.0 / den)
            tiles.append(jnp.where(low, o[0:BAND], o[BAND:2 * BAND]))
            tiles.append(jnp.where(low, o[2 * BAND:3 * BAND], o[3 * BAND:4 * BAND]))
            if want_lse:
                lse = (m + jnp.log2(den)) * LN2
                for r in range(REP):
                    lse_acc = lse_acc + jnp.where(lane == head0 + r, lse[rsl[r]], 0.0)
        o_res = jnp.concatenate(tiles, axis=1)
        for b, s0 in enumerate(starts):
            op_s[pl.ds(s0, BF16_ROWS), :] = o_res[b * BF16_ROWS:(b + 1) * BF16_ROWS]
            if want_lse:
                lsep_s[pl.ds(s0, BF16_ROWS), :] = lse_acc[b * BF16_ROWS:(b + 1) * BF16_ROWS]

    for rho in range(dil):
        residue(rho)

    for k, blk in enumerate(unperm(op_s[...].astype(BF16))):
        if dil == 1:
            o_ref[...] = blk
        else:
            o_ref[k * PERM_CHUNK:(k + 1) * PERM_CHUNK, :] = blk.astype(o_ref.dtype)
    if want_lse:
        if dil == 1:
            lse_nat = lsep_s[...]
        else:
            parts = [unperm(x) for x in _split3(lsep_s[...])]
            lse_nat = jnp.concatenate([a + b + d for a, b, d in zip(*parts)], axis=0)
        if nj == 1:
            lse_ref[...] = lse_nat
        else:
            @pl.when(j == 0)
            def _():
                lse_s[...] = lse_nat

            @pl.when(j > 0)
            def _():
                lse_s[...] = lse_s[...] + lse_nat

            @pl.when(j == nj - 1)
            def _():
                lse_ref[...] = lse_s[...]
    kdp_s[j] = kd_s[...]
    if use_sink:
        keep = (lax.broadcasted_iota(jnp.int32, (rows, LANES), 0) > 0).astype(F32).astype(BF16)
        vdp_s[j] = vd_s[...] * keep
    else:
        vdp_s[j] = vd_s[...]


def _band_attention(p, seq, q_off, k_off, v_off, tables, sinks, *, dil, max_dist, nkv, want_lse, want_krot, name):
    rows = BAND * dil
    nj = N_KV // nkv
    qw, kw = nkv * REP * HEAD_DIM, nkv * HEAD_DIM
    assert seq % rows == 0 and kw % LANES == 0 and q_off % qw == 0 and k_off % kw == 0 and v_off % kw == 0
    assert dil == 1 or (rows % PERM_CHUNK == 0 and PERM_CHUNK % (BF16_ROWS * dil) == 0)
    use_sink = sinks is not None
    d_att = N_HEADS * HEAD_DIM
    assert not use_sink or (dil == 1 and max_dist < BAND)
    blk = lambda width, cb: pl.BlockSpec((rows, width), lambda c, j: (c, cb + j))
    t_spec = pl.BlockSpec((rows, LANES), lambda c, j: (c, 0))
    in_specs = [blk(qw, q_off // qw), blk(kw, k_off // kw), blk(kw, v_off // kw), t_spec, t_spec, t_spec]
    li = jnp.arange(2 * LANES)
    ci = li % HEAD_DIM
    src = jnp.where(ci < ROT_HALF, li + ROT_HALF, jnp.where(ci < 2 * ROT_HALF, li - ROT_HALF, -1))
    rsw = (li[:, None] == src[None, :]).astype(BF16)
    in_specs.append(pl.BlockSpec((2 * LANES, 2 * LANES), lambda c, j: (0, 0)))
    args = [p, p, p] + list(tables) + [rsw]
    if dil > 1:
        pm = _perm_matrix(dil)
        in_specs += [pl.BlockSpec((PERM_CHUNK, PERM_CHUNK), lambda c, j: (0, 0))] * 2
        args += [pm, pm.T]
    if use_sink:
        in_specs.append(pl.BlockSpec(memory_space=pltpu.SMEM))
        args.append(sinks)
    out_shape = [jax.ShapeDtypeStruct((seq, d_att), BF16)]
    out_specs = [pl.BlockSpec((rows, qw), lambda c, j: (c, j))]
    scratch = [pltpu.VMEM((rows, qw), BF16), pltpu.VMEM((rows, qw), BF16),
               pltpu.VMEM((nkv, rows, LANES), BF16), pltpu.VMEM((nkv, rows, LANES), BF16),
               pltpu.VMEM((nj, nkv, rows, LANES), BF16), pltpu.VMEM((nj, nkv, rows, LANES), BF16),
               pltpu.VMEM((rows, qw), F32)]
    if want_lse:
        out_shape.append(jax.ShapeDtypeStruct((seq, LANES), F32))
        out_specs.append(pl.BlockSpec((rows, LANES), lambda c, j: (c, 0)))
        scratch.append(pltpu.VMEM((rows, LANES), F32))
        if nj > 1:
            scratch.append(pltpu.VMEM((rows, LANES), F32))
    if want_krot:
        out_shape.append(jax.ShapeDtypeStruct((seq, N_KV * HEAD_DIM), F32))
        out_specs.append(pl.BlockSpec((rows, kw), lambda c, j: (c, j)))
    return pl.pallas_call(
        functools.partial(_band_kernel, dil=dil, max_dist=max_dist, use_sink=use_sink, want_lse=want_lse,
                          want_krot=want_krot, nkv=nkv, nj=nj),
        out_shape=tuple(out_shape), grid=(seq // rows, nj), in_specs=in_specs, out_specs=tuple(out_specs),
        scratch_shapes=scratch, compiler_params=_cparams(2), name=name)(*args)


def _merge_kernel(o1_ref, o2_ref, o3_ref, l1_ref, l2_ref, l3_ref, e_ref, out_ref):
    ls = [l1_ref[...], l2_ref[...], l3_ref[...]]
    mx = jnp.maximum(jnp.maximum(ls[0], ls[1]), ls[2])
    es = [jnp.exp(l - mx) for l in ls]
    inv = 1.0 / (es[0] + es[1] + es[2])
    e = e_ref[...]
    acc = None
    for en, o_ref in zip(es, (o1_ref, o2_ref, o3_ref)):
        wf = jnp.dot((en * inv).astype(BF16), e, preferred_element_type=F32)
        term = wf * o_ref[...].astype(F32)
        acc = term if acc is None else acc + term
    out_ref[...] = acc.astype(out_ref.dtype)


def _merge(os_, ls_, eexp, *, tm):
    seq, d = os_[0].shape
    o_spec = pl.BlockSpec((tm, d), lambda i: (i, 0))
    l_spec = pl.BlockSpec((tm, LANES), lambda i: (i, 0))
    return pl.pallas_call(
        _merge_kernel, out_shape=jax.ShapeDtypeStruct((seq, d), BF16), grid=(seq // tm,),
        in_specs=[o_spec] * 3 + [l_spec] * 3 + [pl.BlockSpec((LANES, d), lambda i: (0, 0))],
        out_specs=o_spec, compiler_params=_cparams(1), name="dil_merge")(*os_, *ls_, eexp)


def _expand_q(qb, t_len):
    gd = N_KV * HEAD_DIM
    sub = lax.broadcasted_iota(jnp.int32, (N_KV, gd), 0)
    lg = lax.broadcasted_iota(jnp.int32, (N_KV, gd), 1) >> _LOG2_HEAD_DIM
    diag = sub == lg
    tiles = []
    for r in range(REP):
        vr = jnp.concatenate(
            [qb[:, (g * REP + r) * HEAD_DIM:(g * REP + r + 1) * HEAD_DIM] for g in range(N_KV)], axis=1)
        for t in range(t_len):
            tiles.append(jnp.where(diag, jnp.broadcast_to(vr[t:t + 1], (N_KV, gd)), 0.0))
    return jnp.concatenate(tiles, axis=0).astype(BF16)


def _collapse_o(r_full, t_len):
    gd = N_KV * HEAD_DIM
    n = r_full.shape[0]
    sub = lax.broadcasted_iota(jnp.int32, (n, gd), 0) & (N_KV - 1)
    lg = lax.broadcasted_iota(jnp.int32, (n, gd), 1) >> _LOG2_HEAD_DIM
    masked = jnp.where(sub == lg, r_full, 0.0)
    red = jnp.sum(masked.reshape(n // N_KV, N_KV, gd), axis=1)
    pieces = []
    for g in range(N_KV):
        for r in range(REP):
            pieces.append(red[r * t_len:(r + 1) * t_len, g * HEAD_DIM:(g + 1) * HEAD_DIM])
    return jnp.concatenate(pieces, axis=1)


def _row_t(shape, t_len):
    r = lax.broadcasted_iota(jnp.int32, shape, 0)
    return (r >> 3) & (t_len - 1)


def _cached_attend(qx, kt, vt, mask_w, kn, vn, mask_n, sink):
    s_w = jnp.where(mask_w, jnp.dot(qx.astype(kt.dtype), kt, preferred_element_type=F32), NEG)
    s_n = jnp.where(mask_n, lax.dot_general(qx, kn, _NT, preferred_element_type=F32), NEG)
    m = jnp.maximum(jnp.max(s_w, axis=-1, keepdims=True), jnp.max(s_n, axis=-1, keepdims=True))
    if sink is not None:
        m = jnp.maximum(m, sink)
    e_w = jnp.exp(s_w - m)
    e_n = jnp.exp(s_n - m)
    den = jnp.sum(e_w, axis=-1, keepdims=True) + jnp.sum(e_n, axis=-1, keepdims=True)
    if sink is not None:
        den = den + jnp.exp(sink - m)
    r = (lax.dot_general(e_w.astype(vt.dtype), vt, _NT, preferred_element_type=F32)
         + jnp.dot(e_n.astype(BF16), vn, preferred_element_type=F32))
    return r / den, m + jnp.log(den)


def _swa_cached_kernel(q_ref, kn_ref, vn_ref, cos_ref, sa_ref, sb_ref, ck_ref, cv_ref, sink_ref,
                       o_ref, krot_ref, *, nb, t_len):
    tabs = (cos_ref[...], sa_ref[...], sb_ref[...])
    q = _rope(q_ref[...], *tabs) * ATT_SCALE
    kn = _rope(kn_ref[...], *tabs)
    krot_ref[...] = kn
    kn16 = kn.astype(BF16)
    vn16 = vn_ref[...].astype(BF16)
    lb = ck_ref.shape[3]
    gd = N_KV * HEAD_DIM
    nrow = REP * t_len * N_KV
    tq = _row_t((nrow, lb), t_len)
    mask_w = lax.broadcasted_iota(jnp.int32, (nrow, lb), 1) >= tq + 1
    mask_n = lax.broadcasted_iota(jnp.int32, (nrow, BF16_ROWS), 1) <= _row_t((nrow, BF16_ROWS), t_len)
    sk = sink_ref[...]
    pad = jnp.zeros((BF16_ROWS - t_len, gd), BF16)
    outs = []
    for b in range(nb):
        rs = slice(b * t_len, (b + 1) * t_len)
        r_full, _ = _cached_attend(
            _expand_q(q[rs], t_len), ck_ref[b].reshape(gd, lb).astype(BF16), cv_ref[b].reshape(gd, lb).astype(BF16),
            mask_w, jnp.concatenate([kn16[rs], pad], axis=0), jnp.concatenate([vn16[rs], pad], axis=0), mask_n, sk)
        outs.append(_collapse_o(r_full, t_len))
    o_ref[...] = jnp.concatenate(outs, axis=0).astype(o_ref.dtype)


def _swa_cached(p0, off, seq, n_b, t_len, tables, cache_kt, cache_vt, sink_col, *, nb):
    rows = nb * t_len
    d_att = N_HEADS * HEAD_DIM
    d_kv = N_KV * HEAD_DIM
    lb = cache_kt.shape[3]
    assert lb == SWA_WINDOW and t_len <= BF16_ROWS
    r0 = seq // rows
    nrow = REP * t_len * N_KV
    c_spec = pl.BlockSpec((nb, N_KV, HEAD_DIM, lb), lambda i: (i, 0, 0, 0))
    t_spec = pl.BlockSpec((rows, LANES), lambda i: (r0 + i, 0))
    return pl.pallas_call(
        functools.partial(_swa_cached_kernel, nb=nb, t_len=t_len),
        out_shape=(jax.ShapeDtypeStruct((n_b * t_len, d_att), BF16),
                   jax.ShapeDtypeStruct((n_b * t_len, d_kv), F32)),
        grid=(n_b // nb,),
        in_specs=[pl.BlockSpec((rows, d_att), lambda i: (i, off["q"] // d_att)),
                  pl.BlockSpec((rows, d_kv), lambda i: (i, off["k"] // d_kv)),
                  pl.BlockSpec((rows, d_kv), lambda i: (i, off["v"] // d_kv)),
                  t_spec, t_spec, t_spec, c_spec, c_spec,
                  pl.BlockSpec((nrow, 1), lambda i: (0, 0))],
        out_specs=(pl.BlockSpec((rows, d_att), lambda i: (i, 0)),
                   pl.BlockSpec((rows, d_kv), lambda i: (i, 0))),
        compiler_params=_cparams(1), name="swa_cached")(p0, p0, p0, *tables, cache_kt, cache_vt, sink_col)


def _dil_cached_kernel(q1_ref, q2_ref, q3_ref, kn_ref, vn_ref, cos_ref, sa_ref, sb_ref, ck_ref, cv_ref,
                       o_ref, krot_ref, *, nb, t_len):
    tabs = (cos_ref[...], sa_ref[...], sb_ref[...])
    qs = [_rope(r[...], *tabs) * ATT_SCALE for r in (q1_ref, q2_ref, q3_ref)]
    kn = _rope(kn_ref[...], *tabs)
    krot_ref[...] = kn
    kn16 = kn.astype(BF16)
    vn16 = vn_ref[...].astype(BF16)
    lbuf = ck_ref.shape[3]
    gd = N_KV * HEAD_DIM
    nrow = REP * t_len * N_KV
    tqn = _row_t((nrow, BF16_ROWS), t_len)
    coln = lax.broadcasted_iota(jnp.int32, (nrow, BF16_ROWS), 1)
    masks = []
    for window, dil in DIL_PATTERNS:
        tq = _row_t((nrow, window), t_len)
        col = lax.broadcasted_iota(jnp.int32, (nrow, window), 1)
        if dil == 1:
            masks.append((col >= tq, coln <= tqn))
        else:
            masks.append(((col & (dil - 1)) == tq, coln == tqn))
    pad = jnp.zeros((BF16_ROWS - t_len, gd), BF16)
    outs = []
    for b in range(nb):
        rs = slice(b * t_len, (b + 1) * t_len)
        kt = ck_ref[b].reshape(gd, lbuf)
        vt = cv_ref[b].reshape(gd, lbuf)
        knp = jnp.concatenate([kn16[rs], pad], axis=0)
        vnp = jnp.concatenate([vn16[rs], pad], axis=0)
        res = []
        for gi, (window, dil) in enumerate(DIL_PATTERNS):
            res.append(_cached_attend(_expand_q(qs[gi][rs], t_len), kt[:, lbuf - window:], vt[:, lbuf - window:],
                                      masks[gi][0], knp, vnp, masks[gi][1], None))
        mx = jnp.maximum(jnp.maximum(res[0][1], res[1][1]), res[2][1])
        ws = [jnp.exp(l - mx) for _, l in res]
        inv = 1.0 / (ws[0] + ws[1] + ws[2])
        merged = (ws[0] * inv) * res[0][0] + (ws[1] * inv) * res[1][0] + (ws[2] * inv) * res[2][0]
        outs.append(_collapse_o(merged, t_len))
    o_ref[...] = jnp.concatenate(outs, axis=0).astype(o_ref.dtype)


def _dil_cached(p1, seq, n_b, t_len, tables, cache_kt, cache_vt, *, nb):
    rows = nb * t_len
    d_att = N_HEADS * HEAD_DIM
    d_kv = N_KV * HEAD_DIM
    lbuf = cache_kt.shape[3]
    assert rows % 8 == 0 and seq % rows == 0 and t_len <= BF16_ROWS
    for window, dil in DIL_PATTERNS:
        assert window <= lbuf and (PAST_LEN - lbuf) % dil == 0 and lbuf % dil == 0 and (dil == 1 or dil >= t_len)
    r0 = seq // rows
    c_spec = pl.BlockSpec((nb, N_KV, HEAD_DIM, lbuf), lambda i: (i, 0, 0, 0))
    t_spec = pl.BlockSpec((rows, LANES), lambda i: (r0 + i, 0))
    nq = d_att * len(DIL_PATTERNS)
    return pl.pallas_call(
        functools.partial(_dil_cached_kernel, nb=nb, t_len=t_len),
        out_shape=(jax.ShapeDtypeStruct((n_b * t_len, d_att), F32),
                   jax.ShapeDtypeStruct((n_b * t_len, d_kv), F32)),
        grid=(n_b // nb,),
        in_specs=[pl.BlockSpec((rows, d_att), lambda i: (i, 0)),
                  pl.BlockSpec((rows, d_att), lambda i: (i, 1)),
                  pl.BlockSpec((rows, d_att), lambda i: (i, 2)),
                  pl.BlockSpec((rows, d_kv), lambda i: (i, nq // d_kv)),
                  pl.BlockSpec((rows, d_kv), lambda i: (i, nq // d_kv + 1)),
                  t_spec, t_spec, t_spec, c_spec, c_spec],
        out_specs=(pl.BlockSpec((rows, d_att), lambda i: (i, 0)),
                   pl.BlockSpec((rows, d_kv), lambda i: (i, 0))),
        compiler_params=_cparams(1), name="dil_cached")(p1, p1, p1, p1, p1, *tables, cache_kt, cache_vt)


def _row_tile(m, pref):
    for t in (1024, 512, 256, 128, 64, 32, 16, 8):
        if t <= pref and m % t == 0:
            return t
    raise ValueError(m)


def kernel(x_prompt, x_sample, state_conv, state_ssm, cache_swa_k, cache_swa_v, cache_dil_k, cache_dil_v, norm_mix, norm_mlp, e_w_in, e_conv_w, e_conv_b, e_dt_bias, e_a_log, e_d_skip, e_gate_norm, e_sinks, e_w_out, o_w_in, o_w_out, mlp_w1, mlp_w2, norm_final):
    nbp, seq, d = x_prompt.shape
    n_b, t_len, _ = x_sample.shape
    assert nbp == 1 and d == N_HEADS * HEAD_DIM and norm_mix.shape[0] == 2
    ms = n_b * t_len
    m = seq + ms
    tms = (_row_tile(seq, 1024), _row_tile(ms, 512))
    d_kv = N_KV * HEAD_DIM
    d_bc = 2 * SSD_GROUPS * SSD_STATE
    conv_dim = d + d_bc
    both = lambda fn, *rows, **kw: tuple(
        fn(*(r[s] for r in rows), tm=tms[s], name=kw["name"] + ("_p", "_s")[s], **{k: v for k, v in kw.items() if k != "name"})
        for s in range(2))

    h = (x_prompt.reshape(seq, d), x_sample.reshape(ms, d))
    tables = _rope_tables(m, seq, t_len, _row_tile(math.gcd(seq, ms), 512))
    eexp = (lax.broadcasted_iota(jnp.int32, (LANES, d), 1) // HEAD_DIM
            == lax.broadcasted_iota(jnp.int32, (LANES, d), 0)).astype(BF16)
    keys_minor = lambda c: jnp.transpose(c[0], (0, 2, 3, 1))

    wi = e_w_in[0]
    c_z, c_xbc, c_dt, c_q = 0, d, d + conv_dim, d + conv_dim + N_HEADS
    c_k, c_v = c_q + d, c_q + d + d_kv
    tn0 = 1024
    wi16 = wi.astype(BF16)
    w0_q, w0_kv = wi16[:, c_q:c_q + d], wi16[:, c_k:c_v + d_kv]
    w0_dt = jnp.pad(wi16[:, c_dt:c_dt + N_HEADS], ((0, 0), (0, LANES - N_HEADS)))
    assert c_z == 0 and (c_xbc + d) % tn0 == 0 and d_bc % tn0 == 0 and d % tn0 == 0
    segs0 = [(wi16, 0, 2 * d // tn0), (w0_q, 0, d // tn0), (wi16, (c_xbc + d) // tn0, d_bc // tn0),
             (w0_kv, 0, 2 * d_kv // tn0)]
    off0 = {"z": 0, "xs": d, "q": 2 * d, "bc": 3 * d, "k": 3 * d + d_bc, "v": 3 * d + d_bc + d_kv}
    g0 = norm_mix[0]
    (p0_p, dt_p), (p0_s, dt_s) = both(
        lambda x, tm, name: _norm_matmul_multi(x, g0, segs0, w0_dt, tm=tm, tn=tn0, name=name), h, name="l0_in_proj")

    cw, cb = e_conv_w[0], e_conv_b[0]
    wx, wbc = cw[:, :d], cw[:, d:]
    bx, bbc = cb[:d].reshape(1, d), cb[d:].reshape(1, d_bc)
    pad_h = lambda v: jnp.pad(v.reshape(1, N_HEADS), ((0, 0), (0, LANES - N_HEADS)))
    dtb, alog = pad_h(e_dt_bias[0]), pad_h(e_a_log[0])
    dskip = jnp.repeat(e_d_skip[0], HEAD_DIM).reshape(1, d)
    gn = e_gate_norm[0].reshape(1, d)

    ssd_p, tail_x, tail_bc, p_ssm = _ssd_prompt(p0_p, dt_p, off0, seq, wx, wbc, bx, bbc, dtb, alog, dskip, gn, eexp)
    epad = jnp.pad(state_conv[0], ((0, 0), (0, 1), (0, 0))).reshape(ms, conv_dim)
    ssd_s, s_ssm = _ssd_sample(p0_s, dt_s, off0, seq, n_b, t_len, epad, state_ssm[0], wx, wbc, bx, bbc, dtb, alog,
                               dskip, gn, eexp, nb=8)

    att_p, krot0_p = _band_attention(p0_p, seq, off0["q"], off0["k"], off0["v"], tables, e_sinks[0], dil=1,
                                     max_dist=SWA_WINDOW - 1, nkv=N_KV, want_lse=False, want_krot=True,
                                     name="swa_prompt")
    sink_col = jnp.broadcast_to(e_sinks[0].reshape(N_KV, REP).T[:, None, :], (REP, t_len, N_KV)).reshape(-1, 1)
    att_s, krot0_s = _swa_cached(p0_s, off0, seq, n_b, t_len, tables, keys_minor(cache_swa_k),
                                 keys_minor(cache_swa_v), sink_col, nb=8)

    def mlp(hh, layer, g_out=None):
        g = norm_mlp[layer]
        o_s, w1_16, w2_16 = _mlp(hh[1], g, mlp_w1, mlp_w2, g_out, tm=ms, tk=512, name=f"l{layer}_mlp_s", layer=layer)
        o_p = _mlp(hh[0], g, w1_16, w2_16, g_out, tm=tms[0], tk=1024, name=f"l{layer}_mlp_p")
        return o_p, o_s

    def out_proj(xs_p, xs_s, w, hh, name):
        h_s, *w16 = _matmul_resid(xs_s, w, hh[1], tm=ms, tn=512, name=name + "_s", emit=True)
        return _matmul_resid(xs_p, w16, hh[0], tm=tms[0], tn=1024 if len(xs_p) > 1 else 2048, name=name + "_p"), h_s

    h = out_proj([ssd_p, att_p], [ssd_s, att_s], e_w_out[0], h, "l0_out_proj")
    h = mlp(h, 0)

    n_pat = len(DIL_PATTERNS)
    p1_s, w1i = _norm_matmul(h[1], norm_mix[1], o_w_in[0], tm=ms, tn=1024, name="l1_in_proj_s", emit=True)
    p1_p = _norm_matmul(h[0], norm_mix[1], w1i, tm=tms[0], tn=1792, name="l1_in_proj_p")
    k_off, v_off = n_pat * d, n_pat * d + d_kv
    os_, ls_ = [], []
    krot1_p = None
    for gi, (window, dil) in enumerate(DIL_PATTERNS):
        res = _band_attention(p1_p, seq, gi * d, k_off, v_off, tables, None, dil=dil, max_dist=window // dil,
                              nkv=(2 if BAND * dil * N_HEADS * HEAD_DIM * 4 > 8 * 2 ** 20 else N_KV),
                              want_lse=True, want_krot=(gi == 0), name=f"dil_prompt_{dil}")
        os_.append(res[0])
        ls_.append(res[1])
        if gi == 0:
            krot1_p = res[2]
    merged_p = _merge(os_, ls_, eexp, tm=_row_tile(seq, 512))
    dil_s, krot1_s = _dil_cached(p1_s, seq, n_b, t_len, tables, keys_minor(cache_dil_k), keys_minor(cache_dil_v),
                                 nb=2)
    h = out_proj([merged_p], [dil_s.astype(BF16)], o_w_out[0], h, "l1_out_proj")
    y_p, y_s = mlp(h, 1, norm_final)

    keep_swa = min(SWA_WINDOW, seq)
    keep_dil = min(max(w for w, _ in DIL_PATTERNS), seq)
    kv4 = lambda a: a.reshape(a.shape[0], N_KV, HEAD_DIM)
    y_prompt = y_p.reshape(1, seq, d)
    y_sample = y_s.reshape(n_b, t_len, d)
    p_conv = jnp.concatenate([tail_x[8 - (SSD_CONV - 1):], tail_bc[8 - (SSD_CONV - 1):]], axis=1)[None, None]
    p_swa_k = kv4(krot0_p[seq - keep_swa:])[None, None]
    p_swa_v = kv4(p0_p[seq - keep_swa:, off0["v"]:off0["v"] + d_kv])[None, None]
    p_dil_k = kv4(krot1_p[seq - keep_dil:])[None, None]
    p_dil_v = kv4(p1_p[seq - keep_dil:, v_off:v_off + d_kv])[None, None]
    xbc_s = jnp.concatenate([p0_s[:, off0["xs"]:off0["xs"] + d], p0_s[:, off0["bc"]:off0["bc"] + d_bc]], axis=1)
    s_conv = xbc_s.reshape(n_b, t_len, conv_dim)[:, t_len - (SSD_CONV - 1):][None]
    bt = lambda a: a.reshape(n_b, t_len, N_KV, HEAD_DIM)[None]
    s_swa_k = bt(krot0_s)
    s_swa_v = bt(p0_s[:, off0["v"]:off0["v"] + d_kv])
    s_dil_k = bt(krot1_s)
    s_dil_v = bt(p1_s[:, v_off:v_off + d_kv])
    return (y_prompt, y_sample, p_conv, p_ssm[None, None], p_swa_k, p_swa_v, p_dil_k, p_dil_v,
            s_conv, s_ssm[None], s_swa_k, s_swa_v, s_dil_k, s_dil_v)
```
